```python
import math
import jax, jax.numpy as jnp
from jax import lax
import numpy as np

D_MODEL = 2048
BATCH = 4
SEQ = 4096
DEPTH = 2

GRID_W = 64
CTX_LEN = 256

FOUR_W = 512
FOUR_GROUPS = 4
FOUR_GW = FOUR_W // FOUR_GROUPS
HY_W = 512
HY_ORDER = 2
DN_HEADS = 8
DN_DK = 128
DN_DV = 128
DN_W = DN_HEADS * DN_DV
MIX_W = FOUR_W + HY_W + DN_W

OFF_FOUR = 0
OFF_HY = OFF_FOUR + FOUR_W
OFF_DN = OFF_HY + (HY_ORDER + 1) * HY_W
OFF_AB = OFF_DN + 3 * DN_W
OFF_Z = OFF_AB + 4 * DN_HEADS
IN_W = OFF_Z + DN_W
HY_CONV_CH = (HY_ORDER + 1) * HY_W
CONV_CH = OFF_AB - OFF_HY
CONV_K = 3

HY_EMB = 33
HY_FILTER_W = 64
HY_FAST_DECAY = 0.3
HY_SLOW_DECAY = 1.5
HY_TARGET = 1e-2

DN_CHUNK = 64

N_GROUPS = 4
EXPERTS_PER_GROUP = 8
N_EXPERTS = N_GROUPS * EXPERTS_PER_GROUP
TOP_K = 2
EXPERT_FF = 1024
MOE_BLOCK = 128

EPS = 1e-6
F32 = jnp.float32

kernel_name = 'hybrid_fourier_hyena_deltanet_hmoe_dit'


def rmsnorm(x, g):
    xf = x.astype(F32)
    y = xf * lax.rsqrt(jnp.mean(xf * xf, axis=-1, keepdims=True) + EPS)
    return (y * g.astype(F32)).astype(x.dtype)


def modulate(xn, shift, scale):
    return xn * (1 + scale) + shift


def l2norm(x):
    return x * lax.rsqrt(jnp.sum(x * x, axis=-1, keepdims=True) + EPS)


def short_conv(u, w, b, rows, width):
    bsz, L, C = u.shape
    y = lax.conv_general_dilated(u.reshape(bsz, rows, width, C), w[:, :, None, :].astype(u.dtype),
                                 (1, 1), ((1, 1), (1, 1)),
                                 dimension_numbers=('NHWC', 'HWIO', 'NHWC'), feature_group_count=C)
    return y.reshape(bsz, L, C) + b.astype(u.dtype)


def fourier_mix(u):
    bsz, L, _ = u.shape
    uf = u.astype(F32).reshape(bsz, L, FOUR_GROUPS, FOUR_GW)
    return jnp.fft.fft2(uf, axes=(1, 3), norm='ortho').real.reshape(bsz, L, FOUR_W)


def hyena_filters(L, w1, b1, w2, b2, w3, b3, w4, freq):
    t = jnp.linspace(0.0, 1.0, L, dtype=F32)[:, None]
    bands = (HY_EMB - 1) // 2
    ang = (2.0 * math.pi / L) * jnp.arange(L, dtype=F32)[:, None]
    f = jnp.linspace(1e-4, bands - 1, bands, dtype=F32)
    z = jnp.concatenate([t, jnp.cos(f * ang), -jnp.sin(f * ang)], axis=-1)
    fr = freq.astype(F32)
    h = jnp.sin(fr * (z @ w1.astype(F32) + b1.astype(F32)))
    h = jnp.sin(fr * (h @ w2.astype(F32) + b2.astype(F32)))
    h = jnp.sin(fr * (h @ w3.astype(F32) + b3.astype(F32)))
    h = (h @ w4.astype(F32)).reshape(L, 2, HY_ORDER, HY_W)
    max_decay = math.log(HY_TARGET) / HY_FAST_DECAY
    min_decay = math.log(HY_TARGET) / HY_SLOW_DECAY
    deltas = jnp.linspace(min_decay, max_decay, HY_W, dtype=F32)
    h = h * jnp.exp(-t * jnp.abs(deltas))[:, None, None, :]
    hf, hb = h[:, 0], h[:, 1]
    zero = jnp.zeros((1, HY_ORDER, HY_W), F32)
    full = jnp.concatenate([hf, zero, hb[1:][::-1]], axis=0)
    full = full / jnp.sum(jnp.abs(full), axis=0, keepdims=True)
    return jnp.fft.rfft(full, axis=0)


def long_conv(u, hfreq, bias):
    L = u.shape[1]
    y = jnp.fft.irfft(jnp.fft.rfft(u, n=2 * L, axis=1) * hfreq, n=2 * L, axis=1)[:, :L]
    return y + bias * u


def hyena_mix(v, x1, x2, hfreq, bias):
    zz = x1 * long_conv(v, hfreq[:, 0], bias[0])
    return x2 * long_conv(zz, hfreq[:, 1], bias[1])


def dn_heads(qkv_pre, ab, a_log, dt_bias):
    bsz, L, _ = qkv_pre.shape
    qkv = jax.nn.silu(qkv_pre.astype(F32)).reshape(bsz, L, 3, DN_HEADS, DN_DK)
    qkv = jnp.transpose(qkv, (2, 0, 3, 1, 4))
    q = l2norm(qkv[0]) * (DN_DK ** -0.5)
    k = l2norm(qkv[1])
    v = qkv[2]
    ab = jnp.transpose(ab.astype(F32).reshape(bsz, L, 4, DN_HEADS), (2, 0, 3, 1))
    g = -jnp.exp(a_log.astype(F32))[:, None, :, None] * jax.nn.softplus(ab[:2] + dt_bias.astype(F32)[:, None, :, None])
    beta = jax.nn.sigmoid(ab[2:])
    return q, k, v, g, beta


def gated_delta_chunked(q, k, v, g, beta, s0, with_output):
    bsz, H, L, DK = k.shape
    DV = v.shape[-1]
    C = DN_CHUNK
    n = L // C
    chunk = lambda t: jnp.moveaxis(t.reshape(bsz, H, n, C, *t.shape[3:]), 2, 0)
    q, k, v, g, beta = (chunk(t) for t in (q, k, v, g, beta))
    g = jnp.cumsum(g, axis=-1)
    lower = jnp.tril(jnp.ones((C, C), dtype=bool))
    strict = jnp.tril(jnp.ones((C, C), dtype=bool), -1)
    diff = g[..., :, None] - g[..., None, :]
    decay = jnp.where(lower, jnp.exp(jnp.where(lower, diff, 0.0)), 0.0)
    k_beta = k * beta[..., None]
    m = jnp.where(strict, jnp.einsum('nbhid,nbhjd->nbhij', k_beta, k) * decay, 0.0)
    eye = jnp.eye(C, dtype=F32)
    t_inv = lax.linalg.triangular_solve(eye + m, jnp.broadcast_to(eye, m.shape),
                                        left_side=True, lower=True, unit_diagonal=True)
    u = t_inv @ (v * beta[..., None])
    w = t_inv @ (k_beta * jnp.exp(g)[..., None])
    g_last = g[..., -1]
    k_tail = k * jnp.exp(g_last[..., None] - g)[..., None]

    def advance(s, u_c, w_c, kt_c, gl_c):
        v_new = u_c - jnp.einsum('bhck,bhkv->bhcv', w_c, s)
        s_next = s * jnp.exp(gl_c)[..., None, None] + jnp.einsum('bhck,bhcv->bhkv', kt_c, v_new)
        return v_new, s_next

    if with_output:
        attn = jnp.where(lower, jnp.einsum('nbhid,nbhjd->nbhij', q, k) * decay, 0.0)
        q_dec = q * jnp.exp(g)[..., None]

        def step(s, inp):
            u_c, w_c, kt_c, gl_c, qd_c, at_c = inp
            v_new, s_next = advance(s, u_c, w_c, kt_c, gl_c)
            o = jnp.einsum('bhck,bhkv->bhcv', qd_c, s) + jnp.einsum('bhij,bhjv->bhiv', at_c, v_new)
            return s_next, o

        s, o = lax.scan(step, s0, (u, w, k_tail, g_last, q_dec, attn))
        return jnp.moveaxis(o, 0, 2).reshape(bsz, H, L, DV), s

    def step_state(s, inp):
        _, s_next = advance(s, *inp)
        return s_next, None

    s, _ = lax.scan(step_state, s0, (u, w, k_tail, g_last))
    return None, s


def dn_bidirectional(q, k, v, g, beta, s0_f, s0_b, with_output):
    fl = lambda t: jnp.flip(t, axis=2)
    o_f, s_f = gated_delta_chunked(q, k, v, g[0], beta[0], s0_f, with_output)
    o_b, s_b = gated_delta_chunked(fl(q), fl(k), fl(v), fl(g[1]), fl(beta[1]), s0_b, with_output)
    o = o_f + fl(o_b) if with_output else None
    return o, s_f, s_b


def mix_stream(p, rows, width, s0_f, s0_b, conv_w, conv_b, four_g, hy_params, hy_bias, hy_g,
               a_log, dt_bias, dn_g):
    bsz, L, _ = p.shape
    dt = p.dtype
    y_four = rmsnorm(fourier_mix(p[..., OFF_FOUR:OFF_HY]), four_g)
    conv = short_conv(p[..., OFF_HY:OFF_AB], conv_w, conv_b, rows, width)
    hv, hx1, hx2 = jnp.split(conv[..., :HY_CONV_CH].astype(F32), 3, axis=-1)
    hfreq = hyena_filters(L, *hy_params)
    y_hy = rmsnorm(hyena_mix(hv, hx1, hx2, hfreq, hy_bias.astype(F32)), hy_g)
    q, k, v, g, beta = dn_heads(conv[..., HY_CONV_CH:], p[..., OFF_AB:OFF_Z], a_log, dt_bias)
    o, s_f, s_b = dn_bidirectional(q, k, v, g, beta, s0_f, s0_b, True)
    o = jnp.transpose(o, (0, 2, 1, 3))
    zg = p[..., OFF_Z:].astype(F32).reshape(bsz, L, DN_HEADS, DN_DV)
    y_dn = (rmsnorm(o, dn_g) * jax.nn.silu(zg)).reshape(bsz, L, DN_W)
    y = jnp.concatenate([y_four, y_hy, y_dn], axis=-1).astype(dt)
    return y, s_f, s_b


def grouped_experts(h, expert, gate, w_gate, w_up, w_down):
    T, D = h.shape
    A = T * TOP_K
    flat_e = expert.reshape(A)
    flat_tok = jnp.repeat(jnp.arange(T, dtype=jnp.int32), TOP_K)
    flat_w = gate.reshape(A)
    order = jnp.argsort(flat_e)
    e_sorted = flat_e[order]
    counts = jnp.bincount(flat_e, length=N_EXPERTS)
    padded = (counts + MOE_BLOCK - 1) // MOE_BLOCK * MOE_BLOCK
    pad_end = jnp.cumsum(padded)
    pad_start = pad_end - padded
    start = jnp.cumsum(counts) - counts
    dest = pad_start[e_sorted] + jnp.arange(A, dtype=jnp.int32) - start[e_sorted]
    n_blocks = (A + N_EXPERTS * (MOE_BLOCK - 1) + MOE_BLOCK - 1) // MOE_BLOCK
    P = n_blocks * MOE_BLOCK
    buf_tok = jnp.zeros((P,), jnp.int32).at[dest].set(flat_tok[order])
    buf_w = jnp.zeros((P,), F32).at[dest].set(flat_w[order])
    block_e = jnp.minimum(jnp.searchsorted(pad_end, jnp.arange(n_blocks, dtype=jnp.int32) * MOE_BLOCK, side='right'),
                          N_EXPERTS - 1)
    xb = h[buf_tok].reshape(n_blocks, MOE_BLOCK, D)

    def expert_block(args):
        xblk, e = args
        return (jax.nn.silu(xblk @ w_gate[e]) * (xblk @ w_up[e])) @ w_down[e]

    yb = lax.map(expert_block, (xb, block_e))
    y = yb.reshape(P, D) * buf_w[:, None].astype(h.dtype)
    return jnp.zeros_like(h).at[buf_tok].add(y)


def hier_moe(h, w_rc, b_rc, w_rf, b_rf, w_gate, w_up, w_down):
    T = h.shape[0]
    hr = h.astype(F32)
    p_grp = jax.nn.softmax(hr @ w_rc.astype(F32) + b_rc.astype(F32), axis=-1)
    p_sel, grp = lax.top_k(p_grp, 1)
    fine = (hr @ w_rf.astype(F32) + b_rf.astype(F32)).reshape(T, N_GROUPS, EXPERTS_PER_GROUP)
    fine = jnp.take_along_axis(fine, grp[:, :, None], axis=1)[:, 0]
    top_v, top_i = lax.top_k(fine, TOP_K)
    gate = p_sel * jax.nn.softmax(top_v, axis=-1)
    expert = grp * EXPERTS_PER_GROUP + top_i
    return grouped_experts(h, expert, gate, w_gate, w_up, w_down)


def setup_inputs(seed: int = 0) -> dict:
    key = jax.random.key(seed)
    ks = iter(jax.random.split(key, 40))
    D = D_MODEL
    nrm = lambda shape, s: jax.random.normal(next(ks), shape, F32) * s
    x = nrm((BATCH, SEQ, D), 1.0)
    c = nrm((BATCH, D), 1.0)
    ctx = nrm((BATCH, CTX_LEN, D), 1.0)
    c_ctx = nrm((D,), 1.0)
    norm1_g = 1.0 + nrm((DEPTH, D), 0.02)
    norm2_g = 1.0 + nrm((DEPTH, D), 0.02)
    w_mod = nrm((DEPTH, D, 6 * D), 0.5 * D ** -0.5)
    b_mod = nrm((DEPTH, 6 * D), 0.02)
    w_in = nrm((DEPTH, D, IN_W), D ** -0.5)
    conv_w = nrm((DEPTH, CONV_K, CONV_K, CONV_CH), 1.0 / CONV_K)
    conv_b = nrm((DEPTH, CONV_CH), 0.02)
    four_g = 1.0 + nrm((DEPTH, FOUR_W), 0.02)
    hy_w1 = nrm((DEPTH, HY_EMB, HY_FILTER_W), HY_EMB ** -0.5)
    hy_b1 = nrm((DEPTH, HY_FILTER_W), 0.1)
    hy_w2 = nrm((DEPTH, HY_FILTER_W, HY_FILTER_W), HY_FILTER_W ** -0.5)
    hy_b2 = nrm((DEPTH, HY_FILTER_W), 0.1)
    hy_w3 = nrm((DEPTH, HY_FILTER_W, HY_FILTER_W), HY_FILTER_W ** -0.5)
    hy_b3 = nrm((DEPTH, HY_FILTER_W), 0.1)
    hy_w4 = nrm((DEPTH, HY_FILTER_W, 2 * HY_ORDER * HY_W), HY_FILTER_W ** -0.5)
    hy_freq = 1.0 + nrm((DEPTH, HY_FILTER_W), 0.02)
    hy_bias = nrm((DEPTH, HY_ORDER, HY_W), 0.1)
    hy_g = 1.0 + nrm((DEPTH, HY_W), 0.02)
    dn_a_log = jnp.log(jax.random.uniform(next(ks), (DEPTH, 2, DN_HEADS), F32, 1.0, 16.0))
    dt0 = jnp.exp(jax.random.uniform(next(ks), (DEPTH, 2, DN_HEADS), F32, math.log(1e-3), math.log(1e-1)))
    dn_dt_bias = dt0 + jnp.log(-jnp.expm1(-dt0))
    dn_g = 1.0 + nrm((DEPTH, DN_DV), 0.02)
    w_out = nrm((DEPTH, MIX_W, D), MIX_W ** -0.5)
    w_rc = nrm((DEPTH, D, N_GROUPS), D ** -0.5)
    b_rc = nrm((DEPTH, N_GROUPS), 0.01)
    w_rf = nrm((DEPTH, D, N_EXPERTS), D ** -0.5)
    b_rf = nrm((DEPTH, N_EXPERTS), 0.01)
    w_e_gate = nrm((DEPTH, N_EXPERTS, D, EXPERT_FF), D ** -0.5)
    w_e_up = nrm((DEPTH, N_EXPERTS, D, EXPERT_FF), D ** -0.5)
    w_e_down = nrm((DEPTH, N_EXPERTS, EXPERT_FF, D), EXPERT_FF ** -0.5)
    final_g = 1.0 + nrm((D,), 0.02)
    return {'x': x, 'c': c, 'ctx': ctx, 'c_ctx': c_ctx, 'norm1_g': norm1_g, 'norm2_g': norm2_g,
            'w_mod': w_mod, 'b_mod': b_mod, 'w_in': w_in, 'conv_w': conv_w, 'conv_b': conv_b,
            'four_g': four_g, 'hy_w1': hy_w1, 'hy_b1': hy_b1, 'hy_w2': hy_w2, 'hy_b2': hy_b2,
            'hy_w3': hy_w3, 'hy_b3': hy_b3, 'hy_w4': hy_w4, 'hy_freq': hy_freq, 'hy_bias': hy_bias,
            'hy_g': hy_g, 'dn_a_log': dn_a_log, 'dn_dt_bias': dn_dt_bias, 'dn_g': dn_g, 'w_out': w_out,
            'w_rc': w_rc, 'b_rc': b_rc, 'w_rf': w_rf, 'b_rf': b_rf, 'w_e_gate': w_e_gate,
            'w_e_up': w_e_up, 'w_e_down': w_e_down, 'final_g': final_g}


def reference(x, c, ctx, c_ctx, norm1_g, norm2_g, w_mod, b_mod, w_in, conv_w, conv_b, four_g,
              hy_w1, hy_b1, hy_w2, hy_b2, hy_w3, hy_b3, hy_w4, hy_freq, hy_bias, hy_g,
              dn_a_log, dn_dt_bias, dn_g, w_out, w_rc, b_rc, w_rf, b_rf, w_e_gate, w_e_up, w_e_down,
              final_g):
    bsz, L, D = x.shape
    rows = L // GRID_W
    Lc = ctx.shape[1]
    sc = jax.nn.silu(c)
    scc = jax.nn.silu(c_ctx)
    zeros = jnp.zeros((bsz, DN_HEADS, DN_DK, DN_DV), F32)
    for l in range(DEPTH):
        last = l == DEPTH - 1
        mod = [m[:, None, :] for m in jnp.split(sc @ w_mod[l] + b_mod[l], 6, axis=-1)]
        modc = jnp.split(scc @ w_mod[l] + b_mod[l], 6, axis=-1)
        hy_params = (hy_w1[l], hy_b1[l], hy_w2[l], hy_b2[l], hy_w3[l], hy_b3[l], hy_w4[l], hy_freq[l])
        mix_args = (conv_w[l], conv_b[l], four_g[l], hy_params, hy_bias[l], hy_g[l],
                    dn_a_log[l], dn_dt_bias[l], dn_g[l])
        moe_args = (w_rc[l], b_rc[l], w_rf[l], b_rf[l], w_e_gate[l], w_e_up[l], w_e_down[l])

        hc = modulate(rmsnorm(ctx, norm1_g[l]), modc[0], modc[1])
        if last:
            p_dn = hc @ w_in[l][:, OFF_DN:OFF_Z]
            qkv_pre = short_conv(p_dn[..., :3 * DN_W], conv_w[l][..., HY_CONV_CH:], conv_b[l][HY_CONV_CH:], 1, Lc)
            q, k, v, g, beta = dn_heads(qkv_pre, p_dn[..., 3 * DN_W:], dn_a_log[l], dn_dt_bias[l])
            _, s_f, s_b = dn_bidirectional(q, k, v, g, beta, zeros, zeros, False)
        else:
            yc, s_f, s_b = mix_stream(hc @ w_in[l], 1, Lc, zeros, zeros, *mix_args)
            ctx = ctx + modc[2] * (yc @ w_out[l])

        hx = modulate(rmsnorm(x, norm1_g[l]), mod[0], mod[1])
        y, _, _ = mix_stream(hx @ w_in[l], rows, GRID_W, s_f, s_b, *mix_args)
        x = x + mod[2] * (y @ w_out[l])

        hx2 = modulate(rmsnorm(x, norm2_g[l]), mod[3], mod[4])
        if last:
            x = x + mod[5] * hier_moe(hx2.reshape(-1, D), *moe_args).reshape(bsz, L, D)
        else:
            hc2 = modulate(rmsnorm(ctx, norm2_g[l]), modc[3], modc[4])
            f = hier_moe(jnp.concatenate([hx2.reshape(-1, D), hc2.reshape(-1, D)], axis=0), *moe_args)
            x = x + mod[5] * f[:bsz * L].reshape(bsz, L, D)
            ctx = ctx + modc[5] * f[bsz * L:].reshape(bsz, Lc, D)
    return rmsnorm(x, final_g)
```

```python
import functools
import math

import jax
import jax.numpy as jnp
from jax import lax
from jax.experimental import pallas as pl
from jax.experimental.pallas import tpu as pltpu

F32 = jnp.float32
BF16 = jnp.bfloat16
HIGHEST = lax.Precision.HIGHEST

GRID_W = 64
FOUR_W = 512
FOUR_GROUPS = 4
FOUR_GW = FOUR_W // FOUR_GROUPS
HY_W = 512
HY_ORDER = 2
DN_HEADS = 8
DN_DK = 128
DN_DV = 128
DN_W = DN_HEADS * DN_DV
DN_CHUNK = 64
HY_CONV_CH = (HY_ORDER + 1) * HY_W
OFF_HY = FOUR_W
OFF_DN = OFF_HY + HY_CONV_CH
OFF_AB = OFF_DN + 3 * DN_W
N_AB = 4 * DN_HEADS
OFF_Z = OFF_AB + N_AB
HY_EMB = 33
HY_FAST_DECAY = 0.3
HY_SLOW_DECAY = 1.5
HY_TARGET = 1e-2
N_GROUPS = 4
EXPERTS_PER_GROUP = 8
N_EXPERTS = N_GROUPS * EXPERTS_PER_GROUP
TOP_K = 2
EPS = 1e-6

LANE = 128
PCOL_Z = OFF_AB
PCOL_AB = PCOL_Z + DN_W
PROJ_W = PCOL_AB + LANE
ROUTER_W = LANE
VMEM_LIMIT = 56 * 1024 * 1024


def _params(semantics):
    return pltpu.CompilerParams(dimension_semantics=semantics, vmem_limit_bytes=VMEM_LIMIT)


def _silu(v):
    return v * jax.nn.sigmoid(v)


def _rms(v, g):
    return v * lax.rsqrt(jnp.mean(v * v, axis=-1, keepdims=True) + EPS) * g


def _dot(a, b):
    return jnp.dot(a, b, preferred_element_type=F32)


def _mod_kernel(a_ref, w_ref, b_ref, o_ref):
    o_ref[0] = jnp.dot(_silu(a_ref[...]), w_ref[0], precision=HIGHEST,
                       preferred_element_type=F32) + b_ref[0]


def _modulation(c, c_ctx, w_mod, b_mod):
    depth, d, n = w_mod.shape
    bsz = c.shape[0]
    rows = -(-(bsz + 1) // 8) * 8
    a = jnp.concatenate([c, c_ctx[None], jnp.zeros((rows - bsz - 1, d), F32)], axis=0)
    tn = 1024 if n % 1024 == 0 else 512
    assert n % tn == 0
    return pl.pallas_call(
        _mod_kernel, grid=(depth, n // tn),
        in_specs=[pl.BlockSpec((rows, d), lambda l, j: (0, 0)),
                  pl.BlockSpec((1, d, tn), lambda l, j: (l, 0, j)),
                  pl.BlockSpec((1, 1, tn), lambda l, j: (l, 0, j))],
        out_specs=pl.BlockSpec((1, rows, tn), lambda l, j: (l, 0, j)),
        out_shape=jax.ShapeDtypeStruct((depth, rows, n), F32),
        compiler_params=_params(("parallel", "parallel")), name="modulation",
    )(a, w_mod, b_mod.reshape(depth, 1, n))


def _proj_in_kernel(x_ref, g_ref, shift_ref, scale_ref, w_ref, o_ref, a_scr):
    @pl.when(pl.program_id(1) == 0)
    def _():
        y = _rms(x_ref[...], g_ref[...])
        a_scr[...] = (y * (1.0 + scale_ref[0]) + shift_ref[0]).astype(BF16)

    o_ref[...] = _dot(a_scr[...], w_ref[...])


def _proj_in(x2d, g, shift, scale, w, rows_per_mod):
    m, d = x2d.shape
    n = w.shape[1]
    tm = min(512, m)
    tn = 896 if n % 896 == 0 else n
    per = rows_per_mod // tm
    return pl.pallas_call(
        _proj_in_kernel, grid=(m // tm, n // tn),
        in_specs=[pl.BlockSpec((tm, d), lambda i, j: (i, 0)),
                  pl.BlockSpec((1, d), lambda i, j: (0, 0)),
                  pl.BlockSpec((1, 1, d), lambda i, j: (i // per, 0, 0)),
                  pl.BlockSpec((1, 1, d), lambda i, j: (i // per, 0, 0)),
                  pl.BlockSpec((d, tn), lambda i, j: (0, j))],
        out_specs=pl.BlockSpec((tm, tn), lambda i, j: (i, j)),
        out_shape=jax.ShapeDtypeStruct((m, n), F32),
        scratch_shapes=[pltpu.VMEM((tm, d), BF16)],
        compiler_params=_params(("parallel", "arbitrary")), name="proj_in",
    )(x2d, g.reshape(1, d), shift, scale, w)


def _cos_sin_table(nrow, ncol, period):
    i = lax.broadcasted_iota(jnp.int32, (nrow, ncol), 0)
    j = lax.broadcasted_iota(jnp.int32, (nrow, ncol), 1)
    ang = ((i * j) % period).astype(F32) * (2.0 * math.pi / period)
    return jnp.cos(ang), jnp.sin(ang)


def _dft_tables(length, period):
    blk = 64 if length % 64 == 0 else length
    hi_i = lax.broadcasted_iota(jnp.int32, (length // blk, length), 0) * blk
    j = lax.broadcasted_iota(jnp.int32, (length // blk, length), 1)
    ang = ((hi_i * j) % period).astype(F32) * (2.0 * math.pi / period)
    hi_c, hi_s = jnp.cos(ang), jnp.sin(ang)
    lo_c, lo_s = _cos_sin_table(blk, length, period)
    c = hi_c[:, None, :] * lo_c[None, :, :] - hi_s[:, None, :] * lo_s[None, :, :]
    s = hi_s[:, None, :] * lo_c[None, :, :] + hi_c[:, None, :] * lo_s[None, :, :]
    return c.reshape(length, length), s.reshape(length, length)


def _fourier_kernel(u_ref, c_ref, s_ref, bc_ref, bs_ref, g_ref, o_ref, ub_scr, *, scale):
    @pl.when(pl.program_id(1) == 0)
    def _():
        ub_scr[...] = u_ref[0].astype(BF16)

    ub = ub_scr[...]
    p = _dot(c_ref[...], ub).astype(BF16)
    q = _dot(s_ref[...], ub).astype(BF16)
    z = (_dot(p, bc_ref[...]) - _dot(q, bs_ref[...])) * scale
    o_ref[0] = _rms(z, g_ref[...])


def _fourier(p3, cl, sl, bc, bs, four_g):
    bsz, length, _ = p3.shape
    tm = min(512, length)
    kern = functools.partial(_fourier_kernel, scale=1.0 / math.sqrt(length * FOUR_GW))
    return pl.pallas_call(
        kern, grid=(bsz, length // tm),
        in_specs=[pl.BlockSpec((1, length, FOUR_W), lambda b, i: (b, 0, 0)),
                  pl.BlockSpec((tm, length), lambda b, i: (i, 0)),
                  pl.BlockSpec((tm, length), lambda b, i: (i, 0)),
                  pl.BlockSpec((FOUR_W, FOUR_W), lambda b, i: (0, 0)),
                  pl.BlockSpec((FOUR_W, FOUR_W), lambda b, i: (0, 0)),
                  pl.BlockSpec((1, FOUR_W), lambda b, i: (0, 0))],
        out_specs=pl.BlockSpec((1, tm, FOUR_W), lambda b, i: (b, i, 0)),
        out_shape=jax.ShapeDtypeStruct((bsz, length, FOUR_W), F32),
        scratch_shapes=[pltpu.VMEM((length, FOUR_W), BF16)],
        compiler_params=_params(("parallel", "arbitrary")), name="fourier",
    )(p3, cl, sl, bc, bs, four_g.reshape(1, FOUR_W))


def _fourier_tables(length):
    cl, sl = _dft_tables(length, length)
    cc, sc = _cos_sin_table(FOUR_GW, FOUR_GW, FOUR_GW)
    eye = jnp.eye(FOUR_GROUPS, dtype=F32)
    return cl.astype(BF16), sl.astype(BF16), jnp.kron(eye, cc).astype(BF16), jnp.kron(eye, sc).astype(BF16)


def _conv_kernel(x_ref, w_ref, b_ref, o_ref, *, rows, width, heads_mode):
    tc = x_ref.shape[2]
    wts = w_ref[...]
    bias = b_ref[...]
    pos = lax.broadcasted_iota(jnp.int32, (width, tc), 0)
    not_first = pos > 0
    not_last = pos < width - 1
    kind = pl.program_id(1) // DN_HEADS

    def one_row(src, dr):
        left = jnp.where(not_first, pltpu.roll(src, 1, 0), 0.0)
        right = jnp.where(not_last, pltpu.roll(src, width - 1, 0), 0.0)
        return left * wts[dr, 0:1, :] + src * wts[dr, 1:2, :] + right * wts[dr, 2:3, :]

    def body(r, carry):
        base = pl.multiple_of(r * width, width)
        acc = one_row(x_ref[0, pl.ds(base, width), :], 1) + bias
        if rows > 1:
            up = pl.multiple_of(jnp.maximum(r - 1, 0) * width, width)
            dn = pl.multiple_of(jnp.minimum(r + 1, rows - 1) * width, width)
            acc = acc + jnp.where(r > 0, 1.0, 0.0) * one_row(x_ref[0, pl.ds(up, width), :], 0)
            acc = acc + jnp.where(r < rows - 1, 1.0, 0.0) * one_row(x_ref[0, pl.ds(dn, width), :], 2)
        if heads_mode:
            s = _silu(acc)
            inv = lax.rsqrt(jnp.sum(s * s, axis=-1, keepdims=True) + EPS)
            fac = jnp.where(kind == 0, inv * (DN_DK ** -0.5), jnp.where(kind == 1, inv, 1.0))
            acc = s * fac
        o_ref[0, pl.ds(base, width), :] = acc
        return carry

    lax.fori_loop(0, rows, body, 0)


def _short_conv(p3, conv_w, conv_b, rows, width, col0, ch0, nch, tc, heads_mode):
    bsz, length, _ = p3.shape
    kern = functools.partial(_conv_kernel, rows=rows, width=width, heads_mode=heads_mode)
    cb, wb = col0 // tc, ch0 // tc
    return pl.pallas_call(
        kern, grid=(bsz, nch // tc),
        in_specs=[pl.BlockSpec((1, length, tc), lambda b, j: (b, 0, cb + j)),
                  pl.BlockSpec((3, 3, tc), lambda b, j: (0, 0, wb + j)),
                  pl.BlockSpec((1, tc), lambda b, j: (0, wb + j))],
        out_specs=pl.BlockSpec((1, length, tc), lambda b, j: (b, 0, j)),
        out_shape=jax.ShapeDtypeStruct((bsz, length, nch), F32),
        compiler_params=_params(("parallel", "parallel")),
        name="conv_heads" if heads_mode else "conv_hyena",
    )(p3, conv_w, conv_b.reshape(1, -1))


def _filt_kernel(z_ref, w1_ref, b1_ref, w2_ref, b2_ref, w3_ref, b3_ref, w4_ref, fr_ref, dl_ref,
                 g_ref, nrm_ref):
    i = pl.program_id(0)
    z = z_ref[...]
    fr = fr_ref[...]
    hdot = lambda a, b: jnp.dot(a, b, precision=HIGHEST, preferred_element_type=F32)
    h = jnp.sin(fr * (hdot(z, w1_ref[...]) + b1_ref[...]))
    h = jnp.sin(fr * (hdot(h, w2_ref[...]) + b2_ref[...]))
    h = jnp.sin(fr * (hdot(h, w3_ref[...]) + b3_ref[...]))
    h = hdot(h, w4_ref[...])
    decay = jnp.exp(-z[:, 0:1] * jnp.abs(dl_ref[...]))
    decay = jnp.concatenate([decay] * HY_ORDER, axis=1)
    half = HY_ORDER * HY_W
    hf = h[:, :half] * decay
    hb = h[:, half:] * decay
    row = lax.broadcasted_iota(jnp.int32, hb.shape, 0) + i * hb.shape[0]
    hb = jnp.where(row > 0, hb, 0.0)
    gp = hf + hb
    gm = hf - hb
    for o in range(HY_ORDER):
        g_ref[o] = gp[:, o * HY_W:(o + 1) * HY_W].astype(BF16)
        g_ref[HY_ORDER + o] = gm[:, o * HY_W:(o + 1) * HY_W].astype(BF16)
    part = jnp.sum(jnp.abs(hf) + jnp.abs(hb), axis=0, keepdims=True)

    @pl.when(i == 0)
    def _():
        nrm_ref[...] = part

    @pl.when(i > 0)
    def _():
        nrm_ref[...] += part


def _hyena_filter_taps(length, w1, b1, w2, b2, w3, b3, w4, freq):
    t = jnp.linspace(0.0, 1.0, length, dtype=F32)[:, None]
    bands = (HY_EMB - 1) // 2
    ang = (2.0 * math.pi / length) * jnp.arange(length, dtype=F32)[:, None]
    f = jnp.linspace(1e-4, bands - 1, bands, dtype=F32)
    z = jnp.concatenate([t, jnp.cos(f * ang), -jnp.sin(f * ang)], axis=-1)
    max_decay = math.log(HY_TARGET) / HY_FAST_DECAY
    min_decay = math.log(HY_TARGET) / HY_SLOW_DECAY
    deltas = jnp.linspace(min_decay, max_decay, HY_W, dtype=F32)[None, :]
    fw = w1.shape[1]
    tl = min(512, length)
    full = lambda shape: pl.BlockSpec(shape, lambda i: (0,) * len(shape))
    return pl.pallas_call(
        _filt_kernel, grid=(length // tl,),
        in_specs=[pl.BlockSpec((tl, HY_EMB), lambda i: (i, 0)),
                  full((HY_EMB, fw)), full((1, fw)), full((fw, fw)), full((1, fw)),
                  full((fw, fw)), full((1, fw)), full((fw, 2 * HY_ORDER * HY_W)), full((1, fw)),
                  full((1, HY_W))],
        out_specs=[pl.BlockSpec((2 * HY_ORDER, tl, HY_W), lambda i: (0, i, 0)),
                   pl.BlockSpec((1, HY_ORDER * HY_W), lambda i: (0, 0))],
        out_shape=[jax.ShapeDtypeStruct((2 * HY_ORDER, length, HY_W), BF16),
                   jax.ShapeDtypeStruct((1, HY_ORDER * HY_W), F32)],
        compiler_params=_params(("arbitrary",)), name="hyena_filter",
    )(z, w1, b1.reshape(1, fw), w2, b2.reshape(1, fw), w3, b3.reshape(1, fw), w4,
      freq.reshape(1, fw), deltas)


def _dft_fwd_kernel(u_ref, c_ref, s_ref, *rest, with_coef):
    if with_coef:
        c1_ref, c2_ref, c4_ref, yr_ref, yi_ref, ub_scr = rest
    else:
        yr_ref, yi_ref, ub_scr = rest

    @pl.when(pl.program_id(1) == 0)
    def _():
        ub_scr[...] = u_ref[0].astype(BF16)

    ub = ub_scr[...]
    a = _dot(c_ref[...], ub)
    b = _dot(s_ref[...], ub)
    if with_coef:
        c2 = c2_ref[...]
        yr_ref[0] = (a * c1_ref[...] + b * c2).astype(BF16)
        yi_ref[0] = (b * c4_ref[...] - a * c2).astype(BF16)
    else:
        yr_ref[0] = a
        yi_ref[0] = b


def _dft_fwd(u3, col_blk, cf, sf, coef=None):
    nb, length, _ = u3.shape
    tm = min(512, length)
    in_specs = [pl.BlockSpec((1, length, HY_W), lambda b, i: (b, 0, col_blk)),
                pl.BlockSpec((tm, length), lambda b, i: (i, 0)),
                pl.BlockSpec((tm, length), lambda b, i: (i, 0))]
    args = [u3, cf, sf]
    if coef is not None:
        in_specs += [pl.BlockSpec((tm, HY_W), lambda b, i: (i, 0))] * 3
        args += list(coef)
    odt = BF16 if coef is not None else F32
    return pl.pallas_call(
        functools.partial(_dft_fwd_kernel, with_coef=coef is not None), grid=(nb, length // tm),
        in_specs=in_specs,
        out_specs=[pl.BlockSpec((1, tm, HY_W), lambda b, i: (b, i, 0))] * 2,
        out_shape=[jax.ShapeDtypeStruct((nb, length, HY_W), odt)] * 2,
        scratch_shapes=[pltpu.VMEM((length, HY_W), BF16)],
        compiler_params=_params(("parallel", "arbitrary")),
        name="hyena_dft_fwd" if coef is not None else "hyena_filter_dft",
    )(*args)


def _dft_inv_kernel(yr_ref, yi_ref, c_ref, s_ref, u_ref, gate_ref, bias_ref, g_ref, o_ref, *, final):
    y = _dot(c_ref[...], yr_ref[0]) + _dot(s_ref[...], yi_ref[0])
    out = gate_ref[0] * (y + bias_ref[...] * u_ref[0])
    if final:
        out = _rms(out, g_ref[...])
    o_ref[0] = out


def _dft_inv(yr, yi, cf, s_inv, u3, u_blk, gate3, gate_blk, bias, g, final):
    bsz, length, _ = yr.shape
    tm = min(512, length)
    return pl.pallas_call(
        functools.partial(_dft_inv_kernel, final=final), grid=(bsz, length // tm),
        in_specs=[pl.BlockSpec((1, length, HY_W), lambda b, i: (b, 0, 0)),
                  pl.BlockSpec((1, length, HY_W), lambda b, i: (b, 0, 0)),
                  pl.BlockSpec((tm, length), lambda b, i: (i, 0)),
                  pl.BlockSpec((tm, length), lambda b, i: (i, 0)),
                  pl.BlockSpec((1, tm, HY_W), lambda b, i: (b, i, u_blk)),
                  pl.BlockSpec((1, tm, HY_W), lambda b, i: (b, i, gate_blk)),
                  pl.BlockSpec((1, HY_W), lambda b, i: (0, 0)),
                  pl.BlockSpec((1, HY_W), lambda b, i: (0, 0))],
        out_specs=pl.BlockSpec((1, tm, HY_W), lambda b, i: (b, i, 0)),
        out_shape=jax.ShapeDtypeStruct((bsz, length, HY_W), F32),
        compiler_params=_params(("parallel", "arbitrary")), name="hyena_dft_inv",
    )(yr, yi, cf, s_inv, u3, gate3, bias.reshape(1, HY_W), g.reshape(1, HY_W))


def _hyena_tables(length):
    cf, sf = _dft_tables(length, 2 * length)
    alt = jnp.where(jnp.arange(length) % 2 == 0, 1.0, -1.0).astype(F32)
    sf = sf.at[0, :].set(alt)
    return cf.astype(BF16), sf.astype(BF16), sf.T.astype(BF16)


def _hyena_coefs(length, tables, hy_params):
    cf, sf, _ = tables
    taps, nrm = _hyena_filter_taps(length, *hy_params)
    a, b = _dft_fwd(taps, 0, cf, sf)
    inv = (1.0 / nrm).reshape(HY_ORDER, 1, HY_W)
    n = 2.0 * length
    hr = a[:HY_ORDER] * inv
    hi = -b[HY_ORDER:] * inv
    nyq = b[:HY_ORDER, 0:1, :] * inv
    first = (jnp.arange(length) == 0)[None, :, None]
    c1 = jnp.where(first, hr / n, hr * (2.0 / n))
    c2 = jnp.where(first, 0.0, hi * (2.0 / n))
    c4 = jnp.where(first, nyq / n, hr * (2.0 / n))
    return c1, c2, c4


def _hyena(convh, tables, coefs, hy_bias, hy_g):
    cf, sf, s_inv = tables
    c1, c2, c4 = coefs
    yr, yi = _dft_fwd(convh, 0, cf, sf, (c1[0], c2[0], c4[0]))
    zz = _dft_inv(yr, yi, cf, s_inv, convh, 0, convh, 1, hy_bias[0], hy_g, False)
    yr, yi = _dft_fwd(zz, 0, cf, sf, (c1[1], c2[1], c4[1]))
    return _dft_inv(yr, yi, cf, s_inv, zz, 0, convh, 2, hy_bias[1], hy_g, True)


def _softplus(v):
    return jnp.maximum(v, 0.0) + jnp.log(1.0 + jnp.exp(-jnp.abs(v)))


def _delta_unit(q, k, v, gcol, grow, glast, beta, s, upper):
    c = q.shape[0]
    ri = lax.broadcasted_iota(jnp.int32, (c, c), 0)
    ci = lax.broadcasted_iota(jnp.int32, (c, c), 1)
    incl = (ri <= ci) if upper else (ri >= ci)
    strict = (ri < ci) if upper else (ri > ci)
    decay = jnp.where(incl, jnp.exp(jnp.where(incl, gcol - grow, 0.0)), 0.0)
    nt = (((1,), (1,)), ((), ()))
    kb = k * beta
    k16 = k.astype(BF16)
    kk = lax.dot_general(kb.astype(BF16), k16, nt, preferred_element_type=F32)
    nj = jnp.where(strict, -(kk * decay), 0.0)
    pinv = jnp.where(ri == ci, 1.0, 0.0) + nj
    steps = max(1, (c - 1).bit_length()) - 1
    for _ in range(steps):
        nj16 = nj.astype(BF16)
        nj = _dot(nj16, nj16)
        pinv = pinv + _dot(pinv.astype(BF16), nj.astype(BF16))
    eg = jnp.exp(gcol)
    rhs = jnp.concatenate([v * beta, kb * eg], axis=1).astype(BF16)
    uw = _dot(pinv.astype(BF16), rhs)
    dv = v.shape[1]
    s16 = s.astype(BF16)
    v_new = uw[:, :dv] - _dot(uw[:, dv:].astype(BF16), s16)
    v16 = v_new.astype(BF16)
    q16 = q.astype(BF16)
    qk = lax.dot_general(q16, k16, nt, preferred_element_type=F32)
    attn = jnp.where(incl, qk * decay, 0.0)
    o = _dot((q * eg).astype(BF16), s16) + _dot(attn.astype(BF16), v16)
    k_tail = (k * jnp.exp(glast - gcol)).astype(BF16)
    tn = (((0,), (0,)), ((), ()))
    s_next = s * jnp.exp(glast) + lax.dot_general(k_tail, v16, tn, preferred_element_type=F32)
    return o, s_next


def _delta_kernel(qf_ref, kf_ref, vf_ref, qb_ref, kb_ref, vb_ref, abf_ref, abb_ref, abtf_ref, abtb_ref,
                  arow_ref, drow_ref, acol_ref, dcol_ref, s0f_ref, s0b_ref,
                  of_ref, ob_ref, sf_ref, sb_ref, s_scr, *, hpb):
    g = pl.program_id(1)
    c_idx = pl.program_id(2)
    n_chunks = pl.num_programs(2)
    c = abf_ref.shape[1]

    @pl.when(c_idx == 0)
    def _():
        s_scr[0] = s0f_ref[0]
        s_scr[1] = s0b_ref[0]

    ri = lax.broadcasted_iota(jnp.int32, (c, c), 0)
    ci = lax.broadcasted_iota(jnp.int32, (c, c), 1)
    low = jnp.where(ri >= ci, 1.0, 0.0)
    upp = jnp.where(ri <= ci, 1.0, 0.0)
    hdot = lambda a, b: jnp.dot(a, b, precision=HIGHEST, preferred_element_type=F32)
    nh = DN_HEADS
    for d, (q_ref, k_ref, v_ref, ab_ref, abt_ref, o_ref) in enumerate(
            ((qf_ref, kf_ref, vf_ref, abf_ref, abtf_ref, of_ref),
             (qb_ref, kb_ref, vb_ref, abb_ref, abtb_ref, ob_ref))):
        upper = d == 1
        ab = ab_ref[0]
        abt = abt_ref[0, 0]
        gate = -arow_ref[...] * _softplus(ab + drow_ref[...])
        gate_t = -acol_ref[...] * _softplus(abt + dcol_ref[...])
        beta_all = jax.nn.sigmoid(ab)
        gcum = hdot(upp if upper else low, gate)
        gcum_t = hdot(gate_t, low if upper else upp)
        last = 0 if upper else c - 1
        for hl in range(hpb):
            lanes = slice(hl * DN_DK, (hl + 1) * DN_DK)
            sel_col = (lax.broadcasted_iota(jnp.int32, (1, 4 * nh), 1) == (d * nh + g * hpb + hl))
            sel_beta = (lax.broadcasted_iota(jnp.int32, (1, 4 * nh), 1) == ((2 + d) * nh + g * hpb + hl))
            sel_row = (lax.broadcasted_iota(jnp.int32, (4 * nh, 1), 0) == (d * nh + g * hpb + hl))
            gcol = jnp.sum(jnp.where(sel_col, gcum, 0.0), axis=1, keepdims=True)
            beta = jnp.sum(jnp.where(sel_beta, beta_all, 0.0), axis=1, keepdims=True)
            grow = jnp.sum(jnp.where(sel_row, gcum_t, 0.0), axis=0, keepdims=True)
            glast = gcol[last:last + 1, :]
            o, s_next = _delta_unit(q_ref[0, :, lanes], k_ref[0, :, lanes], v_ref[0, :, lanes],
                                    gcol, grow, glast, beta, s_scr[d, hl], upper)
            o_ref[0, :, lanes] = o
            s_scr[d, hl] = s_next

    @pl.when(c_idx == n_chunks - 1)
    def _():
        sf_ref[0] = s_scr[0]
        sb_ref[0] = s_scr[1]


def _delta_rule(qkv, ab, a_log, dt_bias, s0_f, s0_b, hpb=4):
    bsz, length, _ = qkv.shape
    c = DN_CHUNK
    n = length // c
    nh = DN_HEADS
    ng = nh // hpb
    abt = jnp.swapaxes(ab.reshape(bsz, n, c, 4 * nh), 2, 3)
    zeros = jnp.zeros((2 * nh,), F32)
    a_vec = jnp.concatenate([jnp.exp(a_log.astype(F32)).reshape(-1), zeros])
    d_vec = jnp.concatenate([dt_bias.astype(F32).reshape(-1), zeros])
    wq = hpb * DN_DK
    blk = lambda col0, rev: pl.BlockSpec(
        (1, c, wq), (lambda b, g, i: (b, n - 1 - i, col0 + g)) if rev else (lambda b, g, i: (b, i, col0 + g)))
    ab_spec = lambda rev: pl.BlockSpec(
        (1, c, 4 * nh), (lambda b, g, i: (b, n - 1 - i, 0)) if rev else (lambda b, g, i: (b, i, 0)))
    abt_spec = lambda rev: pl.BlockSpec(
        (1, 1, 4 * nh, c), (lambda b, g, i: (b, n - 1 - i, 0, 0)) if rev else (lambda b, g, i: (b, i, 0, 0)))
    small = lambda shape: pl.BlockSpec(shape, lambda b, g, i: (0,) * len(shape))
    st_spec = pl.BlockSpec((1, hpb, DN_DK, DN_DV), lambda b, g, i: (b, g, 0, 0))
    o_shape = jax.ShapeDtypeStruct((bsz, length, DN_W), F32)
    s_shape = jax.ShapeDtypeStruct((bsz, nh, DN_DK, DN_DV), F32)
    return pl.pallas_call(
        functools.partial(_delta_kernel, hpb=hpb), grid=(bsz, ng, n),
        in_specs=[blk(0, False), blk(ng, False), blk(2 * ng, False),
                  blk(0, True), blk(ng, True), blk(2 * ng, True),
                  ab_spec(False), ab_spec(True), abt_spec(False), abt_spec(True),
                  small((1, 4 * nh)), small((1, 4 * nh)), small((4 * nh, 1)), small((4 * nh, 1)),
                  st_spec, st_spec],
        out_specs=[blk(0, False), blk(0, True), st_spec, st_spec],
        out_shape=[o_shape, o_shape, s_shape, s_shape],
        scratch_shapes=[pltpu.VMEM((2, hpb, DN_DK, DN_DV), F32)],
        compiler_params=_params(("parallel", "parallel", "arbitrary")), name="delta_rule",
    )(qkv, qkv, qkv, qkv, qkv, qkv, ab, ab, abt, abt,
      a_vec.reshape(1, -1), d_vec.reshape(1, -1), a_vec.reshape(-1, 1), d_vec.reshape(-1, 1), s0_f, s0_b)


def _proj_out_kernel(yf_ref, yh_ref, of_ref, ob_ref, z_ref, dg_ref, w_ref, res_ref, gate_ref, o_ref, a_scr):
    @pl.when(pl.program_id(1) == 0)
    def _():
        a_scr[:, 0:FOUR_W] = yf_ref[...].astype(BF16)
        a_scr[:, FOUR_W:FOUR_W + HY_W] = yh_ref[...].astype(BF16)
        o = of_ref[...] + ob_ref[...]
        z = z_ref[...]
        for h in range(DN_HEADS):
            lanes = slice(h * DN_DV, (h + 1) * DN_DV)
            y = _rms(o[:, lanes], dg_ref[...]) * _silu(z[:, lanes])
            a_scr[:, FOUR_W + HY_W + h * DN_DV:FOUR_W + HY_W + (h + 1) * DN_DV] = y.astype(BF16)

    o_ref[...] = res_ref[...] + gate_ref[0] * _dot(a_scr[...], w_ref[...])


def _proj_out(y_four, y_hy, o_f, o_b, p2d, dn_g, w, res, gate, rows_per_mod):
    m, d = res.shape
    tm = min(512, m)
    tn = min(1024, d)
    per = rows_per_mod // tm
    zb = PCOL_Z // DN_W
    return pl.pallas_call(
        _proj_out_kernel, grid=(m // tm, d // tn),
        in_specs=[pl.BlockSpec((tm, FOUR_W), lambda i, j: (i, 0)),
                  pl.BlockSpec((tm, HY_W), lambda i, j: (i, 0)),
                  pl.BlockSpec((tm, DN_W), lambda i, j: (i, 0)),
                  pl.BlockSpec((tm, DN_W), lambda i, j: (i, 0)),
                  pl.BlockSpec((tm, DN_W), lambda i, j: (i, zb)),
                  pl.BlockSpec((1, DN_DV), lambda i, j: (0, 0)),
                  pl.BlockSpec((w.shape[0], tn), lambda i, j: (0, j)),
                  pl.BlockSpec((tm, tn), lambda i, j: (i, j)),
                  pl.BlockSpec((1, 1, tn), lambda i, j: (i // per, 0, j))],
        out_specs=pl.BlockSpec((tm, tn), lambda i, j: (i, j)),
        out_shape=jax.ShapeDtypeStruct((m, d), F32),
        scratch_shapes=[pltpu.VMEM((tm, w.shape[0]), BF16)],
        compiler_params=_params(("parallel", "arbitrary")), name="proj_out",
    )(y_four, y_hy, o_f, o_b, p2d, dn_g.reshape(1, DN_DV), w, res, gate)


def _router_kernel(x_ref, c_ref, g_ref, sx_ref, cx_ref, sc_ref, cc_ref, w_ref, b_ref, h_ref, lg_ref, *, n_x):
    i = pl.program_id(0)

    def emit(v, shift, scale):
        h = _rms(v, g_ref[...]) * (1.0 + scale) + shift
        h_ref[...] = h
        lg_ref[...] = jnp.dot(h, w_ref[...], precision=HIGHEST, preferred_element_type=F32) + b_ref[...]

    @pl.when(i < n_x)
    def _():
        emit(x_ref[...], sx_ref[0], cx_ref[0])

    @pl.when(i >= n_x)
    def _():
        emit(c_ref[...], sc_ref[0], cc_ref[0])


def _router(x2d, c2d, g, shift_x, scale_x, shift_c, scale_c, w_r, b_r, rows_per_mod, with_ctx):
    mx, d = x2d.shape
    tm = 512
    n_x = mx // tm
    n_c = c2d.shape[0] // tm if with_ctx else 0
    per = rows_per_mod // tm
    xi = lambda i: (jnp.minimum(i, n_x - 1), 0)
    ci = lambda i: (jnp.maximum(i - n_x, 0), 0)
    return pl.pallas_call(
        functools.partial(_router_kernel, n_x=n_x), grid=(n_x + n_c,),
        in_specs=[pl.BlockSpec((tm, d), xi), pl.BlockSpec((tm, d), ci),
                  pl.BlockSpec((1, d), lambda i: (0, 0)),
                  pl.BlockSpec((1, 1, d), lambda i: (jnp.minimum(i, n_x - 1) // per, 0, 0)),
                  pl.BlockSpec((1, 1, d), lambda i: (jnp.minimum(i, n_x - 1) // per, 0, 0)),
                  pl.BlockSpec((1, 1, d), lambda i: (0, 0, 0)),
                  pl.BlockSpec((1, 1, d), lambda i: (0, 0, 0)),
                  pl.BlockSpec((d, ROUTER_W), lambda i: (0, 0)),
                  pl.BlockSpec((1, ROUTER_W), lambda i: (0, 0))],
        out_specs=[pl.BlockSpec((tm, d), lambda i: (i, 0)),
                   pl.BlockSpec((tm, ROUTER_W), lambda i: (i, 0))],
        out_shape=[jax.ShapeDtypeStruct(((n_x + n_c) * tm, d), F32),
                   jax.ShapeDtypeStruct(((n_x + n_c) * tm, ROUTER_W), F32)],
        compiler_params=_params(("parallel",)), name="moe_router",
    )(x2d, c2d, g.reshape(1, d), shift_x, scale_x, shift_c, scale_c, w_r, b_r)


def _row_copy(src_hbm, row, dst, slot, r, sem):
    return pltpu.make_async_copy(src_hbm.at[pl.ds(row, 1)], dst.at[slot, pl.ds(r, 1)], sem.at[slot])


def _expert_kernel(be_ref, tok_ref, nb_ref, h_hbm, wg_ref, wu_ref, wd_ref, ys_ref, xbuf, sem, *, bm):
    i = pl.program_id(0)
    nb = nb_ref[0]

    def start(blk, slot):
        def body(r, carry):
            _row_copy(h_hbm, tok_ref[blk * bm + r], xbuf, slot, r, sem).start()
            return carry
        lax.fori_loop(0, bm, body, 0)

    @pl.when(i == 0)
    def _():
        start(0, 0)

    @pl.when(i + 1 < nb)
    def _():
        start(i + 1, (i + 1) % 2)

    @pl.when(i < nb)
    def _():
        slot = i % 2

        def body(r, carry):
            _row_copy(h_hbm, 0, xbuf, slot, r, sem).wait()
            return carry
        lax.fori_loop(0, bm, body, 0)
        x = xbuf[slot].astype(BF16)
        act = (_silu(_dot(x, wg_ref[0])) * _dot(x, wu_ref[0])).astype(BF16)
        ys_ref[...] = _dot(act, wd_ref[0])

    @pl.when(i >= nb)
    def _():
        ys_ref[...] = jnp.zeros_like(ys_ref)


def _experts(h, buf_tok, block_e, n_used, wg, wu, wd, bm):
    n_blocks = block_e.shape[0]
    d = h.shape[1]
    ff = wg.shape[2]
    grid_spec = pltpu.PrefetchScalarGridSpec(
        num_scalar_prefetch=3, grid=(n_blocks,),
        in_specs=[pl.BlockSpec(memory_space=pl.ANY),
                  pl.BlockSpec((1, d, ff), lambda i, be, tok, nb: (be[i], 0, 0)),
                  pl.BlockSpec((1, d, ff), lambda i, be, tok, nb: (be[i], 0, 0)),
                  pl.BlockSpec((1, ff, d), lambda i, be, tok, nb: (be[i], 0, 0))],
        out_specs=pl.BlockSpec((bm, d), lambda i, be, tok, nb: (i, 0)),
        scratch_shapes=[pltpu.VMEM((2, bm, d), F32), pltpu.SemaphoreType.DMA((2,))])
    return pl.pallas_call(
        functools.partial(_expert_kernel, bm=bm), grid_spec=grid_spec,
        out_shape=jax.ShapeDtypeStruct((n_blocks * bm, d), F32),
        compiler_params=_params(("arbitrary",)), name="moe_experts",
    )(block_e, buf_tok, n_used, h, wg, wu, wd)


def _combine_kernel(pos_ref, ys_hbm, x_ref, gw_ref, gate_ref, fg_ref, o_ref, ybuf, sem, *, tm, tok0, final):
    i = pl.program_id(0)
    n = pl.num_programs(0)

    def start(blk, slot):
        def body(r, carry):
            t = tok0 + blk * tm + r
            _row_copy(ys_hbm, pos_ref[2 * t], ybuf, slot, r, sem).start()
            _row_copy(ys_hbm, pos_ref[2 * t + 1], ybuf, slot, tm + r, sem).start()
            return carry
        lax.fori_loop(0, tm, body, 0)

    @pl.when(i == 0)
    def _():
        start(0, 0)

    @pl.when(i + 1 < n)
    def _():
        start(i + 1, (i + 1) % 2)

    slot = i % 2

    def body(r, carry):
        _row_copy(ys_hbm, 0, ybuf, slot, r, sem).wait()
        return carry
    lax.fori_loop(0, 2 * tm, body, 0)
    gw = gw_ref[...]
    y = gw[:, 0:1] * ybuf[slot, 0:tm, :] + gw[:, 1:2] * ybuf[slot, tm:2 * tm, :]
    out = x_ref[...] + gate_ref[0] * y
    if final:
        out = _rms(out, fg_ref[...])
    o_ref[...] = out


def _combine(ys, pos, gw, x2d, gate, final_g, tok0, rows_per_mod, final):
    m, d = x2d.shape
    tm = 256
    per = rows_per_mod // tm
    gb = tok0 // tm
    grid_spec = pltpu.PrefetchScalarGridSpec(
        num_scalar_prefetch=1, grid=(m // tm,),
        in_specs=[pl.BlockSpec(memory_space=pl.ANY),
                  pl.BlockSpec((tm, d), lambda i, pos: (i, 0)),
                  pl.BlockSpec((tm, TOP_K), lambda i, pos: (gb + i, 0)),
                  pl.BlockSpec((1, 1, d), lambda i, pos: (i // per, 0, 0)),
                  pl.BlockSpec((1, d), lambda i, pos: (0, 0))],
        out_specs=pl.BlockSpec((tm, d), lambda i, pos: (i, 0)),
        scratch_shapes=[pltpu.VMEM((2, 2 * tm, d), F32), pltpu.SemaphoreType.DMA((2,))])
    return pl.pallas_call(
        functools.partial(_combine_kernel, tm=tm, tok0=tok0, final=final), grid_spec=grid_spec,
        out_shape=jax.ShapeDtypeStruct((m, d), F32),
        compiler_params=_params(("arbitrary",)), name="moe_combine",
    )(pos, ys, x2d, gw, gate, final_g.reshape(1, d))


def _route(logits):
    t = logits.shape[0]
    p_grp = jax.nn.softmax(logits[:, :N_GROUPS], axis=-1)
    p_sel, grp = lax.top_k(p_grp, 1)
    fine = logits[:, N_GROUPS:N_GROUPS + N_EXPERTS].reshape(t, N_GROUPS, EXPERTS_PER_GROUP)
    fine = jnp.take_along_axis(fine, grp[:, :, None], axis=1)[:, 0]
    top_v, top_i = lax.top_k(fine, TOP_K)
    gate = p_sel * jax.nn.softmax(top_v, axis=-1)
    return grp * EXPERTS_PER_GROUP + top_i, gate


def _dispatch_plan(expert, bm):
    t = expert.shape[0]
    a = t * TOP_K
    flat_e = expert.reshape(a)
    onehot = (flat_e[:, None] == jnp.arange(N_EXPERTS, dtype=jnp.int32)[None, :]).astype(jnp.int32)
    rank = jnp.sum((jnp.cumsum(onehot, axis=0) - 1) * onehot, axis=1)
    counts = jnp.sum(onehot, axis=0)
    padded = (counts + bm - 1) // bm * bm
    pad_end = jnp.cumsum(padded)
    pad_start = pad_end - padded
    pos = (pad_start[flat_e] + rank).astype(jnp.int32)
    n_blocks = a // bm + N_EXPERTS
    flat_tok = jnp.arange(a, dtype=jnp.int32) // TOP_K
    buf_tok = jnp.zeros((n_blocks * bm,), jnp.int32).at[pos].set(flat_tok)
    n_used = (pad_end[-1] // bm).astype(jnp.int32)
    blk = jnp.minimum(jnp.arange(n_blocks, dtype=jnp.int32), n_used - 1) * bm
    block_e = jnp.minimum(jnp.searchsorted(pad_end, blk, side='right'), N_EXPERTS - 1).astype(jnp.int32)
    return pos, buf_tok, block_e, n_used.reshape(1)


def _mix_stream(p3, rows, width, s0_f, s0_b, four_tabs, hy_tabs, hy_coefs, conv_w, conv_b, four_g,
                hy_bias, hy_g, a_log, dt_bias, with_mixers):
    ab = p3[..., PCOL_AB:PCOL_AB + N_AB]
    qkv = _short_conv(p3, conv_w, conv_b, rows, width, OFF_DN, HY_CONV_CH, 3 * DN_W, DN_DK, True)
    o_f, o_b, s_f, s_b = _delta_rule(qkv, ab, a_log, dt_bias, s0_f, s0_b)
    if not with_mixers:
        return None, None, o_f, o_b, s_f, s_b
    y_four = _fourier(p3, *four_tabs, four_g)
    convh = _short_conv(p3, conv_w, conv_b, rows, width, OFF_HY, 0, HY_CONV_CH, 256, False)
    y_hy = _hyena(convh, hy_tabs, hy_coefs, hy_bias, hy_g)
    return y_four, y_hy, o_f, o_b, s_f, s_b


def kernel(x, c, ctx, c_ctx, norm1_g, norm2_g, w_mod, b_mod, w_in, conv_w, conv_b, four_g, hy_w1, hy_b1, hy_w2, hy_b2, hy_w3, hy_b3, hy_w4, hy_freq, hy_bias, hy_g, dn_a_log, dn_dt_bias, dn_g, w_out, w_rc, b_rc, w_rf, b_rf, w_e_gate, w_e_up, w_e_down, final_g):
    bsz, length, d = x.shape
    lc = ctx.shape[1]
    depth = w_in.shape[0]
    rows = length // GRID_W
    bm = 256

    mod_all = _modulation(c, c_ctx, w_mod, b_mod)
    four_x, four_c = _fourier_tables(length), _fourier_tables(lc)
    hy_x, hy_c = _hyena_tables(length), _hyena_tables(lc)
    zeros = jnp.zeros((bsz, DN_HEADS, DN_DK, DN_DV), F32)
    x2 = x.reshape(bsz * length, d)
    c2 = ctx.reshape(bsz * lc, d)

    for l in range(depth):
        last = l == depth - 1
        mod = [m[:, None, :] for m in jnp.split(mod_all[l, :bsz], 6, axis=-1)]
        modc = [m[:, None, :] for m in jnp.split(mod_all[l, bsz:bsz + 1], 6, axis=-1)]
        wi = w_in[l]
        w_p = jnp.concatenate([wi[:, :OFF_AB], wi[:, OFF_Z:], wi[:, OFF_AB:OFF_Z],
                               jnp.zeros((d, LANE - N_AB), wi.dtype)], axis=1).astype(BF16)
        w_o = w_out[l].astype(BF16)
        hy_params = (hy_w1[l], hy_b1[l], hy_w2[l], hy_b2[l], hy_w3[l], hy_b3[l], hy_w4[l], hy_freq[l])
        mix_args = (conv_w[l], conv_b[l], four_g[l], hy_bias[l], hy_g[l], dn_a_log[l], dn_dt_bias[l])

        pc = _proj_in(c2, norm1_g[l], modc[0], modc[1], w_p, bsz * lc)
        pc3 = pc.reshape(bsz, lc, PROJ_W)
        coefs_c = None if last else _hyena_coefs(lc, hy_c, hy_params)
        yf, yh, o_f, o_b, s_f, s_b = _mix_stream(pc3, 1, lc, zeros, zeros, four_c, hy_c, coefs_c,
                                                 *mix_args, with_mixers=not last)
        if not last:
            c2 = _proj_out(yf.reshape(-1, FOUR_W), yh.reshape(-1, HY_W), o_f.reshape(-1, DN_W),
                           o_b.reshape(-1, DN_W), pc, dn_g[l], w_o, c2, modc[2], bsz * lc)

        p = _proj_in(x2, norm1_g[l], mod[0], mod[1], w_p, length)
        p3 = p.reshape(bsz, length, PROJ_W)
        coefs_x = _hyena_coefs(length, hy_x, hy_params)
        yf, yh, o_f, o_b, _, _ = _mix_stream(p3, rows, GRID_W, s_f, s_b, four_x, hy_x, coefs_x,
                                             *mix_args, with_mixers=True)
        x2 = _proj_out(yf.reshape(-1, FOUR_W), yh.reshape(-1, HY_W), o_f.reshape(-1, DN_W),
                       o_b.reshape(-1, DN_W), p, dn_g[l], w_o, x2, mod[2], length)

        w_r = jnp.concatenate([w_rc[l], w_rf[l], jnp.zeros((d, ROUTER_W - N_GROUPS - N_EXPERTS), F32)], axis=1)
        b_r = jnp.concatenate([b_rc[l], b_rf[l], jnp.zeros((ROUTER_W - N_GROUPS - N_EXPERTS,), F32)])[None, :]
        h, logits = _router(x2, c2, norm2_g[l], mod[3], mod[4], modc[3], modc[4], w_r, b_r, length,
                            with_ctx=not last)
        expert, gate = _route(logits)
        pos, buf_tok, block_e, n_used = _dispatch_plan(expert, bm)
        ys = _experts(h, buf_tok, block_e, n_used, w_e_gate[l].astype(BF16), w_e_up[l].astype(BF16),
                      w_e_down[l].astype(BF16), bm)
        x2 = _combine(ys, pos, gate, x2, mod[5], final_g, 0, length, final=last)
        if not last:
            c2 = _combine(ys, pos, gate, c2, modc[5], final_g, bsz * length, bsz * lc, final=False)
    return x2.reshape(bsz, length, d)
```

```python
import functools
import math

import jax
import jax.numpy as jnp
from jax import lax
from jax.experimental import pallas as pl
from jax.experimental.pallas import tpu as pltpu

F32 = jnp.float32
BF16 = jnp.bfloat16
HIGHEST = lax.Precision.HIGHEST

GRID_W = 64
FOUR_W = 512
FOUR_GROUPS = 4
FOUR_GW = FOUR_W // FOUR_GROUPS
HY_W = 512
HY_ORDER = 2
DN_HEADS = 8
DN_DK = 128
DN_DV = 128
DN_W = DN_HEADS * DN_DV
DN_CHUNK = 64
DN_SOLVE_BLOCK = DN_CHUNK // 4
HY_CONV_CH = (HY_ORDER + 1) * HY_W
OFF_HY = FOUR_W
OFF_DN = OFF_HY + HY_CONV_CH
OFF_AB = OFF_DN + 3 * DN_W
N_AB = 4 * DN_HEADS
OFF_Z = OFF_AB + N_AB
HY_EMB = 33
HY_FAST_DECAY = 0.3
HY_SLOW_DECAY = 1.5
HY_TARGET = 1e-2
N_GROUPS = 4
EXPERTS_PER_GROUP = 8
N_EXPERTS = N_GROUPS * EXPERTS_PER_GROUP
TOP_K = 2
EPS = 1e-6

LANE = 128
PCOL_Z = OFF_AB
PCOL_AB = PCOL_Z + DN_W
PROJ_W = PCOL_AB + LANE
ROUTER_W = LANE
VMEM_LIMIT = 56 * 1024 * 1024


def _params(semantics):
    return pltpu.CompilerParams(dimension_semantics=semantics, vmem_limit_bytes=VMEM_LIMIT)


def _silu(v):
    return v * jax.nn.sigmoid(v)


def _rms(v, g):
    return v * lax.rsqrt(jnp.mean(v * v, axis=-1, keepdims=True) + EPS) * g


def _dot(a, b):
    return jnp.dot(a, b, preferred_element_type=F32)


def _mod_kernel(a_ref, w_ref, b_ref, o_ref):
    o_ref[0] = jnp.dot(_silu(a_ref[...]), w_ref[0], precision=HIGHEST,
                       preferred_element_type=F32) + b_ref[0]


def _modulation(c, c_ctx, w_mod, b_mod):
    depth, d, n = w_mod.shape
    bsz = c.shape[0]
    rows = -(-(bsz + 1) // 8) * 8
    a = jnp.concatenate([c, c_ctx[None], jnp.zeros((rows - bsz - 1, d), F32)], axis=0)
    tn = 1024 if n % 1024 == 0 else 512
    assert n % tn == 0
    return pl.pallas_call(
        _mod_kernel, grid=(depth, n // tn),
        in_specs=[pl.BlockSpec((rows, d), lambda l, j: (0, 0)),
                  pl.BlockSpec((1, d, tn), lambda l, j: (l, 0, j)),
                  pl.BlockSpec((1, 1, tn), lambda l, j: (l, 0, j))],
        out_specs=pl.BlockSpec((1, rows, tn), lambda l, j: (l, 0, j)),
        out_shape=jax.ShapeDtypeStruct((depth, rows, n), F32),
        compiler_params=_params(("parallel", "parallel")), name="modulation",
    )(a, w_mod, b_mod.reshape(depth, 1, n))


def _proj_in_kernel(x_ref, g_ref, shift_ref, scale_ref, w_ref, o_ref, a_scr):
    @pl.when(pl.program_id(1) == 0)
    def _():
        y = _rms(x_ref[...], g_ref[...])
        a_scr[...] = (y * (1.0 + scale_ref[0]) + shift_ref[0]).astype(BF16)

    o_ref[...] = _dot(a_scr[...], w_ref[...])


def _proj_in(x2d, g, shift, scale, w, rows_per_mod):
    m, d = x2d.shape
    n = w.shape[1]
    tm = min(512, m)
    tn = 896 if n % 896 == 0 else n
    per = rows_per_mod // tm
    return pl.pallas_call(
        _proj_in_kernel, grid=(m // tm, n // tn),
        in_specs=[pl.BlockSpec((tm, d), lambda i, j: (i, 0)),
                  pl.BlockSpec((1, d), lambda i, j: (0, 0)),
                  pl.BlockSpec((1, 1, d), lambda i, j: (i // per, 0, 0)),
                  pl.BlockSpec((1, 1, d), lambda i, j: (i // per, 0, 0)),
                  pl.BlockSpec((d, tn), lambda i, j: (0, j))],
        out_specs=pl.BlockSpec((tm, tn), lambda i, j: (i, j)),
        out_shape=jax.ShapeDtypeStruct((m, n), F32),
        scratch_shapes=[pltpu.VMEM((tm, d), BF16)],
        compiler_params=_params(("parallel", "arbitrary")), name="proj_in",
    )(x2d, g.reshape(1, d), shift, scale, w)


def _cos_sin_table(nrow, ncol, period):
    i = lax.broadcasted_iota(jnp.int32, (nrow, ncol), 0)
    j = lax.broadcasted_iota(jnp.int32, (nrow, ncol), 1)
    ang = ((i * j) % period).astype(F32) * (2.0 * math.pi / period)
    return jnp.cos(ang), jnp.sin(ang)


def _dft_tables(length, period):
    blk = 64 if length % 64 == 0 else length
    hi_i = lax.broadcasted_iota(jnp.int32, (length // blk, length), 0) * blk
    j = lax.broadcasted_iota(jnp.int32, (length // blk, length), 1)
    ang = ((hi_i * j) % period).astype(F32) * (2.0 * math.pi / period)
    hi_c, hi_s = jnp.cos(ang), jnp.sin(ang)
    lo_c, lo_s = _cos_sin_table(blk, length, period)
    c = hi_c[:, None, :] * lo_c[None, :, :] - hi_s[:, None, :] * lo_s[None, :, :]
    s = hi_s[:, None, :] * lo_c[None, :, :] + hi_c[:, None, :] * lo_s[None, :, :]
    return c.reshape(length, length), s.reshape(length, length)


def _fourier_kernel(u_ref, c_ref, s_ref, bc_ref, bs_ref, g_ref, o_ref, ub_scr, *, scale):
    @pl.when(pl.program_id(1) == 0)
    def _():
        ub_scr[...] = u_ref[0].astype(BF16)

    ub = ub_scr[...]
    p = _dot(c_ref[...], ub).astype(BF16)
    q = _dot(s_ref[...], ub).astype(BF16)
    z = (_dot(p, bc_ref[...]) - _dot(q, bs_ref[...])) * scale
    o_ref[0] = _rms(z, g_ref[...])


def _fourier(p3, cl, sl, bc, bs, four_g):
    bsz, length, _ = p3.shape
    tm = min(512, length)
    kern = functools.partial(_fourier_kernel, scale=1.0 / math.sqrt(length * FOUR_GW))
    return pl.pallas_call(
        kern, grid=(bsz, length // tm),
        in_specs=[pl.BlockSpec((1, length, FOUR_W), lambda b, i: (b, 0, 0)),
                  pl.BlockSpec((tm, length), lambda b, i: (i, 0)),
                  pl.BlockSpec((tm, length), lambda b, i: (i, 0)),
                  pl.BlockSpec((FOUR_W, FOUR_W), lambda b, i: (0, 0)),
                  pl.BlockSpec((FOUR_W, FOUR_W), lambda b, i: (0, 0)),
                  pl.BlockSpec((1, FOUR_W), lambda b, i: (0, 0))],
        out_specs=pl.BlockSpec((1, tm, FOUR_W), lambda b, i: (b, i, 0)),
        out_shape=jax.ShapeDtypeStruct((bsz, length, FOUR_W), F32),
        scratch_shapes=[pltpu.VMEM((length, FOUR_W), BF16)],
        compiler_params=_params(("parallel", "arbitrary")), name="fourier",
    )(p3, cl, sl, bc, bs, four_g.reshape(1, FOUR_W))


def _fourier_tables(length):
    cl, sl = _dft_tables(length, length)
    cc, sc = _cos_sin_table(FOUR_GW, FOUR_GW, FOUR_GW)
    eye = jnp.eye(FOUR_GROUPS, dtype=F32)
    return cl.astype(BF16), sl.astype(BF16), jnp.kron(eye, cc).astype(BF16), jnp.kron(eye, sc).astype(BF16)


def _conv_kernel(x_ref, w_ref, b_ref, o_ref, *, rows, width, heads_mode):
    tc = x_ref.shape[2]
    wts = w_ref[...]
    bias = b_ref[...]
    pos = lax.broadcasted_iota(jnp.int32, (width, tc), 0)
    not_first = pos > 0
    not_last = pos < width - 1
    kind = pl.program_id(1) // (DN_W // tc)

    def one_row(src, dr):
        left = jnp.where(not_first, pltpu.roll(src, 1, 0), 0.0)
        right = jnp.where(not_last, pltpu.roll(src, width - 1, 0), 0.0)
        return left * wts[dr, 0:1, :] + src * wts[dr, 1:2, :] + right * wts[dr, 2:3, :]

    def body(r, carry):
        base = pl.multiple_of(r * width, width)
        acc = one_row(x_ref[0, pl.ds(base, width), :], 1) + bias
        if rows > 1:
            up = pl.multiple_of(jnp.maximum(r - 1, 0) * width, width)
            dn = pl.multiple_of(jnp.minimum(r + 1, rows - 1) * width, width)
            acc = acc + jnp.where(r > 0, 1.0, 0.0) * one_row(x_ref[0, pl.ds(up, width), :], 0)
            acc = acc + jnp.where(r < rows - 1, 1.0, 0.0) * one_row(x_ref[0, pl.ds(dn, width), :], 2)
        if heads_mode:
            for h in range(tc // DN_DK):
                s = _silu(acc[:, h * DN_DK:(h + 1) * DN_DK])
                inv = lax.rsqrt(jnp.sum(s * s, axis=-1, keepdims=True) + EPS)
                fac = jnp.where(kind == 0, inv * (DN_DK ** -0.5), jnp.where(kind == 1, inv, 1.0))
                o_ref[0, pl.ds(base, width), h * DN_DK:(h + 1) * DN_DK] = s * fac
        else:
            o_ref[0, pl.ds(base, width), :] = acc
        return carry

    lax.fori_loop(0, rows, body, 0)


def _short_conv(p3, conv_w, conv_b, rows, width, col0, ch0, nch, tc, heads_mode):
    bsz, length, _ = p3.shape
    kern = functools.partial(_conv_kernel, rows=rows, width=width, heads_mode=heads_mode)
    cb, wb = col0 // tc, ch0 // tc
    return pl.pallas_call(
        kern, grid=(bsz, nch // tc),
        in_specs=[pl.BlockSpec((1, length, tc), lambda b, j: (b, 0, cb + j)),
                  pl.BlockSpec((3, 3, tc), lambda b, j: (0, 0, wb + j)),
                  pl.BlockSpec((1, tc), lambda b, j: (0, wb + j))],
        out_specs=pl.BlockSpec((1, length, tc), lambda b, j: (b, 0, j)),
        out_shape=jax.ShapeDtypeStruct((bsz, length, nch), F32),
        compiler_params=_params(("parallel", "parallel")),
        name="conv_heads" if heads_mode else "conv_hyena",
    )(p3, conv_w, conv_b.reshape(1, -1))


def _filt_kernel(z_ref, w1_ref, b1_ref, w2_ref, b2_ref, w3_ref, b3_ref, w4_ref, fr_ref, dl_ref,
                 g_ref, nrm_ref):
    i = pl.program_id(0)
    z = z_ref[...]
    fr = fr_ref[...]
    hdot = lambda a, b: jnp.dot(a, b, precision=HIGHEST, preferred_element_type=F32)
    h = jnp.sin(fr * (hdot(z, w1_ref[...]) + b1_ref[...]))
    h = jnp.sin(fr * (hdot(h, w2_ref[...]) + b2_ref[...]))
    h = jnp.sin(fr * (hdot(h, w3_ref[...]) + b3_ref[...]))
    h = hdot(h, w4_ref[...])
    decay = jnp.exp(-z[:, 0:1] * jnp.abs(dl_ref[...]))
    decay = jnp.concatenate([decay] * HY_ORDER, axis=1)
    half = HY_ORDER * HY_W
    hf = h[:, :half] * decay
    hb = h[:, half:] * decay
    row = lax.broadcasted_iota(jnp.int32, hb.shape, 0) + i * hb.shape[0]
    hb = jnp.where(row > 0, hb, 0.0)
    gp = hf + hb
    gm = hf - hb
    for o in range(HY_ORDER):
        g_ref[o] = gp[:, o * HY_W:(o + 1) * HY_W].astype(BF16)
        g_ref[HY_ORDER + o] = gm[:, o * HY_W:(o + 1) * HY_W].astype(BF16)
    part = jnp.sum(jnp.abs(hf) + jnp.abs(hb), axis=0, keepdims=True)

    @pl.when(i == 0)
    def _():
        nrm_ref[...] = part

    @pl.when(i > 0)
    def _():
        nrm_ref[...] += part


def _hyena_filter_taps(length, w1, b1, w2, b2, w3, b3, w4, freq):
    t = jnp.linspace(0.0, 1.0, length, dtype=F32)[:, None]
    bands = (HY_EMB - 1) // 2
    ang = (2.0 * math.pi / length) * jnp.arange(length, dtype=F32)[:, None]
    f = jnp.linspace(1e-4, bands - 1, bands, dtype=F32)
    z = jnp.concatenate([t, jnp.cos(f * ang), -jnp.sin(f * ang)], axis=-1)
    max_decay = math.log(HY_TARGET) / HY_FAST_DECAY
    min_decay = math.log(HY_TARGET) / HY_SLOW_DECAY
    deltas = jnp.linspace(min_decay, max_decay, HY_W, dtype=F32)[None, :]
    fw = w1.shape[1]
    tl = min(512, length)
    full = lambda shape: pl.BlockSpec(shape, lambda i: (0,) * len(shape))
    return pl.pallas_call(
        _filt_kernel, grid=(length // tl,),
        in_specs=[pl.BlockSpec((tl, HY_EMB), lambda i: (i, 0)),
                  full((HY_EMB, fw)), full((1, fw)), full((fw, fw)), full((1, fw)),
                  full((fw, fw)), full((1, fw)), full((fw, 2 * HY_ORDER * HY_W)), full((1, fw)),
                  full((1, HY_W))],
        out_specs=[pl.BlockSpec((2 * HY_ORDER, tl, HY_W), lambda i: (0, i, 0)),
                   pl.BlockSpec((1, HY_ORDER * HY_W), lambda i: (0, 0))],
        out_shape=[jax.ShapeDtypeStruct((2 * HY_ORDER, length, HY_W), BF16),
                   jax.ShapeDtypeStruct((1, HY_ORDER * HY_W), F32)],
        compiler_params=_params(("arbitrary",)), name="hyena_filter",
    )(z, w1, b1.reshape(1, fw), w2, b2.reshape(1, fw), w3, b3.reshape(1, fw), w4,
      freq.reshape(1, fw), deltas)


def _dft_fwd_kernel(u_ref, c_ref, s_ref, *rest, with_coef):
    if with_coef:
        c1_ref, c2_ref, c4_ref, yr_ref, yi_ref, ub_scr = rest
    else:
        yr_ref, yi_ref, ub_scr = rest

    @pl.when(pl.program_id(1) == 0)
    def _():
        ub_scr[...] = u_ref[0].astype(BF16)

    ub = ub_scr[...]
    a = _dot(c_ref[...], ub)
    b = _dot(s_ref[...], ub)
    if with_coef:
        c2 = c2_ref[...]
        yr_ref[0] = (a * c1_ref[...] + b * c2).astype(BF16)
        yi_ref[0] = (b * c4_ref[...] - a * c2).astype(BF16)
    else:
        yr_ref[0] = a
        yi_ref[0] = b


def _dft_fwd(u3, col_blk, cf, sf, coef=None):
    nb, length, _ = u3.shape
    tm = min(512, length)
    in_specs = [pl.BlockSpec((1, length, HY_W), lambda b, i: (b, 0, col_blk)),
                pl.BlockSpec((tm, length), lambda b, i: (i, 0)),
                pl.BlockSpec((tm, length), lambda b, i: (i, 0))]
    args = [u3, cf, sf]
    if coef is not None:
        in_specs += [pl.BlockSpec((tm, HY_W), lambda b, i: (i, 0))] * 3
        args += list(coef)
    odt = BF16 if coef is not None else F32
    return pl.pallas_call(
        functools.partial(_dft_fwd_kernel, with_coef=coef is not None), grid=(nb, length // tm),
        in_specs=in_specs,
        out_specs=[pl.BlockSpec((1, tm, HY_W), lambda b, i: (b, i, 0))] * 2,
        out_shape=[jax.ShapeDtypeStruct((nb, length, HY_W), odt)] * 2,
        scratch_shapes=[pltpu.VMEM((length, HY_W), BF16)],
        compiler_params=_params(("parallel", "arbitrary")),
        name="hyena_dft_fwd" if coef is not None else "hyena_filter_dft",
    )(*args)


def _dft_inv_kernel(yr_ref, yi_ref, c_ref, s_ref, u_ref, gate_ref, bias_ref, g_ref, o_ref, *, final):
    y = _dot(c_ref[...], yr_ref[0]) + _dot(s_ref[...], yi_ref[0])
    out = gate_ref[0] * (y + bias_ref[...] * u_ref[0])
    if final:
        out = _rms(out, g_ref[...])
    o_ref[0] = out


def _dft_inv(yr, yi, cf, s_inv, u3, u_blk, gate3, gate_blk, bias, g, final):
    bsz, length, _ = yr.shape
    tm = min(512, length)
    return pl.pallas_call(
        functools.partial(_dft_inv_kernel, final=final), grid=(bsz, length // tm),
        in_specs=[pl.BlockSpec((1, length, HY_W), lambda b, i: (b, 0, 0)),
                  pl.BlockSpec((1, length, HY_W), lambda b, i: (b, 0, 0)),
                  pl.BlockSpec((tm, length), lambda b, i: (i, 0)),
                  pl.BlockSpec((tm, length), lambda b, i: (i, 0)),
                  pl.BlockSpec((1, tm, HY_W), lambda b, i: (b, i, u_blk)),
                  pl.BlockSpec((1, tm, HY_W), lambda b, i: (b, i, gate_blk)),
                  pl.BlockSpec((1, HY_W), lambda b, i: (0, 0)),
                  pl.BlockSpec((1, HY_W), lambda b, i: (0, 0))],
        out_specs=pl.BlockSpec((1, tm, HY_W), lambda b, i: (b, i, 0)),
        out_shape=jax.ShapeDtypeStruct((bsz, length, HY_W), F32),
        compiler_params=_params(("parallel", "arbitrary")), name="hyena_dft_inv",
    )(yr, yi, cf, s_inv, u3, gate3, bias.reshape(1, HY_W), g.reshape(1, HY_W))


def _hyena_tables(length):
    cf, sf = _dft_tables(length, 2 * length)
    alt = jnp.where(jnp.arange(length) % 2 == 0, 1.0, -1.0).astype(F32)
    sf = sf.at[0, :].set(alt)
    return cf.astype(BF16), sf.astype(BF16), sf.T.astype(BF16)


def _hyena_coefs(length, tables, hy_params):
    cf, sf, _ = tables
    taps, nrm = _hyena_filter_taps(length, *hy_params)
    a, b = _dft_fwd(taps, 0, cf, sf)
    inv = (1.0 / nrm).reshape(HY_ORDER, 1, HY_W)
    n = 2.0 * length
    hr = a[:HY_ORDER] * inv
    hi = -b[HY_ORDER:] * inv
    nyq = b[:HY_ORDER, 0:1, :] * inv
    first = (jnp.arange(length) == 0)[None, :, None]
    c1 = jnp.where(first, hr / n, hr * (2.0 / n))
    c2 = jnp.where(first, 0.0, hi * (2.0 / n))
    c4 = jnp.where(first, nyq / n, hr * (2.0 / n))
    return c1, c2, c4


def _hyena(convh, tables, coefs, hy_bias, hy_g):
    cf, sf, s_inv = tables
    c1, c2, c4 = coefs
    yr, yi = _dft_fwd(convh, 0, cf, sf, (c1[0], c2[0], c4[0]))
    zz = _dft_inv(yr, yi, cf, s_inv, convh, 0, convh, 1, hy_bias[0], hy_g, False)
    yr, yi = _dft_fwd(zz, 0, cf, sf, (c1[1], c2[1], c4[1]))
    return _dft_inv(yr, yi, cf, s_inv, zz, 0, convh, 2, hy_bias[1], hy_g, True)


def _softplus(v):
    return jnp.maximum(v, 0.0) + jnp.log(1.0 + jnp.exp(-jnp.abs(v)))


def _delta_prep_kernel(qkv_ref, ab_ref, abt_ref, arow_ref, drow_ref, acol_ref, dcol_ref,
                       wq_ref, u_ref, ak_ref, egl_ref):
    c = ab_ref.shape[1]
    nh = DN_HEADS
    ri = lax.broadcasted_iota(jnp.int32, (c, c), 0)
    ci = lax.broadcasted_iota(jnp.int32, (c, c), 1)
    incl = (ri >= ci, ri <= ci)
    strict = (ri > ci, ri < ci)
    low = jnp.where(incl[0], 1.0, 0.0)
    upp = jnp.where(incl[1], 1.0, 0.0)
    hdot = lambda a, b: jnp.dot(a, b, precision=HIGHEST, preferred_element_type=F32)
    ab = ab_ref[0]
    abt = abt_ref[0, 0]
    gate = -arow_ref[...] * _softplus(ab + drow_ref[...])
    gate_t = -acol_ref[...] * _softplus(abt + dcol_ref[...])
    beta_all = jax.nn.sigmoid(ab)
    gcum = (hdot(low, gate), hdot(upp, gate))
    gcum_t = (hdot(gate_t, upp), hdot(gate_t, low))
    er = lax.broadcasted_iota(jnp.int32, (DN_DK, DN_DK), 0)
    ec = lax.broadcasted_iota(jnp.int32, (DN_DK, DN_DK), 1)
    eye16 = jnp.where(er == ec, 1.0, 0.0).astype(BF16)
    nt = (((1,), (1,)), ((), ()))
    q = [qkv_ref[0, :, h * DN_DK:(h + 1) * DN_DK] for h in range(nh)]
    k = [qkv_ref[0, :, DN_W + h * DN_DK:DN_W + (h + 1) * DN_DK] for h in range(nh)]
    v = [qkv_ref[0, :, 2 * DN_W + h * DN_DV:2 * DN_W + (h + 1) * DN_DV] for h in range(nh)]
    k16 = [t.astype(BF16) for t in k]
    kk0 = [lax.dot_general(k16[h], k16[h], nt, preferred_element_type=F32) for h in range(nh)]
    qk0 = [lax.dot_general(q[h].astype(BF16), k16[h], nt, preferred_element_type=F32) for h in range(nh)]
    units = [(d, h) for d in range(2) for h in range(nh)]
    mm, rr = {}, {}
    for d, h in units:
        col = d * nh + h
        gc = gcum[d][:, col:col + 1]
        gr = gcum_t[d][col:col + 1, :]
        be = beta_all[:, 2 * nh + col:2 * nh + col + 1]
        last = 0 if d else c - 1
        gl = gc[last:last + 1, :]
        dec = jnp.where(incl[d], jnp.exp(jnp.where(incl[d], gc - gr, 0.0)), 0.0)
        mm[d, h] = jnp.where(strict[d], be * kk0[h] * dec, 0.0)
        ak_ref[0, 0, d, h, 0:c, :] = jnp.where(incl[d], qk0[h] * dec, 0.0).astype(BF16)
        eg = jnp.exp(gc)
        rr[d, h] = jnp.concatenate([v[h] * be, k[h] * (be * eg)], axis=1)
        wq_ref[0, 0, d, h, c:2 * c, :] = (q[h] * eg).astype(BF16)
        k_tail = (k[h] * jnp.exp(gl - gc)).astype(BF16)
        ak_ref[0, 0, d, h, c:c + DN_DK, :] = lax.dot_general(
            eye16, k_tail, nt, preferred_element_type=F32).astype(BF16)
        egl_ref[0, 0, d, h:h + 1, :] = jnp.broadcast_to(jnp.exp(gl), (1, DN_DV))
    sb = DN_SOLVE_BLOCK
    in_sb = (ri // sb) == (ci // sb)
    in_2sb = (ri // (2 * sb)) == (ci // (2 * sb))
    eye = jnp.where(ri == ci, 1.0, 0.0)
    b16 = lambda t: t.astype(BF16)
    nj = {u: jnp.where(in_sb, -mm[u], 0.0) for u in units}
    inv = {u: eye + nj[u] for u in units}
    for j in range((sb - 1).bit_length() - 1):
        nj = {u: _dot(b16(nj[u]), b16(nj[u])) for u in units}
        inv = {u: inv[u] + _dot(b16(inv[u]), b16(nj[u])) for u in units}
    off = {u: b16(jnp.where(in_2sb & ~in_sb, mm[u], 0.0)) for u in units}
    tmp = {u: _dot(off[u], b16(inv[u])) for u in units}
    inv = {u: inv[u] - _dot(b16(inv[u]), b16(tmp[u])) for u in units}
    inv16 = {u: b16(inv[u]) for u in units}
    off = {u: b16(jnp.where(in_2sb, 0.0, mm[u])) for u in units}
    part = {u: _dot(inv16[u], b16(rr[u])) for u in units}
    tmp = {u: _dot(off[u], b16(part[u])) for u in units}
    rr = {u: part[u] - _dot(inv16[u], b16(tmp[u])) for u in units}
    for d, h in units:
        u_ref[0, 0, d, h] = rr[d, h][:, :DN_DV]
        wq_ref[0, 0, d, h, 0:c, :] = rr[d, h][:, DN_DV:].astype(BF16)


def _delta_scan_kernel(wqf_ref, uf_ref, akf_ref, egf_ref, wqb_ref, ub_ref, akb_ref, egb_ref, s0f_ref, s0b_ref,
                       of_ref, ob_ref, sf_ref, sb_ref, s_scr):
    i = pl.program_id(1)
    bpb = uf_ref.shape[0]
    c = uf_ref.shape[-2]

    @pl.when(i == 0)
    def _():
        s_scr[0] = s0f_ref[...]
        s_scr[1] = s0b_ref[...]

    refs = ((wqf_ref, uf_ref, akf_ref, egf_ref, of_ref), (wqb_ref, ub_ref, akb_ref, egb_ref, ob_ref))
    units = [(d, bb, h) for d in range(2) for bb in range(bpb) for h in range(DN_HEADS)]
    s = {u: s_scr[u] for u in units}
    ws = {(d, bb, h): _dot(refs[d][0][bb, 0, 0, h], s[d, bb, h].astype(BF16)) for d, bb, h in units}
    v16 = {(d, bb, h): (refs[d][1][bb, 0, 0, h] - ws[d, bb, h][:c]).astype(BF16) for d, bb, h in units}
    av = {(d, bb, h): _dot(refs[d][2][bb, 0, 0, h], v16[d, bb, h]) for d, bb, h in units}
    for d, bb, h in units:
        u = (d, bb, h)
        refs[d][4][bb, :, h * DN_DV:(h + 1) * DN_DV] = ws[u][c:] + av[u][:c]
        s_scr[u] = s[u] * refs[d][3][bb, 0, 0, h:h + 1, :] + av[u][c:]

    @pl.when(i == pl.num_programs(1) - 1)
    def _():
        sf_ref[...] = s_scr[0]
        sb_ref[...] = s_scr[1]


def _delta_rule(qkv, ab, a_log, dt_bias, s0_f, s0_b, bpb=1):
    bsz, length, _ = qkv.shape
    c = DN_CHUNK
    n = length // c
    nh = DN_HEADS
    abt = jnp.swapaxes(ab.reshape(bsz, n, c, 4 * nh), 2, 3)
    zeros = jnp.zeros((2 * nh,), F32)
    a_vec = jnp.concatenate([jnp.exp(a_log.astype(F32)).reshape(-1), zeros])
    d_vec = jnp.concatenate([dt_bias.astype(F32).reshape(-1), zeros])
    small = lambda shape: pl.BlockSpec(shape, lambda b, i: (0,) * len(shape))
    per_chunk = lambda *tail: pl.BlockSpec((1, 1, 2, nh) + tail, lambda b, i: (b, i, 0, 0) + (0,) * len(tail))
    wq, u, ak, egl = pl.pallas_call(
        _delta_prep_kernel, grid=(bsz, n),
        in_specs=[pl.BlockSpec((1, c, 3 * DN_W), lambda b, i: (b, i, 0)),
                  pl.BlockSpec((1, c, 4 * nh), lambda b, i: (b, i, 0)),
                  pl.BlockSpec((1, 1, 4 * nh, c), lambda b, i: (b, i, 0, 0)),
                  small((1, 4 * nh)), small((1, 4 * nh)), small((4 * nh, 1)), small((4 * nh, 1))],
        out_specs=[per_chunk(2 * c, DN_DK), per_chunk(c, DN_DV), per_chunk(c + DN_DK, c),
                   pl.BlockSpec((1, 1, 2, nh, DN_DV), lambda b, i: (b, i, 0, 0, 0))],
        out_shape=[jax.ShapeDtypeStruct((bsz, n, 2, nh, 2 * c, DN_DK), BF16),
                   jax.ShapeDtypeStruct((bsz, n, 2, nh, c, DN_DV), F32),
                   jax.ShapeDtypeStruct((bsz, n, 2, nh, c + DN_DK, c), BF16),
                   jax.ShapeDtypeStruct((bsz, n, 2, nh, DN_DV), F32)],
        compiler_params=_params(("parallel", "parallel")), name="delta_prep",
    )(qkv, ab, abt, a_vec.reshape(1, -1), d_vec.reshape(1, -1), a_vec.reshape(-1, 1), d_vec.reshape(-1, 1))

    def side(d, *tail):
        idx = (lambda b, i: (b, n - 1 - i, 1, 0) + (0,) * len(tail)) if d else (
            lambda b, i: (b, i, 0, 0) + (0,) * len(tail))
        return pl.BlockSpec((bpb, 1, 1, nh) + tail, idx)

    def side_specs(d):
        return [side(d, 2 * c, DN_DK), side(d, c, DN_DV), side(d, c + DN_DK, c),
                pl.BlockSpec((bpb, 1, 1, nh, DN_DV),
                             (lambda b, i: (b, n - 1 - i, 1, 0, 0)) if d else (lambda b, i: (b, i, 0, 0, 0)))]

    st_spec = pl.BlockSpec((bpb, nh, DN_DK, DN_DV), lambda b, i: (b, 0, 0, 0))
    o_shape = jax.ShapeDtypeStruct((bsz, length, DN_W), F32)
    s_shape = jax.ShapeDtypeStruct((bsz, nh, DN_DK, DN_DV), F32)
    return pl.pallas_call(
        _delta_scan_kernel, grid=(bsz // bpb, n),
        in_specs=side_specs(0) + side_specs(1) + [st_spec, st_spec],
        out_specs=[pl.BlockSpec((bpb, c, DN_W), lambda b, i: (b, i, 0)),
                   pl.BlockSpec((bpb, c, DN_W), lambda b, i: (b, n - 1 - i, 0)), st_spec, st_spec],
        out_shape=[o_shape, o_shape, s_shape, s_shape],
        scratch_shapes=[pltpu.VMEM((2, bpb, nh, DN_DK, DN_DV), F32)],
        compiler_params=_params(("parallel", "arbitrary")), name="delta_scan",
    )(wq, u, ak, egl, wq, u, ak, egl, s0_f, s0_b)


def _proj_out_kernel(yf_ref, yh_ref, of_ref, ob_ref, z_ref, dg_ref, w_ref, res_ref, gate_ref, o_ref, a_scr):
    @pl.when(pl.program_id(1) == 0)
    def _():
        a_scr[:, 0:FOUR_W] = yf_ref[...].astype(BF16)
        a_scr[:, FOUR_W:FOUR_W + HY_W] = yh_ref[...].astype(BF16)
        o = of_ref[...] + ob_ref[...]
        z = z_ref[...]
        for h in range(DN_HEADS):
            lanes = slice(h * DN_DV, (h + 1) * DN_DV)
            y = _rms(o[:, lanes], dg_ref[...]) * _silu(z[:, lanes])
            a_scr[:, FOUR_W + HY_W + h * DN_DV:FOUR_W + HY_W + (h + 1) * DN_DV] = y.astype(BF16)

    o_ref[...] = res_ref[...] + gate_ref[0] * _dot(a_scr[...], w_ref[...])


def _proj_out(y_four, y_hy, o_f, o_b, p2d, dn_g, w, res, gate, rows_per_mod):
    m, d = res.shape
    tm = min(512, m)
    tn = min(1024, d)
    per = rows_per_mod // tm
    zb = PCOL_Z // DN_W
    return pl.pallas_call(
        _proj_out_kernel, grid=(m // tm, d // tn),
        in_specs=[pl.BlockSpec((tm, FOUR_W), lambda i, j: (i, 0)),
                  pl.BlockSpec((tm, HY_W), lambda i, j: (i, 0)),
                  pl.BlockSpec((tm, DN_W), lambda i, j: (i, 0)),
                  pl.BlockSpec((tm, DN_W), lambda i, j: (i, 0)),
                  pl.BlockSpec((tm, DN_W), lambda i, j: (i, zb)),
                  pl.BlockSpec((1, DN_DV), lambda i, j: (0, 0)),
                  pl.BlockSpec((w.shape[0], tn), lambda i, j: (0, j)),
                  pl.BlockSpec((tm, tn), lambda i, j: (i, j)),
                  pl.BlockSpec((1, 1, tn), lambda i, j: (i // per, 0, j))],
        out_specs=pl.BlockSpec((tm, tn), lambda i, j: (i, j)),
        out_shape=jax.ShapeDtypeStruct((m, d), F32),
        scratch_shapes=[pltpu.VMEM((tm, w.shape[0]), BF16)],
        compiler_params=_params(("parallel", "arbitrary")), name="proj_out",
    )(y_four, y_hy, o_f, o_b, p2d, dn_g.reshape(1, DN_DV), w, res, gate)


def _router_kernel(x_ref, c_ref, g_ref, sx_ref, cx_ref, sc_ref, cc_ref, w_ref, b_ref, h_ref, lg_ref, *, n_x):
    i = pl.program_id(0)

    def emit(v, shift, scale):
        h = _rms(v, g_ref[...]) * (1.0 + scale) + shift
        h_ref[...] = h
        lg_ref[...] = jnp.dot(h, w_ref[...], precision=HIGHEST, preferred_element_type=F32) + b_ref[...]

    @pl.when(i < n_x)
    def _():
        emit(x_ref[...], sx_ref[0], cx_ref[0])

    @pl.when(i >= n_x)
    def _():
        emit(c_ref[...], sc_ref[0], cc_ref[0])


def _router(x2d, c2d, g, shift_x, scale_x, shift_c, scale_c, w_r, b_r, rows_per_mod, with_ctx):
    mx, d = x2d.shape
    tm = 512
    n_x = mx // tm
    n_c = c2d.shape[0] // tm if with_ctx else 0
    per = rows_per_mod // tm
    xi = lambda i: (jnp.minimum(i, n_x - 1), 0)
    ci = lambda i: (jnp.maximum(i - n_x, 0), 0)
    return pl.pallas_call(
        functools.partial(_router_kernel, n_x=n_x), grid=(n_x + n_c,),
        in_specs=[pl.BlockSpec((tm, d), xi), pl.BlockSpec((tm, d), ci),
                  pl.BlockSpec((1, d), lambda i: (0, 0)),
                  pl.BlockSpec((1, 1, d), lambda i: (jnp.minimum(i, n_x - 1) // per, 0, 0)),
                  pl.BlockSpec((1, 1, d), lambda i: (jnp.minimum(i, n_x - 1) // per, 0, 0)),
                  pl.BlockSpec((1, 1, d), lambda i: (0, 0, 0)),
                  pl.BlockSpec((1, 1, d), lambda i: (0, 0, 0)),
                  pl.BlockSpec((d, ROUTER_W), lambda i: (0, 0)),
                  pl.BlockSpec((1, ROUTER_W), lambda i: (0, 0))],
        out_specs=[pl.BlockSpec((tm, d), lambda i: (i, 0)),
                   pl.BlockSpec((tm, ROUTER_W), lambda i: (i, 0))],
        out_shape=[jax.ShapeDtypeStruct(((n_x + n_c) * tm, d), F32),
                   jax.ShapeDtypeStruct(((n_x + n_c) * tm, ROUTER_W), F32)],
        compiler_params=_params(("parallel",)), name="moe_router",
    )(x2d, c2d, g.reshape(1, d), shift_x, scale_x, shift_c, scale_c, w_r, b_r)


def _row_copy(src_hbm, row, dst, slot, r, sem):
    return pltpu.make_async_copy(src_hbm.at[pl.ds(row, 1)], dst.at[slot, pl.ds(r, 1)], sem.at[slot])


def _expert_kernel(be_ref, tok_ref, nb_ref, h_hbm, wg_ref, wu_ref, wd_ref, ys_ref, xbuf, sem, *, bm):
    i = pl.program_id(0)
    nb = nb_ref[0]

    def start(blk, slot):
        def body(r, carry):
            _row_copy(h_hbm, tok_ref[blk * bm + r], xbuf, slot, r, sem).start()
            return carry
        lax.fori_loop(0, bm, body, 0)

    @pl.when(i == 0)
    def _():
        start(0, 0)

    @pl.when(i + 1 < nb)
    def _():
        start(i + 1, (i + 1) % 2)

    @pl.when(i < nb)
    def _():
        slot = i % 2

        def body(r, carry):
            _row_copy(h_hbm, 0, xbuf, slot, r, sem).wait()
            return carry
        lax.fori_loop(0, bm, body, 0)
        x = xbuf[slot].astype(BF16)
        act = (_silu(_dot(x, wg_ref[0])) * _dot(x, wu_ref[0])).astype(BF16)
        ys_ref[...] = _dot(act, wd_ref[0])

    @pl.when(i >= nb)
    def _():
        ys_ref[...] = jnp.zeros_like(ys_ref)


def _experts(h, buf_tok, block_e, n_used, wg, wu, wd, bm):
    n_blocks = block_e.shape[0]
    d = h.shape[1]
    ff = wg.shape[2]
    grid_spec = pltpu.PrefetchScalarGridSpec(
        num_scalar_prefetch=3, grid=(n_blocks,),
        in_specs=[pl.BlockSpec(memory_space=pl.ANY),
                  pl.BlockSpec((1, d, ff), lambda i, be, tok, nb: (be[i], 0, 0)),
                  pl.BlockSpec((1, d, ff), lambda i, be, tok, nb: (be[i], 0, 0)),
                  pl.BlockSpec((1, ff, d), lambda i, be, tok, nb: (be[i], 0, 0))],
        out_specs=pl.BlockSpec((bm, d), lambda i, be, tok, nb: (i, 0)),
        scratch_shapes=[pltpu.VMEM((2, bm, d), F32), pltpu.SemaphoreType.DMA((2,))])
    return pl.pallas_call(
        functools.partial(_expert_kernel, bm=bm), grid_spec=grid_spec,
        out_shape=jax.ShapeDtypeStruct((n_blocks * bm, d), F32),
        compiler_params=_params(("arbitrary",)), name="moe_experts",
    )(block_e, buf_tok, n_used, h, wg, wu, wd)


def _combine_kernel(pos_ref, ys_hbm, x_ref, gw_ref, gate_ref, fg_ref, o_ref, ybuf, sem, *, tm, tok0, final):
    i = pl.program_id(0)
    n = pl.num_programs(0)

    def start(blk, slot):
        def body(r, carry):
            t = tok0 + blk * tm + r
            _row_copy(ys_hbm, pos_ref[2 * t], ybuf, slot, r, sem).start()
            _row_copy(ys_hbm, pos_ref[2 * t + 1], ybuf, slot, tm + r, sem).start()
            return carry
        lax.fori_loop(0, tm, body, 0)

    @pl.when(i == 0)
    def _():
        start(0, 0)

    @pl.when(i + 1 < n)
    def _():
        start(i + 1, (i + 1) % 2)

    slot = i % 2

    def body(r, carry):
        _row_copy(ys_hbm, 0, ybuf, slot, r, sem).wait()
        return carry
    lax.fori_loop(0, 2 * tm, body, 0)
    gw = gw_ref[...]
    y = gw[:, 0:1] * ybuf[slot, 0:tm, :] + gw[:, 1:2] * ybuf[slot, tm:2 * tm, :]
    out = x_ref[...] + gate_ref[0] * y
    if final:
        out = _rms(out, fg_ref[...])
    o_ref[...] = out


def _combine(ys, pos, gw, x2d, gate, final_g, tok0, rows_per_mod, final):
    m, d = x2d.shape
    tm = 256
    per = rows_per_mod // tm
    gb = tok0 // tm
    grid_spec = pltpu.PrefetchScalarGridSpec(
        num_scalar_prefetch=1, grid=(m // tm,),
        in_specs=[pl.BlockSpec(memory_space=pl.ANY),
                  pl.BlockSpec((tm, d), lambda i, pos: (i, 0)),
                  pl.BlockSpec((tm, TOP_K), lambda i, pos: (gb + i, 0)),
                  pl.BlockSpec((1, 1, d), lambda i, pos: (i // per, 0, 0)),
                  pl.BlockSpec((1, d), lambda i, pos: (0, 0))],
        out_specs=pl.BlockSpec((tm, d), lambda i, pos: (i, 0)),
        scratch_shapes=[pltpu.VMEM((2, 2 * tm, d), F32), pltpu.SemaphoreType.DMA((2,))])
    return pl.pallas_call(
        functools.partial(_combine_kernel, tm=tm, tok0=tok0, final=final), grid_spec=grid_spec,
        out_shape=jax.ShapeDtypeStruct((m, d), F32),
        compiler_params=_params(("arbitrary",)), name="moe_combine",
    )(pos, ys, x2d, gw, gate, final_g.reshape(1, d))


def _route(logits):
    t = logits.shape[0]
    p_grp = jax.nn.softmax(logits[:, :N_GROUPS], axis=-1)
    p_sel, grp = lax.top_k(p_grp, 1)
    fine = logits[:, N_GROUPS:N_GROUPS + N_EXPERTS].reshape(t, N_GROUPS, EXPERTS_PER_GROUP)
    fine = jnp.take_along_axis(fine, grp[:, :, None], axis=1)[:, 0]
    top_v, top_i = lax.top_k(fine, TOP_K)
    gate = p_sel * jax.nn.softmax(top_v, axis=-1)
    return grp * EXPERTS_PER_GROUP + top_i, gate


def _rank_kernel(e_ref, rank_ref, cnt_ref, run_scr):
    @pl.when(pl.program_id(0) == 0)
    def _():
        run_scr[...] = jnp.zeros_like(run_scr)

    tb = e_ref.shape[0]
    lane = lax.broadcasted_iota(jnp.int32, (tb, LANE), 1)
    onehot = jnp.where(e_ref[...] == lane, 1.0, 0.0)
    ri = lax.broadcasted_iota(jnp.int32, (tb, tb), 0)
    ci = lax.broadcasted_iota(jnp.int32, (tb, tb), 1)
    before = jnp.where(ri > ci, 1.0, 0.0).astype(BF16)
    prefix = _dot(before, onehot.astype(BF16)) + run_scr[...]
    rank_ref[...] = jnp.sum(prefix * onehot, axis=1, keepdims=True).astype(jnp.int32)
    run_scr[...] += jnp.sum(onehot, axis=0, keepdims=True)
    cnt_ref[...] = run_scr[...]


def _expert_ranks(flat_e):
    a = flat_e.shape[0]
    tb = 512
    rank, cnt = pl.pallas_call(
        _rank_kernel, grid=(a // tb,),
        in_specs=[pl.BlockSpec((tb, 1), lambda i: (i, 0))],
        out_specs=[pl.BlockSpec((tb, 1), lambda i: (i, 0)), pl.BlockSpec((1, LANE), lambda i: (0, 0))],
        out_shape=[jax.ShapeDtypeStruct((a, 1), jnp.int32), jax.ShapeDtypeStruct((1, LANE), F32)],
        scratch_shapes=[pltpu.VMEM((1, LANE), F32)],
        compiler_params=_params(("arbitrary",)), name="moe_rank",
    )(flat_e.reshape(a, 1))
    return rank.reshape(a), cnt[0, :N_EXPERTS].astype(jnp.int32)


def _dispatch_plan(expert, bm):
    t = expert.shape[0]
    a = t * TOP_K
    flat_e = expert.reshape(a).astype(jnp.int32)
    rank, counts = _expert_ranks(flat_e)
    padded = (counts + bm - 1) // bm * bm
    pad_end = jnp.cumsum(padded)
    pad_start = pad_end - padded
    pos = (pad_start[flat_e] + rank).astype(jnp.int32)
    n_blocks = a // bm + N_EXPERTS
    flat_tok = jnp.arange(a, dtype=jnp.int32) // TOP_K
    buf_tok = jnp.zeros((n_blocks * bm,), jnp.int32).at[pos].set(flat_tok)
    n_used = (pad_end[-1] // bm).astype(jnp.int32)
    blk = jnp.minimum(jnp.arange(n_blocks, dtype=jnp.int32), n_used - 1) * bm
    block_e = jnp.sum((pad_end[None, :] <= blk[:, None]).astype(jnp.int32), axis=1)
    block_e = jnp.minimum(block_e, N_EXPERTS - 1).astype(jnp.int32)
    return pos, buf_tok, block_e, n_used.reshape(1)


def _mix_stream(p3, rows, width, s0_f, s0_b, four_tabs, hy_tabs, hy_coefs, conv_w, conv_b, four_g,
                hy_bias, hy_g, a_log, dt_bias, with_mixers):
    ab = p3[..., PCOL_AB:PCOL_AB + N_AB]
    qkv = _short_conv(p3, conv_w, conv_b, rows, width, OFF_DN, HY_CONV_CH, 3 * DN_W, 4 * DN_DK, True)
    o_f, o_b, s_f, s_b = _delta_rule(qkv, ab, a_log, dt_bias, s0_f, s0_b)
    if not with_mixers:
        return None, None, o_f, o_b, s_f, s_b
    y_four = _fourier(p3, *four_tabs, four_g)
    convh = _short_conv(p3, conv_w, conv_b, rows, width, OFF_HY, 0, HY_CONV_CH, 256, False)
    y_hy = _hyena(convh, hy_tabs, hy_coefs, hy_bias, hy_g)
    return y_four, y_hy, o_f, o_b, s_f, s_b


def kernel(x, c, ctx, c_ctx, norm1_g, norm2_g, w_mod, b_mod, w_in, conv_w, conv_b, four_g, hy_w1, hy_b1, hy_w2, hy_b2, hy_w3, hy_b3, hy_w4, hy_freq, hy_bias, hy_g, dn_a_log, dn_dt_bias, dn_g, w_out, w_rc, b_rc, w_rf, b_rf, w_e_gate, w_e_up, w_e_down, final_g):
    bsz, length, d = x.shape
    lc = ctx.shape[1]
    depth = w_in.shape[0]
    rows = length // GRID_W
    bm = 256

    mod_all = _modulation(c, c_ctx, w_mod, b_mod)
    four_x, four_c = _fourier_tables(length), _fourier_tables(lc)
    hy_x, hy_c = _hyena_tables(length), _hyena_tables(lc)
    zeros = jnp.zeros((bsz, DN_HEADS, DN_DK, DN_DV), F32)
    x2 = x.reshape(bsz * length, d)
    c2 = ctx.reshape(bsz * lc, d)

    for l in range(depth):
        last = l == depth - 1
        mod = [m[:, None, :] for m in jnp.split(mod_all[l, :bsz], 6, axis=-1)]
        modc = [m[:, None, :] for m in jnp.split(mod_all[l, bsz:bsz + 1], 6, axis=-1)]
        wi = w_in[l]
        w_p = jnp.concatenate([wi[:, :OFF_AB], wi[:, OFF_Z:], wi[:, OFF_AB:OFF_Z],
                               jnp.zeros((d, LANE - N_AB), wi.dtype)], axis=1).astype(BF16)
        w_o = w_out[l].astype(BF16)
        hy_params = (hy_w1[l], hy_b1[l], hy_w2[l], hy_b2[l], hy_w3[l], hy_b3[l], hy_w4[l], hy_freq[l])
        mix_args = (conv_w[l], conv_b[l], four_g[l], hy_bias[l], hy_g[l], dn_a_log[l], dn_dt_bias[l])

        pc = _proj_in(c2, norm1_g[l], modc[0], modc[1], w_p, bsz * lc)
        pc3 = pc.reshape(bsz, lc, PROJ_W)
        coefs_c = None if last else _hyena_coefs(lc, hy_c, hy_params)
        yf, yh, o_f, o_b, s_f, s_b = _mix_stream(pc3, 1, lc, zeros, zeros, four_c, hy_c, coefs_c,
                                                 *mix_args, with_mixers=not last)
        if not last:
            c2 = _proj_out(yf.reshape(-1, FOUR_W), yh.reshape(-1, HY_W), o_f.reshape(-1, DN_W),
                           o_b.reshape(-1, DN_W), pc, dn_g[l], w_o, c2, modc[2], bsz * lc)

        p = _proj_in(x2, norm1_g[l], mod[0], mod[1], w_p, length)
        p3 = p.reshape(bsz, length, PROJ_W)
        coefs_x = _hyena_coefs(length, hy_x, hy_params)
        yf, yh, o_f, o_b, _, _ = _mix_stream(p3, rows, GRID_W, s_f, s_b, four_x, hy_x, coefs_x,
                                             *mix_args, with_mixers=True)
        x2 = _proj_out(yf.reshape(-1, FOUR_W), yh.reshape(-1, HY_W), o_f.reshape(-1, DN_W),
                       o_b.reshape(-1, DN_W), p, dn_g[l], w_o, x2, mod[2], length)

        w_r = jnp.concatenate([w_rc[l], w_rf[l], jnp.zeros((d, ROUTER_W - N_GROUPS - N_EXPERTS), F32)], axis=1)
        b_r = jnp.concatenate([b_rc[l], b_rf[l], jnp.zeros((ROUTER_W - N_GROUPS - N_EXPERTS,), F32)])[None, :]
        h, logits = _router(x2, c2, norm2_g[l], mod[3], mod[4], modc[3], modc[4], w_r, b_r, length,
                            with_ctx=not last)
        expert, gate = _route(logits)
        pos, buf_tok, block_e, n_used = _dispatch_plan(expert, bm)
        ys = _experts(h, buf_tok, block_e, n_used, w_e_gate[l].astype(BF16), w_e_up[l].astype(BF16),
                      w_e_down[l].astype(BF16), bm)
        x2 = _combine(ys, pos, gate, x2, mod[5], final_g, 0, length, final=last)
        if not last:
            c2 = _combine(ys, pos, gate, c2, modc[5], final_g, bsz * length, bsz * lc, final=False)
    return x2.reshape(bsz, length, d)
```

```python
import functools
import math

import jax
import jax.numpy as jnp
from jax import lax
from jax.experimental import pallas as pl
from jax.experimental.pallas import tpu as pltpu

F32 = jnp.float32
BF16 = jnp.bfloat16
HIGHEST = lax.Precision.HIGHEST

GRID_W = 64
FOUR_W = 512
FOUR_GROUPS = 4
FOUR_GW = FOUR_W // FOUR_GROUPS
HY_W = 512
HY_ORDER = 2
DN_HEADS = 8
DN_DK = 128
DN_DV = 128
DN_W = DN_HEADS * DN_DV
DN_CHUNK = 64
DN_SOLVE_BLOCK = DN_CHUNK // 4
HY_CONV_CH = (HY_ORDER + 1) * HY_W
OFF_HY = FOUR_W
OFF_DN = OFF_HY + HY_CONV_CH
OFF_AB = OFF_DN + 3 * DN_W
N_AB = 4 * DN_HEADS
OFF_Z = OFF_AB + N_AB
HY_EMB = 33
HY_FAST_DECAY = 0.3
HY_SLOW_DECAY = 1.5
HY_TARGET = 1e-2
N_GROUPS = 4
EXPERTS_PER_GROUP = 8
N_EXPERTS = N_GROUPS * EXPERTS_PER_GROUP
TOP_K = 2
EPS = 1e-6

LANE = 128
PCOL_Z = OFF_AB
PCOL_AB = PCOL_Z + DN_W
PROJ_W = PCOL_AB + LANE
ROUTER_W = LANE
VMEM_LIMIT = 56 * 1024 * 1024


def _params(semantics):
    return pltpu.CompilerParams(dimension_semantics=semantics, vmem_limit_bytes=VMEM_LIMIT)


def _silu(v):
    return v * jax.nn.sigmoid(v)


def _rms(v, g):
    return v * lax.rsqrt(jnp.mean(v * v, axis=-1, keepdims=True) + EPS) * g


def _dot(a, b):
    return jnp.dot(a, b, preferred_element_type=F32)


def _mod_kernel(a_ref, w_ref, b_ref, o_ref):
    o_ref[0] = jnp.dot(_silu(a_ref[...]), w_ref[0], precision=HIGHEST,
                       preferred_element_type=F32) + b_ref[0]


def _modulation(c, c_ctx, w_mod, b_mod):
    depth, d, n = w_mod.shape
    bsz = c.shape[0]
    rows = -(-(bsz + 1) // 8) * 8
    a = jnp.concatenate([c, c_ctx[None], jnp.zeros((rows - bsz - 1, d), F32)], axis=0)
    tn = 1024 if n % 1024 == 0 else 512
    assert n % tn == 0
    return pl.pallas_call(
        _mod_kernel, grid=(depth, n // tn),
        in_specs=[pl.BlockSpec((rows, d), lambda l, j: (0, 0)),
                  pl.BlockSpec((1, d, tn), lambda l, j: (l, 0, j)),
                  pl.BlockSpec((1, 1, tn), lambda l, j: (l, 0, j))],
        out_specs=pl.BlockSpec((1, rows, tn), lambda l, j: (l, 0, j)),
        out_shape=jax.ShapeDtypeStruct((depth, rows, n), F32),
        compiler_params=_params(("parallel", "parallel")), name="modulation",
    )(a, w_mod, b_mod.reshape(depth, 1, n))


def _proj_in_kernel(x_ref, g_ref, shift_ref, scale_ref, w_ref, o_ref, a_scr):
    @pl.when(pl.program_id(1) == 0)
    def _():
        y = _rms(x_ref[...], g_ref[...])
        a_scr[...] = (y * (1.0 + scale_ref[0]) + shift_ref[0]).astype(BF16)

    o_ref[...] = _dot(a_scr[...], w_ref[...])


def _proj_in(x2d, g, shift, scale, w, rows_per_mod):
    m, d = x2d.shape
    n = w.shape[1]
    tm = min(1024, m, rows_per_mod)
    tn = 896 if n % 896 == 0 else n
    per = rows_per_mod // tm
    return pl.pallas_call(
        _proj_in_kernel, grid=(m // tm, n // tn),
        in_specs=[pl.BlockSpec((tm, d), lambda i, j: (i, 0)),
                  pl.BlockSpec((1, d), lambda i, j: (0, 0)),
                  pl.BlockSpec((1, 1, d), lambda i, j: (i // per, 0, 0)),
                  pl.BlockSpec((1, 1, d), lambda i, j: (i // per, 0, 0)),
                  pl.BlockSpec((d, tn), lambda i, j: (0, j))],
        out_specs=pl.BlockSpec((tm, tn), lambda i, j: (i, j)),
        out_shape=jax.ShapeDtypeStruct((m, n), F32),
        scratch_shapes=[pltpu.VMEM((tm, d), BF16)],
        compiler_params=_params(("parallel", "arbitrary")), name="proj_in",
    )(x2d, g.reshape(1, d), shift, scale, w)


def _cos_sin_table(nrow, ncol, period):
    i = lax.broadcasted_iota(jnp.int32, (nrow, ncol), 0)
    j = lax.broadcasted_iota(jnp.int32, (nrow, ncol), 1)
    ang = ((i * j) % period).astype(F32) * (2.0 * math.pi / period)
    return jnp.cos(ang), jnp.sin(ang)


def _dft_tables(length, period):
    blk = 64 if length % 64 == 0 else length
    hi_i = lax.broadcasted_iota(jnp.int32, (length // blk, length), 0) * blk
    j = lax.broadcasted_iota(jnp.int32, (length // blk, length), 1)
    ang = ((hi_i * j) % period).astype(F32) * (2.0 * math.pi / period)
    hi_c, hi_s = jnp.cos(ang), jnp.sin(ang)
    lo_c, lo_s = _cos_sin_table(blk, length, period)
    c = hi_c[:, None, :] * lo_c[None, :, :] - hi_s[:, None, :] * lo_s[None, :, :]
    s = hi_s[:, None, :] * lo_c[None, :, :] + hi_c[:, None, :] * lo_s[None, :, :]
    return c.reshape(length, length), s.reshape(length, length)


def _fourier_kernel(u_ref, c_ref, s_ref, bc_ref, bs_ref, g_ref, o_ref, ub_scr, *, scale):
    @pl.when(pl.program_id(1) == 0)
    def _():
        ub_scr[...] = u_ref[0].astype(BF16)

    ub = ub_scr[...]
    p = _dot(c_ref[...], ub).astype(BF16)
    q = _dot(s_ref[...], ub).astype(BF16)
    z = (_dot(p, bc_ref[...]) - _dot(q, bs_ref[...])) * scale
    o_ref[0] = _rms(z, g_ref[...])


def _fourier(p3, cl, sl, bc, bs, four_g):
    bsz, length, _ = p3.shape
    tm = min(512, length)
    kern = functools.partial(_fourier_kernel, scale=1.0 / math.sqrt(length * FOUR_GW))
    return pl.pallas_call(
        kern, grid=(bsz, length // tm),
        in_specs=[pl.BlockSpec((1, length, FOUR_W), lambda b, i: (b, 0, 0)),
                  pl.BlockSpec((tm, length), lambda b, i: (i, 0)),
                  pl.BlockSpec((tm, length), lambda b, i: (i, 0)),
                  pl.BlockSpec((FOUR_W, FOUR_W), lambda b, i: (0, 0)),
                  pl.BlockSpec((FOUR_W, FOUR_W), lambda b, i: (0, 0)),
                  pl.BlockSpec((1, FOUR_W), lambda b, i: (0, 0))],
        out_specs=pl.BlockSpec((1, tm, FOUR_W), lambda b, i: (b, i, 0)),
        out_shape=jax.ShapeDtypeStruct((bsz, length, FOUR_W), F32),
        scratch_shapes=[pltpu.VMEM((length, FOUR_W), BF16)],
        compiler_params=_params(("parallel", "arbitrary")), name="fourier",
    )(p3, cl, sl, bc, bs, four_g.reshape(1, FOUR_W))


def _fourier_tables(length):
    cl, sl = _dft_tables(length, length)
    cc, sc = _cos_sin_table(FOUR_GW, FOUR_GW, FOUR_GW)
    eye = jnp.eye(FOUR_GROUPS, dtype=F32)
    return cl.astype(BF16), sl.astype(BF16), jnp.kron(eye, cc).astype(BF16), jnp.kron(eye, sc).astype(BF16)


def _conv_kernel(x_ref, w_ref, b_ref, o_ref, *, rows, width, heads_mode):
    tc = x_ref.shape[2]
    wts = w_ref[...]
    bias = b_ref[...]
    pos = lax.broadcasted_iota(jnp.int32, (width, tc), 0)
    not_first = pos > 0
    not_last = pos < width - 1
    kind = pl.program_id(1) // (DN_W // tc)

    def body(r, carry):
        base = pl.multiple_of(r * width, width)
        cen = x_ref[0, pl.ds(base, width), :]
        if rows > 1:
            up = x_ref[0, pl.ds(pl.multiple_of(jnp.maximum(r - 1, 0) * width, width), width), :]
            dn = x_ref[0, pl.ds(pl.multiple_of(jnp.minimum(r + 1, rows - 1) * width, width), width), :]
            w_up = wts[0] * jnp.where(r > 0, 1.0, 0.0)
            w_dn = wts[2] * jnp.where(r < rows - 1, 1.0, 0.0)
            col = lambda j: up * w_up[j:j + 1, :] + cen * wts[1, j:j + 1, :] + dn * w_dn[j:j + 1, :]
        else:
            col = lambda j: cen * wts[1, j:j + 1, :]
        left = jnp.where(not_first, pltpu.roll(col(0), 1, 0), 0.0)
        right = jnp.where(not_last, pltpu.roll(col(2), width - 1, 0), 0.0)
        acc = left + col(1) + right + bias
        if heads_mode:
            for h in range(tc // DN_DK):
                s = _silu(acc[:, h * DN_DK:(h + 1) * DN_DK])
                inv = lax.rsqrt(jnp.sum(s * s, axis=-1, keepdims=True) + EPS)
                fac = jnp.where(kind == 0, inv * (DN_DK ** -0.5), jnp.where(kind == 1, inv, 1.0))
                o_ref[0, pl.ds(base, width), h * DN_DK:(h + 1) * DN_DK] = s * fac
        else:
            o_ref[0, pl.ds(base, width), :] = acc
        return carry

    lax.fori_loop(0, rows, body, 0)


def _short_conv(p3, conv_w, conv_b, rows, width, col0, ch0, nch, tc, heads_mode):
    bsz, length, _ = p3.shape
    kern = functools.partial(_conv_kernel, rows=rows, width=width, heads_mode=heads_mode)
    cb, wb = col0 // tc, ch0 // tc
    return pl.pallas_call(
        kern, grid=(bsz, nch // tc),
        in_specs=[pl.BlockSpec((1, length, tc), lambda b, j: (b, 0, cb + j)),
                  pl.BlockSpec((3, 3, tc), lambda b, j: (0, 0, wb + j)),
                  pl.BlockSpec((1, tc), lambda b, j: (0, wb + j))],
        out_specs=pl.BlockSpec((1, length, tc), lambda b, j: (b, 0, j)),
        out_shape=jax.ShapeDtypeStruct((bsz, length, nch), F32),
        compiler_params=_params(("parallel", "parallel")),
        name="conv_heads" if heads_mode else "conv_hyena",
    )(p3, conv_w, conv_b.reshape(1, -1))


def _filt_kernel(z_ref, w1_ref, b1_ref, w2_ref, b2_ref, w3_ref, b3_ref, w4_ref, fr_ref, dl_ref,
                 g_ref, nrm_ref):
    i = pl.program_id(0)
    z = z_ref[...]
    fr = fr_ref[...]
    hdot = lambda a, b: jnp.dot(a, b, precision=HIGHEST, preferred_element_type=F32)
    h = jnp.sin(fr * (hdot(z, w1_ref[...]) + b1_ref[...]))
    h = jnp.sin(fr * (hdot(h, w2_ref[...]) + b2_ref[...]))
    h = jnp.sin(fr * (hdot(h, w3_ref[...]) + b3_ref[...]))
    h = hdot(h, w4_ref[...])
    decay = jnp.exp(-z[:, 0:1] * jnp.abs(dl_ref[...]))
    decay = jnp.concatenate([decay] * HY_ORDER, axis=1)
    half = HY_ORDER * HY_W
    hf = h[:, :half] * decay
    hb = h[:, half:] * decay
    row = lax.broadcasted_iota(jnp.int32, hb.shape, 0) + i * hb.shape[0]
    hb = jnp.where(row > 0, hb, 0.0)
    gp = hf + hb
    gm = hf - hb
    for o in range(HY_ORDER):
        g_ref[o] = gp[:, o * HY_W:(o + 1) * HY_W].astype(BF16)
        g_ref[HY_ORDER + o] = gm[:, o * HY_W:(o + 1) * HY_W].astype(BF16)
    part = jnp.sum(jnp.abs(hf) + jnp.abs(hb), axis=0, keepdims=True)

    @pl.when(i == 0)
    def _():
        nrm_ref[...] = part

    @pl.when(i > 0)
    def _():
        nrm_ref[...] += part


def _hyena_filter_taps(length, w1, b1, w2, b2, w3, b3, w4, freq):
    t = jnp.linspace(0.0, 1.0, length, dtype=F32)[:, None]
    bands = (HY_EMB - 1) // 2
    ang = (2.0 * math.pi / length) * jnp.arange(length, dtype=F32)[:, None]
    f = jnp.linspace(1e-4, bands - 1, bands, dtype=F32)
    z = jnp.concatenate([t, jnp.cos(f * ang), -jnp.sin(f * ang)], axis=-1)
    max_decay = math.log(HY_TARGET) / HY_FAST_DECAY
    min_decay = math.log(HY_TARGET) / HY_SLOW_DECAY
    deltas = jnp.linspace(min_decay, max_decay, HY_W, dtype=F32)[None, :]
    fw = w1.shape[1]
    tl = min(512, length)
    full = lambda shape: pl.BlockSpec(shape, lambda i: (0,) * len(shape))
    return pl.pallas_call(
        _filt_kernel, grid=(length // tl,),
        in_specs=[pl.BlockSpec((tl, HY_EMB), lambda i: (i, 0)),
                  full((HY_EMB, fw)), full((1, fw)), full((fw, fw)), full((1, fw)),
                  full((fw, fw)), full((1, fw)), full((fw, 2 * HY_ORDER * HY_W)), full((1, fw)),
                  full((1, HY_W))],
        out_specs=[pl.BlockSpec((2 * HY_ORDER, tl, HY_W), lambda i: (0, i, 0)),
                   pl.BlockSpec((1, HY_ORDER * HY_W), lambda i: (0, 0))],
        out_shape=[jax.ShapeDtypeStruct((2 * HY_ORDER, length, HY_W), BF16),
                   jax.ShapeDtypeStruct((1, HY_ORDER * HY_W), F32)],
        compiler_params=_params(("arbitrary",)), name="hyena_filter",
    )(z, w1, b1.reshape(1, fw), w2, b2.reshape(1, fw), w3, b3.reshape(1, fw), w4,
      freq.reshape(1, fw), deltas)


def _dft_fwd_kernel(u_ref, c_ref, s_ref, *rest, with_coef):
    if with_coef:
        c1_ref, c2_ref, c4_ref, yr_ref, yi_ref, ub_scr = rest
    else:
        yr_ref, yi_ref, ub_scr = rest

    @pl.when(pl.program_id(1) == 0)
    def _():
        ub_scr[...] = u_ref[0].astype(BF16)

    ub = ub_scr[...]
    a = _dot(c_ref[...], ub)
    b = _dot(s_ref[...], ub)
    if with_coef:
        c2 = c2_ref[...]
        yr_ref[0] = (a * c1_ref[...] + b * c2).astype(BF16)
        yi_ref[0] = (b * c4_ref[...] - a * c2).astype(BF16)
    else:
        yr_ref[0] = a
        yi_ref[0] = b


def _dft_fwd(u3, col_blk, cf, sf, coef=None):
    nb, length, _ = u3.shape
    tm = min(512, length)
    in_specs = [pl.BlockSpec((1, length, HY_W), lambda b, i: (b, 0, col_blk)),
                pl.BlockSpec((tm, length), lambda b, i: (i, 0)),
                pl.BlockSpec((tm, length), lambda b, i: (i, 0))]
    args = [u3, cf, sf]
    if coef is not None:
        in_specs += [pl.BlockSpec((tm, HY_W), lambda b, i: (i, 0))] * 3
        args += list(coef)
    odt = BF16 if coef is not None else F32
    return pl.pallas_call(
        functools.partial(_dft_fwd_kernel, with_coef=coef is not None), grid=(nb, length // tm),
        in_specs=in_specs,
        out_specs=[pl.BlockSpec((1, tm, HY_W), lambda b, i: (b, i, 0))] * 2,
        out_shape=[jax.ShapeDtypeStruct((nb, length, HY_W), odt)] * 2,
        scratch_shapes=[pltpu.VMEM((length, HY_W), BF16)],
        compiler_params=_params(("parallel", "arbitrary")),
        name="hyena_dft_fwd" if coef is not None else "hyena_filter_dft",
    )(*args)


def _dft_inv_kernel(yr_ref, yi_ref, c_ref, s_ref, u_ref, gate_ref, bias_ref, g_ref, o_ref, *, final):
    y = _dot(c_ref[...], yr_ref[0]) + _dot(s_ref[...], yi_ref[0])
    out = gate_ref[0] * (y + bias_ref[...] * u_ref[0])
    if final:
        out = _rms(out, g_ref[...])
    o_ref[0] = out


def _dft_inv(yr, yi, cf, s_inv, u3, u_blk, gate3, gate_blk, bias, g, final):
    bsz, length, _ = yr.shape
    tm = min(512, length)
    return pl.pallas_call(
        functools.partial(_dft_inv_kernel, final=final), grid=(bsz, length // tm),
        in_specs=[pl.BlockSpec((1, length, HY_W), lambda b, i: (b, 0, 0)),
                  pl.BlockSpec((1, length, HY_W), lambda b, i: (b, 0, 0)),
                  pl.BlockSpec((tm, length), lambda b, i: (i, 0)),
                  pl.BlockSpec((tm, length), lambda b, i: (i, 0)),
                  pl.BlockSpec((1, tm, HY_W), lambda b, i: (b, i, u_blk)),
                  pl.BlockSpec((1, tm, HY_W), lambda b, i: (b, i, gate_blk)),
                  pl.BlockSpec((1, HY_W), lambda b, i: (0, 0)),
                  pl.BlockSpec((1, HY_W), lambda b, i: (0, 0))],
        out_specs=pl.BlockSpec((1, tm, HY_W), lambda b, i: (b, i, 0)),
        out_shape=jax.ShapeDtypeStruct((bsz, length, HY_W), F32),
        compiler_params=_params(("parallel", "arbitrary")), name="hyena_dft_inv",
    )(yr, yi, cf, s_inv, u3, gate3, bias.reshape(1, HY_W), g.reshape(1, HY_W))


def _hyena_tables(length):
    cf, sf = _dft_tables(length, 2 * length)
    alt = jnp.where(jnp.arange(length) % 2 == 0, 1.0, -1.0).astype(F32)
    sf = sf.at[0, :].set(alt)
    return cf.astype(BF16), sf.astype(BF16), sf.T.astype(BF16)


def _hyena_coefs(length, tables, hy_params):
    cf, sf, _ = tables
    taps, nrm = _hyena_filter_taps(length, *hy_params)
    a, b = _dft_fwd(taps, 0, cf, sf)
    inv = (1.0 / nrm).reshape(HY_ORDER, 1, HY_W)
    n = 2.0 * length
    hr = a[:HY_ORDER] * inv
    hi = -b[HY_ORDER:] * inv
    nyq = b[:HY_ORDER, 0:1, :] * inv
    first = (jnp.arange(length) == 0)[None, :, None]
    c1 = jnp.where(first, hr / n, hr * (2.0 / n))
    c2 = jnp.where(first, 0.0, hi * (2.0 / n))
    c4 = jnp.where(first, nyq / n, hr * (2.0 / n))
    return c1, c2, c4


def _hyena(convh, tables, coefs, hy_bias, hy_g):
    cf, sf, s_inv = tables
    c1, c2, c4 = coefs
    yr, yi = _dft_fwd(convh, 0, cf, sf, (c1[0], c2[0], c4[0]))
    zz = _dft_inv(yr, yi, cf, s_inv, convh, 0, convh, 1, hy_bias[0], hy_g, False)
    yr, yi = _dft_fwd(zz, 0, cf, sf, (c1[1], c2[1], c4[1]))
    return _dft_inv(yr, yi, cf, s_inv, zz, 0, convh, 2, hy_bias[1], hy_g, True)


def _softplus(v):
    return jnp.maximum(v, 0.0) + jnp.log(1.0 + jnp.exp(-jnp.abs(v)))


def _delta_prep_kernel(qkv_ref, ab_ref, abt_ref, arow_ref, drow_ref, acol_ref, dcol_ref,
                       wq_ref, u_ref, ak_ref, egl_ref):
    c = ab_ref.shape[1]
    nh = DN_HEADS
    ri = lax.broadcasted_iota(jnp.int32, (c, c), 0)
    ci = lax.broadcasted_iota(jnp.int32, (c, c), 1)
    incl = (ri >= ci, ri <= ci)
    strict = (ri > ci, ri < ci)
    low = jnp.where(incl[0], 1.0, 0.0)
    upp = jnp.where(incl[1], 1.0, 0.0)
    hdot = lambda a, b: jnp.dot(a, b, precision=HIGHEST, preferred_element_type=F32)
    ab = ab_ref[0]
    abt = abt_ref[0, 0]
    gate = -arow_ref[...] * _softplus(ab + drow_ref[...])
    gate_t = -acol_ref[...] * _softplus(abt + dcol_ref[...])
    beta_all = jax.nn.sigmoid(ab)
    gcum = (hdot(low, gate), hdot(upp, gate))
    gcum_t = (hdot(gate_t, upp), hdot(gate_t, low))
    er = lax.broadcasted_iota(jnp.int32, (DN_DK, DN_DK), 0)
    ec = lax.broadcasted_iota(jnp.int32, (DN_DK, DN_DK), 1)
    eye16 = jnp.where(er == ec, 1.0, 0.0).astype(BF16)
    nt = (((1,), (1,)), ((), ()))
    q = [qkv_ref[0, :, h * DN_DK:(h + 1) * DN_DK] for h in range(nh)]
    k = [qkv_ref[0, :, DN_W + h * DN_DK:DN_W + (h + 1) * DN_DK] for h in range(nh)]
    v = [qkv_ref[0, :, 2 * DN_W + h * DN_DV:2 * DN_W + (h + 1) * DN_DV] for h in range(nh)]
    k16 = [t.astype(BF16) for t in k]
    kk0 = [lax.dot_general(k16[h], k16[h], nt, preferred_element_type=F32) for h in range(nh)]
    qk0 = [lax.dot_general(q[h].astype(BF16), k16[h], nt, preferred_element_type=F32) for h in range(nh)]
    units = [(d, h) for d in range(2) for h in range(nh)]
    mm, rr = {}, {}
    for d, h in units:
        col = d * nh + h
        gc = gcum[d][:, col:col + 1]
        gr = gcum_t[d][col:col + 1, :]
        be = beta_all[:, 2 * nh + col:2 * nh + col + 1]
        last = 0 if d else c - 1
        gl = gc[last:last + 1, :]
        dec = jnp.where(incl[d], jnp.exp(jnp.where(incl[d], gc - gr, 0.0)), 0.0)
        mm[d, h] = jnp.where(strict[d], be * kk0[h] * dec, 0.0)
        ak_ref[0, 0, d, h, 0:c, :] = jnp.where(incl[d], qk0[h] * dec, 0.0).astype(BF16)
        eg = jnp.exp(gc)
        rr[d, h] = jnp.concatenate([v[h] * be, k[h] * (be * eg)], axis=1)
        wq_ref[0, 0, d, h, c:2 * c, :] = (q[h] * eg).astype(BF16)
        k_tail = (k[h] * jnp.exp(gl - gc)).astype(BF16)
        ak_ref[0, 0, d, h, c:c + DN_DK, :] = lax.dot_general(
            eye16, k_tail, nt, preferred_element_type=F32).astype(BF16)
        egl_ref[0, 0, d, h:h + 1, :] = jnp.broadcast_to(jnp.exp(gl), (1, DN_DV))
    sb = DN_SOLVE_BLOCK
    in_sb = (ri // sb) == (ci // sb)
    in_2sb = (ri // (2 * sb)) == (ci // (2 * sb))
    eye = jnp.where(ri == ci, 1.0, 0.0)
    b16 = lambda t: t.astype(BF16)
    nj = {u: jnp.where(in_sb, -mm[u], 0.0) for u in units}
    inv = {u: eye + nj[u] for u in units}
    for j in range((sb - 1).bit_length() - 1):
        nj = {u: _dot(b16(nj[u]), b16(nj[u])) for u in units}
        inv = {u: inv[u] + _dot(b16(inv[u]), b16(nj[u])) for u in units}
    off = {u: b16(jnp.where(in_2sb & ~in_sb, mm[u], 0.0)) for u in units}
    tmp = {u: _dot(off[u], b16(inv[u])) for u in units}
    inv = {u: inv[u] - _dot(b16(inv[u]), b16(tmp[u])) for u in units}
    inv16 = {u: b16(inv[u]) for u in units}
    off = {u: b16(jnp.where(in_2sb, 0.0, mm[u])) for u in units}
    part = {u: _dot(inv16[u], b16(rr[u])) for u in units}
    tmp = {u: _dot(off[u], b16(part[u])) for u in units}
    rr = {u: part[u] - _dot(inv16[u], b16(tmp[u])) for u in units}
    for d, h in units:
        u_ref[0, 0, d, h] = rr[d, h][:, :DN_DV]
        wq_ref[0, 0, d, h, 0:c, :] = rr[d, h][:, DN_DV:].astype(BF16)


def _delta_scan_kernel(wqf_ref, uf_ref, akf_ref, egf_ref, wqb_ref, ub_ref, akb_ref, egb_ref, s0f_ref, s0b_ref,
                       of_ref, ob_ref, sf_ref, sb_ref, s_scr):
    i = pl.program_id(1)
    bpb = uf_ref.shape[0]
    c = uf_ref.shape[-2]

    @pl.when(i == 0)
    def _():
        s_scr[0] = s0f_ref[...]
        s_scr[1] = s0b_ref[...]

    refs = ((wqf_ref, uf_ref, akf_ref, egf_ref, of_ref), (wqb_ref, ub_ref, akb_ref, egb_ref, ob_ref))
    units = [(d, bb, h) for d in range(2) for bb in range(bpb) for h in range(DN_HEADS)]
    s = {u: s_scr[u] for u in units}
    ws = {(d, bb, h): _dot(refs[d][0][bb, 0, 0, h], s[d, bb, h].astype(BF16)) for d, bb, h in units}
    v16 = {(d, bb, h): (refs[d][1][bb, 0, 0, h] - ws[d, bb, h][:c]).astype(BF16) for d, bb, h in units}
    av = {(d, bb, h): _dot(refs[d][2][bb, 0, 0, h], v16[d, bb, h]) for d, bb, h in units}
    for d, bb, h in units:
        u = (d, bb, h)
        refs[d][4][bb, :, h * DN_DV:(h + 1) * DN_DV] = ws[u][c:] + av[u][:c]
        s_scr[u] = s[u] * refs[d][3][bb, 0, 0, h:h + 1, :] + av[u][c:]

    @pl.when(i == pl.num_programs(1) - 1)
    def _():
        sf_ref[...] = s_scr[0]
        sb_ref[...] = s_scr[1]


def _delta_rule(qkv, ab, a_log, dt_bias, s0_f, s0_b, bpb=1):
    bsz, length, _ = qkv.shape
    c = DN_CHUNK
    n = length // c
    nh = DN_HEADS
    abt = jnp.swapaxes(ab.reshape(bsz, n, c, 4 * nh), 2, 3)
    zeros = jnp.zeros((2 * nh,), F32)
    a_vec = jnp.concatenate([jnp.exp(a_log.astype(F32)).reshape(-1), zeros])
    d_vec = jnp.concatenate([dt_bias.astype(F32).reshape(-1), zeros])
    small = lambda shape: pl.BlockSpec(shape, lambda b, i: (0,) * len(shape))
    per_chunk = lambda *tail: pl.BlockSpec((1, 1, 2, nh) + tail, lambda b, i: (b, i, 0, 0) + (0,) * len(tail))
    wq, u, ak, egl = pl.pallas_call(
        _delta_prep_kernel, grid=(bsz, n),
        in_specs=[pl.BlockSpec((1, c, 3 * DN_W), lambda b, i: (b, i, 0)),
                  pl.BlockSpec((1, c, 4 * nh), lambda b, i: (b, i, 0)),
                  pl.BlockSpec((1, 1, 4 * nh, c), lambda b, i: (b, i, 0, 0)),
                  small((1, 4 * nh)), small((1, 4 * nh)), small((4 * nh, 1)), small((4 * nh, 1))],
        out_specs=[per_chunk(2 * c, DN_DK), per_chunk(c, DN_DV), per_chunk(c + DN_DK, c),
                   pl.BlockSpec((1, 1, 2, nh, DN_DV), lambda b, i: (b, i, 0, 0, 0))],
        out_shape=[jax.ShapeDtypeStruct((bsz, n, 2, nh, 2 * c, DN_DK), BF16),
                   jax.ShapeDtypeStruct((bsz, n, 2, nh, c, DN_DV), F32),
                   jax.ShapeDtypeStruct((bsz, n, 2, nh, c + DN_DK, c), BF16),
                   jax.ShapeDtypeStruct((bsz, n, 2, nh, DN_DV), F32)],
        compiler_params=_params(("parallel", "parallel")), name="delta_prep",
    )(qkv, ab, abt, a_vec.reshape(1, -1), d_vec.reshape(1, -1), a_vec.reshape(-1, 1), d_vec.reshape(-1, 1))

    def side(d, *tail):
        idx = (lambda b, i: (b, n - 1 - i, 1, 0) + (0,) * len(tail)) if d else (
            lambda b, i: (b, i, 0, 0) + (0,) * len(tail))
        return pl.BlockSpec((bpb, 1, 1, nh) + tail, idx)

    def side_specs(d):
        return [side(d, 2 * c, DN_DK), side(d, c, DN_DV), side(d, c + DN_DK, c),
                pl.BlockSpec((bpb, 1, 1, nh, DN_DV),
                             (lambda b, i: (b, n - 1 - i, 1, 0, 0)) if d else (lambda b, i: (b, i, 0, 0, 0)))]

    st_spec = pl.BlockSpec((bpb, nh, DN_DK, DN_DV), lambda b, i: (b, 0, 0, 0))
    o_shape = jax.ShapeDtypeStruct((bsz, length, DN_W), F32)
    s_shape = jax.ShapeDtypeStruct((bsz, nh, DN_DK, DN_DV), F32)
    return pl.pallas_call(
        _delta_scan_kernel, grid=(bsz // bpb, n),
        in_specs=side_specs(0) + side_specs(1) + [st_spec, st_spec],
        out_specs=[pl.BlockSpec((bpb, c, DN_W), lambda b, i: (b, i, 0)),
                   pl.BlockSpec((bpb, c, DN_W), lambda b, i: (b, n - 1 - i, 0)), st_spec, st_spec],
        out_shape=[o_shape, o_shape, s_shape, s_shape],
        scratch_shapes=[pltpu.VMEM((2, bpb, nh, DN_DK, DN_DV), F32)],
        compiler_params=_params(("parallel", "arbitrary")), name="delta_scan",
    )(wq, u, ak, egl, wq, u, ak, egl, s0_f, s0_b)


def _proj_out_kernel(yf_ref, yh_ref, of_ref, ob_ref, z_ref, dg_ref, w_ref, res_ref, gate_ref, o_ref, a_scr):
    @pl.when(pl.program_id(1) == 0)
    def _():
        a_scr[:, 0:FOUR_W] = yf_ref[...].astype(BF16)
        a_scr[:, FOUR_W:FOUR_W + HY_W] = yh_ref[...].astype(BF16)
        o = of_ref[...] + ob_ref[...]
        z = z_ref[...]
        for h in range(DN_HEADS):
            lanes = slice(h * DN_DV, (h + 1) * DN_DV)
            y = _rms(o[:, lanes], dg_ref[...]) * _silu(z[:, lanes])
            a_scr[:, FOUR_W + HY_W + h * DN_DV:FOUR_W + HY_W + (h + 1) * DN_DV] = y.astype(BF16)

    o_ref[...] = res_ref[...] + gate_ref[0] * _dot(a_scr[...], w_ref[...])


def _proj_out(y_four, y_hy, o_f, o_b, p2d, dn_g, w, res, gate, rows_per_mod):
    m, d = res.shape
    tm = min(512, m)
    tn = min(1024, d)
    per = rows_per_mod // tm
    zb = PCOL_Z // DN_W
    return pl.pallas_call(
        _proj_out_kernel, grid=(m // tm, d // tn),
        in_specs=[pl.BlockSpec((tm, FOUR_W), lambda i, j: (i, 0)),
                  pl.BlockSpec((tm, HY_W), lambda i, j: (i, 0)),
                  pl.BlockSpec((tm, DN_W), lambda i, j: (i, 0)),
                  pl.BlockSpec((tm, DN_W), lambda i, j: (i, 0)),
                  pl.BlockSpec((tm, DN_W), lambda i, j: (i, zb)),
                  pl.BlockSpec((1, DN_DV), lambda i, j: (0, 0)),
                  pl.BlockSpec((w.shape[0], tn), lambda i, j: (0, j)),
                  pl.BlockSpec((tm, tn), lambda i, j: (i, j)),
                  pl.BlockSpec((1, 1, tn), lambda i, j: (i // per, 0, j))],
        out_specs=pl.BlockSpec((tm, tn), lambda i, j: (i, j)),
        out_shape=jax.ShapeDtypeStruct((m, d), F32),
        scratch_shapes=[pltpu.VMEM((tm, w.shape[0]), BF16)],
        compiler_params=_params(("parallel", "arbitrary")), name="proj_out",
    )(y_four, y_hy, o_f, o_b, p2d, dn_g.reshape(1, DN_DV), w, res, gate)


def _router_kernel(x_ref, c_ref, g_ref, sx_ref, cx_ref, sc_ref, cc_ref, w_ref, b_ref, h_ref, lg_ref, *, n_x):
    i = pl.program_id(0)

    def emit(v, shift, scale):
        h = _rms(v, g_ref[...]) * (1.0 + scale) + shift
        h_ref[...] = h
        lg_ref[...] = jnp.dot(h, w_ref[...], precision=HIGHEST, preferred_element_type=F32) + b_ref[...]

    @pl.when(i < n_x)
    def _():
        emit(x_ref[...], sx_ref[0], cx_ref[0])

    @pl.when(i >= n_x)
    def _():
        emit(c_ref[...], sc_ref[0], cc_ref[0])


def _router(x2d, c2d, g, shift_x, scale_x, shift_c, scale_c, w_r, b_r, rows_per_mod, with_ctx):
    mx, d = x2d.shape
    tm = 512
    n_x = mx // tm
    n_c = c2d.shape[0] // tm if with_ctx else 0
    per = rows_per_mod // tm
    xi = lambda i: (jnp.minimum(i, n_x - 1), 0)
    ci = lambda i: (jnp.maximum(i - n_x, 0), 0)
    return pl.pallas_call(
        functools.partial(_router_kernel, n_x=n_x), grid=(n_x + n_c,),
        in_specs=[pl.BlockSpec((tm, d), xi), pl.BlockSpec((tm, d), ci),
                  pl.BlockSpec((1, d), lambda i: (0, 0)),
                  pl.BlockSpec((1, 1, d), lambda i: (jnp.minimum(i, n_x - 1) // per, 0, 0)),
                  pl.BlockSpec((1, 1, d), lambda i: (jnp.minimum(i, n_x - 1) // per, 0, 0)),
                  pl.BlockSpec((1, 1, d), lambda i: (0, 0, 0)),
                  pl.BlockSpec((1, 1, d), lambda i: (0, 0, 0)),
                  pl.BlockSpec((d, ROUTER_W), lambda i: (0, 0)),
                  pl.BlockSpec((1, ROUTER_W), lambda i: (0, 0))],
        out_specs=[pl.BlockSpec((tm, d), lambda i: (i, 0)),
                   pl.BlockSpec((tm, ROUTER_W), lambda i: (i, 0))],
        out_shape=[jax.ShapeDtypeStruct(((n_x + n_c) * tm, d), F32),
                   jax.ShapeDtypeStruct(((n_x + n_c) * tm, ROUTER_W), F32)],
        compiler_params=_params(("parallel",)), name="moe_router",
    )(x2d, c2d, g.reshape(1, d), shift_x, scale_x, shift_c, scale_c, w_r, b_r)


def _row_copy(src_hbm, row, dst, slot, r, sem):
    return pltpu.make_async_copy(src_hbm.at[pl.ds(row, 1)], dst.at[slot, pl.ds(r, 1)], sem.at[slot])


def _expert_kernel(be_ref, tok_ref, nb_ref, h_hbm, wg_ref, wu_ref, wd_ref, ys_ref, xbuf, sem, *, bm):
    i = pl.program_id(0)
    nb = nb_ref[0]

    def start(blk, slot):
        def body(r, carry):
            _row_copy(h_hbm, tok_ref[blk * bm + r], xbuf, slot, r, sem).start()
            return carry
        lax.fori_loop(0, bm, body, 0, unroll=8)

    @pl.when(i == 0)
    def _():
        start(0, 0)

    @pl.when(i + 1 < nb)
    def _():
        start(i + 1, (i + 1) % 2)

    @pl.when(i < nb)
    def _():
        slot = i % 2

        pltpu.make_async_copy(h_hbm.at[pl.ds(0, bm)], xbuf.at[slot], sem.at[slot]).wait()
        x = xbuf[slot].astype(BF16)
        act = (_silu(_dot(x, wg_ref[0, 0])) * _dot(x, wu_ref[0, 0])).astype(BF16)
        ys_ref[...] = _dot(act, wd_ref[0, 0])

    @pl.when(i >= nb)
    def _():
        ys_ref[...] = jnp.zeros_like(ys_ref)


def _experts(h, buf_tok, block_e, n_used, wg, wu, wd, layer, bm):
    n_blocks = block_e.shape[0]
    d = h.shape[1]
    ff = wg.shape[3]
    grid_spec = pltpu.PrefetchScalarGridSpec(
        num_scalar_prefetch=3, grid=(n_blocks,),
        in_specs=[pl.BlockSpec(memory_space=pl.ANY),
                  pl.BlockSpec((1, 1, d, ff), lambda i, be, tok, nb: (layer, be[i], 0, 0)),
                  pl.BlockSpec((1, 1, d, ff), lambda i, be, tok, nb: (layer, be[i], 0, 0)),
                  pl.BlockSpec((1, 1, ff, d), lambda i, be, tok, nb: (layer, be[i], 0, 0))],
        out_specs=pl.BlockSpec((bm, d), lambda i, be, tok, nb: (i, 0)),
        scratch_shapes=[pltpu.VMEM((2, bm, d), F32), pltpu.SemaphoreType.DMA((2,))])
    return pl.pallas_call(
        functools.partial(_expert_kernel, bm=bm), grid_spec=grid_spec,
        out_shape=jax.ShapeDtypeStruct((n_blocks * bm, d), F32),
        compiler_params=_params(("arbitrary",)), name="moe_experts",
    )(block_e, buf_tok, n_used, h, wg, wu, wd)


def _combine_kernel(pos_ref, ys_hbm, x_ref, gw_ref, gate_ref, fg_ref, o_ref, ybuf, sem, *, tm, tok0, final):
    i = pl.program_id(0)
    n = pl.num_programs(0)

    def start(blk, slot):
        def body(r, carry):
            t = tok0 + blk * tm + r
            _row_copy(ys_hbm, pos_ref[2 * t], ybuf, slot, r, sem).start()
            _row_copy(ys_hbm, pos_ref[2 * t + 1], ybuf, slot, tm + r, sem).start()
            return carry
        lax.fori_loop(0, tm, body, 0, unroll=8)

    @pl.when(i == 0)
    def _():
        start(0, 0)

    @pl.when(i + 1 < n)
    def _():
        start(i + 1, (i + 1) % 2)

    slot = i % 2

    pltpu.make_async_copy(ys_hbm.at[pl.ds(0, 2 * tm)], ybuf.at[slot], sem.at[slot]).wait()
    gw = gw_ref[...]
    y = gw[:, 0:1] * ybuf[slot, 0:tm, :] + gw[:, 1:2] * ybuf[slot, tm:2 * tm, :]
    out = x_ref[...] + gate_ref[0] * y
    if final:
        out = _rms(out, fg_ref[...])
    o_ref[...] = out


def _combine(ys, pos, gw, x2d, gate, final_g, tok0, rows_per_mod, final):
    m, d = x2d.shape
    tm = 256
    per = rows_per_mod // tm
    gb = tok0 // tm
    grid_spec = pltpu.PrefetchScalarGridSpec(
        num_scalar_prefetch=1, grid=(m // tm,),
        in_specs=[pl.BlockSpec(memory_space=pl.ANY),
                  pl.BlockSpec((tm, d), lambda i, pos: (i, 0)),
                  pl.BlockSpec((tm, TOP_K), lambda i, pos: (gb + i, 0)),
                  pl.BlockSpec((1, 1, d), lambda i, pos: (i // per, 0, 0)),
                  pl.BlockSpec((1, d), lambda i, pos: (0, 0))],
        out_specs=pl.BlockSpec((tm, d), lambda i, pos: (i, 0)),
        scratch_shapes=[pltpu.VMEM((2, 2 * tm, d), F32), pltpu.SemaphoreType.DMA((2,))])
    return pl.pallas_call(
        functools.partial(_combine_kernel, tm=tm, tok0=tok0, final=final), grid_spec=grid_spec,
        out_shape=jax.ShapeDtypeStruct((m, d), F32),
        compiler_params=_params(("arbitrary",)), name="moe_combine",
    )(pos, ys, x2d, gw, gate, final_g.reshape(1, d))


def _route(logits):
    t = logits.shape[0]
    p_grp = jax.nn.softmax(logits[:, :N_GROUPS], axis=-1)
    p_sel = jnp.max(p_grp, axis=-1, keepdims=True)
    grp = jnp.argmax(p_grp, axis=-1).astype(jnp.int32)[:, None]
    fine = logits[:, N_GROUPS:N_GROUPS + N_EXPERTS].reshape(t, N_GROUPS, EXPERTS_PER_GROUP)
    in_grp = (jnp.arange(N_GROUPS, dtype=jnp.int32)[None, :] == grp)[:, :, None]
    fine = jnp.sum(jnp.where(in_grp, fine, 0.0), axis=1)
    lanes = jnp.arange(EXPERTS_PER_GROUP, dtype=jnp.int32)[None, :]
    i1 = jnp.argmax(fine, axis=-1).astype(jnp.int32)[:, None]
    v1 = jnp.max(fine, axis=-1, keepdims=True)
    rest = jnp.where(lanes == i1, -jnp.inf, fine)
    i2 = jnp.argmax(rest, axis=-1).astype(jnp.int32)[:, None]
    v2 = jnp.max(rest, axis=-1, keepdims=True)
    top_v = jnp.concatenate([v1, v2], axis=-1)
    top_i = jnp.concatenate([i1, i2], axis=-1)
    gate = p_sel * jax.nn.softmax(top_v, axis=-1)
    return grp * EXPERTS_PER_GROUP + top_i, gate


def _rank_kernel(e_ref, rank_ref, cnt_ref, run_scr):
    @pl.when(pl.program_id(0) == 0)
    def _():
        run_scr[...] = jnp.zeros_like(run_scr)

    tb = e_ref.shape[0]
    lane = lax.broadcasted_iota(jnp.int32, (tb, LANE), 1)
    onehot = jnp.where(e_ref[...] == lane, 1.0, 0.0)
    ri = lax.broadcasted_iota(jnp.int32, (tb, tb), 0)
    ci = lax.broadcasted_iota(jnp.int32, (tb, tb), 1)
    before = jnp.where(ri > ci, 1.0, 0.0).astype(BF16)
    prefix = _dot(before, onehot.astype(BF16)) + run_scr[...]
    rank_ref[...] = jnp.sum(prefix * onehot, axis=1, keepdims=True).astype(jnp.int32)
    run_scr[...] += jnp.sum(onehot, axis=0, keepdims=True)
    cnt_ref[...] = run_scr[...]


def _expert_ranks(flat_e):
    a = flat_e.shape[0]
    tb = 512
    rank, cnt = pl.pallas_call(
        _rank_kernel, grid=(a // tb,),
        in_specs=[pl.BlockSpec((tb, 1), lambda i: (i, 0))],
        out_specs=[pl.BlockSpec((tb, 1), lambda i: (i, 0)), pl.BlockSpec((1, LANE), lambda i: (0, 0))],
        out_shape=[jax.ShapeDtypeStruct((a, 1), jnp.int32), jax.ShapeDtypeStruct((1, LANE), F32)],
        scratch_shapes=[pltpu.VMEM((1, LANE), F32)],
        compiler_params=_params(("arbitrary",)), name="moe_rank",
    )(flat_e.reshape(a, 1))
    return rank.reshape(a), cnt[0, :N_EXPERTS].astype(jnp.int32)


def _dispatch_plan(expert, bm):
    t = expert.shape[0]
    a = t * TOP_K
    flat_e = expert.reshape(a).astype(jnp.int32)
    rank, counts = _expert_ranks(flat_e)
    padded = (counts + bm - 1) // bm * bm
    pad_end = jnp.cumsum(padded)
    pad_start = pad_end - padded
    mine = flat_e[:, None] == jnp.arange(N_EXPERTS, dtype=jnp.int32)[None, :]
    pos = (jnp.sum(jnp.where(mine, pad_start[None, :], 0), axis=1) + rank).astype(jnp.int32)
    n_blocks = a // bm + N_EXPERTS
    flat_tok = jnp.arange(a, dtype=jnp.int32) // TOP_K
    buf_tok = jnp.zeros((n_blocks * bm,), jnp.int32).at[pos].set(flat_tok)
    n_used = (pad_end[-1] // bm).astype(jnp.int32)
    blk = jnp.minimum(jnp.arange(n_blocks, dtype=jnp.int32), n_used - 1) * bm
    block_e = jnp.sum((pad_end[None, :] <= blk[:, None]).astype(jnp.int32), axis=1)
    block_e = jnp.minimum(block_e, N_EXPERTS - 1).astype(jnp.int32)
    return pos, buf_tok, block_e, n_used.reshape(1)


def _mix_stream(p3, rows, width, s0_f, s0_b, four_tabs, hy_tabs, hy_coefs, conv_w, conv_b, four_g,
                hy_bias, hy_g, a_log, dt_bias, with_mixers):
    ab = p3[..., PCOL_AB:PCOL_AB + N_AB]
    qkv = _short_conv(p3, conv_w, conv_b, rows, width, OFF_DN, HY_CONV_CH, 3 * DN_W, 4 * DN_DK, True)
    o_f, o_b, s_f, s_b = _delta_rule(qkv, ab, a_log, dt_bias, s0_f, s0_b)
    if not with_mixers:
        return None, None, o_f, o_b, s_f, s_b
    y_four = _fourier(p3, *four_tabs, four_g)
    convh = _short_conv(p3, conv_w, conv_b, rows, width, OFF_HY, 0, HY_CONV_CH, 256, False)
    y_hy = _hyena(convh, hy_tabs, hy_coefs, hy_bias, hy_g)
    return y_four, y_hy, o_f, o_b, s_f, s_b


def kernel(x, c, ctx, c_ctx, norm1_g, norm2_g, w_mod, b_mod, w_in, conv_w, conv_b, four_g, hy_w1, hy_b1, hy_w2, hy_b2, hy_w3, hy_b3, hy_w4, hy_freq, hy_bias, hy_g, dn_a_log, dn_dt_bias, dn_g, w_out, w_rc, b_rc, w_rf, b_rf, w_e_gate, w_e_up, w_e_down, final_g):
    bsz, length, d = x.shape
    lc = ctx.shape[1]
    depth = w_in.shape[0]
    rows = length // GRID_W
    bm = 256

    mod_all = _modulation(c, c_ctx, w_mod, b_mod)
    four_x, four_c = _fourier_tables(length), _fourier_tables(lc)
    hy_x, hy_c = _hyena_tables(length), _hyena_tables(lc)
    zeros = jnp.zeros((bsz, DN_HEADS, DN_DK, DN_DV), F32)
    x2 = x.reshape(bsz * length, d)
    c2 = ctx.reshape(bsz * lc, d)
    wg16, wu16, wd16 = w_e_gate.astype(BF16), w_e_up.astype(BF16), w_e_down.astype(BF16)

    for l in range(depth):
        last = l == depth - 1
        mod = [m[:, None, :] for m in jnp.split(mod_all[l, :bsz], 6, axis=-1)]
        modc = [m[:, None, :] for m in jnp.split(mod_all[l, bsz:bsz + 1], 6, axis=-1)]
        wi = w_in[l]
        w_p = jnp.concatenate([wi[:, :OFF_AB], wi[:, OFF_Z:], wi[:, OFF_AB:OFF_Z],
                               jnp.zeros((d, LANE - N_AB), wi.dtype)], axis=1).astype(BF16)
        w_o = w_out[l].astype(BF16)
        hy_params = (hy_w1[l], hy_b1[l], hy_w2[l], hy_b2[l], hy_w3[l], hy_b3[l], hy_w4[l], hy_freq[l])
        mix_args = (conv_w[l], conv_b[l], four_g[l], hy_bias[l], hy_g[l], dn_a_log[l], dn_dt_bias[l])

        pc = _proj_in(c2, norm1_g[l], modc[0], modc[1], w_p, bsz * lc)
        pc3 = pc.reshape(bsz, lc, PROJ_W)
        coefs_c = None if last else _hyena_coefs(lc, hy_c, hy_params)
        yf, yh, o_f, o_b, s_f, s_b = _mix_stream(pc3, 1, lc, zeros, zeros, four_c, hy_c, coefs_c,
                                                 *mix_args, with_mixers=not last)
        if not last:
            c2 = _proj_out(yf.reshape(-1, FOUR_W), yh.reshape(-1, HY_W), o_f.reshape(-1, DN_W),
                           o_b.reshape(-1, DN_W), pc, dn_g[l], w_o, c2, modc[2], bsz * lc)

        p = _proj_in(x2, norm1_g[l], mod[0], mod[1], w_p, length)
        p3 = p.reshape(bsz, length, PROJ_W)
        coefs_x = _hyena_coefs(length, hy_x, hy_params)
        yf, yh, o_f, o_b, _, _ = _mix_stream(p3, rows, GRID_W, s_f, s_b, four_x, hy_x, coefs_x,
                                             *mix_args, with_mixers=True)
        x2 = _proj_out(yf.reshape(-1, FOUR_W), yh.reshape(-1, HY_W), o_f.reshape(-1, DN_W),
                       o_b.reshape(-1, DN_W), p, dn_g[l], w_o, x2, mod[2], length)

        w_r = jnp.concatenate([w_rc[l], w_rf[l], jnp.zeros((d, ROUTER_W - N_GROUPS - N_EXPERTS), F32)], axis=1)
        b_r = jnp.concatenate([b_rc[l], b_rf[l], jnp.zeros((ROUTER_W - N_GROUPS - N_EXPERTS,), F32)])[None, :]
        h, logits = _router(x2, c2, norm2_g[l], mod[3], mod[4], modc[3], modc[4], w_r, b_r, length,
                            with_ctx=not last)
        expert, gate = _route(logits)
        pos, buf_tok, block_e, n_used = _dispatch_plan(expert, bm)
        ys = _experts(h, buf_tok, block_e, n_used, wg16, wu16, wd16, l, bm)
        x2 = _combine(ys, pos, gate, x2, mod[5], final_g, 0, length, final=last)
        if not last:
            c2 = _combine(ys, pos, gate, c2, modc[5], final_g, bsz * length, bsz * lc, final=False)
    return x2.reshape(bsz, length, d)
```

```python
import functools
import math

import jax
import jax.numpy as jnp
from jax import lax
from jax.experimental import pallas as pl
from jax.experimental.pallas import tpu as pltpu

F32 = jnp.float32
BF16 = jnp.bfloat16
HIGHEST = lax.Precision.HIGHEST

GRID_W = 64
FOUR_W = 512
FOUR_GROUPS = 4
FOUR_GW = FOUR_W // FOUR_GROUPS
HY_W = 512
HY_ORDER = 2
DN_HEADS = 8
DN_DK = 128
DN_DV = 128
DN_W = DN_HEADS * DN_DV
DN_CHUNK = 64
DN_SOLVE_BLOCK = DN_CHUNK // 4
HY_CONV_CH = (HY_ORDER + 1) * HY_W
OFF_HY = FOUR_W
OFF_DN = OFF_HY + HY_CONV_CH
OFF_AB = OFF_DN + 3 * DN_W
N_AB = 4 * DN_HEADS
OFF_Z = OFF_AB + N_AB
HY_EMB = 33
HY_FAST_DECAY = 0.3
HY_SLOW_DECAY = 1.5
HY_TARGET = 1e-2
N_GROUPS = 4
EXPERTS_PER_GROUP = 8
N_EXPERTS = N_GROUPS * EXPERTS_PER_GROUP
TOP_K = 2
EPS = 1e-6

LANE = 128
PCOL_Z = OFF_AB
PCOL_AB = PCOL_Z + DN_W
PROJ_W = PCOL_AB + LANE
ROUTER_W = LANE
VMEM_LIMIT = 56 * 1024 * 1024


def _params(semantics):
    return pltpu.CompilerParams(dimension_semantics=semantics, vmem_limit_bytes=VMEM_LIMIT)


def _silu(v):
    return v * jax.nn.sigmoid(v)


def _rms(v, g):
    return v * lax.rsqrt(jnp.mean(v * v, axis=-1, keepdims=True) + EPS) * g


def _dot(a, b):
    return jnp.dot(a, b, preferred_element_type=F32)


def _mod_kernel(a_ref, w_ref, b_ref, o_ref):
    o_ref[0] = jnp.dot(_silu(a_ref[...]), w_ref[0], precision=HIGHEST,
                       preferred_element_type=F32) + b_ref[0]


def _modulation(c, c_ctx, w_mod, b_mod):
    depth, d, n = w_mod.shape
    bsz = c.shape[0]
    rows = -(-(bsz + 1) // 8) * 8
    a = jnp.concatenate([c, c_ctx[None], jnp.zeros((rows - bsz - 1, d), F32)], axis=0)
    tn = 1024 if n % 1024 == 0 else 512
    assert n % tn == 0
    return pl.pallas_call(
        _mod_kernel, grid=(depth, n // tn),
        in_specs=[pl.BlockSpec((rows, d), lambda l, j: (0, 0)),
                  pl.BlockSpec((1, d, tn), lambda l, j: (l, 0, j)),
                  pl.BlockSpec((1, 1, tn), lambda l, j: (l, 0, j))],
        out_specs=pl.BlockSpec((1, rows, tn), lambda l, j: (l, 0, j)),
        out_shape=jax.ShapeDtypeStruct((depth, rows, n), F32),
        compiler_params=_params(("parallel", "parallel")), name="modulation",
    )(a, w_mod, b_mod.reshape(depth, 1, n))


def _proj_in_kernel(x_ref, g_ref, shift_ref, scale_ref, w_ref, o_ref, ab_ref, a_scr):
    j = pl.program_id(1)

    @pl.when(j == 0)
    def _():
        y = _rms(x_ref[...], g_ref[...])
        a_scr[...] = (y * (1.0 + scale_ref[0]) + shift_ref[0]).astype(BF16)

    acc = _dot(a_scr[...], w_ref[...])
    o_ref[...] = acc.astype(BF16)

    @pl.when(j == pl.num_programs(1) - 1)
    def _():
        ab_ref[...] = acc[:, acc.shape[1] - LANE:]


def _proj_in(x2d, g, shift, scale, w, rows_per_mod):
    m, d = x2d.shape
    n = w.shape[1]
    tm = min(1024, m, rows_per_mod)
    tn = 896 if n % 896 == 0 else n
    per = rows_per_mod // tm
    return pl.pallas_call(
        _proj_in_kernel, grid=(m // tm, n // tn),
        in_specs=[pl.BlockSpec((tm, d), lambda i, j: (i, 0)),
                  pl.BlockSpec((1, d), lambda i, j: (0, 0)),
                  pl.BlockSpec((1, 1, d), lambda i, j: (i // per, 0, 0)),
                  pl.BlockSpec((1, 1, d), lambda i, j: (i // per, 0, 0)),
                  pl.BlockSpec((d, tn), lambda i, j: (0, j))],
        out_specs=[pl.BlockSpec((tm, tn), lambda i, j: (i, j)),
                   pl.BlockSpec((tm, LANE), lambda i, j: (i, 0))],
        out_shape=[jax.ShapeDtypeStruct((m, n), BF16), jax.ShapeDtypeStruct((m, LANE), F32)],
        scratch_shapes=[pltpu.VMEM((tm, d), BF16)],
        compiler_params=_params(("parallel", "arbitrary")), name="proj_in",
    )(x2d, g.reshape(1, d), shift, scale, w)


def _cos_sin_table(nrow, ncol, period):
    i = lax.broadcasted_iota(jnp.int32, (nrow, ncol), 0)
    j = lax.broadcasted_iota(jnp.int32, (nrow, ncol), 1)
    ang = ((i * j) % period).astype(F32) * (2.0 * math.pi / period)
    return jnp.cos(ang), jnp.sin(ang)


def _dft_tables(length, period):
    blk = 64 if length % 64 == 0 else length
    hi_i = lax.broadcasted_iota(jnp.int32, (length // blk, length), 0) * blk
    j = lax.broadcasted_iota(jnp.int32, (length // blk, length), 1)
    ang = ((hi_i * j) % period).astype(F32) * (2.0 * math.pi / period)
    hi_c, hi_s = jnp.cos(ang), jnp.sin(ang)
    lo_c, lo_s = _cos_sin_table(blk, length, period)
    c = hi_c[:, None, :] * lo_c[None, :, :] - hi_s[:, None, :] * lo_s[None, :, :]
    s = hi_s[:, None, :] * lo_c[None, :, :] + hi_c[:, None, :] * lo_s[None, :, :]
    return c.reshape(length, length), s.reshape(length, length)


def _fourier_kernel(u_ref, c_ref, s_ref, bc_ref, bs_ref, g_ref, o_ref, ub_scr, *, scale):
    @pl.when(pl.program_id(1) == 0)
    def _():
        ub_scr[...] = u_ref[0].astype(BF16)

    ub = ub_scr[...]
    p = _dot(c_ref[...], ub).astype(BF16)
    q = _dot(s_ref[...], ub).astype(BF16)
    z = (_dot(p, bc_ref[...]) - _dot(q, bs_ref[...])) * scale
    o_ref[0] = _rms(z, g_ref[...]).astype(o_ref.dtype)


def _fourier(p3, cl, sl, bc, bs, four_g):
    bsz, length, _ = p3.shape
    tm = min(512, length)
    kern = functools.partial(_fourier_kernel, scale=1.0 / math.sqrt(length * FOUR_GW))
    return pl.pallas_call(
        kern, grid=(bsz, length // tm),
        in_specs=[pl.BlockSpec((1, length, FOUR_W), lambda b, i: (b, 0, 0)),
                  pl.BlockSpec((tm, length), lambda b, i: (i, 0)),
                  pl.BlockSpec((tm, length), lambda b, i: (i, 0)),
                  pl.BlockSpec((FOUR_W, FOUR_W), lambda b, i: (0, 0)),
                  pl.BlockSpec((FOUR_W, FOUR_W), lambda b, i: (0, 0)),
                  pl.BlockSpec((1, FOUR_W), lambda b, i: (0, 0))],
        out_specs=pl.BlockSpec((1, tm, FOUR_W), lambda b, i: (b, i, 0)),
        out_shape=jax.ShapeDtypeStruct((bsz, length, FOUR_W), BF16),
        scratch_shapes=[pltpu.VMEM((length, FOUR_W), BF16)],
        compiler_params=_params(("parallel", "arbitrary")), name="fourier",
    )(p3, cl, sl, bc, bs, four_g.reshape(1, FOUR_W))


def _fourier_tables(length):
    cl, sl = _dft_tables(length, length)
    cc, sc = _cos_sin_table(FOUR_GW, FOUR_GW, FOUR_GW)
    eye = jnp.eye(FOUR_GROUPS, dtype=F32)
    return cl.astype(BF16), sl.astype(BF16), jnp.kron(eye, cc).astype(BF16), jnp.kron(eye, sc).astype(BF16)


def _conv_kernel(x_ref, w_ref, b_ref, o_ref, *, rows, width, heads_mode):
    tc = x_ref.shape[2]
    wts = w_ref[...]
    bias = b_ref[...]
    pos = lax.broadcasted_iota(jnp.int32, (width, tc), 0)
    not_first = pos > 0
    not_last = pos < width - 1
    kind = pl.program_id(1) // (DN_W // tc)

    def body(r, carry):
        base = pl.multiple_of(r * width, width)
        cen = x_ref[0, pl.ds(base, width), :].astype(F32)
        if rows > 1:
            up = x_ref[0, pl.ds(pl.multiple_of(jnp.maximum(r - 1, 0) * width, width), width), :].astype(F32)
            dn = x_ref[0, pl.ds(pl.multiple_of(jnp.minimum(r + 1, rows - 1) * width, width), width), :].astype(F32)
            w_up = wts[0] * jnp.where(r > 0, 1.0, 0.0)
            w_dn = wts[2] * jnp.where(r < rows - 1, 1.0, 0.0)
            col = lambda j: up * w_up[j:j + 1, :] + cen * wts[1, j:j + 1, :] + dn * w_dn[j:j + 1, :]
        else:
            col = lambda j: cen * wts[1, j:j + 1, :]
        left = jnp.where(not_first, pltpu.roll(col(0), 1, 0), 0.0)
        right = jnp.where(not_last, pltpu.roll(col(2), width - 1, 0), 0.0)
        acc = left + col(1) + right + bias
        if heads_mode:
            for h in range(tc // DN_DK):
                s = _silu(acc[:, h * DN_DK:(h + 1) * DN_DK])
                inv = lax.rsqrt(jnp.sum(s * s, axis=-1, keepdims=True) + EPS)
                fac = jnp.where(kind == 0, inv * (DN_DK ** -0.5), jnp.where(kind == 1, inv, 1.0))
                o_ref[0, pl.ds(base, width), h * DN_DK:(h + 1) * DN_DK] = s * fac
        else:
            o_ref[0, pl.ds(base, width), :] = acc
        return carry

    lax.fori_loop(0, rows, body, 0)


def _short_conv(p3, conv_w, conv_b, rows, width, col0, ch0, nch, tc, heads_mode):
    bsz, length, _ = p3.shape
    kern = functools.partial(_conv_kernel, rows=rows, width=width, heads_mode=heads_mode)
    cb, wb = col0 // tc, ch0 // tc
    return pl.pallas_call(
        kern, grid=(bsz, nch // tc),
        in_specs=[pl.BlockSpec((1, length, tc), lambda b, j: (b, 0, cb + j)),
                  pl.BlockSpec((3, 3, tc), lambda b, j: (0, 0, wb + j)),
                  pl.BlockSpec((1, tc), lambda b, j: (0, wb + j))],
        out_specs=pl.BlockSpec((1, length, tc), lambda b, j: (b, 0, j)),
        out_shape=jax.ShapeDtypeStruct((bsz, length, nch), F32),
        compiler_params=_params(("parallel", "parallel")),
        name="conv_heads" if heads_mode else "conv_hyena",
    )(p3, conv_w, conv_b.reshape(1, -1))


def _filt_kernel(z_ref, w1_ref, b1_ref, w2_ref, b2_ref, w3_ref, b3_ref, w4_ref, fr_ref, dl_ref,
                 g_ref, nrm_ref):
    i = pl.program_id(0)
    z = z_ref[...]
    fr = fr_ref[...]
    hdot = lambda a, b: jnp.dot(a, b, precision=HIGHEST, preferred_element_type=F32)
    h = jnp.sin(fr * (hdot(z, w1_ref[...]) + b1_ref[...]))
    h = jnp.sin(fr * (hdot(h, w2_ref[...]) + b2_ref[...]))
    h = jnp.sin(fr * (hdot(h, w3_ref[...]) + b3_ref[...]))
    h = hdot(h, w4_ref[...])
    decay = jnp.exp(-z[:, 0:1] * jnp.abs(dl_ref[...]))
    decay = jnp.concatenate([decay] * HY_ORDER, axis=1)
    half = HY_ORDER * HY_W
    hf = h[:, :half] * decay
    hb = h[:, half:] * decay
    row = lax.broadcasted_iota(jnp.int32, hb.shape, 0) + i * hb.shape[0]
    hb = jnp.where(row > 0, hb, 0.0)
    gp = hf + hb
    gm = hf - hb
    for o in range(HY_ORDER):
        g_ref[o] = gp[:, o * HY_W:(o + 1) * HY_W].astype(BF16)
        g_ref[HY_ORDER + o] = gm[:, o * HY_W:(o + 1) * HY_W].astype(BF16)
    part = jnp.sum(jnp.abs(hf) + jnp.abs(hb), axis=0, keepdims=True)

    @pl.when(i == 0)
    def _():
        nrm_ref[...] = part

    @pl.when(i > 0)
    def _():
        nrm_ref[...] += part


def _hyena_filter_taps(length, w1, b1, w2, b2, w3, b3, w4, freq):
    t = jnp.linspace(0.0, 1.0, length, dtype=F32)[:, None]
    bands = (HY_EMB - 1) // 2
    ang = (2.0 * math.pi / length) * jnp.arange(length, dtype=F32)[:, None]
    f = jnp.linspace(1e-4, bands - 1, bands, dtype=F32)
    z = jnp.concatenate([t, jnp.cos(f * ang), -jnp.sin(f * ang)], axis=-1)
    max_decay = math.log(HY_TARGET) / HY_FAST_DECAY
    min_decay = math.log(HY_TARGET) / HY_SLOW_DECAY
    deltas = jnp.linspace(min_decay, max_decay, HY_W, dtype=F32)[None, :]
    fw = w1.shape[1]
    tl = min(512, length)
    full = lambda shape: pl.BlockSpec(shape, lambda i: (0,) * len(shape))
    return pl.pallas_call(
        _filt_kernel, grid=(length // tl,),
        in_specs=[pl.BlockSpec((tl, HY_EMB), lambda i: (i, 0)),
                  full((HY_EMB, fw)), full((1, fw)), full((fw, fw)), full((1, fw)),
                  full((fw, fw)), full((1, fw)), full((fw, 2 * HY_ORDER * HY_W)), full((1, fw)),
                  full((1, HY_W))],
        out_specs=[pl.BlockSpec((2 * HY_ORDER, tl, HY_W), lambda i: (0, i, 0)),
                   pl.BlockSpec((1, HY_ORDER * HY_W), lambda i: (0, 0))],
        out_shape=[jax.ShapeDtypeStruct((2 * HY_ORDER, length, HY_W), BF16),
                   jax.ShapeDtypeStruct((1, HY_ORDER * HY_W), F32)],
        compiler_params=_params(("arbitrary",)), name="hyena_filter",
    )(z, w1, b1.reshape(1, fw), w2, b2.reshape(1, fw), w3, b3.reshape(1, fw), w4,
      freq.reshape(1, fw), deltas)


def _dft_fwd_kernel(u_ref, c_ref, s_ref, *rest, with_coef):
    if with_coef:
        c1_ref, c2_ref, c4_ref, yr_ref, yi_ref, ub_scr = rest
    else:
        yr_ref, yi_ref, ub_scr = rest

    @pl.when(pl.program_id(1) == 0)
    def _():
        ub_scr[...] = u_ref[0].astype(BF16)

    ub = ub_scr[...]
    a = _dot(c_ref[...], ub)
    b = _dot(s_ref[...], ub)
    if with_coef:
        c2 = c2_ref[...]
        yr_ref[0] = (a * c1_ref[...] + b * c2).astype(BF16)
        yi_ref[0] = (b * c4_ref[...] - a * c2).astype(BF16)
    else:
        yr_ref[0] = a
        yi_ref[0] = b


def _dft_fwd(u3, col_blk, cf, sf, coef=None):
    nb, length, _ = u3.shape
    tm = min(512, length)
    in_specs = [pl.BlockSpec((1, length, HY_W), lambda b, i: (b, 0, col_blk)),
                pl.BlockSpec((tm, length), lambda b, i: (i, 0)),
                pl.BlockSpec((tm, length), lambda b, i: (i, 0))]
    args = [u3, cf, sf]
    if coef is not None:
        in_specs += [pl.BlockSpec((tm, HY_W), lambda b, i: (i, 0))] * 3
        args += list(coef)
    odt = BF16 if coef is not None else F32
    return pl.pallas_call(
        functools.partial(_dft_fwd_kernel, with_coef=coef is not None), grid=(nb, length // tm),
        in_specs=in_specs,
        out_specs=[pl.BlockSpec((1, tm, HY_W), lambda b, i: (b, i, 0))] * 2,
        out_shape=[jax.ShapeDtypeStruct((nb, length, HY_W), odt)] * 2,
        scratch_shapes=[pltpu.VMEM((length, HY_W), BF16)],
        compiler_params=_params(("parallel", "arbitrary")),
        name="hyena_dft_fwd" if coef is not None else "hyena_filter_dft",
    )(*args)


def _dft_inv_kernel(yr_ref, yi_ref, c_ref, s_ref, u_ref, gate_ref, bias_ref, g_ref, o_ref, *, final):
    y = _dot(c_ref[...], yr_ref[0]) + _dot(s_ref[...], yi_ref[0])
    out = gate_ref[0] * (y + bias_ref[...] * u_ref[0])
    if final:
        out = _rms(out, g_ref[...])
    o_ref[0] = out.astype(o_ref.dtype)


def _dft_inv(yr, yi, cf, s_inv, u3, u_blk, gate3, gate_blk, bias, g, final):
    bsz, length, _ = yr.shape
    tm = min(512, length)
    return pl.pallas_call(
        functools.partial(_dft_inv_kernel, final=final), grid=(bsz, length // tm),
        in_specs=[pl.BlockSpec((1, length, HY_W), lambda b, i: (b, 0, 0)),
                  pl.BlockSpec((1, length, HY_W), lambda b, i: (b, 0, 0)),
                  pl.BlockSpec((tm, length), lambda b, i: (i, 0)),
                  pl.BlockSpec((tm, length), lambda b, i: (i, 0)),
                  pl.BlockSpec((1, tm, HY_W), lambda b, i: (b, i, u_blk)),
                  pl.BlockSpec((1, tm, HY_W), lambda b, i: (b, i, gate_blk)),
                  pl.BlockSpec((1, HY_W), lambda b, i: (0, 0)),
                  pl.BlockSpec((1, HY_W), lambda b, i: (0, 0))],
        out_specs=pl.BlockSpec((1, tm, HY_W), lambda b, i: (b, i, 0)),
        out_shape=jax.ShapeDtypeStruct((bsz, length, HY_W), BF16 if final else F32),
        compiler_params=_params(("parallel", "arbitrary")), name="hyena_dft_inv",
    )(yr, yi, cf, s_inv, u3, gate3, bias.reshape(1, HY_W), g.reshape(1, HY_W))


def _hyena_tables(length):
    cf, sf = _dft_tables(length, 2 * length)
    alt = jnp.where(jnp.arange(length) % 2 == 0, 1.0, -1.0).astype(F32)
    sf = sf.at[0, :].set(alt)
    return cf.astype(BF16), sf.astype(BF16), sf.T.astype(BF16)


def _hyena_coefs(length, tables, hy_params):
    cf, sf, _ = tables
    taps, nrm = _hyena_filter_taps(length, *hy_params)
    a, b = _dft_fwd(taps, 0, cf, sf)
    inv = (1.0 / nrm).reshape(HY_ORDER, 1, HY_W)
    n = 2.0 * length
    hr = a[:HY_ORDER] * inv
    hi = -b[HY_ORDER:] * inv
    nyq = b[:HY_ORDER, 0:1, :] * inv
    first = (jnp.arange(length) == 0)[None, :, None]
    c1 = jnp.where(first, hr / n, hr * (2.0 / n))
    c2 = jnp.where(first, 0.0, hi * (2.0 / n))
    c4 = jnp.where(first, nyq / n, hr * (2.0 / n))
    return c1, c2, c4


def _hyena(convh, tables, coefs, hy_bias, hy_g):
    cf, sf, s_inv = tables
    c1, c2, c4 = coefs
    yr, yi = _dft_fwd(convh, 0, cf, sf, (c1[0], c2[0], c4[0]))
    zz = _dft_inv(yr, yi, cf, s_inv, convh, 0, convh, 1, hy_bias[0], hy_g, False)
    yr, yi = _dft_fwd(zz, 0, cf, sf, (c1[1], c2[1], c4[1]))
    return _dft_inv(yr, yi, cf, s_inv, zz, 0, convh, 2, hy_bias[1], hy_g, True)


def _softplus(v):
    return jnp.maximum(v, 0.0) + jnp.log(1.0 + jnp.exp(-jnp.abs(v)))


def _delta_prep_kernel(qkv_ref, ab_ref, abt_ref, arow_ref, drow_ref, acol_ref, dcol_ref,
                       wq_ref, u_ref, ak_ref, egl_ref):
    c = ab_ref.shape[1]
    nh = DN_HEADS
    ri = lax.broadcasted_iota(jnp.int32, (c, c), 0)
    ci = lax.broadcasted_iota(jnp.int32, (c, c), 1)
    incl = (ri >= ci, ri <= ci)
    strict = (ri > ci, ri < ci)
    low = jnp.where(incl[0], 1.0, 0.0)
    upp = jnp.where(incl[1], 1.0, 0.0)
    hdot = lambda a, b: jnp.dot(a, b, precision=HIGHEST, preferred_element_type=F32)
    ab = ab_ref[0]
    abt = abt_ref[0, 0]
    gate = -arow_ref[...] * _softplus(ab + drow_ref[...])
    gate_t = -acol_ref[...] * _softplus(abt + dcol_ref[...])
    beta_all = jax.nn.sigmoid(ab)
    gcum = (hdot(low, gate), hdot(upp, gate))
    gcum_t = (hdot(gate_t, upp), hdot(gate_t, low))
    er = lax.broadcasted_iota(jnp.int32, (DN_DK, DN_DK), 0)
    ec = lax.broadcasted_iota(jnp.int32, (DN_DK, DN_DK), 1)
    eye16 = jnp.where(er == ec, 1.0, 0.0).astype(BF16)
    nt = (((1,), (1,)), ((), ()))
    q = [qkv_ref[0, :, h * DN_DK:(h + 1) * DN_DK] for h in range(nh)]
    k = [qkv_ref[0, :, DN_W + h * DN_DK:DN_W + (h + 1) * DN_DK] for h in range(nh)]
    v = [qkv_ref[0, :, 2 * DN_W + h * DN_DV:2 * DN_W + (h + 1) * DN_DV] for h in range(nh)]
    k16 = [t.astype(BF16) for t in k]
    kk0 = [lax.dot_general(k16[h], k16[h], nt, preferred_element_type=F32) for h in range(nh)]
    qk0 = [lax.dot_general(q[h].astype(BF16), k16[h], nt, preferred_element_type=F32) for h in range(nh)]
    units = [(d, h) for d in range(2) for h in range(nh)]
    mm, rr = {}, {}
    for d, h in units:
        col = d * nh + h
        gc = gcum[d][:, col:col + 1]
        gr = gcum_t[d][col:col + 1, :]
        be = beta_all[:, 2 * nh + col:2 * nh + col + 1]
        last = 0 if d else c - 1
        gl = gc[last:last + 1, :]
        dec = jnp.where(incl[d], jnp.exp(jnp.where(incl[d], gc - gr, 0.0)), 0.0)
        mm[d, h] = jnp.where(strict[d], be * kk0[h] * dec, 0.0)
        ak_ref[0, 0, d, h, 0:c, :] = jnp.where(incl[d], qk0[h] * dec, 0.0).astype(BF16)
        eg = jnp.exp(gc)
        rr[d, h] = jnp.concatenate([v[h] * be, k[h] * (be * eg)], axis=1)
        wq_ref[0, 0, d, h, c:2 * c, :] = (q[h] * eg).astype(BF16)
        k_tail = (k[h] * jnp.exp(gl - gc)).astype(BF16)
        ak_ref[0, 0, d, h, c:c + DN_DK, :] = lax.dot_general(
            eye16, k_tail, nt, preferred_element_type=F32).astype(BF16)
        egl_ref[0, 0, d, h:h + 1, :] = jnp.broadcast_to(jnp.exp(gl), (1, DN_DV))
    sb = DN_SOLVE_BLOCK
    in_sb = (ri // sb) == (ci // sb)
    in_2sb = (ri // (2 * sb)) == (ci // (2 * sb))
    eye = jnp.where(ri == ci, 1.0, 0.0)
    b16 = lambda t: t.astype(BF16)
    nj = {u: jnp.where(in_sb, -mm[u], 0.0) for u in units}
    inv = {u: eye + nj[u] for u in units}
    for j in range((sb - 1).bit_length() - 1):
        nj = {u: _dot(b16(nj[u]), b16(nj[u])) for u in units}
        inv = {u: inv[u] + _dot(b16(inv[u]), b16(nj[u])) for u in units}
    off = {u: b16(jnp.where(in_2sb & ~in_sb, mm[u], 0.0)) for u in units}
    tmp = {u: _dot(off[u], b16(inv[u])) for u in units}
    inv = {u: inv[u] - _dot(b16(inv[u]), b16(tmp[u])) for u in units}
    inv16 = {u: b16(inv[u]) for u in units}
    off = {u: b16(jnp.where(in_2sb, 0.0, mm[u])) for u in units}
    part = {u: _dot(inv16[u], b16(rr[u])) for u in units}
    tmp = {u: _dot(off[u], b16(part[u])) for u in units}
    rr = {u: part[u] - _dot(inv16[u], b16(tmp[u])) for u in units}
    for d, h in units:
        u_ref[0, 0, d, h] = rr[d, h][:, :DN_DV]
        wq_ref[0, 0, d, h, 0:c, :] = rr[d, h][:, DN_DV:].astype(BF16)


def _delta_scan_kernel(wqf_ref, uf_ref, akf_ref, egf_ref, wqb_ref, ub_ref, akb_ref, egb_ref, s0f_ref, s0b_ref,
                       of_ref, ob_ref, sf_ref, sb_ref, s_scr):
    i = pl.program_id(1)
    bpb = uf_ref.shape[0]
    c = uf_ref.shape[-2]

    @pl.when(i == 0)
    def _():
        s_scr[0] = s0f_ref[...]
        s_scr[1] = s0b_ref[...]

    refs = ((wqf_ref, uf_ref, akf_ref, egf_ref, of_ref), (wqb_ref, ub_ref, akb_ref, egb_ref, ob_ref))
    units = [(d, bb, h) for d in range(2) for bb in range(bpb) for h in range(DN_HEADS)]
    s = {u: s_scr[u] for u in units}
    ws = {(d, bb, h): _dot(refs[d][0][bb, 0, 0, h], s[d, bb, h].astype(BF16)) for d, bb, h in units}
    v16 = {(d, bb, h): (refs[d][1][bb, 0, 0, h] - ws[d, bb, h][:c]).astype(BF16) for d, bb, h in units}
    av = {(d, bb, h): _dot(refs[d][2][bb, 0, 0, h], v16[d, bb, h]) for d, bb, h in units}
    for d, bb, h in units:
        u = (d, bb, h)
        refs[d][4][bb, :, h * DN_DV:(h + 1) * DN_DV] = (ws[u][c:] + av[u][:c]).astype(BF16)
        s_scr[u] = s[u] * refs[d][3][bb, 0, 0, h:h + 1, :] + av[u][c:]

    @pl.when(i == pl.num_programs(1) - 1)
    def _():
        sf_ref[...] = s_scr[0]
        sb_ref[...] = s_scr[1]


def _delta_rule(qkv, ab, a_log, dt_bias, s0_f, s0_b, bpb=1):
    bsz, length, _ = qkv.shape
    c = DN_CHUNK
    n = length // c
    nh = DN_HEADS
    abt = jnp.swapaxes(ab.reshape(bsz, n, c, 4 * nh), 2, 3)
    zeros = jnp.zeros((2 * nh,), F32)
    a_vec = jnp.concatenate([jnp.exp(a_log.astype(F32)).reshape(-1), zeros])
    d_vec = jnp.concatenate([dt_bias.astype(F32).reshape(-1), zeros])
    small = lambda shape: pl.BlockSpec(shape, lambda b, i: (0,) * len(shape))
    per_chunk = lambda *tail: pl.BlockSpec((1, 1, 2, nh) + tail, lambda b, i: (b, i, 0, 0) + (0,) * len(tail))
    wq, u, ak, egl = pl.pallas_call(
        _delta_prep_kernel, grid=(bsz, n),
        in_specs=[pl.BlockSpec((1, c, 3 * DN_W), lambda b, i: (b, i, 0)),
                  pl.BlockSpec((1, c, 4 * nh), lambda b, i: (b, i, 0)),
                  pl.BlockSpec((1, 1, 4 * nh, c), lambda b, i: (b, i, 0, 0)),
                  small((1, 4 * nh)), small((1, 4 * nh)), small((4 * nh, 1)), small((4 * nh, 1))],
        out_specs=[per_chunk(2 * c, DN_DK), per_chunk(c, DN_DV), per_chunk(c + DN_DK, c),
                   pl.BlockSpec((1, 1, 2, nh, DN_DV), lambda b, i: (b, i, 0, 0, 0))],
        out_shape=[jax.ShapeDtypeStruct((bsz, n, 2, nh, 2 * c, DN_DK), BF16),
                   jax.ShapeDtypeStruct((bsz, n, 2, nh, c, DN_DV), F32),
                   jax.ShapeDtypeStruct((bsz, n, 2, nh, c + DN_DK, c), BF16),
                   jax.ShapeDtypeStruct((bsz, n, 2, nh, DN_DV), F32)],
        compiler_params=_params(("parallel", "parallel")), name="delta_prep",
    )(qkv, ab, abt, a_vec.reshape(1, -1), d_vec.reshape(1, -1), a_vec.reshape(-1, 1), d_vec.reshape(-1, 1))

    def side(d, *tail):
        idx = (lambda b, i: (b, n - 1 - i, 1, 0) + (0,) * len(tail)) if d else (
            lambda b, i: (b, i, 0, 0) + (0,) * len(tail))
        return pl.BlockSpec((bpb, 1, 1, nh) + tail, idx)

    def side_specs(d):
        return [side(d, 2 * c, DN_DK), side(d, c, DN_DV), side(d, c + DN_DK, c),
                pl.BlockSpec((bpb, 1, 1, nh, DN_DV),
                             (lambda b, i: (b, n - 1 - i, 1, 0, 0)) if d else (lambda b, i: (b, i, 0, 0, 0)))]

    st_spec = pl.BlockSpec((bpb, nh, DN_DK, DN_DV), lambda b, i: (b, 0, 0, 0))
    o_shape = jax.ShapeDtypeStruct((bsz, length, DN_W), BF16)
    s_shape = jax.ShapeDtypeStruct((bsz, nh, DN_DK, DN_DV), F32)
    return pl.pallas_call(
        _delta_scan_kernel, grid=(bsz // bpb, n),
        in_specs=side_specs(0) + side_specs(1) + [st_spec, st_spec],
        out_specs=[pl.BlockSpec((bpb, c, DN_W), lambda b, i: (b, i, 0)),
                   pl.BlockSpec((bpb, c, DN_W), lambda b, i: (b, n - 1 - i, 0)), st_spec, st_spec],
        out_shape=[o_shape, o_shape, s_shape, s_shape],
        scratch_shapes=[pltpu.VMEM((2, bpb, nh, DN_DK, DN_DV), F32)],
        compiler_params=_params(("parallel", "arbitrary")), name="delta_scan",
    )(wq, u, ak, egl, wq, u, ak, egl, s0_f, s0_b)


def _proj_out_kernel(yf_ref, yh_ref, of_ref, ob_ref, z_ref, dg_ref, w_ref, res_ref, gate_ref, o_ref, a_scr):
    @pl.when(pl.program_id(1) == 0)
    def _():
        a_scr[:, 0:FOUR_W] = yf_ref[...]
        a_scr[:, FOUR_W:FOUR_W + HY_W] = yh_ref[...]
        o = of_ref[...].astype(F32) + ob_ref[...].astype(F32)
        z = z_ref[...].astype(F32)
        for h in range(DN_HEADS):
            lanes = slice(h * DN_DV, (h + 1) * DN_DV)
            y = _rms(o[:, lanes], dg_ref[...]) * _silu(z[:, lanes])
            a_scr[:, FOUR_W + HY_W + h * DN_DV:FOUR_W + HY_W + (h + 1) * DN_DV] = y.astype(BF16)

    o_ref[...] = res_ref[...] + gate_ref[0] * _dot(a_scr[...], w_ref[...])


def _proj_out(y_four, y_hy, o_f, o_b, p2d, dn_g, w, res, gate, rows_per_mod):
    m, d = res.shape
    tm = min(512, m)
    tn = d
    per = rows_per_mod // tm
    zb = PCOL_Z // DN_W
    return pl.pallas_call(
        _proj_out_kernel, grid=(m // tm, d // tn),
        in_specs=[pl.BlockSpec((tm, FOUR_W), lambda i, j: (i, 0)),
                  pl.BlockSpec((tm, HY_W), lambda i, j: (i, 0)),
                  pl.BlockSpec((tm, DN_W), lambda i, j: (i, 0)),
                  pl.BlockSpec((tm, DN_W), lambda i, j: (i, 0)),
                  pl.BlockSpec((tm, DN_W), lambda i, j: (i, zb)),
                  pl.BlockSpec((1, DN_DV), lambda i, j: (0, 0)),
                  pl.BlockSpec((w.shape[0], tn), lambda i, j: (0, j)),
                  pl.BlockSpec((tm, tn), lambda i, j: (i, j)),
                  pl.BlockSpec((1, 1, tn), lambda i, j: (i // per, 0, j))],
        out_specs=pl.BlockSpec((tm, tn), lambda i, j: (i, j)),
        out_shape=jax.ShapeDtypeStruct((m, d), F32),
        scratch_shapes=[pltpu.VMEM((tm, w.shape[0]), BF16)],
        compiler_params=_params(("parallel", "arbitrary")), name="proj_out",
    )(y_four, y_hy, o_f, o_b, p2d, dn_g.reshape(1, DN_DV), w, res, gate)


def _router_kernel(x_ref, c_ref, g_ref, sx_ref, cx_ref, sc_ref, cc_ref, w_ref, b_ref, h_ref, lg_ref, *, n_x):
    i = pl.program_id(0)

    def emit(v, shift, scale):
        h = _rms(v, g_ref[...]) * (1.0 + scale) + shift
        h_ref[...] = h
        lg_ref[...] = jnp.dot(h, w_ref[...], precision=HIGHEST, preferred_element_type=F32) + b_ref[...]

    @pl.when(i < n_x)
    def _():
        emit(x_ref[...], sx_ref[0], cx_ref[0])

    @pl.when(i >= n_x)
    def _():
        emit(c_ref[...], sc_ref[0], cc_ref[0])


def _router(x2d, c2d, g, shift_x, scale_x, shift_c, scale_c, w_r, b_r, rows_per_mod, with_ctx):
    mx, d = x2d.shape
    tm = 512
    n_x = mx // tm
    n_c = c2d.shape[0] // tm if with_ctx else 0
    per = rows_per_mod // tm
    xi = lambda i: (jnp.minimum(i, n_x - 1), 0)
    ci = lambda i: (jnp.maximum(i - n_x, 0), 0)
    return pl.pallas_call(
        functools.partial(_router_kernel, n_x=n_x), grid=(n_x + n_c,),
        in_specs=[pl.BlockSpec((tm, d), xi), pl.BlockSpec((tm, d), ci),
                  pl.BlockSpec((1, d), lambda i: (0, 0)),
                  pl.BlockSpec((1, 1, d), lambda i: (jnp.minimum(i, n_x - 1) // per, 0, 0)),
                  pl.BlockSpec((1, 1, d), lambda i: (jnp.minimum(i, n_x - 1) // per, 0, 0)),
                  pl.BlockSpec((1, 1, d), lambda i: (0, 0, 0)),
                  pl.BlockSpec((1, 1, d), lambda i: (0, 0, 0)),
                  pl.BlockSpec((d, ROUTER_W), lambda i: (0, 0)),
                  pl.BlockSpec((1, ROUTER_W), lambda i: (0, 0))],
        out_specs=[pl.BlockSpec((tm, d), lambda i: (i, 0)),
                   pl.BlockSpec((tm, ROUTER_W), lambda i: (i, 0))],
        out_shape=[jax.ShapeDtypeStruct(((n_x + n_c) * tm, d), F32),
                   jax.ShapeDtypeStruct(((n_x + n_c) * tm, ROUTER_W), F32)],
        compiler_params=_params(("parallel",)), name="moe_router",
    )(x2d, c2d, g.reshape(1, d), shift_x, scale_x, shift_c, scale_c, w_r, b_r)


def _row_copy(src_hbm, row, dst, slot, r, sem):
    return pltpu.make_async_copy(src_hbm.at[pl.ds(row, 1)], dst.at[slot, pl.ds(r, 1)], sem.at[slot])


def _expert_kernel(be_ref, tok_ref, nb_ref, h_hbm, wg_ref, wu_ref, wd_ref, ys_ref, xbuf, sem, *, bm):
    i = pl.program_id(0)
    nb = nb_ref[0]

    def start(blk, slot):
        def body(r, carry):
            _row_copy(h_hbm, tok_ref[blk * bm + r], xbuf, slot, r, sem).start()
            return carry
        lax.fori_loop(0, bm, body, 0, unroll=8)

    def wait(slot):
        pltpu.make_async_copy(h_hbm.at[pl.ds(0, bm)], xbuf.at[slot], sem.at[slot]).wait()

    @pl.when(i == 0)
    def _():
        start(0, 0)

    @pl.when(i < nb)
    def _():
        slot = i % 2
        wait(slot)
        x = xbuf[slot].astype(BF16)
        nxt = jnp.minimum(i + 1, nb - 1) * bm
        for r in range(bm):
            _row_copy(h_hbm, tok_ref[nxt + r], xbuf, 1 - slot, r, sem).start()
        act = (_silu(_dot(x, wg_ref[0, 0])) * _dot(x, wu_ref[0, 0])).astype(BF16)
        ys_ref[...] = _dot(act, wd_ref[0, 0])

        @pl.when(i == nb - 1)
        def _():
            wait(1 - slot)

    @pl.when(i >= nb)
    def _():
        ys_ref[...] = jnp.zeros_like(ys_ref)


def _experts(h, buf_tok, block_e, n_used, wg, wu, wd, layer, bm):
    n_blocks = block_e.shape[0]
    d = h.shape[1]
    ff = wg.shape[3]
    grid_spec = pltpu.PrefetchScalarGridSpec(
        num_scalar_prefetch=3, grid=(n_blocks,),
        in_specs=[pl.BlockSpec(memory_space=pl.ANY),
                  pl.BlockSpec((1, 1, d, ff), lambda i, be, tok, nb: (layer, be[i], 0, 0)),
                  pl.BlockSpec((1, 1, d, ff), lambda i, be, tok, nb: (layer, be[i], 0, 0)),
                  pl.BlockSpec((1, 1, ff, d), lambda i, be, tok, nb: (layer, be[i], 0, 0))],
        out_specs=pl.BlockSpec((bm, d), lambda i, be, tok, nb: (i, 0)),
        scratch_shapes=[pltpu.VMEM((2, bm, d), F32), pltpu.SemaphoreType.DMA((2,))])
    return pl.pallas_call(
        functools.partial(_expert_kernel, bm=bm), grid_spec=grid_spec,
        out_shape=jax.ShapeDtypeStruct((n_blocks * bm, d), F32),
        compiler_params=_params(("arbitrary",)), name="moe_experts",
    )(block_e, buf_tok, n_used, h, wg, wu, wd)


def _combine_kernel(pos_ref, ys_hbm, x_ref, gw_ref, gate_ref, fg_ref, o_ref, ybuf, sem, *, tm, tok0, final):
    i = pl.program_id(0)
    n = pl.num_programs(0)

    def start(blk, slot):
        def body(r, carry):
            t = tok0 + blk * tm + r
            _row_copy(ys_hbm, pos_ref[2 * t], ybuf, slot, r, sem).start()
            _row_copy(ys_hbm, pos_ref[2 * t + 1], ybuf, slot, tm + r, sem).start()
            return carry
        lax.fori_loop(0, tm, body, 0, unroll=8)

    @pl.when(i == 0)
    def _():
        start(0, 0)

    @pl.when(i + 1 < n)
    def _():
        start(i + 1, (i + 1) % 2)

    slot = i % 2

    pltpu.make_async_copy(ys_hbm.at[pl.ds(0, 2 * tm)], ybuf.at[slot], sem.at[slot]).wait()
    gw = gw_ref[...]
    y = gw[:, 0:1] * ybuf[slot, 0:tm, :] + gw[:, 1:2] * ybuf[slot, tm:2 * tm, :]
    out = x_ref[...] + gate_ref[0] * y
    if final:
        out = _rms(out, fg_ref[...])
    o_ref[...] = out


def _combine(ys, pos, gw, x2d, gate, final_g, tok0, rows_per_mod, final):
    m, d = x2d.shape
    tm = 256
    per = rows_per_mod // tm
    gb = tok0 // tm
    grid_spec = pltpu.PrefetchScalarGridSpec(
        num_scalar_prefetch=1, grid=(m // tm,),
        in_specs=[pl.BlockSpec(memory_space=pl.ANY),
                  pl.BlockSpec((tm, d), lambda i, pos: (i, 0)),
                  pl.BlockSpec((tm, TOP_K), lambda i, pos: (gb + i, 0)),
                  pl.BlockSpec((1, 1, d), lambda i, pos: (i // per, 0, 0)),
                  pl.BlockSpec((1, d), lambda i, pos: (0, 0))],
        out_specs=pl.BlockSpec((tm, d), lambda i, pos: (i, 0)),
        scratch_shapes=[pltpu.VMEM((2, 2 * tm, d), F32), pltpu.SemaphoreType.DMA((2,))])
    return pl.pallas_call(
        functools.partial(_combine_kernel, tm=tm, tok0=tok0, final=final), grid_spec=grid_spec,
        out_shape=jax.ShapeDtypeStruct((m, d), F32),
        compiler_params=_params(("arbitrary",)), name="moe_combine",
    )(pos, ys, x2d, gw, gate, final_g.reshape(1, d))


def _route(logits):
    t = logits.shape[0]
    p_grp = jax.nn.softmax(logits[:, :N_GROUPS], axis=-1)
    p_sel = jnp.max(p_grp, axis=-1, keepdims=True)
    grp = jnp.argmax(p_grp, axis=-1).astype(jnp.int32)[:, None]
    fine = logits[:, N_GROUPS:N_GROUPS + N_EXPERTS].reshape(t, N_GROUPS, EXPERTS_PER_GROUP)
    in_grp = (jnp.arange(N_GROUPS, dtype=jnp.int32)[None, :] == grp)[:, :, None]
    fine = jnp.sum(jnp.where(in_grp, fine, 0.0), axis=1)
    lanes = jnp.arange(EXPERTS_PER_GROUP, dtype=jnp.int32)[None, :]
    i1 = jnp.argmax(fine, axis=-1).astype(jnp.int32)[:, None]
    v1 = jnp.max(fine, axis=-1, keepdims=True)
    rest = jnp.where(lanes == i1, -jnp.inf, fine)
    i2 = jnp.argmax(rest, axis=-1).astype(jnp.int32)[:, None]
    v2 = jnp.max(rest, axis=-1, keepdims=True)
    top_v = jnp.concatenate([v1, v2], axis=-1)
    top_i = jnp.concatenate([i1, i2], axis=-1)
    gate = p_sel * jax.nn.softmax(top_v, axis=-1)
    return grp * EXPERTS_PER_GROUP + top_i, gate


def _rank_kernel(e_ref, rank_ref, cnt_ref, run_scr):
    @pl.when(pl.program_id(0) == 0)
    def _():
        run_scr[...] = jnp.zeros_like(run_scr)

    tb = e_ref.shape[0]
    lane = lax.broadcasted_iota(jnp.int32, (tb, LANE), 1)
    onehot = jnp.where(e_ref[...] == lane, 1.0, 0.0)
    ri = lax.broadcasted_iota(jnp.int32, (tb, tb), 0)
    ci = lax.broadcasted_iota(jnp.int32, (tb, tb), 1)
    before = jnp.where(ri > ci, 1.0, 0.0).astype(BF16)
    prefix = _dot(before, onehot.astype(BF16)) + run_scr[...]
    rank_ref[...] = jnp.sum(prefix * onehot, axis=1, keepdims=True).astype(jnp.int32)
    run_scr[...] += jnp.sum(onehot, axis=0, keepdims=True)
    cnt_ref[...] = run_scr[...]


def _expert_ranks(flat_e):
    a = flat_e.shape[0]
    tb = 512
    rank, cnt = pl.pallas_call(
        _rank_kernel, grid=(a // tb,),
        in_specs=[pl.BlockSpec((tb, 1), lambda i: (i, 0))],
        out_specs=[pl.BlockSpec((tb, 1), lambda i: (i, 0)), pl.BlockSpec((1, LANE), lambda i: (0, 0))],
        out_shape=[jax.ShapeDtypeStruct((a, 1), jnp.int32), jax.ShapeDtypeStruct((1, LANE), F32)],
        scratch_shapes=[pltpu.VMEM((1, LANE), F32)],
        compiler_params=_params(("arbitrary",)), name="moe_rank",
    )(flat_e.reshape(a, 1))
    return rank.reshape(a), cnt[0, :N_EXPERTS].astype(jnp.int32)


def _dispatch_plan(expert, bm):
    t = expert.shape[0]
    a = t * TOP_K
    flat_e = expert.reshape(a).astype(jnp.int32)
    rank, counts = _expert_ranks(flat_e)
    padded = (counts + bm - 1) // bm * bm
    pad_end = jnp.cumsum(padded)
    pad_start = pad_end - padded
    mine = flat_e[:, None] == jnp.arange(N_EXPERTS, dtype=jnp.int32)[None, :]
    pos = (jnp.sum(jnp.where(mine, pad_start[None, :], 0), axis=1) + rank).astype(jnp.int32)
    n_blocks = a // bm + N_EXPERTS
    flat_tok = jnp.arange(a, dtype=jnp.int32) // TOP_K
    buf_tok = jnp.zeros((n_blocks * bm,), jnp.int32).at[pos].set(flat_tok)
    n_used = (pad_end[-1] // bm).astype(jnp.int32)
    blk = jnp.minimum(jnp.arange(n_blocks, dtype=jnp.int32), n_used - 1) * bm
    block_e = jnp.sum((pad_end[None, :] <= blk[:, None]).astype(jnp.int32), axis=1)
    block_e = jnp.minimum(block_e, N_EXPERTS - 1).astype(jnp.int32)
    return pos, buf_tok, block_e, n_used.reshape(1)


def _mix_stream(p3, ab, rows, width, s0_f, s0_b, four_tabs, hy_tabs, hy_coefs, conv_w, conv_b, four_g,
                hy_bias, hy_g, a_log, dt_bias, with_mixers):
    qkv = _short_conv(p3, conv_w, conv_b, rows, width, OFF_DN, HY_CONV_CH, 3 * DN_W, 4 * DN_DK, True)
    o_f, o_b, s_f, s_b = _delta_rule(qkv, ab, a_log, dt_bias, s0_f, s0_b)
    if not with_mixers:
        return None, None, o_f, o_b, s_f, s_b
    y_four = _fourier(p3, *four_tabs, four_g)
    convh = _short_conv(p3, conv_w, conv_b, rows, width, OFF_HY, 0, HY_CONV_CH, 256, False)
    y_hy = _hyena(convh, hy_tabs, hy_coefs, hy_bias, hy_g)
    return y_four, y_hy, o_f, o_b, s_f, s_b


def kernel(x, c, ctx, c_ctx, norm1_g, norm2_g, w_mod, b_mod, w_in, conv_w, conv_b, four_g, hy_w1, hy_b1, hy_w2, hy_b2, hy_w3, hy_b3, hy_w4, hy_freq, hy_bias, hy_g, dn_a_log, dn_dt_bias, dn_g, w_out, w_rc, b_rc, w_rf, b_rf, w_e_gate, w_e_up, w_e_down, final_g):
    bsz, length, d = x.shape
    lc = ctx.shape[1]
    depth = w_in.shape[0]
    rows = length // GRID_W
    bm = 256

    mod_all = _modulation(c, c_ctx, w_mod, b_mod)
    four_x, four_c = _fourier_tables(length), _fourier_tables(lc)
    hy_x, hy_c = _hyena_tables(length), _hyena_tables(lc)
    zeros = jnp.zeros((bsz, DN_HEADS, DN_DK, DN_DV), F32)
    x2 = x.reshape(bsz * length, d)
    c2 = ctx.reshape(bsz * lc, d)
    wg16, wu16, wd16 = w_e_gate.astype(BF16), w_e_up.astype(BF16), w_e_down.astype(BF16)

    for l in range(depth):
        last = l == depth - 1
        mod = [m[:, None, :] for m in jnp.split(mod_all[l, :bsz], 6, axis=-1)]
        modc = [m[:, None, :] for m in jnp.split(mod_all[l, bsz:bsz + 1], 6, axis=-1)]
        wi = w_in[l]
        w_p = jnp.concatenate([wi[:, :OFF_AB], wi[:, OFF_Z:], wi[:, OFF_AB:OFF_Z],
                               jnp.zeros((d, LANE - N_AB), wi.dtype)], axis=1).astype(BF16)
        w_o = w_out[l].astype(BF16)
        hy_params = (hy_w1[l], hy_b1[l], hy_w2[l], hy_b2[l], hy_w3[l], hy_b3[l], hy_w4[l], hy_freq[l])
        mix_args = (conv_w[l], conv_b[l], four_g[l], hy_bias[l], hy_g[l], dn_a_log[l], dn_dt_bias[l])

        pc, abc = _proj_in(c2, norm1_g[l], modc[0], modc[1], w_p, bsz * lc)
        pc3 = pc.reshape(bsz, lc, PROJ_W)
        abc = abc[:, :N_AB].reshape(bsz, lc, N_AB)
        coefs_c = None if last else _hyena_coefs(lc, hy_c, hy_params)
        yf, yh, o_f, o_b, s_f, s_b = _mix_stream(pc3, abc, 1, lc, zeros, zeros, four_c, hy_c, coefs_c,
                                                 *mix_args, with_mixers=not last)
        if not last:
            c2 = _proj_out(yf.reshape(-1, FOUR_W), yh.reshape(-1, HY_W), o_f.reshape(-1, DN_W),
                           o_b.reshape(-1, DN_W), pc, dn_g[l], w_o, c2, modc[2], bsz * lc)

        p, abx = _proj_in(x2, norm1_g[l], mod[0], mod[1], w_p, length)
        p3 = p.reshape(bsz, length, PROJ_W)
        abx = abx[:, :N_AB].reshape(bsz, length, N_AB)
        coefs_x = _hyena_coefs(length, hy_x, hy_params)
        yf, yh, o_f, o_b, _, _ = _mix_stream(p3, abx, rows, GRID_W, s_f, s_b, four_x, hy_x, coefs_x,
                                             *mix_args, with_mixers=True)
        x2 = _proj_out(yf.reshape(-1, FOUR_W), yh.reshape(-1, HY_W), o_f.reshape(-1, DN_W),
                       o_b.reshape(-1, DN_W), p, dn_g[l], w_o, x2, mod[2], length)

        w_r = jnp.concatenate([w_rc[l], w_rf[l], jnp.zeros((d, ROUTER_W - N_GROUPS - N_EXPERTS), F32)], axis=1)
        b_r = jnp.concatenate([b_rc[l], b_rf[l], jnp.zeros((ROUTER_W - N_GROUPS - N_EXPERTS,), F32)])[None, :]
        h, logits = _router(x2, c2, norm2_g[l], mod[3], mod[4], modc[3], modc[4], w_r, b_r, length,
                            with_ctx=not last)
        expert, gate = _route(logits)
        pos, buf_tok, block_e, n_used = _dispatch_plan(expert, bm)
        ys = _experts(h, buf_tok, block_e, n_used, wg16, wu16, wd16, l, bm)
        x2 = _combine(ys, pos, gate, x2, mod[5], final_g, 0, length, final=last)
        if not last:
            c2 = _combine(ys, pos, gate, c2, modc[5], final_g, bsz * length, bsz * lc, final=False)
    return x2.reshape(bsz, length, d)
```

```python
import functools
import math

import jax
import jax.numpy as jnp
from jax import lax
from jax.experimental import pallas as pl
from jax.experimental.pallas import tpu as pltpu

F32 = jnp.float32
BF16 = jnp.bfloat16
HIGHEST = lax.Precision.HIGHEST

GRID_W = 64
FOUR_W = 512
FOUR_GROUPS = 4
FOUR_GW = FOUR_W // FOUR_GROUPS
HY_W = 512
HY_ORDER = 2
DN_HEADS = 8
DN_DK = 128
DN_DV = 128
DN_W = DN_HEADS * DN_DV
DN_CHUNK = 64
DN_SOLVE_BLOCK = DN_CHUNK // 4
HY_CONV_CH = (HY_ORDER + 1) * HY_W
OFF_HY = FOUR_W
OFF_DN = OFF_HY + HY_CONV_CH
OFF_AB = OFF_DN + 3 * DN_W
N_AB = 4 * DN_HEADS
OFF_Z = OFF_AB + N_AB
HY_EMB = 33
HY_FAST_DECAY = 0.3
HY_SLOW_DECAY = 1.5
HY_TARGET = 1e-2
N_GROUPS = 4
EXPERTS_PER_GROUP = 8
N_EXPERTS = N_GROUPS * EXPERTS_PER_GROUP
TOP_K = 2
EPS = 1e-6

LANE = 128
PCOL_Z = OFF_AB
PCOL_AB = PCOL_Z + DN_W
MXU_W = 256
PROJ_TN = 5 * MXU_W
PROJ_W = -(-(PCOL_AB + LANE) // PROJ_TN) * PROJ_TN
ROUTER_W = LANE
VMEM_LIMIT = 56 * 1024 * 1024


def _params(semantics):
    return pltpu.CompilerParams(dimension_semantics=semantics, vmem_limit_bytes=VMEM_LIMIT)


def _silu(v):
    return v * jax.nn.sigmoid(v)


def _rms(v, g):
    return v * lax.rsqrt(jnp.mean(v * v, axis=-1, keepdims=True) + EPS) * g


def _dot(a, b):
    return jnp.dot(a, b, preferred_element_type=F32)


def _mod_kernel(a_ref, w_ref, b_ref, o_ref):
    o_ref[0] = jnp.dot(_silu(a_ref[...]), w_ref[0], precision=HIGHEST,
                       preferred_element_type=F32) + b_ref[0]


def _modulation(c, c_ctx, w_mod, b_mod):
    depth, d, n = w_mod.shape
    bsz = c.shape[0]
    rows = -(-(bsz + 1) // 8) * 8
    a = jnp.concatenate([c, c_ctx[None], jnp.zeros((rows - bsz - 1, d), F32)], axis=0)
    tn = 1024 if n % 1024 == 0 else 512
    assert n % tn == 0
    return pl.pallas_call(
        _mod_kernel, grid=(depth, n // tn),
        in_specs=[pl.BlockSpec((rows, d), lambda l, j: (0, 0)),
                  pl.BlockSpec((1, d, tn), lambda l, j: (l, 0, j)),
                  pl.BlockSpec((1, 1, tn), lambda l, j: (l, 0, j))],
        out_specs=pl.BlockSpec((1, rows, tn), lambda l, j: (l, 0, j)),
        out_shape=jax.ShapeDtypeStruct((depth, rows, n), F32),
        compiler_params=_params(("parallel", "parallel")), name="modulation",
    )(a, w_mod, b_mod.reshape(depth, 1, n))


def _proj_in_kernel(x_ref, g_ref, shift_ref, scale_ref, w_ref, o_ref, ab_ref, a_scr, *, ab_off):
    j = pl.program_id(1)

    @pl.when(j == 0)
    def _():
        y = _rms(x_ref[...], g_ref[...])
        a_scr[...] = (y * (1.0 + scale_ref[0]) + shift_ref[0]).astype(BF16)

    acc = _dot(a_scr[...], w_ref[...])
    o_ref[...] = acc.astype(BF16)

    @pl.when(j == pl.num_programs(1) - 1)
    def _():
        ab_ref[...] = acc[:, ab_off:ab_off + LANE]


def _proj_in(x2d, g, shift, scale, w, rows_per_mod):
    m, d = x2d.shape
    n = w.shape[1]
    tm = min(1024, m, rows_per_mod)
    tn = PROJ_TN
    assert n % tn == 0 and PCOL_AB >= n - tn
    per = rows_per_mod // tm
    return pl.pallas_call(
        functools.partial(_proj_in_kernel, ab_off=PCOL_AB - (n - tn)), grid=(m // tm, n // tn),
        in_specs=[pl.BlockSpec((tm, d), lambda i, j: (i, 0)),
                  pl.BlockSpec((1, d), lambda i, j: (0, 0)),
                  pl.BlockSpec((1, 1, d), lambda i, j: (i // per, 0, 0)),
                  pl.BlockSpec((1, 1, d), lambda i, j: (i // per, 0, 0)),
                  pl.BlockSpec((d, tn), lambda i, j: (0, j))],
        out_specs=[pl.BlockSpec((tm, tn), lambda i, j: (i, j)),
                   pl.BlockSpec((tm, LANE), lambda i, j: (i, 0))],
        out_shape=[jax.ShapeDtypeStruct((m, n), BF16), jax.ShapeDtypeStruct((m, LANE), F32)],
        scratch_shapes=[pltpu.VMEM((tm, d), BF16)],
        compiler_params=_params(("parallel", "arbitrary")), name="proj_in",
    )(x2d, g.reshape(1, d), shift, scale, w)


def _cos_sin_table(nrow, ncol, period):
    i = lax.broadcasted_iota(jnp.int32, (nrow, ncol), 0)
    j = lax.broadcasted_iota(jnp.int32, (nrow, ncol), 1)
    ang = ((i * j) % period).astype(F32) * (2.0 * math.pi / period)
    return jnp.cos(ang), jnp.sin(ang)


def _dft_tables(length, period):
    blk = 64 if length % 64 == 0 else length
    hi_i = lax.broadcasted_iota(jnp.int32, (length // blk, length), 0) * blk
    j = lax.broadcasted_iota(jnp.int32, (length // blk, length), 1)
    ang = ((hi_i * j) % period).astype(F32) * (2.0 * math.pi / period)
    hi_c, hi_s = jnp.cos(ang), jnp.sin(ang)
    lo_c, lo_s = _cos_sin_table(blk, length, period)
    c = hi_c[:, None, :] * lo_c[None, :, :] - hi_s[:, None, :] * lo_s[None, :, :]
    s = hi_s[:, None, :] * lo_c[None, :, :] + hi_c[:, None, :] * lo_s[None, :, :]
    return c.reshape(length, length), s.reshape(length, length)


def _fourier_kernel(u_ref, c_ref, s_ref, bc_ref, bs_ref, g_ref, o_ref, ub_scr, *, scale):
    @pl.when(pl.program_id(1) == 0)
    def _():
        ub_scr[...] = u_ref[0].astype(BF16)

    ub = ub_scr[...]
    p = _dot(c_ref[...], ub).astype(BF16)
    q = _dot(s_ref[...], ub).astype(BF16)
    z = (_dot(p, bc_ref[...]) - _dot(q, bs_ref[...])) * scale
    o_ref[0] = _rms(z, g_ref[...]).astype(o_ref.dtype)


def _fourier(p3, cl, sl, bc, bs, four_g):
    bsz, length, _ = p3.shape
    tm = min(512, length)
    kern = functools.partial(_fourier_kernel, scale=1.0 / math.sqrt(length * FOUR_GW))
    return pl.pallas_call(
        kern, grid=(bsz, length // tm),
        in_specs=[pl.BlockSpec((1, length, FOUR_W), lambda b, i: (b, 0, 0)),
                  pl.BlockSpec((tm, length), lambda b, i: (i, 0)),
                  pl.BlockSpec((tm, length), lambda b, i: (i, 0)),
                  pl.BlockSpec((FOUR_W, FOUR_W), lambda b, i: (0, 0)),
                  pl.BlockSpec((FOUR_W, FOUR_W), lambda b, i: (0, 0)),
                  pl.BlockSpec((1, FOUR_W), lambda b, i: (0, 0))],
        out_specs=pl.BlockSpec((1, tm, FOUR_W), lambda b, i: (b, i, 0)),
        out_shape=jax.ShapeDtypeStruct((bsz, length, FOUR_W), BF16),
        scratch_shapes=[pltpu.VMEM((length, FOUR_W), BF16)],
        compiler_params=_params(("parallel", "arbitrary")), name="fourier",
    )(p3, cl, sl, bc, bs, four_g.reshape(1, FOUR_W))


def _fourier_tables(length):
    cl, sl = _dft_tables(length, length)
    cc, sc = _cos_sin_table(FOUR_GW, FOUR_GW, FOUR_GW)
    eye = jnp.eye(FOUR_GROUPS, dtype=F32)
    return cl.astype(BF16), sl.astype(BF16), jnp.kron(eye, cc).astype(BF16), jnp.kron(eye, sc).astype(BF16)


def _conv_kernel(x_ref, w_ref, b_ref, o_ref, *, rows, width, heads_mode):
    tc = x_ref.shape[2]
    wts = w_ref[...]
    bias = b_ref[...]
    pos = lax.broadcasted_iota(jnp.int32, (width, tc), 0)
    not_first = pos > 0
    not_last = pos < width - 1
    kind = pl.program_id(1) // (DN_W // tc)

    def body(r, carry):
        base = pl.multiple_of(r * width, width)
        cen = x_ref[0, pl.ds(base, width), :].astype(F32)
        if rows > 1:
            up = x_ref[0, pl.ds(pl.multiple_of(jnp.maximum(r - 1, 0) * width, width), width), :].astype(F32)
            dn = x_ref[0, pl.ds(pl.multiple_of(jnp.minimum(r + 1, rows - 1) * width, width), width), :].astype(F32)
            w_up = wts[0] * jnp.where(r > 0, 1.0, 0.0)
            w_dn = wts[2] * jnp.where(r < rows - 1, 1.0, 0.0)
            col = lambda j: up * w_up[j:j + 1, :] + cen * wts[1, j:j + 1, :] + dn * w_dn[j:j + 1, :]
        else:
            col = lambda j: cen * wts[1, j:j + 1, :]
        left = jnp.where(not_first, pltpu.roll(col(0), 1, 0), 0.0)
        right = jnp.where(not_last, pltpu.roll(col(2), width - 1, 0), 0.0)
        acc = left + col(1) + right + bias
        if heads_mode:
            for h in range(tc // DN_DK):
                s = _silu(acc[:, h * DN_DK:(h + 1) * DN_DK])
                inv = lax.rsqrt(jnp.sum(s * s, axis=-1, keepdims=True) + EPS)
                fac = jnp.where(kind == 0, inv * (DN_DK ** -0.5), jnp.where(kind == 1, inv, 1.0))
                o_ref[0, pl.ds(base, width), h * DN_DK:(h + 1) * DN_DK] = s * fac
        else:
            o_ref[0, pl.ds(base, width), :] = acc
        return carry

    lax.fori_loop(0, rows, body, 0)


def _short_conv(p3, conv_w, conv_b, rows, width, col0, ch0, nch, tc, heads_mode, split=1):
    bsz, length, _ = p3.shape
    kern = functools.partial(_conv_kernel, rows=rows, width=width, heads_mode=heads_mode)
    cb, wb = col0 // tc, ch0 // tc
    per = nch // split // tc
    out = pl.pallas_call(
        kern, grid=(bsz, nch // tc),
        in_specs=[pl.BlockSpec((1, length, tc), lambda b, j: (b, 0, cb + j)),
                  pl.BlockSpec((3, 3, tc), lambda b, j: (0, 0, wb + j)),
                  pl.BlockSpec((1, tc), lambda b, j: (0, wb + j))],
        out_specs=pl.BlockSpec((1, length, tc), lambda b, j: ((j // per) * bsz + b, 0, j % per)),
        out_shape=jax.ShapeDtypeStruct((split * bsz, length, nch // split), F32),
        compiler_params=_params(("parallel", "parallel")),
        name="conv_heads" if heads_mode else "conv_hyena",
    )(p3, conv_w, conv_b.reshape(1, -1))
    return out.reshape(split, bsz, length, nch // split)


def _filt_kernel(z_ref, w1_ref, b1_ref, w2_ref, b2_ref, w3_ref, b3_ref, w4_ref, fr_ref, dl_ref,
                 g_ref, nrm_ref):
    i = pl.program_id(0)
    z = z_ref[...]
    fr = fr_ref[...]
    hdot = lambda a, b: jnp.dot(a, b, precision=HIGHEST, preferred_element_type=F32)
    h = jnp.sin(fr * (hdot(z, w1_ref[...]) + b1_ref[...]))
    h = jnp.sin(fr * (hdot(h, w2_ref[...]) + b2_ref[...]))
    h = jnp.sin(fr * (hdot(h, w3_ref[...]) + b3_ref[...]))
    h = hdot(h, w4_ref[...])
    decay = jnp.exp(-z[:, 0:1] * jnp.abs(dl_ref[...]))
    decay = jnp.concatenate([decay] * HY_ORDER, axis=1)
    half = HY_ORDER * HY_W
    hf = h[:, :half] * decay
    hb = h[:, half:] * decay
    row = lax.broadcasted_iota(jnp.int32, hb.shape, 0) + i * hb.shape[0]
    hb = jnp.where(row > 0, hb, 0.0)
    gp = hf + hb
    gm = hf - hb
    for o in range(HY_ORDER):
        g_ref[o] = gp[:, o * HY_W:(o + 1) * HY_W].astype(BF16)
        g_ref[HY_ORDER + o] = gm[:, o * HY_W:(o + 1) * HY_W].astype(BF16)
    part = jnp.sum(jnp.abs(hf) + jnp.abs(hb), axis=0, keepdims=True)

    @pl.when(i == 0)
    def _():
        nrm_ref[...] = part

    @pl.when(i > 0)
    def _():
        nrm_ref[...] += part


def _hyena_filter_taps(length, w1, b1, w2, b2, w3, b3, w4, freq):
    t = jnp.linspace(0.0, 1.0, length, dtype=F32)[:, None]
    bands = (HY_EMB - 1) // 2
    ang = (2.0 * math.pi / length) * jnp.arange(length, dtype=F32)[:, None]
    f = jnp.linspace(1e-4, bands - 1, bands, dtype=F32)
    z = jnp.concatenate([t, jnp.cos(f * ang), -jnp.sin(f * ang)], axis=-1)
    max_decay = math.log(HY_TARGET) / HY_FAST_DECAY
    min_decay = math.log(HY_TARGET) / HY_SLOW_DECAY
    deltas = jnp.linspace(min_decay, max_decay, HY_W, dtype=F32)[None, :]
    fw = w1.shape[1]
    tl = min(512, length)
    full = lambda shape: pl.BlockSpec(shape, lambda i: (0,) * len(shape))
    return pl.pallas_call(
        _filt_kernel, grid=(length // tl,),
        in_specs=[pl.BlockSpec((tl, HY_EMB), lambda i: (i, 0)),
                  full((HY_EMB, fw)), full((1, fw)), full((fw, fw)), full((1, fw)),
                  full((fw, fw)), full((1, fw)), full((fw, 2 * HY_ORDER * HY_W)), full((1, fw)),
                  full((1, HY_W))],
        out_specs=[pl.BlockSpec((2 * HY_ORDER, tl, HY_W), lambda i: (0, i, 0)),
                   pl.BlockSpec((1, HY_ORDER * HY_W), lambda i: (0, 0))],
        out_shape=[jax.ShapeDtypeStruct((2 * HY_ORDER, length, HY_W), BF16),
                   jax.ShapeDtypeStruct((1, HY_ORDER * HY_W), F32)],
        compiler_params=_params(("arbitrary",)), name="hyena_filter",
    )(z, w1, b1.reshape(1, fw), w2, b2.reshape(1, fw), w3, b3.reshape(1, fw), w4,
      freq.reshape(1, fw), deltas)


def _dft_fwd_kernel(u_ref, c_ref, s_ref, *rest, with_coef):
    if with_coef:
        c1_ref, c2_ref, c4_ref, yr_ref, yi_ref, ub_scr = rest
    else:
        yr_ref, yi_ref, ub_scr = rest

    @pl.when(pl.program_id(1) == 0)
    def _():
        ub_scr[...] = u_ref[0].astype(BF16)

    ub = ub_scr[...]
    a = _dot(c_ref[...], ub)
    b = _dot(s_ref[...], ub)
    if with_coef:
        c2 = c2_ref[...]
        yr_ref[0] = (a * c1_ref[...] + b * c2).astype(BF16)
        yi_ref[0] = (b * c4_ref[...] - a * c2).astype(BF16)
    else:
        yr_ref[0] = a
        yi_ref[0] = b


def _dft_fwd(u3, col_blk, cf, sf, coef=None):
    nb, length, _ = u3.shape
    tm = min(512, length)
    in_specs = [pl.BlockSpec((1, length, HY_W), lambda b, i: (b, 0, col_blk)),
                pl.BlockSpec((tm, length), lambda b, i: (i, 0)),
                pl.BlockSpec((tm, length), lambda b, i: (i, 0))]
    args = [u3, cf, sf]
    if coef is not None:
        in_specs += [pl.BlockSpec((tm, HY_W), lambda b, i: (i, 0))] * 3
        args += list(coef)
    odt = BF16 if coef is not None else F32
    return pl.pallas_call(
        functools.partial(_dft_fwd_kernel, with_coef=coef is not None), grid=(nb, length // tm),
        in_specs=in_specs,
        out_specs=[pl.BlockSpec((1, tm, HY_W), lambda b, i: (b, i, 0))] * 2,
        out_shape=[jax.ShapeDtypeStruct((nb, length, HY_W), odt)] * 2,
        scratch_shapes=[pltpu.VMEM((length, HY_W), BF16)],
        compiler_params=_params(("parallel", "arbitrary")),
        name="hyena_dft_fwd" if coef is not None else "hyena_filter_dft",
    )(*args)


def _dft_inv_kernel(yr_ref, yi_ref, c_ref, s_ref, u_ref, gate_ref, bias_ref, g_ref, o_ref, *, final):
    y = _dot(c_ref[...], yr_ref[0]) + _dot(s_ref[...], yi_ref[0])
    out = gate_ref[0] * (y + bias_ref[...] * u_ref[0])
    if final:
        out = _rms(out, g_ref[...])
    o_ref[0] = out.astype(o_ref.dtype)


def _dft_inv(yr, yi, cf, s_inv, u3, u_blk, gate3, gate_blk, bias, g, final):
    bsz, length, _ = yr.shape
    tm = min(512, length)
    return pl.pallas_call(
        functools.partial(_dft_inv_kernel, final=final), grid=(bsz, length // tm),
        in_specs=[pl.BlockSpec((1, length, HY_W), lambda b, i: (b, 0, 0)),
                  pl.BlockSpec((1, length, HY_W), lambda b, i: (b, 0, 0)),
                  pl.BlockSpec((tm, length), lambda b, i: (i, 0)),
                  pl.BlockSpec((tm, length), lambda b, i: (i, 0)),
                  pl.BlockSpec((1, tm, HY_W), lambda b, i: (b, i, u_blk)),
                  pl.BlockSpec((1, tm, HY_W), lambda b, i: (b, i, gate_blk)),
                  pl.BlockSpec((1, HY_W), lambda b, i: (0, 0)),
                  pl.BlockSpec((1, HY_W), lambda b, i: (0, 0))],
        out_specs=pl.BlockSpec((1, tm, HY_W), lambda b, i: (b, i, 0)),
        out_shape=jax.ShapeDtypeStruct((bsz, length, HY_W), BF16 if final else F32),
        compiler_params=_params(("parallel", "arbitrary")), name="hyena_dft_inv",
    )(yr, yi, cf, s_inv, u3, gate3, bias.reshape(1, HY_W), g.reshape(1, HY_W))


def _hyena_tables(length):
    cf, sf = _dft_tables(length, 2 * length)
    alt = jnp.where(jnp.arange(length) % 2 == 0, 1.0, -1.0).astype(F32)
    sf = sf.at[0, :].set(alt)
    return cf.astype(BF16), sf.astype(BF16), sf.T.astype(BF16)


def _hyena_coefs(length, tables, hy_params):
    cf, sf, _ = tables
    taps, nrm = _hyena_filter_taps(length, *hy_params)
    a, b = _dft_fwd(taps, 0, cf, sf)
    inv = (1.0 / nrm).reshape(HY_ORDER, 1, HY_W)
    n = 2.0 * length
    hr = a[:HY_ORDER] * inv
    hi = -b[HY_ORDER:] * inv
    nyq = b[:HY_ORDER, 0:1, :] * inv
    first = (jnp.arange(length) == 0)[None, :, None]
    c1 = jnp.where(first, hr / n, hr * (2.0 / n))
    c2 = jnp.where(first, 0.0, hi * (2.0 / n))
    c4 = jnp.where(first, nyq / n, hr * (2.0 / n))
    return c1, c2, c4


def _hyena(convh, tables, coefs, hy_bias, hy_g):
    cf, sf, s_inv = tables
    c1, c2, c4 = coefs
    yr, yi = _dft_fwd(convh, 0, cf, sf, (c1[0], c2[0], c4[0]))
    zz = _dft_inv(yr, yi, cf, s_inv, convh, 0, convh, 1, hy_bias[0], hy_g, False)
    yr, yi = _dft_fwd(zz, 0, cf, sf, (c1[1], c2[1], c4[1]))
    return _dft_inv(yr, yi, cf, s_inv, zz, 0, convh, 2, hy_bias[1], hy_g, True)


HY_FAST = 128


def _hy2_tables(length):
    nk = length // HY_FAST
    n2 = 2 * length
    k1 = lax.broadcasted_iota(jnp.int32, (nk, nk), 0)
    s1 = lax.broadcasted_iota(jnp.int32, (nk, nk), 1)
    ang = (((2 * k1 + 1) * s1) % (4 * nk)).astype(F32) * (2.0 * math.pi / (4 * nk))
    f1 = jnp.concatenate([jnp.cos(ang), -jnp.sin(ang)], axis=0)
    shape = (nk, HY_FAST, HY_FAST)
    kk = lax.broadcasted_iota(jnp.int32, shape, 0) + 2 * nk * lax.broadcasted_iota(jnp.int32, shape, 1)
    s2 = lax.broadcasted_iota(jnp.int32, shape, 2)
    phi = (((2 * kk + 1) * s2) % (2 * n2)).astype(F32) * (2.0 * math.pi / (2 * n2))
    c, s = jnp.cos(phi), jnp.sin(phi)
    f2 = jnp.concatenate([jnp.concatenate([c, s], axis=2), jnp.concatenate([-s, c], axis=2)], axis=1)
    return f1.astype(BF16), f2.astype(BF16), jnp.swapaxes(f2, 1, 2).astype(BF16), f1.T.astype(BF16)


def _hy2_stage1_kernel(x_ref, f1_ref, z_ref):
    z_ref[0] = _dot(f1_ref[...], x_ref[0, 0].astype(BF16)).astype(BF16)


def _hy2_stage1(x4, which, f1):
    _, nb, nk, cols = x4.shape
    tc = min(8192, cols)
    return pl.pallas_call(
        _hy2_stage1_kernel, grid=(nb, cols // tc),
        in_specs=[pl.BlockSpec((1, 1, nk, tc), lambda b, j: (which, b, 0, j)),
                  pl.BlockSpec((2 * nk, nk), lambda b, j: (0, 0))],
        out_specs=pl.BlockSpec((1, 2 * nk, tc), lambda b, j: (b, 0, j)),
        out_shape=jax.ShapeDtypeStruct((nb, 2 * nk, cols), BF16),
        compiler_params=_params(("parallel", "parallel")), name="hyena2_stage1",
    )(x4, f1)


def _hy2_stage2_kernel(zr_ref, zi_ref, f2_ref, *rest, filtered):
    if filtered:
        g2_ref, hr_ref, hi_ref, vr_ref, vi_ref = rest
    else:
        vr_ref, vi_ref = rest
    for k in range(zr_ref.shape[2]):
        x = _dot(f2_ref[k], jnp.concatenate([zr_ref[0, 0, k], zi_ref[0, 0, k]], axis=0))
        xr, xi = x[:HY_FAST], x[HY_FAST:]
        if filtered:
            hr, hi = hr_ref[k], hi_ref[k]
            y = jnp.concatenate([xr * hr - xi * hi, xr * hi + xi * hr], axis=0).astype(BF16)
            v = _dot(g2_ref[k], y)
            vr_ref[0, 0, k] = v[:HY_FAST].astype(BF16)
            vi_ref[0, 0, k] = v[HY_FAST:].astype(BF16)
        else:
            vr_ref[0, 0, k] = xr
            vi_ref[0, 0, k] = xi


def _hy2_stage2(z, f2, g2=None, hr=None, hi=None):
    nb, _, nk, _, cw = z.shape
    kb = 8
    filtered = hr is not None
    zspec = lambda part: pl.BlockSpec((1, 1, kb, HY_FAST, cw), lambda k, b: (b, part, k, 0, 0))
    mat = pl.BlockSpec((kb, 2 * HY_FAST, 2 * HY_FAST), lambda k, b: (k, 0, 0))
    in_specs = [zspec(0), zspec(1), mat]
    args = [z, z, f2]
    if filtered:
        in_specs += [mat, pl.BlockSpec((kb, HY_FAST, cw), lambda k, b: (k, 0, 0)),
                     pl.BlockSpec((kb, HY_FAST, cw), lambda k, b: (k, 0, 0))]
        args += [g2, hr, hi]
    return pl.pallas_call(
        functools.partial(_hy2_stage2_kernel, filtered=filtered), grid=(nk // kb, nb),
        in_specs=in_specs, out_specs=[zspec(0), zspec(0)],
        out_shape=[jax.ShapeDtypeStruct((nb, 1, nk, HY_FAST, cw), BF16 if filtered else F32)] * 2,
        compiler_params=_params(("parallel", "parallel")),
        name="hyena2_stage2" if filtered else "hyena2_filter_spectrum",
    )(*args)


def _hy2_inv1_kernel(vr_ref, vi_ref, g1_ref, u_ref, gate_ref, bias_ref, g_ref, o_ref, *, final):
    v = jnp.concatenate([vr_ref[0, 0], vi_ref[0, 0]], axis=0)
    out = gate_ref[0, 0] * (_dot(g1_ref[...], v) + bias_ref[...] * u_ref[0, 0])
    if final:
        for t in range(out.shape[1] // HY_W):
            seg = out[:, t * HY_W:(t + 1) * HY_W]
            o_ref[0, :, t * HY_W:(t + 1) * HY_W] = _rms(seg, g_ref[...]).astype(o_ref.dtype)
    else:
        o_ref[0] = out.astype(o_ref.dtype)


def _hy2_inv1(vr, vi, g1, u4, which_u, gate4, which_g, bias, g, final):
    nb, _, nk, _, cw = vr.shape
    cols = HY_FAST * cw
    tc = min(8192, cols)
    vspec = pl.BlockSpec((1, 1, nk, tc), lambda b, j: (b, 0, 0, j))
    v2 = lambda t: t.reshape(nb, 1, nk, cols)
    return pl.pallas_call(
        functools.partial(_hy2_inv1_kernel, final=final), grid=(nb, cols // tc),
        in_specs=[vspec, vspec,
                  pl.BlockSpec((nk, 2 * nk), lambda b, j: (0, 0)),
                  pl.BlockSpec((1, 1, nk, tc), lambda b, j: (which_u, b, 0, j)),
                  pl.BlockSpec((1, 1, nk, tc), lambda b, j: (which_g, b, 0, j)),
                  pl.BlockSpec((1, tc), lambda b, j: (0, 0)),
                  pl.BlockSpec((1, HY_W), lambda b, j: (0, 0))],
        out_specs=pl.BlockSpec((1, nk, tc), lambda b, j: (b, 0, j)),
        out_shape=jax.ShapeDtypeStruct((nb, nk, cols), BF16 if final else F32),
        compiler_params=_params(("parallel", "parallel")), name="hyena2_inv1",
    )(v2(vr), v2(vi), g1, u4, gate4, jnp.tile(bias.reshape(1, cw), (1, tc // cw)), g.reshape(1, cw))


def _hy2_coefs(length, tables, hy_params):
    f1, f2, _, _ = tables
    nk = length // HY_FAST
    taps, nrm = _hyena_filter_taps(length, *hy_params)
    z = _hy2_stage1(taps.reshape(1, 2 * HY_ORDER, nk, HY_FAST * HY_W), 0, f1)
    xr, xi = _hy2_stage2(z.reshape(2 * HY_ORDER, 2, nk, HY_FAST, HY_W), f2)
    scale = (1.0 / length) / nrm.reshape(HY_ORDER, 1, 1, HY_W)
    return xr[:HY_ORDER, 0] * scale, xi[HY_ORDER:, 0] * scale


def _hyena2(conv3, tables, coefs, hy_bias, hy_g):
    f1, f2, g2, g1 = tables
    hr, hi = coefs
    _, bsz, length, cw = conv3.shape
    nk = length // HY_FAST
    c4 = conv3.reshape(3, bsz, nk, HY_FAST * cw)

    def conv(x4, which, order, gate_idx, final):
        z = _hy2_stage1(x4, which, f1).reshape(bsz, 2, nk, HY_FAST, cw)
        vr, vi = _hy2_stage2(z, f2, g2, hr[order], hi[order])
        return _hy2_inv1(vr, vi, g1, x4, which, c4, gate_idx, hy_bias[order], hy_g, final)

    zz = conv(c4, 0, 0, 1, False)
    return conv(zz[None], 0, 1, 2, True).reshape(bsz, length, cw)


def _softplus(v):
    return jnp.maximum(v, 0.0) + jnp.log(1.0 + jnp.exp(-jnp.abs(v)))


def _delta_prep_kernel(qkv_ref, ab_ref, abt_ref, arow_ref, drow_ref, acol_ref, dcol_ref,
                       wq_ref, u_ref, ak_ref, egl_ref):
    c = ab_ref.shape[1]
    nh = DN_HEADS
    ri = lax.broadcasted_iota(jnp.int32, (c, c), 0)
    ci = lax.broadcasted_iota(jnp.int32, (c, c), 1)
    incl = (ri >= ci, ri <= ci)
    strict = (ri > ci, ri < ci)
    low = jnp.where(incl[0], 1.0, 0.0)
    upp = jnp.where(incl[1], 1.0, 0.0)
    hdot = lambda a, b: jnp.dot(a, b, precision=HIGHEST, preferred_element_type=F32)
    ab = ab_ref[0]
    abt = abt_ref[0, 0]
    gate = -arow_ref[...] * _softplus(ab + drow_ref[...])
    gate_t = -acol_ref[...] * _softplus(abt + dcol_ref[...])
    beta_all = jax.nn.sigmoid(ab)
    gcum = (hdot(low, gate), hdot(upp, gate))
    gcum_t = (hdot(gate_t, upp), hdot(gate_t, low))
    er = lax.broadcasted_iota(jnp.int32, (DN_DK, DN_DK), 0)
    ec = lax.broadcasted_iota(jnp.int32, (DN_DK, DN_DK), 1)
    eye16 = jnp.where(er == ec, 1.0, 0.0).astype(BF16)
    nt = (((1,), (1,)), ((), ()))
    q = [qkv_ref[0, :, h * DN_DK:(h + 1) * DN_DK] for h in range(nh)]
    k = [qkv_ref[0, :, DN_W + h * DN_DK:DN_W + (h + 1) * DN_DK] for h in range(nh)]
    v = [qkv_ref[0, :, 2 * DN_W + h * DN_DV:2 * DN_W + (h + 1) * DN_DV] for h in range(nh)]
    k16 = [t.astype(BF16) for t in k]
    kk0 = [lax.dot_general(k16[h], k16[h], nt, preferred_element_type=F32) for h in range(nh)]
    qk0 = [lax.dot_general(q[h].astype(BF16), k16[h], nt, preferred_element_type=F32) for h in range(nh)]
    units = [(d, h) for d in range(2) for h in range(nh)]
    mm, rr = {}, {}
    for d, h in units:
        col = d * nh + h
        gc = gcum[d][:, col:col + 1]
        gr = gcum_t[d][col:col + 1, :]
        be = beta_all[:, 2 * nh + col:2 * nh + col + 1]
        last = 0 if d else c - 1
        gl = gc[last:last + 1, :]
        dec = jnp.where(incl[d], jnp.exp(jnp.where(incl[d], gc - gr, 0.0)), 0.0)
        mm[d, h] = jnp.where(strict[d], be * kk0[h] * dec, 0.0)
        ak_ref[0, 0, d, h, 0:c, :] = jnp.where(incl[d], qk0[h] * dec, 0.0).astype(BF16)
        eg = jnp.exp(gc)
        rr[d, h] = jnp.concatenate([v[h] * be, k[h] * (be * eg)], axis=1)
        wq_ref[0, 0, d, h, c:2 * c, :] = (q[h] * eg).astype(BF16)
        k_tail = (k[h] * jnp.exp(gl - gc)).astype(BF16)
        ak_ref[0, 0, d, h, c:c + DN_DK, :] = lax.dot_general(
            eye16, k_tail, nt, preferred_element_type=F32).astype(BF16)
        egl_ref[0, 0, d, h:h + 1, :] = jnp.broadcast_to(jnp.exp(gl), (1, DN_DV))
    sb = DN_SOLVE_BLOCK
    in_sb = (ri // sb) == (ci // sb)
    in_2sb = (ri // (2 * sb)) == (ci // (2 * sb))
    eye = jnp.where(ri == ci, 1.0, 0.0)
    b16 = lambda t: t.astype(BF16)
    nj = {u: jnp.where(in_sb, -mm[u], 0.0) for u in units}
    inv = {u: eye + nj[u] for u in units}
    for j in range((sb - 1).bit_length() - 1):
        nj = {u: _dot(b16(nj[u]), b16(nj[u])) for u in units}
        inv = {u: inv[u] + _dot(b16(inv[u]), b16(nj[u])) for u in units}
    off = {u: b16(jnp.where(in_2sb & ~in_sb, mm[u], 0.0)) for u in units}
    tmp = {u: _dot(off[u], b16(inv[u])) for u in units}
    inv = {u: inv[u] - _dot(b16(inv[u]), b16(tmp[u])) for u in units}
    inv16 = {u: b16(inv[u]) for u in units}
    off = {u: b16(jnp.where(in_2sb, 0.0, mm[u])) for u in units}
    part = {u: _dot(inv16[u], b16(rr[u])) for u in units}
    tmp = {u: _dot(off[u], b16(part[u])) for u in units}
    rr = {u: part[u] - _dot(inv16[u], b16(tmp[u])) for u in units}
    for d, h in units:
        u_ref[0, 0, d, h] = rr[d, h][:, :DN_DV]
        wq_ref[0, 0, d, h, 0:c, :] = rr[d, h][:, DN_DV:].astype(BF16)


def _delta_scan_kernel(wqf_ref, uf_ref, akf_ref, egf_ref, wqb_ref, ub_ref, akb_ref, egb_ref, s0f_ref, s0b_ref,
                       of_ref, ob_ref, sf_ref, sb_ref, s_scr):
    i = pl.program_id(1)
    bpb = uf_ref.shape[0]
    c = uf_ref.shape[-2]

    @pl.when(i == 0)
    def _():
        s_scr[0] = s0f_ref[...]
        s_scr[1] = s0b_ref[...]

    refs = ((wqf_ref, uf_ref, akf_ref, egf_ref, of_ref), (wqb_ref, ub_ref, akb_ref, egb_ref, ob_ref))
    units = [(d, bb, h) for d in range(2) for bb in range(bpb) for h in range(DN_HEADS)]
    s = {u: s_scr[u] for u in units}
    ws = {(d, bb, h): _dot(refs[d][0][bb, 0, 0, h], s[d, bb, h].astype(BF16)) for d, bb, h in units}
    v16 = {(d, bb, h): (refs[d][1][bb, 0, 0, h] - ws[d, bb, h][:c]).astype(BF16) for d, bb, h in units}
    av = {(d, bb, h): _dot(refs[d][2][bb, 0, 0, h], v16[d, bb, h]) for d, bb, h in units}
    for d, bb, h in units:
        u = (d, bb, h)
        refs[d][4][bb, :, h * DN_DV:(h + 1) * DN_DV] = (ws[u][c:] + av[u][:c]).astype(BF16)
        s_scr[u] = s[u] * refs[d][3][bb, 0, 0, h:h + 1, :] + av[u][c:]

    @pl.when(i == pl.num_programs(1) - 1)
    def _():
        sf_ref[...] = s_scr[0]
        sb_ref[...] = s_scr[1]


def _delta_rule(qkv, ab, a_log, dt_bias, s0_f, s0_b):
    bsz, length, _ = qkv.shape
    bpb = 2 if bsz % 2 == 0 else 1
    c = DN_CHUNK
    n = length // c
    nh = DN_HEADS
    abt = jnp.swapaxes(ab.reshape(bsz, n, c, 4 * nh), 2, 3)
    zeros = jnp.zeros((2 * nh,), F32)
    a_vec = jnp.concatenate([jnp.exp(a_log.astype(F32)).reshape(-1), zeros])
    d_vec = jnp.concatenate([dt_bias.astype(F32).reshape(-1), zeros])
    small = lambda shape: pl.BlockSpec(shape, lambda b, i: (0,) * len(shape))
    per_chunk = lambda *tail: pl.BlockSpec((1, 1, 2, nh) + tail, lambda b, i: (b, i, 0, 0) + (0,) * len(tail))
    wq, u, ak, egl = pl.pallas_call(
        _delta_prep_kernel, grid=(bsz, n),
        in_specs=[pl.BlockSpec((1, c, 3 * DN_W), lambda b, i: (b, i, 0)),
                  pl.BlockSpec((1, c, 4 * nh), lambda b, i: (b, i, 0)),
                  pl.BlockSpec((1, 1, 4 * nh, c), lambda b, i: (b, i, 0, 0)),
                  small((1, 4 * nh)), small((1, 4 * nh)), small((4 * nh, 1)), small((4 * nh, 1))],
        out_specs=[per_chunk(2 * c, DN_DK), per_chunk(c, DN_DV), per_chunk(c + DN_DK, c),
                   pl.BlockSpec((1, 1, 2, nh, DN_DV), lambda b, i: (b, i, 0, 0, 0))],
        out_shape=[jax.ShapeDtypeStruct((bsz, n, 2, nh, 2 * c, DN_DK), BF16),
                   jax.ShapeDtypeStruct((bsz, n, 2, nh, c, DN_DV), F32),
                   jax.ShapeDtypeStruct((bsz, n, 2, nh, c + DN_DK, c), BF16),
                   jax.ShapeDtypeStruct((bsz, n, 2, nh, DN_DV), F32)],
        compiler_params=_params(("parallel", "parallel")), name="delta_prep",
    )(qkv, ab, abt, a_vec.reshape(1, -1), d_vec.reshape(1, -1), a_vec.reshape(-1, 1), d_vec.reshape(-1, 1))

    def side(d, *tail):
        idx = (lambda b, i: (b, n - 1 - i, 1, 0) + (0,) * len(tail)) if d else (
            lambda b, i: (b, i, 0, 0) + (0,) * len(tail))
        return pl.BlockSpec((bpb, 1, 1, nh) + tail, idx)

    def side_specs(d):
        return [side(d, 2 * c, DN_DK), side(d, c, DN_DV), side(d, c + DN_DK, c),
                pl.BlockSpec((bpb, 1, 1, nh, DN_DV),
                             (lambda b, i: (b, n - 1 - i, 1, 0, 0)) if d else (lambda b, i: (b, i, 0, 0, 0)))]

    st_spec = pl.BlockSpec((bpb, nh, DN_DK, DN_DV), lambda b, i: (b, 0, 0, 0))
    o_shape = jax.ShapeDtypeStruct((bsz, length, DN_W), BF16)
    s_shape = jax.ShapeDtypeStruct((bsz, nh, DN_DK, DN_DV), F32)
    return pl.pallas_call(
        _delta_scan_kernel, grid=(bsz // bpb, n),
        in_specs=side_specs(0) + side_specs(1) + [st_spec, st_spec],
        out_specs=[pl.BlockSpec((bpb, c, DN_W), lambda b, i: (b, i, 0)),
                   pl.BlockSpec((bpb, c, DN_W), lambda b, i: (b, n - 1 - i, 0)), st_spec, st_spec],
        out_shape=[o_shape, o_shape, s_shape, s_shape],
        scratch_shapes=[pltpu.VMEM((2, bpb, nh, DN_DK, DN_DV), F32)],
        compiler_params=_params(("parallel", "arbitrary")), name="delta_scan",
    )(wq, u, ak, egl, wq, u, ak, egl, s0_f, s0_b)


def _proj_out_kernel(yf_ref, yh_ref, of_ref, ob_ref, z_ref, dg_ref, w_ref, res_ref, gate_ref, o_ref, a_scr):
    @pl.when(pl.program_id(1) == 0)
    def _():
        a_scr[:, 0:FOUR_W] = yf_ref[...]
        a_scr[:, FOUR_W:FOUR_W + HY_W] = yh_ref[...]
        o = of_ref[...].astype(F32) + ob_ref[...].astype(F32)
        z = z_ref[...].astype(F32)
        for h in range(DN_HEADS):
            lanes = slice(h * DN_DV, (h + 1) * DN_DV)
            y = _rms(o[:, lanes], dg_ref[...]) * _silu(z[:, lanes])
            a_scr[:, FOUR_W + HY_W + h * DN_DV:FOUR_W + HY_W + (h + 1) * DN_DV] = y.astype(BF16)

    o_ref[...] = res_ref[...] + gate_ref[0] * _dot(a_scr[...], w_ref[...])


def _proj_out(y_four, y_hy, o_f, o_b, p2d, dn_g, w, res, gate, rows_per_mod):
    m, d = res.shape
    tm = min(512, m)
    tn = d
    per = rows_per_mod // tm
    zb = PCOL_Z // DN_W
    return pl.pallas_call(
        _proj_out_kernel, grid=(m // tm, d // tn),
        in_specs=[pl.BlockSpec((tm, FOUR_W), lambda i, j: (i, 0)),
                  pl.BlockSpec((tm, HY_W), lambda i, j: (i, 0)),
                  pl.BlockSpec((tm, DN_W), lambda i, j: (i, 0)),
                  pl.BlockSpec((tm, DN_W), lambda i, j: (i, 0)),
                  pl.BlockSpec((tm, DN_W), lambda i, j: (i, zb)),
                  pl.BlockSpec((1, DN_DV), lambda i, j: (0, 0)),
                  pl.BlockSpec((w.shape[0], tn), lambda i, j: (0, j)),
                  pl.BlockSpec((tm, tn), lambda i, j: (i, j)),
                  pl.BlockSpec((1, 1, tn), lambda i, j: (i // per, 0, j))],
        out_specs=pl.BlockSpec((tm, tn), lambda i, j: (i, j)),
        out_shape=jax.ShapeDtypeStruct((m, d), F32),
        scratch_shapes=[pltpu.VMEM((tm, w.shape[0]), BF16)],
        compiler_params=_params(("parallel", "arbitrary")), name="proj_out",
    )(y_four, y_hy, o_f, o_b, p2d, dn_g.reshape(1, DN_DV), w, res, gate)


def _router_kernel(x_ref, c_ref, g_ref, sx_ref, cx_ref, sc_ref, cc_ref, w_ref, b_ref, h_ref, lg_ref, *, n_x):
    i = pl.program_id(0)

    def emit(v, shift, scale):
        h = _rms(v, g_ref[...]) * (1.0 + scale) + shift
        h_ref[...] = h
        lg_ref[...] = jnp.dot(h, w_ref[...], precision=HIGHEST, preferred_element_type=F32) + b_ref[...]

    @pl.when(i < n_x)
    def _():
        emit(x_ref[...], sx_ref[0], cx_ref[0])

    @pl.when(i >= n_x)
    def _():
        emit(c_ref[...], sc_ref[0], cc_ref[0])


def _router(x2d, c2d, g, shift_x, scale_x, shift_c, scale_c, w_r, b_r, rows_per_mod, with_ctx):
    mx, d = x2d.shape
    tm = 512
    n_x = mx // tm
    n_c = c2d.shape[0] // tm if with_ctx else 0
    per = rows_per_mod // tm
    xi = lambda i: (jnp.minimum(i, n_x - 1), 0)
    ci = lambda i: (jnp.maximum(i - n_x, 0), 0)
    return pl.pallas_call(
        functools.partial(_router_kernel, n_x=n_x), grid=(n_x + n_c,),
        in_specs=[pl.BlockSpec((tm, d), xi), pl.BlockSpec((tm, d), ci),
                  pl.BlockSpec((1, d), lambda i: (0, 0)),
                  pl.BlockSpec((1, 1, d), lambda i: (jnp.minimum(i, n_x - 1) // per, 0, 0)),
                  pl.BlockSpec((1, 1, d), lambda i: (jnp.minimum(i, n_x - 1) // per, 0, 0)),
                  pl.BlockSpec((1, 1, d), lambda i: (0, 0, 0)),
                  pl.BlockSpec((1, 1, d), lambda i: (0, 0, 0)),
                  pl.BlockSpec((d, ROUTER_W), lambda i: (0, 0)),
                  pl.BlockSpec((1, ROUTER_W), lambda i: (0, 0))],
        out_specs=[pl.BlockSpec((tm, d), lambda i: (i, 0)),
                   pl.BlockSpec((tm, ROUTER_W), lambda i: (i, 0))],
        out_shape=[jax.ShapeDtypeStruct(((n_x + n_c) * tm, d), F32),
                   jax.ShapeDtypeStruct(((n_x + n_c) * tm, ROUTER_W), F32)],
        compiler_params=_params(("parallel",)), name="moe_router",
    )(x2d, c2d, g.reshape(1, d), shift_x, scale_x, shift_c, scale_c, w_r, b_r)


def _row_copy(src_hbm, row, dst, slot, r, sem):
    return pltpu.make_async_copy(src_hbm.at[pl.ds(row, 1)], dst.at[slot, pl.ds(r, 1)], sem.at[slot])


def _expert_kernel(be_ref, tok_ref, nb_ref, h_hbm, wg_ref, wu_ref, wd_ref, ys_ref, xbuf, sem, *, bm):
    i = pl.program_id(0)
    nb = nb_ref[0]

    def start(blk, slot):
        def body(r, carry):
            _row_copy(h_hbm, tok_ref[blk * bm + r], xbuf, slot, r, sem).start()
            return carry
        lax.fori_loop(0, bm, body, 0, unroll=8)

    def wait(slot):
        pltpu.make_async_copy(h_hbm.at[pl.ds(0, bm)], xbuf.at[slot], sem.at[slot]).wait()

    @pl.when(i == 0)
    def _():
        start(0, 0)

    @pl.when(i + 1 < nb)
    def _():
        start(i + 1, (i + 1) % 2)

    @pl.when(i < nb)
    def _():
        slot = i % 2
        wait(slot)
        x = xbuf[slot].astype(BF16)
        act = (_silu(_dot(x, wg_ref[0, 0])) * _dot(x, wu_ref[0, 0])).astype(BF16)
        ys_ref[...] = _dot(act, wd_ref[0, 0])

    @pl.when(i >= nb)
    def _():
        ys_ref[...] = jnp.zeros_like(ys_ref)


def _experts(h, buf_tok, block_e, n_used, wg, wu, wd, layer, bm):
    n_blocks = block_e.shape[0]
    d = h.shape[1]
    ff = wg.shape[3]
    grid_spec = pltpu.PrefetchScalarGridSpec(
        num_scalar_prefetch=3, grid=(n_blocks,),
        in_specs=[pl.BlockSpec(memory_space=pl.ANY),
                  pl.BlockSpec((1, 1, d, ff), lambda i, be, tok, nb: (layer, be[i], 0, 0)),
                  pl.BlockSpec((1, 1, d, ff), lambda i, be, tok, nb: (layer, be[i], 0, 0)),
                  pl.BlockSpec((1, 1, ff, d), lambda i, be, tok, nb: (layer, be[i], 0, 0))],
        out_specs=pl.BlockSpec((bm, d), lambda i, be, tok, nb: (i, 0)),
        scratch_shapes=[pltpu.VMEM((2, bm, d), F32), pltpu.SemaphoreType.DMA((2,))])
    return pl.pallas_call(
        functools.partial(_expert_kernel, bm=bm), grid_spec=grid_spec,
        out_shape=jax.ShapeDtypeStruct((n_blocks * bm, d), F32),
        compiler_params=_params(("arbitrary",)), name="moe_experts",
    )(block_e, buf_tok, n_used, h, wg, wu, wd)


def _combine_kernel(pos_ref, ys_hbm, x_ref, gw_ref, gate_ref, fg_ref, o_ref, ybuf, sem, *, tm, tok0, final):
    i = pl.program_id(0)
    n = pl.num_programs(0)

    def start(blk, slot):
        def body(r, carry):
            t = tok0 + blk * tm + r
            _row_copy(ys_hbm, pos_ref[2 * t], ybuf, slot, r, sem).start()
            _row_copy(ys_hbm, pos_ref[2 * t + 1], ybuf, slot, tm + r, sem).start()
            return carry
        lax.fori_loop(0, tm, body, 0, unroll=8)

    @pl.when(i == 0)
    def _():
        start(0, 0)

    @pl.when(i + 1 < n)
    def _():
        start(i + 1, (i + 1) % 2)

    slot = i % 2

    pltpu.make_async_copy(ys_hbm.at[pl.ds(0, 2 * tm)], ybuf.at[slot], sem.at[slot]).wait()
    gw = gw_ref[...]
    y = gw[:, 0:1] * ybuf[slot, 0:tm, :] + gw[:, 1:2] * ybuf[slot, tm:2 * tm, :]
    out = x_ref[...] + gate_ref[0] * y
    if final:
        out = _rms(out, fg_ref[...])
    o_ref[...] = out


def _combine(ys, pos, gw, x2d, gate, final_g, tok0, rows_per_mod, final):
    m, d = x2d.shape
    tm = 256
    per = rows_per_mod // tm
    gb = tok0 // tm
    grid_spec = pltpu.PrefetchScalarGridSpec(
        num_scalar_prefetch=1, grid=(m // tm,),
        in_specs=[pl.BlockSpec(memory_space=pl.ANY),
                  pl.BlockSpec((tm, d), lambda i, pos: (i, 0)),
                  pl.BlockSpec((tm, TOP_K), lambda i, pos: (gb + i, 0)),
                  pl.BlockSpec((1, 1, d), lambda i, pos: (i // per, 0, 0)),
                  pl.BlockSpec((1, d), lambda i, pos: (0, 0))],
        out_specs=pl.BlockSpec((tm, d), lambda i, pos: (i, 0)),
        scratch_shapes=[pltpu.VMEM((2, 2 * tm, d), F32), pltpu.SemaphoreType.DMA((2,))])
    return pl.pallas_call(
        functools.partial(_combine_kernel, tm=tm, tok0=tok0, final=final), grid_spec=grid_spec,
        out_shape=jax.ShapeDtypeStruct((m, d), F32),
        compiler_params=_params(("arbitrary",)), name="moe_combine",
    )(pos, ys, x2d, gw, gate, final_g.reshape(1, d))


def _route(logits):
    t = logits.shape[0]
    p_grp = jax.nn.softmax(logits[:, :N_GROUPS], axis=-1)
    p_sel = jnp.max(p_grp, axis=-1, keepdims=True)
    grp = jnp.argmax(p_grp, axis=-1).astype(jnp.int32)[:, None]
    fine = logits[:, N_GROUPS:N_GROUPS + N_EXPERTS].reshape(t, N_GROUPS, EXPERTS_PER_GROUP)
    in_grp = (jnp.arange(N_GROUPS, dtype=jnp.int32)[None, :] == grp)[:, :, None]
    fine = jnp.sum(jnp.where(in_grp, fine, 0.0), axis=1)
    lanes = jnp.arange(EXPERTS_PER_GROUP, dtype=jnp.int32)[None, :]
    i1 = jnp.argmax(fine, axis=-1).astype(jnp.int32)[:, None]
    v1 = jnp.max(fine, axis=-1, keepdims=True)
    rest = jnp.where(lanes == i1, -jnp.inf, fine)
    i2 = jnp.argmax(rest, axis=-1).astype(jnp.int32)[:, None]
    v2 = jnp.max(rest, axis=-1, keepdims=True)
    top_v = jnp.concatenate([v1, v2], axis=-1)
    top_i = jnp.concatenate([i1, i2], axis=-1)
    gate = p_sel * jax.nn.softmax(top_v, axis=-1)
    return grp * EXPERTS_PER_GROUP + top_i, gate


def _rank_kernel(e_ref, rank_ref, cnt_ref, run_scr):
    @pl.when(pl.program_id(0) == 0)
    def _():
        run_scr[...] = jnp.zeros_like(run_scr)

    tb = e_ref.shape[0]
    lane = lax.broadcasted_iota(jnp.int32, (tb, LANE), 1)
    onehot = jnp.where(e_ref[...] == lane, 1.0, 0.0)
    ri = lax.broadcasted_iota(jnp.int32, (tb, tb), 0)
    ci = lax.broadcasted_iota(jnp.int32, (tb, tb), 1)
    before = jnp.where(ri > ci, 1.0, 0.0).astype(BF16)
    prefix = _dot(before, onehot.astype(BF16)) + run_scr[...]
    rank_ref[...] = jnp.sum(prefix * onehot, axis=1, keepdims=True).astype(jnp.int32)
    run_scr[...] += jnp.sum(onehot, axis=0, keepdims=True)
    cnt_ref[...] = run_scr[...]


def _expert_ranks(flat_e):
    a = flat_e.shape[0]
    tb = 512
    rank, cnt = pl.pallas_call(
        _rank_kernel, grid=(a // tb,),
        in_specs=[pl.BlockSpec((tb, 1), lambda i: (i, 0))],
        out_specs=[pl.BlockSpec((tb, 1), lambda i: (i, 0)), pl.BlockSpec((1, LANE), lambda i: (0, 0))],
        out_shape=[jax.ShapeDtypeStruct((a, 1), jnp.int32), jax.ShapeDtypeStruct((1, LANE), F32)],
        scratch_shapes=[pltpu.VMEM((1, LANE), F32)],
        compiler_params=_params(("arbitrary",)), name="moe_rank",
    )(flat_e.reshape(a, 1))
    return rank.reshape(a), cnt[0, :N_EXPERTS].astype(jnp.int32)


def _dispatch_plan(expert, bm):
    t = expert.shape[0]
    a = t * TOP_K
    flat_e = expert.reshape(a).astype(jnp.int32)
    rank, counts = _expert_ranks(flat_e)
    padded = (counts + bm - 1) // bm * bm
    pad_end = jnp.cumsum(padded)
    pad_start = pad_end - padded
    mine = flat_e[:, None] == jnp.arange(N_EXPERTS, dtype=jnp.int32)[None, :]
    pos = (jnp.sum(jnp.where(mine, pad_start[None, :], 0), axis=1) + rank).astype(jnp.int32)
    n_blocks = a // bm + N_EXPERTS
    flat_tok = jnp.arange(a, dtype=jnp.int32) // TOP_K
    buf_tok = jnp.zeros((n_blocks * bm,), jnp.int32).at[pos].set(flat_tok)
    n_used = (pad_end[-1] // bm).astype(jnp.int32)
    blk = jnp.minimum(jnp.arange(n_blocks, dtype=jnp.int32), n_used - 1) * bm
    block_e = jnp.sum((pad_end[None, :] <= blk[:, None]).astype(jnp.int32), axis=1)
    block_e = jnp.minimum(block_e, N_EXPERTS - 1).astype(jnp.int32)
    return pos, buf_tok, block_e, n_used.reshape(1)


def _mix_stream(p3, ab, rows, width, s0_f, s0_b, four_tabs, hy_tabs, hy_coefs, conv_w, conv_b, four_g,
                hy_bias, hy_g, a_log, dt_bias, with_mixers):
    qkv = _short_conv(p3, conv_w, conv_b, rows, width, OFF_DN, HY_CONV_CH, 3 * DN_W, 4 * DN_DK, True)[0]
    o_f, o_b, s_f, s_b = _delta_rule(qkv, ab, a_log, dt_bias, s0_f, s0_b)
    if not with_mixers:
        return None, None, o_f, o_b, s_f, s_b
    y_four = _fourier(p3, *four_tabs, four_g)
    if _hy_two_level(p3.shape[1]):
        conv3 = _short_conv(p3, conv_w, conv_b, rows, width, OFF_HY, 0, HY_CONV_CH, 256, False, split=3)
        y_hy = _hyena2(conv3, hy_tabs, hy_coefs, hy_bias, hy_g)
    else:
        convh = _short_conv(p3, conv_w, conv_b, rows, width, OFF_HY, 0, HY_CONV_CH, 256, False)[0]
        y_hy = _hyena(convh, hy_tabs, hy_coefs, hy_bias, hy_g)
    return y_four, y_hy, o_f, o_b, s_f, s_b


def _hy_two_level(length):
    return length % (8 * HY_FAST) == 0


def kernel(x, c, ctx, c_ctx, norm1_g, norm2_g, w_mod, b_mod, w_in, conv_w, conv_b, four_g, hy_w1, hy_b1, hy_w2, hy_b2, hy_w3, hy_b3, hy_w4, hy_freq, hy_bias, hy_g, dn_a_log, dn_dt_bias, dn_g, w_out, w_rc, b_rc, w_rf, b_rf, w_e_gate, w_e_up, w_e_down, final_g):
    bsz, length, d = x.shape
    lc = ctx.shape[1]
    depth = w_in.shape[0]
    rows = length // GRID_W
    bm = 256

    mod_all = _modulation(c, c_ctx, w_mod, b_mod)
    four_x, four_c = _fourier_tables(length), _fourier_tables(lc)
    hy_tables = lambda n: _hy2_tables(n) if _hy_two_level(n) else _hyena_tables(n)
    hy_coefs = lambda n, tabs, prm: (_hy2_coefs if _hy_two_level(n) else _hyena_coefs)(n, tabs, prm)
    hy_x, hy_c = hy_tables(length), hy_tables(lc)
    zeros = jnp.zeros((bsz, DN_HEADS, DN_DK, DN_DV), F32)
    x2 = x.reshape(bsz * length, d)
    c2 = ctx.reshape(bsz * lc, d)
    wg16, wu16, wd16 = w_e_gate.astype(BF16), w_e_up.astype(BF16), w_e_down.astype(BF16)

    for l in range(depth):
        last = l == depth - 1
        mod = [m[:, None, :] for m in jnp.split(mod_all[l, :bsz], 6, axis=-1)]
        modc = [m[:, None, :] for m in jnp.split(mod_all[l, bsz:bsz + 1], 6, axis=-1)]
        wi = w_in[l]
        w_p = jnp.concatenate([wi[:, :OFF_AB], wi[:, OFF_Z:], wi[:, OFF_AB:OFF_Z],
                               jnp.zeros((d, PROJ_W - PCOL_AB - N_AB), wi.dtype)], axis=1).astype(BF16)
        w_o = w_out[l].astype(BF16)
        hy_params = (hy_w1[l], hy_b1[l], hy_w2[l], hy_b2[l], hy_w3[l], hy_b3[l], hy_w4[l], hy_freq[l])
        mix_args = (conv_w[l], conv_b[l], four_g[l], hy_bias[l], hy_g[l], dn_a_log[l], dn_dt_bias[l])

        pc, abc = _proj_in(c2, norm1_g[l], modc[0], modc[1], w_p, bsz * lc)
        pc3 = pc.reshape(bsz, lc, PROJ_W)
        abc = abc[:, :N_AB].reshape(bsz, lc, N_AB)
        coefs_c = None if last else hy_coefs(lc, hy_c, hy_params)
        yf, yh, o_f, o_b, s_f, s_b = _mix_stream(pc3, abc, 1, lc, zeros, zeros, four_c, hy_c, coefs_c,
                                                 *mix_args, with_mixers=not last)
        if not last:
            c2 = _proj_out(yf.reshape(-1, FOUR_W), yh.reshape(-1, HY_W), o_f.reshape(-1, DN_W),
                           o_b.reshape(-1, DN_W), pc, dn_g[l], w_o, c2, modc[2], bsz * lc)

        p, abx = _proj_in(x2, norm1_g[l], mod[0], mod[1], w_p, length)
        p3 = p.reshape(bsz, length, PROJ_W)
        abx = abx[:, :N_AB].reshape(bsz, length, N_AB)
        coefs_x = hy_coefs(length, hy_x, hy_params)
        yf, yh, o_f, o_b, _, _ = _mix_stream(p3, abx, rows, GRID_W, s_f, s_b, four_x, hy_x, coefs_x,
                                             *mix_args, with_mixers=True)
        x2 = _proj_out(yf.reshape(-1, FOUR_W), yh.reshape(-1, HY_W), o_f.reshape(-1, DN_W),
                       o_b.reshape(-1, DN_W), p, dn_g[l], w_o, x2, mod[2], length)

        w_r = jnp.concatenate([w_rc[l], w_rf[l], jnp.zeros((d, ROUTER_W - N_GROUPS - N_EXPERTS), F32)], axis=1)
        b_r = jnp.concatenate([b_rc[l], b_rf[l], jnp.zeros((ROUTER_W - N_GROUPS - N_EXPERTS,), F32)])[None, :]
        h, logits = _router(x2, c2, norm2_g[l], mod[3], mod[4], modc[3], modc[4], w_r, b_r, length,
                            with_ctx=not last)
        expert, gate = _route(logits)
        pos, buf_tok, block_e, n_used = _dispatch_plan(expert, bm)
        ys = _experts(h, buf_tok, block_e, n_used, wg16, wu16, wd16, l, bm)
        x2 = _combine(ys, pos, gate, x2, mod[5], final_g, 0, length, final=last)
        if not last:
            c2 = _combine(ys, pos, gate, c2, modc[5], final_g, bsz * length, bsz * lc, final=False)
    return x2.reshape(bsz, length, d)
```

```python
import functools
import math

import jax
import jax.numpy as jnp
from jax import lax
from jax.experimental import pallas as pl
from jax.experimental.pallas import tpu as pltpu

F32 = jnp.float32
BF16 = jnp.bfloat16
HIGHEST = lax.Precision.HIGHEST

GRID_W = 64
FOUR_W = 512
FOUR_GROUPS = 4
FOUR_GW = FOUR_W // FOUR_GROUPS
HY_W = 512
HY_ORDER = 2
DN_HEADS = 8
DN_DK = 128
DN_DV = 128
DN_W = DN_HEADS * DN_DV
DN_CHUNK = 64
DN_SOLVE_BLOCK = DN_CHUNK // 4
HY_CONV_CH = (HY_ORDER + 1) * HY_W
OFF_HY = FOUR_W
OFF_DN = OFF_HY + HY_CONV_CH
OFF_AB = OFF_DN + 3 * DN_W
N_AB = 4 * DN_HEADS
OFF_Z = OFF_AB + N_AB
HY_EMB = 33
HY_FAST_DECAY = 0.3
HY_SLOW_DECAY = 1.5
HY_TARGET = 1e-2
N_GROUPS = 4
EXPERTS_PER_GROUP = 8
N_EXPERTS = N_GROUPS * EXPERTS_PER_GROUP
TOP_K = 2
EPS = 1e-6

LANE = 128
PCOL_Z = OFF_AB
PCOL_AB = PCOL_Z + DN_W
MXU_W = 256
PROJ_TN = 5 * MXU_W
PROJ_W = -(-(PCOL_AB + LANE) // PROJ_TN) * PROJ_TN
ROUTER_W = LANE
VMEM_LIMIT = 56 * 1024 * 1024


def _params(semantics):
    return pltpu.CompilerParams(dimension_semantics=semantics, vmem_limit_bytes=VMEM_LIMIT)


def _silu(v):
    return v * jax.nn.sigmoid(v)


def _rms(v, g):
    return v * lax.rsqrt(jnp.mean(v * v, axis=-1, keepdims=True) + EPS) * g


def _dot(a, b):
    return jnp.dot(a, b, preferred_element_type=F32)


def _mod_kernel(a_ref, w_ref, b_ref, o_ref):
    o_ref[0] = jnp.dot(_silu(a_ref[...]), w_ref[0], precision=HIGHEST,
                       preferred_element_type=F32) + b_ref[0]


def _modulation(c, c_ctx, w_mod, b_mod):
    depth, d, n = w_mod.shape
    bsz = c.shape[0]
    rows = -(-(bsz + 1) // 8) * 8
    a = jnp.concatenate([c, c_ctx[None], jnp.zeros((rows - bsz - 1, d), F32)], axis=0)
    tn = 1024 if n % 1024 == 0 else 512
    assert n % tn == 0
    return pl.pallas_call(
        _mod_kernel, grid=(depth, n // tn),
        in_specs=[pl.BlockSpec((rows, d), lambda l, j: (0, 0)),
                  pl.BlockSpec((1, d, tn), lambda l, j: (l, 0, j)),
                  pl.BlockSpec((1, 1, tn), lambda l, j: (l, 0, j))],
        out_specs=pl.BlockSpec((1, rows, tn), lambda l, j: (l, 0, j)),
        out_shape=jax.ShapeDtypeStruct((depth, rows, n), F32),
        compiler_params=_params(("parallel", "parallel")), name="modulation",
    )(a, w_mod, b_mod.reshape(depth, 1, n))


def _proj_in_kernel(x_ref, g_ref, shift_ref, scale_ref, w_ref, o_ref, ab_ref, a_scr, *, ab_off):
    j = pl.program_id(1)

    @pl.when(j == 0)
    def _():
        y = _rms(x_ref[...], g_ref[...])
        a_scr[...] = (y * (1.0 + scale_ref[0]) + shift_ref[0]).astype(BF16)

    acc = _dot(a_scr[...], w_ref[...])
    o_ref[...] = acc.astype(BF16)

    @pl.when(j == pl.num_programs(1) - 1)
    def _():
        ab_ref[...] = acc[:, ab_off:ab_off + LANE]


def _proj_in(x2d, g, shift, scale, w, rows_per_mod):
    m, d = x2d.shape
    n = w.shape[1]
    tm = min(1024, m, rows_per_mod)
    tn = PROJ_TN
    assert n % tn == 0 and PCOL_AB >= n - tn
    per = rows_per_mod // tm
    return pl.pallas_call(
        functools.partial(_proj_in_kernel, ab_off=PCOL_AB - (n - tn)), grid=(m // tm, n // tn),
        in_specs=[pl.BlockSpec((tm, d), lambda i, j: (i, 0)),
                  pl.BlockSpec((1, d), lambda i, j: (0, 0)),
                  pl.BlockSpec((1, 1, d), lambda i, j: (i // per, 0, 0)),
                  pl.BlockSpec((1, 1, d), lambda i, j: (i // per, 0, 0)),
                  pl.BlockSpec((d, tn), lambda i, j: (0, j))],
        out_specs=[pl.BlockSpec((tm, tn), lambda i, j: (i, j)),
                   pl.BlockSpec((tm, LANE), lambda i, j: (i, 0))],
        out_shape=[jax.ShapeDtypeStruct((m, n), BF16), jax.ShapeDtypeStruct((m, LANE), F32)],
        scratch_shapes=[pltpu.VMEM((tm, d), BF16)],
        compiler_params=_params(("parallel", "arbitrary")), name="proj_in",
    )(x2d, g.reshape(1, d), shift, scale, w)


def _cos_sin_table(nrow, ncol, period):
    i = lax.broadcasted_iota(jnp.int32, (nrow, ncol), 0)
    j = lax.broadcasted_iota(jnp.int32, (nrow, ncol), 1)
    ang = ((i * j) % period).astype(F32) * (2.0 * math.pi / period)
    return jnp.cos(ang), jnp.sin(ang)


def _dft_tables(length, period):
    blk = 64 if length % 64 == 0 else length
    hi_i = lax.broadcasted_iota(jnp.int32, (length // blk, length), 0) * blk
    j = lax.broadcasted_iota(jnp.int32, (length // blk, length), 1)
    ang = ((hi_i * j) % period).astype(F32) * (2.0 * math.pi / period)
    hi_c, hi_s = jnp.cos(ang), jnp.sin(ang)
    lo_c, lo_s = _cos_sin_table(blk, length, period)
    c = hi_c[:, None, :] * lo_c[None, :, :] - hi_s[:, None, :] * lo_s[None, :, :]
    s = hi_s[:, None, :] * lo_c[None, :, :] + hi_c[:, None, :] * lo_s[None, :, :]
    return c.reshape(length, length), s.reshape(length, length)


def _fourier_kernel(u_ref, c_ref, s_ref, bc_ref, bs_ref, g_ref, o_ref, ub_scr, *, scale):
    @pl.when(pl.program_id(1) == 0)
    def _():
        ub_scr[...] = u_ref[0].astype(BF16)

    ub = ub_scr[...]
    p = _dot(c_ref[...], ub).astype(BF16)
    q = _dot(s_ref[...], ub).astype(BF16)
    z = (_dot(p, bc_ref[...]) - _dot(q, bs_ref[...])) * scale
    o_ref[0] = _rms(z, g_ref[...]).astype(o_ref.dtype)


def _fourier(p3, cl, sl, bc, bs, four_g):
    bsz, length, _ = p3.shape
    tm = min(512, length)
    kern = functools.partial(_fourier_kernel, scale=1.0 / math.sqrt(length * FOUR_GW))
    return pl.pallas_call(
        kern, grid=(bsz, length // tm),
        in_specs=[pl.BlockSpec((1, length, FOUR_W), lambda b, i: (b, 0, 0)),
                  pl.BlockSpec((tm, length), lambda b, i: (i, 0)),
                  pl.BlockSpec((tm, length), lambda b, i: (i, 0)),
                  pl.BlockSpec((FOUR_W, FOUR_W), lambda b, i: (0, 0)),
                  pl.BlockSpec((FOUR_W, FOUR_W), lambda b, i: (0, 0)),
                  pl.BlockSpec((1, FOUR_W), lambda b, i: (0, 0))],
        out_specs=pl.BlockSpec((1, tm, FOUR_W), lambda b, i: (b, i, 0)),
        out_shape=jax.ShapeDtypeStruct((bsz, length, FOUR_W), BF16),
        scratch_shapes=[pltpu.VMEM((length, FOUR_W), BF16)],
        compiler_params=_params(("parallel", "arbitrary")), name="fourier",
    )(p3, cl, sl, bc, bs, four_g.reshape(1, FOUR_W))


def _fourier_tables(length):
    cl, sl = _dft_tables(length, length)
    cc, sc = _cos_sin_table(FOUR_GW, FOUR_GW, FOUR_GW)
    eye = jnp.eye(FOUR_GROUPS, dtype=F32)
    return cl.astype(BF16), sl.astype(BF16), jnp.kron(eye, cc).astype(BF16), jnp.kron(eye, sc).astype(BF16)


def _conv_kernel(x_ref, w_ref, b_ref, o_ref, *, rows, width):
    tc = x_ref.shape[2]
    wts = w_ref[...]
    bias = b_ref[...]
    pos = lax.broadcasted_iota(jnp.int32, (width, tc), 0)
    not_first = pos > 0
    not_last = pos < width - 1

    def body(r, carry):
        base = pl.multiple_of(r * width, width)
        cen = x_ref[0, pl.ds(base, width), :].astype(F32)
        if rows > 1:
            up = x_ref[0, pl.ds(pl.multiple_of(jnp.maximum(r - 1, 0) * width, width), width), :].astype(F32)
            dn = x_ref[0, pl.ds(pl.multiple_of(jnp.minimum(r + 1, rows - 1) * width, width), width), :].astype(F32)
            w_up = wts[0] * jnp.where(r > 0, 1.0, 0.0)
            w_dn = wts[2] * jnp.where(r < rows - 1, 1.0, 0.0)
            col = lambda j: up * w_up[j:j + 1, :] + cen * wts[1, j:j + 1, :] + dn * w_dn[j:j + 1, :]
        else:
            col = lambda j: cen * wts[1, j:j + 1, :]
        left = jnp.where(not_first, pltpu.roll(col(0), 1, 0), 0.0)
        right = jnp.where(not_last, pltpu.roll(col(2), width - 1, 0), 0.0)
        o_ref[0, pl.ds(base, width), :] = left + col(1) + right + bias
        return carry

    lax.fori_loop(0, rows, body, 0)


def _short_conv(p3, conv_w, conv_b, rows, width, col0, ch0, nch, tc, split=1):
    bsz, length, _ = p3.shape
    kern = functools.partial(_conv_kernel, rows=rows, width=width)
    cb, wb = col0 // tc, ch0 // tc
    per = nch // split // tc
    out = pl.pallas_call(
        kern, grid=(bsz, nch // tc),
        in_specs=[pl.BlockSpec((1, length, tc), lambda b, j: (b, 0, cb + j)),
                  pl.BlockSpec((3, 3, tc), lambda b, j: (0, 0, wb + j)),
                  pl.BlockSpec((1, tc), lambda b, j: (0, wb + j))],
        out_specs=pl.BlockSpec((1, length, tc), lambda b, j: ((j // per) * bsz + b, 0, j % per)),
        out_shape=jax.ShapeDtypeStruct((split * bsz, length, nch // split), F32),
        compiler_params=_params(("parallel", "parallel")),
        name="short_conv",
    )(p3, conv_w, conv_b.reshape(1, -1))
    return out.reshape(split, bsz, length, nch // split)


def _filt_kernel(z_ref, w1_ref, b1_ref, w2_ref, b2_ref, w3_ref, b3_ref, w4_ref, fr_ref, dl_ref,
                 g_ref, nrm_ref):
    i = pl.program_id(0)
    z = z_ref[...]
    fr = fr_ref[...]
    hdot = lambda a, b: jnp.dot(a, b, precision=HIGHEST, preferred_element_type=F32)
    h = jnp.sin(fr * (hdot(z, w1_ref[...]) + b1_ref[...]))
    h = jnp.sin(fr * (hdot(h, w2_ref[...]) + b2_ref[...]))
    h = jnp.sin(fr * (hdot(h, w3_ref[...]) + b3_ref[...]))
    h = hdot(h, w4_ref[...])
    decay = jnp.exp(-z[:, 0:1] * jnp.abs(dl_ref[...]))
    decay = jnp.concatenate([decay] * HY_ORDER, axis=1)
    half = HY_ORDER * HY_W
    hf = h[:, :half] * decay
    hb = h[:, half:] * decay
    row = lax.broadcasted_iota(jnp.int32, hb.shape, 0) + i * hb.shape[0]
    hb = jnp.where(row > 0, hb, 0.0)
    gp = hf + hb
    gm = hf - hb
    for o in range(HY_ORDER):
        g_ref[o] = gp[:, o * HY_W:(o + 1) * HY_W].astype(BF16)
        g_ref[HY_ORDER + o] = gm[:, o * HY_W:(o + 1) * HY_W].astype(BF16)
    part = jnp.sum(jnp.abs(hf) + jnp.abs(hb), axis=0, keepdims=True)

    @pl.when(i == 0)
    def _():
        nrm_ref[...] = part

    @pl.when(i > 0)
    def _():
        nrm_ref[...] += part


def _hyena_filter_taps(length, w1, b1, w2, b2, w3, b3, w4, freq):
    t = jnp.linspace(0.0, 1.0, length, dtype=F32)[:, None]
    bands = (HY_EMB - 1) // 2
    ang = (2.0 * math.pi / length) * jnp.arange(length, dtype=F32)[:, None]
    f = jnp.linspace(1e-4, bands - 1, bands, dtype=F32)
    z = jnp.concatenate([t, jnp.cos(f * ang), -jnp.sin(f * ang)], axis=-1)
    max_decay = math.log(HY_TARGET) / HY_FAST_DECAY
    min_decay = math.log(HY_TARGET) / HY_SLOW_DECAY
    deltas = jnp.linspace(min_decay, max_decay, HY_W, dtype=F32)[None, :]
    fw = w1.shape[1]
    tl = min(512, length)
    full = lambda shape: pl.BlockSpec(shape, lambda i: (0,) * len(shape))
    return pl.pallas_call(
        _filt_kernel, grid=(length // tl,),
        in_specs=[pl.BlockSpec((tl, HY_EMB), lambda i: (i, 0)),
                  full((HY_EMB, fw)), full((1, fw)), full((fw, fw)), full((1, fw)),
                  full((fw, fw)), full((1, fw)), full((fw, 2 * HY_ORDER * HY_W)), full((1, fw)),
                  full((1, HY_W))],
        out_specs=[pl.BlockSpec((2 * HY_ORDER, tl, HY_W), lambda i: (0, i, 0)),
                   pl.BlockSpec((1, HY_ORDER * HY_W), lambda i: (0, 0))],
        out_shape=[jax.ShapeDtypeStruct((2 * HY_ORDER, length, HY_W), BF16),
                   jax.ShapeDtypeStruct((1, HY_ORDER * HY_W), F32)],
        compiler_params=_params(("arbitrary",)), name="hyena_filter",
    )(z, w1, b1.reshape(1, fw), w2, b2.reshape(1, fw), w3, b3.reshape(1, fw), w4,
      freq.reshape(1, fw), deltas)


def _dft_fwd_kernel(u_ref, c_ref, s_ref, *rest, with_coef):
    if with_coef:
        c1_ref, c2_ref, c4_ref, yr_ref, yi_ref, ub_scr = rest
    else:
        yr_ref, yi_ref, ub_scr = rest

    @pl.when(pl.program_id(1) == 0)
    def _():
        ub_scr[...] = u_ref[0].astype(BF16)

    ub = ub_scr[...]
    a = _dot(c_ref[...], ub)
    b = _dot(s_ref[...], ub)
    if with_coef:
        c2 = c2_ref[...]
        yr_ref[0] = (a * c1_ref[...] + b * c2).astype(BF16)
        yi_ref[0] = (b * c4_ref[...] - a * c2).astype(BF16)
    else:
        yr_ref[0] = a
        yi_ref[0] = b


def _dft_fwd(u3, col_blk, cf, sf, coef=None):
    nb, length, _ = u3.shape
    tm = min(512, length)
    in_specs = [pl.BlockSpec((1, length, HY_W), lambda b, i: (b, 0, col_blk)),
                pl.BlockSpec((tm, length), lambda b, i: (i, 0)),
                pl.BlockSpec((tm, length), lambda b, i: (i, 0))]
    args = [u3, cf, sf]
    if coef is not None:
        in_specs += [pl.BlockSpec((tm, HY_W), lambda b, i: (i, 0))] * 3
        args += list(coef)
    odt = BF16 if coef is not None else F32
    return pl.pallas_call(
        functools.partial(_dft_fwd_kernel, with_coef=coef is not None), grid=(nb, length // tm),
        in_specs=in_specs,
        out_specs=[pl.BlockSpec((1, tm, HY_W), lambda b, i: (b, i, 0))] * 2,
        out_shape=[jax.ShapeDtypeStruct((nb, length, HY_W), odt)] * 2,
        scratch_shapes=[pltpu.VMEM((length, HY_W), BF16)],
        compiler_params=_params(("parallel", "arbitrary")),
        name="hyena_dft_fwd" if coef is not None else "hyena_filter_dft",
    )(*args)


def _dft_inv_kernel(yr_ref, yi_ref, c_ref, s_ref, u_ref, gate_ref, bias_ref, g_ref, o_ref, *, final):
    y = _dot(c_ref[...], yr_ref[0]) + _dot(s_ref[...], yi_ref[0])
    out = gate_ref[0] * (y + bias_ref[...] * u_ref[0])
    if final:
        out = _rms(out, g_ref[...])
    o_ref[0] = out.astype(o_ref.dtype)


def _dft_inv(yr, yi, cf, s_inv, u3, u_blk, gate3, gate_blk, bias, g, final):
    bsz, length, _ = yr.shape
    tm = min(512, length)
    return pl.pallas_call(
        functools.partial(_dft_inv_kernel, final=final), grid=(bsz, length // tm),
        in_specs=[pl.BlockSpec((1, length, HY_W), lambda b, i: (b, 0, 0)),
                  pl.BlockSpec((1, length, HY_W), lambda b, i: (b, 0, 0)),
                  pl.BlockSpec((tm, length), lambda b, i: (i, 0)),
                  pl.BlockSpec((tm, length), lambda b, i: (i, 0)),
                  pl.BlockSpec((1, tm, HY_W), lambda b, i: (b, i, u_blk)),
                  pl.BlockSpec((1, tm, HY_W), lambda b, i: (b, i, gate_blk)),
                  pl.BlockSpec((1, HY_W), lambda b, i: (0, 0)),
                  pl.BlockSpec((1, HY_W), lambda b, i: (0, 0))],
        out_specs=pl.BlockSpec((1, tm, HY_W), lambda b, i: (b, i, 0)),
        out_shape=jax.ShapeDtypeStruct((bsz, length, HY_W), BF16 if final else F32),
        compiler_params=_params(("parallel", "arbitrary")), name="hyena_dft_inv",
    )(yr, yi, cf, s_inv, u3, gate3, bias.reshape(1, HY_W), g.reshape(1, HY_W))


def _hyena_tables(length):
    cf, sf = _dft_tables(length, 2 * length)
    alt = jnp.where(jnp.arange(length) % 2 == 0, 1.0, -1.0).astype(F32)
    sf = sf.at[0, :].set(alt)
    return cf.astype(BF16), sf.astype(BF16), sf.T.astype(BF16)


def _hyena_coefs(length, tables, hy_params):
    cf, sf, _ = tables
    taps, nrm = _hyena_filter_taps(length, *hy_params)
    a, b = _dft_fwd(taps, 0, cf, sf)
    inv = (1.0 / nrm).reshape(HY_ORDER, 1, HY_W)
    n = 2.0 * length
    hr = a[:HY_ORDER] * inv
    hi = -b[HY_ORDER:] * inv
    nyq = b[:HY_ORDER, 0:1, :] * inv
    first = (jnp.arange(length) == 0)[None, :, None]
    c1 = jnp.where(first, hr / n, hr * (2.0 / n))
    c2 = jnp.where(first, 0.0, hi * (2.0 / n))
    c4 = jnp.where(first, nyq / n, hr * (2.0 / n))
    return c1, c2, c4


def _hyena(convh, tables, coefs, hy_bias, hy_g):
    cf, sf, s_inv = tables
    c1, c2, c4 = coefs
    yr, yi = _dft_fwd(convh, 0, cf, sf, (c1[0], c2[0], c4[0]))
    zz = _dft_inv(yr, yi, cf, s_inv, convh, 0, convh, 1, hy_bias[0], hy_g, False)
    yr, yi = _dft_fwd(zz, 0, cf, sf, (c1[1], c2[1], c4[1]))
    return _dft_inv(yr, yi, cf, s_inv, zz, 0, convh, 2, hy_bias[1], hy_g, True)


HY_FAST = 128


def _hy2_tables(length):
    nk = length // HY_FAST
    n2 = 2 * length
    k1 = lax.broadcasted_iota(jnp.int32, (nk, nk), 0)
    s1 = lax.broadcasted_iota(jnp.int32, (nk, nk), 1)
    ang = (((2 * k1 + 1) * s1) % (4 * nk)).astype(F32) * (2.0 * math.pi / (4 * nk))
    f1 = jnp.concatenate([jnp.cos(ang), -jnp.sin(ang)], axis=0)
    shape = (nk, HY_FAST, HY_FAST)
    kk = lax.broadcasted_iota(jnp.int32, shape, 0) + 2 * nk * lax.broadcasted_iota(jnp.int32, shape, 1)
    s2 = lax.broadcasted_iota(jnp.int32, shape, 2)
    phi = (((2 * kk + 1) * s2) % (2 * n2)).astype(F32) * (2.0 * math.pi / (2 * n2))
    c, s = jnp.cos(phi), jnp.sin(phi)
    f2 = jnp.concatenate([jnp.concatenate([c, s], axis=2), jnp.concatenate([-s, c], axis=2)], axis=1)
    f1k = jnp.kron(f1, jnp.eye(HY_SUB, dtype=F32))
    return f1k.astype(BF16), f2.astype(BF16), jnp.swapaxes(f2, 1, 2).astype(BF16), f1k.T.astype(BF16)


HY_SUB = 8


def _hy2_stage1_kernel(x_ref, f1_ref, z_ref):
    nk = x_ref.shape[2]
    tc = x_ref.shape[4]
    for g in range(HY_FAST // HY_SUB):
        rows = slice(g * HY_SUB, (g + 1) * HY_SUB)
        rhs = x_ref[0, 0, :, rows, :].reshape(nk * HY_SUB, tc).astype(BF16)
        z = _dot(f1_ref[...], rhs).astype(BF16)
        z_ref[0, :, :, rows, :] = z.reshape(2, nk, HY_SUB, tc)


def _hy2_stage1(x5, which, f1k):
    _, nb, nk, _, cw = x5.shape
    tc = min(256, cw)
    return pl.pallas_call(
        _hy2_stage1_kernel, grid=(nb, cw // tc),
        in_specs=[pl.BlockSpec((1, 1, nk, HY_FAST, tc), lambda b, j: (which, b, 0, 0, j)),
                  pl.BlockSpec(f1k.shape, lambda b, j: (0, 0))],
        out_specs=pl.BlockSpec((1, 2, nk, HY_FAST, tc), lambda b, j: (b, 0, 0, 0, j)),
        out_shape=jax.ShapeDtypeStruct((nb, 2, nk, HY_FAST, cw), BF16),
        compiler_params=_params(("parallel", "parallel")), name="hyena2_stage1",
    )(x5, f1k)


def _hy2_stage2_kernel(zr_ref, zi_ref, f2_ref, *rest, filtered):
    if filtered:
        g2_ref, hr_ref, hi_ref, vr_ref, vi_ref = rest
    else:
        vr_ref, vi_ref = rest
    for k in range(zr_ref.shape[2]):
        x = _dot(f2_ref[k], jnp.concatenate([zr_ref[0, 0, k], zi_ref[0, 0, k]], axis=0))
        xr, xi = x[:HY_FAST], x[HY_FAST:]
        if filtered:
            hr, hi = hr_ref[k], hi_ref[k]
            y = jnp.concatenate([xr * hr - xi * hi, xr * hi + xi * hr], axis=0).astype(BF16)
            v = _dot(g2_ref[k], y)
            vr_ref[0, 0, k] = v[:HY_FAST].astype(BF16)
            vi_ref[0, 0, k] = v[HY_FAST:].astype(BF16)
        else:
            vr_ref[0, 0, k] = xr
            vi_ref[0, 0, k] = xi


def _hy2_stage2(z, f2, g2=None, hr=None, hi=None):
    nb, _, nk, _, cw = z.shape
    kb = 8
    filtered = hr is not None
    zspec = lambda part: pl.BlockSpec((1, 1, kb, HY_FAST, cw), lambda k, b: (b, part, k, 0, 0))
    mat = pl.BlockSpec((kb, 2 * HY_FAST, 2 * HY_FAST), lambda k, b: (k, 0, 0))
    in_specs = [zspec(0), zspec(1), mat]
    args = [z, z, f2]
    if filtered:
        in_specs += [mat, pl.BlockSpec((kb, HY_FAST, cw), lambda k, b: (k, 0, 0)),
                     pl.BlockSpec((kb, HY_FAST, cw), lambda k, b: (k, 0, 0))]
        args += [g2, hr, hi]
    return pl.pallas_call(
        functools.partial(_hy2_stage2_kernel, filtered=filtered), grid=(nk // kb, nb),
        in_specs=in_specs, out_specs=[zspec(0), zspec(0)],
        out_shape=[jax.ShapeDtypeStruct((nb, 1, nk, HY_FAST, cw), BF16 if filtered else F32)] * 2,
        compiler_params=_params(("parallel", "parallel")),
        name="hyena2_stage2" if filtered else "hyena2_filter_spectrum",
    )(*args)


def _hy2_inv1_kernel(vr_ref, vi_ref, g1_ref, u_ref, gate_ref, bias_ref, g_ref, o_ref, *, final):
    nk = vr_ref.shape[2]
    cw = vr_ref.shape[4]
    for g in range(vr_ref.shape[3] // HY_SUB):
        rows = slice(g * HY_SUB, (g + 1) * HY_SUB)
        rhs = jnp.concatenate([vr_ref[0, 0, :, rows, :].reshape(nk * HY_SUB, cw),
                               vi_ref[0, 0, :, rows, :].reshape(nk * HY_SUB, cw)], axis=0)
        y = _dot(g1_ref[...], rhs).reshape(nk, HY_SUB, cw)
        out = gate_ref[0, 0, :, rows, :] * (y + bias_ref[...] * u_ref[0, 0, :, rows, :])
        if final:
            out = _rms(out, g_ref[...])
        o_ref[0, :, rows, :] = out.astype(o_ref.dtype)


def _hy2_inv1(vr, vi, g1k, u5, which_u, gate5, which_g, bias, g, final):
    nb, _, nk, _, cw = vr.shape
    tg = 32
    vspec = pl.BlockSpec((1, 1, nk, tg, cw), lambda b, j: (b, 0, 0, j, 0))
    return pl.pallas_call(
        functools.partial(_hy2_inv1_kernel, final=final), grid=(nb, HY_FAST // tg),
        in_specs=[vspec, vspec,
                  pl.BlockSpec(g1k.shape, lambda b, j: (0, 0)),
                  pl.BlockSpec((1, 1, nk, tg, cw), lambda b, j: (which_u, b, 0, j, 0)),
                  pl.BlockSpec((1, 1, nk, tg, cw), lambda b, j: (which_g, b, 0, j, 0)),
                  pl.BlockSpec((1, cw), lambda b, j: (0, 0)),
                  pl.BlockSpec((1, cw), lambda b, j: (0, 0))],
        out_specs=pl.BlockSpec((1, nk, tg, cw), lambda b, j: (b, 0, j, 0)),
        out_shape=jax.ShapeDtypeStruct((nb, nk, HY_FAST, cw), BF16 if final else F32),
        compiler_params=_params(("parallel", "parallel")), name="hyena2_inv1",
    )(vr, vi, g1k, u5, gate5, bias.reshape(1, cw), g.reshape(1, cw))


def _hy2_coefs(length, tables, hy_params):
    f1k, f2, _, _ = tables
    nk = length // HY_FAST
    taps, nrm = _hyena_filter_taps(length, *hy_params)
    z = _hy2_stage1(taps.reshape(1, 2 * HY_ORDER, nk, HY_FAST, HY_W), 0, f1k)
    xr, xi = _hy2_stage2(z, f2)
    scale = (1.0 / length) / nrm.reshape(HY_ORDER, 1, 1, HY_W)
    return xr[:HY_ORDER, 0] * scale, xi[HY_ORDER:, 0] * scale


def _hyena2(conv3, tables, coefs, hy_bias, hy_g):
    f1k, f2, g2, g1k = tables
    hr, hi = coefs
    _, bsz, length, cw = conv3.shape
    nk = length // HY_FAST
    c5 = conv3.reshape(3, bsz, nk, HY_FAST, cw)

    def conv(x5, which, order, gate_idx, final):
        vr, vi = _hy2_stage2(_hy2_stage1(x5, which, f1k), f2, g2, hr[order], hi[order])
        return _hy2_inv1(vr, vi, g1k, x5, which, c5, gate_idx, hy_bias[order], hy_g, final)

    zz = conv(c5, 0, 0, 1, False)
    return conv(zz[None], 0, 1, 2, True).reshape(bsz, length, cw)


def _softplus(v):
    return jnp.maximum(v, 0.0) + jnp.log(1.0 + jnp.exp(-jnp.abs(v)))


def _delta_prep_kernel(qkv_ref, ab_ref, abt_ref, arow_ref, drow_ref, acol_ref, dcol_ref,
                       wq_ref, u_ref, ak_ref, egl_ref):
    c = ab_ref.shape[1]
    nh = DN_HEADS
    ri = lax.broadcasted_iota(jnp.int32, (c, c), 0)
    ci = lax.broadcasted_iota(jnp.int32, (c, c), 1)
    incl = (ri >= ci, ri <= ci)
    strict = (ri > ci, ri < ci)
    low = jnp.where(incl[0], 1.0, 0.0)
    upp = jnp.where(incl[1], 1.0, 0.0)
    hdot = lambda a, b: jnp.dot(a, b, precision=HIGHEST, preferred_element_type=F32)
    ab = ab_ref[0]
    abt = abt_ref[0, 0]
    gate = -arow_ref[...] * _softplus(ab + drow_ref[...])
    gate_t = -acol_ref[...] * _softplus(abt + dcol_ref[...])
    beta_all = jax.nn.sigmoid(ab)
    gcum = (hdot(low, gate), hdot(upp, gate))
    gcum_t = (hdot(gate_t, upp), hdot(gate_t, low))
    nt = (((1,), (1,)), ((), ()))
    def l2n(t):
        return t * lax.rsqrt(jnp.sum(t * t, axis=-1, keepdims=True) + EPS)

    q = [l2n(_silu(qkv_ref[0, :, h * DN_DK:(h + 1) * DN_DK])) * (DN_DK ** -0.5) for h in range(nh)]
    k = [l2n(_silu(qkv_ref[0, :, DN_W + h * DN_DK:DN_W + (h + 1) * DN_DK])) for h in range(nh)]
    v = [_silu(qkv_ref[0, :, 2 * DN_W + h * DN_DV:2 * DN_W + (h + 1) * DN_DV]) for h in range(nh)]
    k16 = [t.astype(BF16) for t in k]
    kk0 = [lax.dot_general(k16[h], k16[h], nt, preferred_element_type=F32) for h in range(nh)]
    qk0 = [lax.dot_general(q[h].astype(BF16), k16[h], nt, preferred_element_type=F32) for h in range(nh)]
    units = [(d, h) for d in range(2) for h in range(nh)]
    mm, rr = {}, {}
    for d, h in units:
        col = d * nh + h
        gc = gcum[d][:, col:col + 1]
        gr = gcum_t[d][col:col + 1, :]
        be = beta_all[:, 2 * nh + col:2 * nh + col + 1]
        last = 0 if d else c - 1
        gl = gc[last:last + 1, :]
        dec = jnp.where(incl[d], jnp.exp(jnp.where(incl[d], gc - gr, 0.0)), 0.0)
        mm[d, h] = jnp.where(strict[d], be * kk0[h] * dec, 0.0)
        ak_ref[0, 0, d, h, 0:c, :] = jnp.where(incl[d], qk0[h] * dec, 0.0).astype(BF16)
        eg = jnp.exp(gc)
        rr[d, h] = jnp.concatenate([v[h] * be, k[h] * (be * eg)], axis=1)
        wq_ref[0, 0, d, h, c:2 * c, :] = (q[h] * eg).astype(BF16)
        k_tail = k[h] * jnp.exp(gl - gc)
        ak_ref[0, 0, d, h, c:c + DN_DK, :] = jnp.transpose(k_tail).astype(BF16)
        egl_ref[0, 0, d, h:h + 1, :] = jnp.broadcast_to(jnp.exp(gl), (1, DN_DV))
    sb = DN_SOLVE_BLOCK
    in_sb = (ri // sb) == (ci // sb)
    in_2sb = (ri // (2 * sb)) == (ci // (2 * sb))
    eye = jnp.where(ri == ci, 1.0, 0.0)
    b16 = lambda t: t.astype(BF16)
    nj = {u: jnp.where(in_sb, -mm[u], 0.0) for u in units}
    inv = {u: eye + nj[u] for u in units}
    for j in range((sb - 1).bit_length() - 1):
        nj = {u: _dot(b16(nj[u]), b16(nj[u])) for u in units}
        inv = {u: inv[u] + _dot(b16(inv[u]), b16(nj[u])) for u in units}
    off = {u: b16(jnp.where(in_2sb & ~in_sb, mm[u], 0.0)) for u in units}
    tmp = {u: _dot(off[u], b16(inv[u])) for u in units}
    inv = {u: inv[u] - _dot(b16(inv[u]), b16(tmp[u])) for u in units}
    inv16 = {u: b16(inv[u]) for u in units}
    off = {u: b16(jnp.where(in_2sb, 0.0, mm[u])) for u in units}
    part = {u: _dot(inv16[u], b16(rr[u])) for u in units}
    tmp = {u: _dot(off[u], b16(part[u])) for u in units}
    rr = {u: part[u] - _dot(inv16[u], b16(tmp[u])) for u in units}
    for d, h in units:
        u_ref[0, 0, d, h] = rr[d, h][:, :DN_DV]
        wq_ref[0, 0, d, h, 0:c, :] = rr[d, h][:, DN_DV:].astype(BF16)


def _delta_scan_kernel(wqf_ref, uf_ref, akf_ref, egf_ref, wqb_ref, ub_ref, akb_ref, egb_ref, s0f_ref, s0b_ref,
                       of_ref, ob_ref, sf_ref, sb_ref, s_scr):
    i = pl.program_id(1)
    bpb = uf_ref.shape[0]
    c = uf_ref.shape[-2]

    @pl.when(i == 0)
    def _():
        s_scr[0] = s0f_ref[...]
        s_scr[1] = s0b_ref[...]

    refs = ((wqf_ref, uf_ref, akf_ref, egf_ref, of_ref), (wqb_ref, ub_ref, akb_ref, egb_ref, ob_ref))
    units = [(d, bb, h) for d in range(2) for bb in range(bpb) for h in range(DN_HEADS)]
    s = {u: s_scr[u] for u in units}
    ws = {(d, bb, h): _dot(refs[d][0][bb, 0, 0, h], s[d, bb, h].astype(BF16)) for d, bb, h in units}
    v16 = {(d, bb, h): (refs[d][1][bb, 0, 0, h] - ws[d, bb, h][:c]).astype(BF16) for d, bb, h in units}
    av = {(d, bb, h): _dot(refs[d][2][bb, 0, 0, h], v16[d, bb, h]) for d, bb, h in units}
    for d, bb, h in units:
        u = (d, bb, h)
        refs[d][4][bb, :, h * DN_DV:(h + 1) * DN_DV] = (ws[u][c:] + av[u][:c]).astype(BF16)
        s_scr[u] = s[u] * refs[d][3][bb, 0, 0, h:h + 1, :] + av[u][c:]

    @pl.when(i == pl.num_programs(1) - 1)
    def _():
        sf_ref[...] = s_scr[0]
        sb_ref[...] = s_scr[1]


def _delta_rule(qkv, ab, a_log, dt_bias, s0_f, s0_b):
    bsz, length, _ = qkv.shape
    bpb = 2 if bsz % 2 == 0 else 1
    c = DN_CHUNK
    n = length // c
    nh = DN_HEADS
    abt = jnp.swapaxes(ab.reshape(bsz, n, c, 4 * nh), 2, 3)
    zeros = jnp.zeros((2 * nh,), F32)
    a_vec = jnp.concatenate([jnp.exp(a_log.astype(F32)).reshape(-1), zeros])
    d_vec = jnp.concatenate([dt_bias.astype(F32).reshape(-1), zeros])
    small = lambda shape: pl.BlockSpec(shape, lambda b, i: (0,) * len(shape))
    per_chunk = lambda *tail: pl.BlockSpec((1, 1, 2, nh) + tail, lambda b, i: (b, i, 0, 0) + (0,) * len(tail))
    wq, u, ak, egl = pl.pallas_call(
        _delta_prep_kernel, grid=(bsz, n),
        in_specs=[pl.BlockSpec((1, c, 3 * DN_W), lambda b, i: (b, i, 0)),
                  pl.BlockSpec((1, c, 4 * nh), lambda b, i: (b, i, 0)),
                  pl.BlockSpec((1, 1, 4 * nh, c), lambda b, i: (b, i, 0, 0)),
                  small((1, 4 * nh)), small((1, 4 * nh)), small((4 * nh, 1)), small((4 * nh, 1))],
        out_specs=[per_chunk(2 * c, DN_DK), per_chunk(c, DN_DV), per_chunk(c + DN_DK, c),
                   pl.BlockSpec((1, 1, 2, nh, DN_DV), lambda b, i: (b, i, 0, 0, 0))],
        out_shape=[jax.ShapeDtypeStruct((bsz, n, 2, nh, 2 * c, DN_DK), BF16),
                   jax.ShapeDtypeStruct((bsz, n, 2, nh, c, DN_DV), F32),
                   jax.ShapeDtypeStruct((bsz, n, 2, nh, c + DN_DK, c), BF16),
                   jax.ShapeDtypeStruct((bsz, n, 2, nh, DN_DV), F32)],
        compiler_params=_params(("parallel", "parallel")), name="delta_prep",
    )(qkv, ab, abt, a_vec.reshape(1, -1), d_vec.reshape(1, -1), a_vec.reshape(-1, 1), d_vec.reshape(-1, 1))

    def side(d, *tail):
        idx = (lambda b, i: (b, n - 1 - i, 1, 0) + (0,) * len(tail)) if d else (
            lambda b, i: (b, i, 0, 0) + (0,) * len(tail))
        return pl.BlockSpec((bpb, 1, 1, nh) + tail, idx)

    def side_specs(d):
        return [side(d, 2 * c, DN_DK), side(d, c, DN_DV), side(d, c + DN_DK, c),
                pl.BlockSpec((bpb, 1, 1, nh, DN_DV),
                             (lambda b, i: (b, n - 1 - i, 1, 0, 0)) if d else (lambda b, i: (b, i, 0, 0, 0)))]

    st_spec = pl.BlockSpec((bpb, nh, DN_DK, DN_DV), lambda b, i: (b, 0, 0, 0))
    o_shape = jax.ShapeDtypeStruct((bsz, length, DN_W), BF16)
    s_shape = jax.ShapeDtypeStruct((bsz, nh, DN_DK, DN_DV), F32)
    return pl.pallas_call(
        _delta_scan_kernel, grid=(bsz // bpb, n),
        in_specs=side_specs(0) + side_specs(1) + [st_spec, st_spec],
        out_specs=[pl.BlockSpec((bpb, c, DN_W), lambda b, i: (b, i, 0)),
                   pl.BlockSpec((bpb, c, DN_W), lambda b, i: (b, n - 1 - i, 0)), st_spec, st_spec],
        out_shape=[o_shape, o_shape, s_shape, s_shape],
        scratch_shapes=[pltpu.VMEM((2, bpb, nh, DN_DK, DN_DV), F32)],
        compiler_params=_params(("parallel", "arbitrary")), name="delta_scan",
    )(wq, u, ak, egl, wq, u, ak, egl, s0_f, s0_b)


def _proj_out_kernel(yf_ref, yh_ref, of_ref, ob_ref, z_ref, dg_ref, w_ref, res_ref, gate_ref, o_ref, a_scr):
    @pl.when(pl.program_id(1) == 0)
    def _():
        a_scr[:, 0:FOUR_W] = yf_ref[...]
        a_scr[:, FOUR_W:FOUR_W + HY_W] = yh_ref[...]
        o = of_ref[...].astype(F32) + ob_ref[...].astype(F32)
        z = z_ref[...].astype(F32)
        for h in range(DN_HEADS):
            lanes = slice(h * DN_DV, (h + 1) * DN_DV)
            y = _rms(o[:, lanes], dg_ref[...]) * _silu(z[:, lanes])
            a_scr[:, FOUR_W + HY_W + h * DN_DV:FOUR_W + HY_W + (h + 1) * DN_DV] = y.astype(BF16)

    o_ref[...] = res_ref[...] + gate_ref[0] * _dot(a_scr[...], w_ref[...])


def _proj_out(y_four, y_hy, o_f, o_b, p2d, dn_g, w, res, gate, rows_per_mod):
    m, d = res.shape
    tm = min(512, m)
    tn = d
    per = rows_per_mod // tm
    zb = PCOL_Z // DN_W
    return pl.pallas_call(
        _proj_out_kernel, grid=(m // tm, d // tn),
        in_specs=[pl.BlockSpec((tm, FOUR_W), lambda i, j: (i, 0)),
                  pl.BlockSpec((tm, HY_W), lambda i, j: (i, 0)),
                  pl.BlockSpec((tm, DN_W), lambda i, j: (i, 0)),
                  pl.BlockSpec((tm, DN_W), lambda i, j: (i, 0)),
                  pl.BlockSpec((tm, DN_W), lambda i, j: (i, zb)),
                  pl.BlockSpec((1, DN_DV), lambda i, j: (0, 0)),
                  pl.BlockSpec((w.shape[0], tn), lambda i, j: (0, j)),
                  pl.BlockSpec((tm, tn), lambda i, j: (i, j)),
                  pl.BlockSpec((1, 1, tn), lambda i, j: (i // per, 0, j))],
        out_specs=pl.BlockSpec((tm, tn), lambda i, j: (i, j)),
        out_shape=jax.ShapeDtypeStruct((m, d), F32),
        scratch_shapes=[pltpu.VMEM((tm, w.shape[0]), BF16)],
        compiler_params=_params(("parallel", "arbitrary")), name="proj_out",
    )(y_four, y_hy, o_f, o_b, p2d, dn_g.reshape(1, DN_DV), w, res, gate)


def _router_kernel(x_ref, c_ref, g_ref, sx_ref, cx_ref, sc_ref, cc_ref, w_ref, b_ref, h_ref, lg_ref, *, n_x):
    i = pl.program_id(0)

    def emit(v, shift, scale):
        h = _rms(v, g_ref[...]) * (1.0 + scale) + shift
        h_ref[...] = h
        lg_ref[...] = jnp.dot(h, w_ref[...], precision=HIGHEST, preferred_element_type=F32) + b_ref[...]

    @pl.when(i < n_x)
    def _():
        emit(x_ref[...], sx_ref[0], cx_ref[0])

    @pl.when(i >= n_x)
    def _():
        emit(c_ref[...], sc_ref[0], cc_ref[0])


def _router(x2d, c2d, g, shift_x, scale_x, shift_c, scale_c, w_r, b_r, rows_per_mod, with_ctx):
    mx, d = x2d.shape
    tm = 512
    n_x = mx // tm
    n_c = c2d.shape[0] // tm if with_ctx else 0
    per = rows_per_mod // tm
    xi = lambda i: (jnp.minimum(i, n_x - 1), 0)
    ci = lambda i: (jnp.maximum(i - n_x, 0), 0)
    return pl.pallas_call(
        functools.partial(_router_kernel, n_x=n_x), grid=(n_x + n_c,),
        in_specs=[pl.BlockSpec((tm, d), xi), pl.BlockSpec((tm, d), ci),
                  pl.BlockSpec((1, d), lambda i: (0, 0)),
                  pl.BlockSpec((1, 1, d), lambda i: (jnp.minimum(i, n_x - 1) // per, 0, 0)),
                  pl.BlockSpec((1, 1, d), lambda i: (jnp.minimum(i, n_x - 1) // per, 0, 0)),
                  pl.BlockSpec((1, 1, d), lambda i: (0, 0, 0)),
                  pl.BlockSpec((1, 1, d), lambda i: (0, 0, 0)),
                  pl.BlockSpec((d, ROUTER_W), lambda i: (0, 0)),
                  pl.BlockSpec((1, ROUTER_W), lambda i: (0, 0))],
        out_specs=[pl.BlockSpec((tm, d), lambda i: (i, 0)),
                   pl.BlockSpec((tm, ROUTER_W), lambda i: (i, 0))],
        out_shape=[jax.ShapeDtypeStruct(((n_x + n_c) * tm, d), F32),
                   jax.ShapeDtypeStruct(((n_x + n_c) * tm, ROUTER_W), F32)],
        compiler_params=_params(("parallel",)), name="moe_router",
    )(x2d, c2d, g.reshape(1, d), shift_x, scale_x, shift_c, scale_c, w_r, b_r)


def _row_copy(src_hbm, row, dst, slot, r, sem):
    return pltpu.make_async_copy(src_hbm.at[pl.ds(row, 1)], dst.at[slot, pl.ds(r, 1)], sem.at[slot])


def _expert_kernel(be_ref, tok_ref, nb_ref, h_hbm, wg_ref, wu_ref, wd_ref, ys_ref, xbuf, sem, *, bm):
    i = pl.program_id(0)
    nb = nb_ref[0]

    def start(blk, slot):
        def body(r, carry):
            _row_copy(h_hbm, tok_ref[blk * bm + r], xbuf, slot, r, sem).start()
            return carry
        lax.fori_loop(0, bm, body, 0, unroll=8)

    def wait(slot):
        pltpu.make_async_copy(h_hbm.at[pl.ds(0, bm)], xbuf.at[slot], sem.at[slot]).wait()

    @pl.when(i == 0)
    def _():
        start(0, 0)

    @pl.when(i + 1 < nb)
    def _():
        start(i + 1, (i + 1) % 2)

    @pl.when(i < nb)
    def _():
        slot = i % 2
        wait(slot)
        x = xbuf[slot].astype(BF16)
        act = (_silu(_dot(x, wg_ref[0, 0])) * _dot(x, wu_ref[0, 0])).astype(BF16)
        ys_ref[...] = _dot(act, wd_ref[0, 0])

    @pl.when(i >= nb)
    def _():
        ys_ref[...] = jnp.zeros_like(ys_ref)


def _experts(h, buf_tok, block_e, n_used, wg, wu, wd, layer, bm):
    n_blocks = block_e.shape[0]
    d = h.shape[1]
    ff = wg.shape[3]
    grid_spec = pltpu.PrefetchScalarGridSpec(
        num_scalar_prefetch=3, grid=(n_blocks,),
        in_specs=[pl.BlockSpec(memory_space=pl.ANY),
                  pl.BlockSpec((1, 1, d, ff), lambda i, be, tok, nb: (layer, be[i], 0, 0)),
                  pl.BlockSpec((1, 1, d, ff), lambda i, be, tok, nb: (layer, be[i], 0, 0)),
                  pl.BlockSpec((1, 1, ff, d), lambda i, be, tok, nb: (layer, be[i], 0, 0))],
        out_specs=pl.BlockSpec((bm, d), lambda i, be, tok, nb: (i, 0)),
        scratch_shapes=[pltpu.VMEM((2, bm, d), F32), pltpu.SemaphoreType.DMA((2,))])
    return pl.pallas_call(
        functools.partial(_expert_kernel, bm=bm), grid_spec=grid_spec,
        out_shape=jax.ShapeDtypeStruct((n_blocks * bm, d), F32),
        compiler_params=_params(("arbitrary",)), name="moe_experts",
    )(block_e, buf_tok, n_used, h, wg, wu, wd)


def _combine_kernel(pos_ref, ys_hbm, x_ref, gw_ref, gate_ref, fg_ref, o_ref, ybuf, sem, *, tm, tok0, final):
    i = pl.program_id(0)
    n = pl.num_programs(0)

    def start(blk, slot):
        def body(r, carry):
            t = tok0 + blk * tm + r
            _row_copy(ys_hbm, pos_ref[2 * t], ybuf, slot, r, sem).start()
            _row_copy(ys_hbm, pos_ref[2 * t + 1], ybuf, slot, tm + r, sem).start()
            return carry
        lax.fori_loop(0, tm, body, 0, unroll=8)

    @pl.when(i == 0)
    def _():
        start(0, 0)

    @pl.when(i + 1 < n)
    def _():
        start(i + 1, (i + 1) % 2)

    slot = i % 2

    pltpu.make_async_copy(ys_hbm.at[pl.ds(0, 2 * tm)], ybuf.at[slot], sem.at[slot]).wait()
    gw = gw_ref[...]
    y = gw[:, 0:1] * ybuf[slot, 0:tm, :] + gw[:, 1:2] * ybuf[slot, tm:2 * tm, :]
    out = x_ref[...] + gate_ref[0] * y
    if final:
        out = _rms(out, fg_ref[...])
    o_ref[...] = out


def _combine(ys, pos, gw, x2d, gate, final_g, tok0, rows_per_mod, final):
    m, d = x2d.shape
    tm = 256
    per = rows_per_mod // tm
    gb = tok0 // tm
    grid_spec = pltpu.PrefetchScalarGridSpec(
        num_scalar_prefetch=1, grid=(m // tm,),
        in_specs=[pl.BlockSpec(memory_space=pl.ANY),
                  pl.BlockSpec((tm, d), lambda i, pos: (i, 0)),
                  pl.BlockSpec((tm, TOP_K), lambda i, pos: (gb + i, 0)),
                  pl.BlockSpec((1, 1, d), lambda i, pos: (i // per, 0, 0)),
                  pl.BlockSpec((1, d), lambda i, pos: (0, 0))],
        out_specs=pl.BlockSpec((tm, d), lambda i, pos: (i, 0)),
        scratch_shapes=[pltpu.VMEM((2, 2 * tm, d), F32), pltpu.SemaphoreType.DMA((2,))])
    return pl.pallas_call(
        functools.partial(_combine_kernel, tm=tm, tok0=tok0, final=final), grid_spec=grid_spec,
        out_shape=jax.ShapeDtypeStruct((m, d), F32),
        compiler_params=_params(("arbitrary",)), name="moe_combine",
    )(pos, ys, x2d, gw, gate, final_g.reshape(1, d))


def _route(logits):
    t = logits.shape[0]
    p_grp = jax.nn.softmax(logits[:, :N_GROUPS], axis=-1)
    p_sel = jnp.max(p_grp, axis=-1, keepdims=True)
    grp = jnp.argmax(p_grp, axis=-1).astype(jnp.int32)[:, None]
    fine = logits[:, N_GROUPS:N_GROUPS + N_EXPERTS].reshape(t, N_GROUPS, EXPERTS_PER_GROUP)
    in_grp = (jnp.arange(N_GROUPS, dtype=jnp.int32)[None, :] == grp)[:, :, None]
    fine = jnp.sum(jnp.where(in_grp, fine, 0.0), axis=1)
    lanes = jnp.arange(EXPERTS_PER_GROUP, dtype=jnp.int32)[None, :]
    i1 = jnp.argmax(fine, axis=-1).astype(jnp.int32)[:, None]
    v1 = jnp.max(fine, axis=-1, keepdims=True)
    rest = jnp.where(lanes == i1, -jnp.inf, fine)
    i2 = jnp.argmax(rest, axis=-1).astype(jnp.int32)[:, None]
    v2 = jnp.max(rest, axis=-1, keepdims=True)
    top_v = jnp.concatenate([v1, v2], axis=-1)
    top_i = jnp.concatenate([i1, i2], axis=-1)
    gate = p_sel * jax.nn.softmax(top_v, axis=-1)
    return grp * EXPERTS_PER_GROUP + top_i, gate


def _rank_kernel(e_ref, rank_ref, cnt_ref, run_scr):
    @pl.when(pl.program_id(0) == 0)
    def _():
        run_scr[...] = jnp.zeros_like(run_scr)

    tb = e_ref.shape[0]
    lane = lax.broadcasted_iota(jnp.int32, (tb, LANE), 1)
    onehot = jnp.where(e_ref[...] == lane, 1.0, 0.0)
    ri = lax.broadcasted_iota(jnp.int32, (tb, tb), 0)
    ci = lax.broadcasted_iota(jnp.int32, (tb, tb), 1)
    before = jnp.where(ri > ci, 1.0, 0.0).astype(BF16)
    prefix = _dot(before, onehot.astype(BF16)) + run_scr[...]
    rank_ref[...] = jnp.sum(prefix * onehot, axis=1, keepdims=True).astype(jnp.int32)
    run_scr[...] += jnp.sum(onehot, axis=0, keepdims=True)
    cnt_ref[...] = run_scr[...]


def _expert_ranks(flat_e):
    a = flat_e.shape[0]
    tb = 512
    rank, cnt = pl.pallas_call(
        _rank_kernel, grid=(a // tb,),
        in_specs=[pl.BlockSpec((tb, 1), lambda i: (i, 0))],
        out_specs=[pl.BlockSpec((tb, 1), lambda i: (i, 0)), pl.BlockSpec((1, LANE), lambda i: (0, 0))],
        out_shape=[jax.ShapeDtypeStruct((a, 1), jnp.int32), jax.ShapeDtypeStruct((1, LANE), F32)],
        scratch_shapes=[pltpu.VMEM((1, LANE), F32)],
        compiler_params=_params(("arbitrary",)), name="moe_rank",
    )(flat_e.reshape(a, 1))
    return rank.reshape(a), cnt[0, :N_EXPERTS].astype(jnp.int32)


def _dispatch_plan(expert, bm):
    t = expert.shape[0]
    a = t * TOP_K
    flat_e = expert.reshape(a).astype(jnp.int32)
    rank, counts = _expert_ranks(flat_e)
    padded = (counts + bm - 1) // bm * bm
    pad_end = jnp.cumsum(padded)
    pad_start = pad_end - padded
    mine = flat_e[:, None] == jnp.arange(N_EXPERTS, dtype=jnp.int32)[None, :]
    pos = (jnp.sum(jnp.where(mine, pad_start[None, :], 0), axis=1) + rank).astype(jnp.int32)
    n_blocks = a // bm + N_EXPERTS
    flat_tok = jnp.arange(a, dtype=jnp.int32) // TOP_K
    buf_tok = jnp.zeros((n_blocks * bm,), jnp.int32).at[pos].set(flat_tok)
    n_used = (pad_end[-1] // bm).astype(jnp.int32)
    blk = jnp.minimum(jnp.arange(n_blocks, dtype=jnp.int32), n_used - 1) * bm
    block_e = jnp.sum((pad_end[None, :] <= blk[:, None]).astype(jnp.int32), axis=1)
    block_e = jnp.minimum(block_e, N_EXPERTS - 1).astype(jnp.int32)
    return pos, buf_tok, block_e, n_used.reshape(1)


def _mix_stream(p3, ab, rows, width, s0_f, s0_b, four_tabs, hy_tabs, hy_coefs, conv_w, conv_b, four_g,
                hy_bias, hy_g, a_log, dt_bias, with_mixers):
    qkv = _short_conv(p3, conv_w, conv_b, rows, width, OFF_DN, HY_CONV_CH, 3 * DN_W, 512)[0]
    o_f, o_b, s_f, s_b = _delta_rule(qkv, ab, a_log, dt_bias, s0_f, s0_b)
    if not with_mixers:
        return None, None, o_f, o_b, s_f, s_b
    y_four = _fourier(p3, *four_tabs, four_g)
    if _hy_two_level(p3.shape[1]):
        conv3 = _short_conv(p3, conv_w, conv_b, rows, width, OFF_HY, 0, HY_CONV_CH, 256, split=3)
        y_hy = _hyena2(conv3, hy_tabs, hy_coefs, hy_bias, hy_g)
    else:
        convh = _short_conv(p3, conv_w, conv_b, rows, width, OFF_HY, 0, HY_CONV_CH, 256)[0]
        y_hy = _hyena(convh, hy_tabs, hy_coefs, hy_bias, hy_g)
    return y_four, y_hy, o_f, o_b, s_f, s_b


def _hy_two_level(length):
    return length % (8 * HY_FAST) == 0


def kernel(x, c, ctx, c_ctx, norm1_g, norm2_g, w_mod, b_mod, w_in, conv_w, conv_b, four_g, hy_w1, hy_b1, hy_w2, hy_b2, hy_w3, hy_b3, hy_w4, hy_freq, hy_bias, hy_g, dn_a_log, dn_dt_bias, dn_g, w_out, w_rc, b_rc, w_rf, b_rf, w_e_gate, w_e_up, w_e_down, final_g):
    bsz, length, d = x.shape
    lc = ctx.shape[1]
    depth = w_in.shape[0]
    rows = length // GRID_W
    bm = 256

    mod_all = _modulation(c, c_ctx, w_mod, b_mod)
    four_x, four_c = _fourier_tables(length), _fourier_tables(lc)
    hy_tables = lambda n: _hy2_tables(n) if _hy_two_level(n) else _hyena_tables(n)
    hy_coefs = lambda n, tabs, prm: (_hy2_coefs if _hy_two_level(n) else _hyena_coefs)(n, tabs, prm)
    hy_x, hy_c = hy_tables(length), hy_tables(lc)
    zeros = jnp.zeros((bsz, DN_HEADS, DN_DK, DN_DV), F32)
    x2 = x.reshape(bsz * length, d)
    c2 = ctx.reshape(bsz * lc, d)
    wg16, wu16, wd16 = w_e_gate.astype(BF16), w_e_up.astype(BF16), w_e_down.astype(BF16)

    for l in range(depth):
        last = l == depth - 1
        mod = [m[:, None, :] for m in jnp.split(mod_all[l, :bsz], 6, axis=-1)]
        modc = [m[:, None, :] for m in jnp.split(mod_all[l, bsz:bsz + 1], 6, axis=-1)]
        wi = w_in[l]
        w_p = jnp.concatenate([wi[:, :OFF_AB], wi[:, OFF_Z:], wi[:, OFF_AB:OFF_Z],
                               jnp.zeros((d, PROJ_W - PCOL_AB - N_AB), wi.dtype)], axis=1).astype(BF16)
        w_o = w_out[l].astype(BF16)
        hy_params = (hy_w1[l], hy_b1[l], hy_w2[l], hy_b2[l], hy_w3[l], hy_b3[l], hy_w4[l], hy_freq[l])
        mix_args = (conv_w[l], conv_b[l], four_g[l], hy_bias[l], hy_g[l], dn_a_log[l], dn_dt_bias[l])

        pc, abc = _proj_in(c2, norm1_g[l], modc[0], modc[1], w_p, bsz * lc)
        pc3 = pc.reshape(bsz, lc, PROJ_W)
        abc = abc[:, :N_AB].reshape(bsz, lc, N_AB)
        coefs_c = None if last else hy_coefs(lc, hy_c, hy_params)
        yf, yh, o_f, o_b, s_f, s_b = _mix_stream(pc3, abc, 1, lc, zeros, zeros, four_c, hy_c, coefs_c,
                                                 *mix_args, with_mixers=not last)
        if not last:
            c2 = _proj_out(yf.reshape(-1, FOUR_W), yh.reshape(-1, HY_W), o_f.reshape(-1, DN_W),
                           o_b.reshape(-1, DN_W), pc, dn_g[l], w_o, c2, modc[2], bsz * lc)

        p, abx = _proj_in(x2, norm1_g[l], mod[0], mod[1], w_p, length)
        p3 = p.reshape(bsz, length, PROJ_W)
        abx = abx[:, :N_AB].reshape(bsz, length, N_AB)
        coefs_x = hy_coefs(length, hy_x, hy_params)
        yf, yh, o_f, o_b, _, _ = _mix_stream(p3, abx, rows, GRID_W, s_f, s_b, four_x, hy_x, coefs_x,
                                             *mix_args, with_mixers=True)
        x2 = _proj_out(yf.reshape(-1, FOUR_W), yh.reshape(-1, HY_W), o_f.reshape(-1, DN_W),
                       o_b.reshape(-1, DN_W), p, dn_g[l], w_o, x2, mod[2], length)

        w_r = jnp.concatenate([w_rc[l], w_rf[l], jnp.zeros((d, ROUTER_W - N_GROUPS - N_EXPERTS), F32)], axis=1)
        b_r = jnp.concatenate([b_rc[l], b_rf[l], jnp.zeros((ROUTER_W - N_GROUPS - N_EXPERTS,), F32)])[None, :]
        h, logits = _router(x2, c2, norm2_g[l], mod[3], mod[4], modc[3], modc[4], w_r, b_r, length,
                            with_ctx=not last)
        expert, gate = _route(logits)
        pos, buf_tok, block_e, n_used = _dispatch_plan(expert, bm)
        ys = _experts(h, buf_tok, block_e, n_used, wg16, wu16, wd16, l, bm)
        x2 = _combine(ys, pos, gate, x2, mod[5], final_g, 0, length, final=last)
        if not last:
            c2 = _combine(ys, pos, gate, c2, modc[5], final_g, bsz * length, bsz * lc, final=False)
    return x2.reshape(bsz, length, d)
```

```python
import functools
import math

import jax
import jax.numpy as jnp
from jax import lax
from jax.experimental import pallas as pl
from jax.experimental.pallas import tpu as pltpu

F32 = jnp.float32
BF16 = jnp.bfloat16
HIGHEST = lax.Precision.HIGHEST

GRID_W = 64
FOUR_W = 512
FOUR_GROUPS = 4
FOUR_GW = FOUR_W // FOUR_GROUPS
HY_W = 512
HY_ORDER = 2
DN_HEADS = 8
DN_DK = 128
DN_DV = 128
DN_W = DN_HEADS * DN_DV
DN_CHUNK = 64
DN_SOLVE_BLOCK = DN_CHUNK // 4
HY_CONV_CH = (HY_ORDER + 1) * HY_W
OFF_HY = FOUR_W
OFF_DN = OFF_HY + HY_CONV_CH
OFF_AB = OFF_DN + 3 * DN_W
N_AB = 4 * DN_HEADS
OFF_Z = OFF_AB + N_AB
HY_EMB = 33
HY_FAST_DECAY = 0.3
HY_SLOW_DECAY = 1.5
HY_TARGET = 1e-2
N_GROUPS = 4
EXPERTS_PER_GROUP = 8
N_EXPERTS = N_GROUPS * EXPERTS_PER_GROUP
TOP_K = 2
EPS = 1e-6

LANE = 128
PCOL_Z = OFF_AB
PCOL_AB = PCOL_Z + DN_W
MXU_W = 256
PROJ_TN = 5 * MXU_W
PROJ_W = -(-(PCOL_AB + LANE) // PROJ_TN) * PROJ_TN
ROUTER_W = LANE
VMEM_LIMIT = 56 * 1024 * 1024


def _params(semantics):
    return pltpu.CompilerParams(dimension_semantics=semantics, vmem_limit_bytes=VMEM_LIMIT)


def _silu(v):
    return v * jax.nn.sigmoid(v)


def _rms(v, g):
    return v * lax.rsqrt(jnp.mean(v * v, axis=-1, keepdims=True) + EPS) * g


def _dot(a, b):
    return jnp.dot(a, b, preferred_element_type=F32)


def _mod_kernel(a_ref, w_ref, b_ref, o_ref):
    o_ref[0] = jnp.dot(_silu(a_ref[...]), w_ref[0], precision=HIGHEST,
                       preferred_element_type=F32) + b_ref[0]


def _modulation(c, c_ctx, w_mod, b_mod):
    depth, d, n = w_mod.shape
    bsz = c.shape[0]
    rows = -(-(bsz + 1) // 8) * 8
    a = jnp.concatenate([c, c_ctx[None], jnp.zeros((rows - bsz - 1, d), F32)], axis=0)
    tn = 1024 if n % 1024 == 0 else 512
    assert n % tn == 0
    return pl.pallas_call(
        _mod_kernel, grid=(depth, n // tn),
        in_specs=[pl.BlockSpec((rows, d), lambda l, j: (0, 0)),
                  pl.BlockSpec((1, d, tn), lambda l, j: (l, 0, j)),
                  pl.BlockSpec((1, 1, tn), lambda l, j: (l, 0, j))],
        out_specs=pl.BlockSpec((1, rows, tn), lambda l, j: (l, 0, j)),
        out_shape=jax.ShapeDtypeStruct((depth, rows, n), F32),
        compiler_params=_params(("parallel", "parallel")), name="modulation",
    )(a, w_mod, b_mod.reshape(depth, 1, n))


def _proj_in_kernel(x_ref, g_ref, shift_ref, scale_ref, w_ref, o_ref, ab_ref, a_scr, *, ab_off):
    j = pl.program_id(1)

    @pl.when(j == 0)
    def _():
        y = _rms(x_ref[...], g_ref[...])
        a_scr[...] = (y * (1.0 + scale_ref[0]) + shift_ref[0]).astype(BF16)

    acc = _dot(a_scr[...], w_ref[...])
    o_ref[...] = acc.astype(BF16)

    @pl.when(j == pl.num_programs(1) - 1)
    def _():
        ab_ref[...] = acc[:, ab_off:ab_off + LANE]


def _proj_in(x2d, g, shift, scale, w, rows_per_mod):
    m, d = x2d.shape
    n = w.shape[1]
    tm = min(1024, m, rows_per_mod)
    tn = PROJ_TN
    assert n % tn == 0 and PCOL_AB >= n - tn
    per = rows_per_mod // tm
    return pl.pallas_call(
        functools.partial(_proj_in_kernel, ab_off=PCOL_AB - (n - tn)), grid=(m // tm, n // tn),
        in_specs=[pl.BlockSpec((tm, d), lambda i, j: (i, 0)),
                  pl.BlockSpec((1, d), lambda i, j: (0, 0)),
                  pl.BlockSpec((1, 1, d), lambda i, j: (i // per, 0, 0)),
                  pl.BlockSpec((1, 1, d), lambda i, j: (i // per, 0, 0)),
                  pl.BlockSpec((d, tn), lambda i, j: (0, j))],
        out_specs=[pl.BlockSpec((tm, tn), lambda i, j: (i, j)),
                   pl.BlockSpec((tm, LANE), lambda i, j: (i, 0))],
        out_shape=[jax.ShapeDtypeStruct((m, n), BF16), jax.ShapeDtypeStruct((m, LANE), F32)],
        scratch_shapes=[pltpu.VMEM((tm, d), BF16)],
        compiler_params=_params(("parallel", "arbitrary")), name="proj_in",
    )(x2d, g.reshape(1, d), shift, scale, w)


def _cos_sin_table(nrow, ncol, period):
    i = lax.broadcasted_iota(jnp.int32, (nrow, ncol), 0)
    j = lax.broadcasted_iota(jnp.int32, (nrow, ncol), 1)
    ang = ((i * j) % period).astype(F32) * (2.0 * math.pi / period)
    return jnp.cos(ang), jnp.sin(ang)


def _dft_tables(length, period):
    blk = 64 if length % 64 == 0 else length
    hi_i = lax.broadcasted_iota(jnp.int32, (length // blk, length), 0) * blk
    j = lax.broadcasted_iota(jnp.int32, (length // blk, length), 1)
    ang = ((hi_i * j) % period).astype(F32) * (2.0 * math.pi / period)
    hi_c, hi_s = jnp.cos(ang), jnp.sin(ang)
    lo_c, lo_s = _cos_sin_table(blk, length, period)
    c = hi_c[:, None, :] * lo_c[None, :, :] - hi_s[:, None, :] * lo_s[None, :, :]
    s = hi_s[:, None, :] * lo_c[None, :, :] + hi_c[:, None, :] * lo_s[None, :, :]
    return c.reshape(length, length), s.reshape(length, length)


def _fourier_kernel(u_ref, c_ref, s_ref, bc_ref, bs_ref, g_ref, o_ref, ub_scr, *, scale):
    @pl.when(pl.program_id(1) == 0)
    def _():
        ub_scr[...] = u_ref[0].astype(BF16)

    ub = ub_scr[...]
    p = _dot(c_ref[...], ub).astype(BF16)
    q = _dot(s_ref[...], ub).astype(BF16)
    z = (_dot(p, bc_ref[...]) - _dot(q, bs_ref[...])) * scale
    o_ref[0] = _rms(z, g_ref[...]).astype(o_ref.dtype)


def _fourier(p3, cl, sl, bc, bs, four_g):
    bsz, length, _ = p3.shape
    tm = min(512, length)
    kern = functools.partial(_fourier_kernel, scale=1.0 / math.sqrt(length * FOUR_GW))
    return pl.pallas_call(
        kern, grid=(bsz, length // tm),
        in_specs=[pl.BlockSpec((1, length, FOUR_W), lambda b, i: (b, 0, 0)),
                  pl.BlockSpec((tm, length), lambda b, i: (i, 0)),
                  pl.BlockSpec((tm, length), lambda b, i: (i, 0)),
                  pl.BlockSpec((FOUR_W, FOUR_W), lambda b, i: (0, 0)),
                  pl.BlockSpec((FOUR_W, FOUR_W), lambda b, i: (0, 0)),
                  pl.BlockSpec((1, FOUR_W), lambda b, i: (0, 0))],
        out_specs=pl.BlockSpec((1, tm, FOUR_W), lambda b, i: (b, i, 0)),
        out_shape=jax.ShapeDtypeStruct((bsz, length, FOUR_W), BF16),
        scratch_shapes=[pltpu.VMEM((length, FOUR_W), BF16)],
        compiler_params=_params(("parallel", "arbitrary")), name="fourier",
    )(p3, cl, sl, bc, bs, four_g.reshape(1, FOUR_W))


def _fourier_tables(length):
    cl, sl = _dft_tables(length, length)
    cc, sc = _cos_sin_table(FOUR_GW, FOUR_GW, FOUR_GW)
    eye = jnp.eye(FOUR_GROUPS, dtype=F32)
    return cl.astype(BF16), sl.astype(BF16), jnp.kron(eye, cc).astype(BF16), jnp.kron(eye, sc).astype(BF16)


def _conv_kernel(x_ref, w_ref, b_ref, o_ref, *, rows, width):
    tc = x_ref.shape[2]
    wts = w_ref[...]
    bias = b_ref[...]
    pos = lax.broadcasted_iota(jnp.int32, (width, tc), 0)
    not_first = pos > 0
    not_last = pos < width - 1

    def body(r, carry):
        base = pl.multiple_of(r * width, width)
        cen = x_ref[0, pl.ds(base, width), :].astype(F32)
        if rows > 1:
            up = x_ref[0, pl.ds(pl.multiple_of(jnp.maximum(r - 1, 0) * width, width), width), :].astype(F32)
            dn = x_ref[0, pl.ds(pl.multiple_of(jnp.minimum(r + 1, rows - 1) * width, width), width), :].astype(F32)
            w_up = wts[0] * jnp.where(r > 0, 1.0, 0.0)
            w_dn = wts[2] * jnp.where(r < rows - 1, 1.0, 0.0)
            col = lambda j: up * w_up[j:j + 1, :] + cen * wts[1, j:j + 1, :] + dn * w_dn[j:j + 1, :]
        else:
            col = lambda j: cen * wts[1, j:j + 1, :]
        left = jnp.where(not_first, pltpu.roll(col(0), 1, 0), 0.0)
        right = jnp.where(not_last, pltpu.roll(col(2), width - 1, 0), 0.0)
        o_ref[0, pl.ds(base, width), :] = left + col(1) + right + bias
        return carry

    lax.fori_loop(0, rows, body, 0)


def _short_conv(p3, conv_w, conv_b, rows, width, col0, ch0, nch, tc, split=1):
    bsz, length, _ = p3.shape
    kern = functools.partial(_conv_kernel, rows=rows, width=width)
    cb, wb = col0 // tc, ch0 // tc
    per = nch // split // tc
    out = pl.pallas_call(
        kern, grid=(bsz, nch // tc),
        in_specs=[pl.BlockSpec((1, length, tc), lambda b, j: (b, 0, cb + j)),
                  pl.BlockSpec((3, 3, tc), lambda b, j: (0, 0, wb + j)),
                  pl.BlockSpec((1, tc), lambda b, j: (0, wb + j))],
        out_specs=pl.BlockSpec((1, length, tc), lambda b, j: ((j // per) * bsz + b, 0, j % per)),
        out_shape=jax.ShapeDtypeStruct((split * bsz, length, nch // split), F32),
        compiler_params=_params(("parallel", "parallel")),
        name="short_conv",
    )(p3, conv_w, conv_b.reshape(1, -1))
    return out.reshape(split, bsz, length, nch // split)


def _filt_kernel(z_ref, w1_ref, b1_ref, w2_ref, b2_ref, w3_ref, b3_ref, w4_ref, fr_ref, dl_ref,
                 g_ref, nrm_ref):
    i = pl.program_id(0)
    z = z_ref[...]
    fr = fr_ref[...]
    hdot = lambda a, b: jnp.dot(a, b, precision=HIGHEST, preferred_element_type=F32)
    h = jnp.sin(fr * (hdot(z, w1_ref[...]) + b1_ref[...]))
    h = jnp.sin(fr * (hdot(h, w2_ref[...]) + b2_ref[...]))
    h = jnp.sin(fr * (hdot(h, w3_ref[...]) + b3_ref[...]))
    h = hdot(h, w4_ref[...])
    decay = jnp.exp(-z[:, 0:1] * jnp.abs(dl_ref[...]))
    decay = jnp.concatenate([decay] * HY_ORDER, axis=1)
    half = HY_ORDER * HY_W
    hf = h[:, :half] * decay
    hb = h[:, half:] * decay
    row = lax.broadcasted_iota(jnp.int32, hb.shape, 0) + i * hb.shape[0]
    hb = jnp.where(row > 0, hb, 0.0)
    gp = hf + hb
    gm = hf - hb
    for o in range(HY_ORDER):
        g_ref[o] = gp[:, o * HY_W:(o + 1) * HY_W].astype(BF16)
        g_ref[HY_ORDER + o] = gm[:, o * HY_W:(o + 1) * HY_W].astype(BF16)
    part = jnp.sum(jnp.abs(hf) + jnp.abs(hb), axis=0, keepdims=True)

    @pl.when(i == 0)
    def _():
        nrm_ref[...] = part

    @pl.when(i > 0)
    def _():
        nrm_ref[...] += part


def _hyena_filter_taps(length, w1, b1, w2, b2, w3, b3, w4, freq):
    t = jnp.linspace(0.0, 1.0, length, dtype=F32)[:, None]
    bands = (HY_EMB - 1) // 2
    ang = (2.0 * math.pi / length) * jnp.arange(length, dtype=F32)[:, None]
    f = jnp.linspace(1e-4, bands - 1, bands, dtype=F32)
    z = jnp.concatenate([t, jnp.cos(f * ang), -jnp.sin(f * ang)], axis=-1)
    max_decay = math.log(HY_TARGET) / HY_FAST_DECAY
    min_decay = math.log(HY_TARGET) / HY_SLOW_DECAY
    deltas = jnp.linspace(min_decay, max_decay, HY_W, dtype=F32)[None, :]
    fw = w1.shape[1]
    tl = min(512, length)
    full = lambda shape: pl.BlockSpec(shape, lambda i: (0,) * len(shape))
    return pl.pallas_call(
        _filt_kernel, grid=(length // tl,),
        in_specs=[pl.BlockSpec((tl, HY_EMB), lambda i: (i, 0)),
                  full((HY_EMB, fw)), full((1, fw)), full((fw, fw)), full((1, fw)),
                  full((fw, fw)), full((1, fw)), full((fw, 2 * HY_ORDER * HY_W)), full((1, fw)),
                  full((1, HY_W))],
        out_specs=[pl.BlockSpec((2 * HY_ORDER, tl, HY_W), lambda i: (0, i, 0)),
                   pl.BlockSpec((1, HY_ORDER * HY_W), lambda i: (0, 0))],
        out_shape=[jax.ShapeDtypeStruct((2 * HY_ORDER, length, HY_W), BF16),
                   jax.ShapeDtypeStruct((1, HY_ORDER * HY_W), F32)],
        compiler_params=_params(("arbitrary",)), name="hyena_filter",
    )(z, w1, b1.reshape(1, fw), w2, b2.reshape(1, fw), w3, b3.reshape(1, fw), w4,
      freq.reshape(1, fw), deltas)


def _dft_fwd_kernel(u_ref, c_ref, s_ref, *rest, with_coef):
    if with_coef:
        c1_ref, c2_ref, c4_ref, yr_ref, yi_ref, ub_scr = rest
    else:
        yr_ref, yi_ref, ub_scr = rest

    @pl.when(pl.program_id(1) == 0)
    def _():
        ub_scr[...] = u_ref[0].astype(BF16)

    ub = ub_scr[...]
    a = _dot(c_ref[...], ub)
    b = _dot(s_ref[...], ub)
    if with_coef:
        c2 = c2_ref[...]
        yr_ref[0] = (a * c1_ref[...] + b * c2).astype(BF16)
        yi_ref[0] = (b * c4_ref[...] - a * c2).astype(BF16)
    else:
        yr_ref[0] = a
        yi_ref[0] = b


def _dft_fwd(u3, col_blk, cf, sf, coef=None):
    nb, length, _ = u3.shape
    tm = min(512, length)
    in_specs = [pl.BlockSpec((1, length, HY_W), lambda b, i: (b, 0, col_blk)),
                pl.BlockSpec((tm, length), lambda b, i: (i, 0)),
                pl.BlockSpec((tm, length), lambda b, i: (i, 0))]
    args = [u3, cf, sf]
    if coef is not None:
        in_specs += [pl.BlockSpec((tm, HY_W), lambda b, i: (i, 0))] * 3
        args += list(coef)
    odt = BF16 if coef is not None else F32
    return pl.pallas_call(
        functools.partial(_dft_fwd_kernel, with_coef=coef is not None), grid=(nb, length // tm),
        in_specs=in_specs,
        out_specs=[pl.BlockSpec((1, tm, HY_W), lambda b, i: (b, i, 0))] * 2,
        out_shape=[jax.ShapeDtypeStruct((nb, length, HY_W), odt)] * 2,
        scratch_shapes=[pltpu.VMEM((length, HY_W), BF16)],
        compiler_params=_params(("parallel", "arbitrary")),
        name="hyena_dft_fwd" if coef is not None else "hyena_filter_dft",
    )(*args)


def _dft_inv_kernel(yr_ref, yi_ref, c_ref, s_ref, u_ref, gate_ref, bias_ref, g_ref, o_ref, *, final):
    y = _dot(c_ref[...], yr_ref[0]) + _dot(s_ref[...], yi_ref[0])
    out = gate_ref[0] * (y + bias_ref[...] * u_ref[0])
    if final:
        out = _rms(out, g_ref[...])
    o_ref[0] = out.astype(o_ref.dtype)


def _dft_inv(yr, yi, cf, s_inv, u3, u_blk, gate3, gate_blk, bias, g, final):
    bsz, length, _ = yr.shape
    tm = min(512, length)
    return pl.pallas_call(
        functools.partial(_dft_inv_kernel, final=final), grid=(bsz, length // tm),
        in_specs=[pl.BlockSpec((1, length, HY_W), lambda b, i: (b, 0, 0)),
                  pl.BlockSpec((1, length, HY_W), lambda b, i: (b, 0, 0)),
                  pl.BlockSpec((tm, length), lambda b, i: (i, 0)),
                  pl.BlockSpec((tm, length), lambda b, i: (i, 0)),
                  pl.BlockSpec((1, tm, HY_W), lambda b, i: (b, i, u_blk)),
                  pl.BlockSpec((1, tm, HY_W), lambda b, i: (b, i, gate_blk)),
                  pl.BlockSpec((1, HY_W), lambda b, i: (0, 0)),
                  pl.BlockSpec((1, HY_W), lambda b, i: (0, 0))],
        out_specs=pl.BlockSpec((1, tm, HY_W), lambda b, i: (b, i, 0)),
        out_shape=jax.ShapeDtypeStruct((bsz, length, HY_W), BF16 if final else F32),
        compiler_params=_params(("parallel", "arbitrary")), name="hyena_dft_inv",
    )(yr, yi, cf, s_inv, u3, gate3, bias.reshape(1, HY_W), g.reshape(1, HY_W))


def _hyena_tables(length):
    cf, sf = _dft_tables(length, 2 * length)
    alt = jnp.where(jnp.arange(length) % 2 == 0, 1.0, -1.0).astype(F32)
    sf = sf.at[0, :].set(alt)
    return cf.astype(BF16), sf.astype(BF16), sf.T.astype(BF16)


def _hyena_coefs(length, tables, hy_params):
    cf, sf, _ = tables
    taps, nrm = _hyena_filter_taps(length, *hy_params)
    a, b = _dft_fwd(taps, 0, cf, sf)
    inv = (1.0 / nrm).reshape(HY_ORDER, 1, HY_W)
    n = 2.0 * length
    hr = a[:HY_ORDER] * inv
    hi = -b[HY_ORDER:] * inv
    nyq = b[:HY_ORDER, 0:1, :] * inv
    first = (jnp.arange(length) == 0)[None, :, None]
    c1 = jnp.where(first, hr / n, hr * (2.0 / n))
    c2 = jnp.where(first, 0.0, hi * (2.0 / n))
    c4 = jnp.where(first, nyq / n, hr * (2.0 / n))
    return c1, c2, c4


def _hyena(convh, tables, coefs, hy_bias, hy_g):
    cf, sf, s_inv = tables
    c1, c2, c4 = coefs
    yr, yi = _dft_fwd(convh, 0, cf, sf, (c1[0], c2[0], c4[0]))
    zz = _dft_inv(yr, yi, cf, s_inv, convh, 0, convh, 1, hy_bias[0], hy_g, False)
    yr, yi = _dft_fwd(zz, 0, cf, sf, (c1[1], c2[1], c4[1]))
    return _dft_inv(yr, yi, cf, s_inv, zz, 0, convh, 2, hy_bias[1], hy_g, True)


HY_FAST = 128


def _hy2_tables(length):
    nk = length // HY_FAST
    n2 = 2 * length
    k1 = lax.broadcasted_iota(jnp.int32, (nk, nk), 0)
    s1 = lax.broadcasted_iota(jnp.int32, (nk, nk), 1)
    ang = (((2 * k1 + 1) * s1) % (4 * nk)).astype(F32) * (2.0 * math.pi / (4 * nk))
    f1 = jnp.concatenate([jnp.cos(ang), -jnp.sin(ang)], axis=0)
    shape = (nk, HY_FAST, HY_FAST)
    kk = lax.broadcasted_iota(jnp.int32, shape, 0) + 2 * nk * lax.broadcasted_iota(jnp.int32, shape, 1)
    s2 = lax.broadcasted_iota(jnp.int32, shape, 2)
    phi = (((2 * kk + 1) * s2) % (2 * n2)).astype(F32) * (2.0 * math.pi / (2 * n2))
    c, s = jnp.cos(phi), jnp.sin(phi)
    f2 = jnp.concatenate([jnp.concatenate([c, s], axis=2), jnp.concatenate([-s, c], axis=2)], axis=1)
    f1k = jnp.kron(f1, jnp.eye(HY_SUB, dtype=F32))
    return f1k.astype(BF16), f2.astype(BF16), jnp.swapaxes(f2, 1, 2).astype(BF16), f1k.T.astype(BF16)


HY_SUB = 8


def _hy2_stage1_kernel(x_ref, f1_ref, z_ref):
    nk = x_ref.shape[2]
    tc = x_ref.shape[4]
    for g in range(HY_FAST // HY_SUB):
        rows = slice(g * HY_SUB, (g + 1) * HY_SUB)
        rhs = x_ref[0, 0, :, rows, :].reshape(nk * HY_SUB, tc).astype(BF16)
        z = _dot(f1_ref[...], rhs).astype(BF16)
        z_ref[0, :, :, rows, :] = z.reshape(2, nk, HY_SUB, tc)


def _hy2_stage1(x5, which, f1k, cw=None):
    _, nb, nk, _, call = x5.shape
    cw = call if cw is None else cw
    tc = min(256, cw)
    return pl.pallas_call(
        _hy2_stage1_kernel, grid=(nb, cw // tc),
        in_specs=[pl.BlockSpec((1, 1, nk, HY_FAST, tc), lambda b, j: (which, b, 0, 0, j)),
                  pl.BlockSpec(f1k.shape, lambda b, j: (0, 0))],
        out_specs=pl.BlockSpec((1, 2, nk, HY_FAST, tc), lambda b, j: (b, 0, 0, 0, j)),
        out_shape=jax.ShapeDtypeStruct((nb, 2, nk, HY_FAST, cw), BF16),
        compiler_params=_params(("parallel", "parallel")), name="hyena2_stage1",
    )(x5, f1k)


def _hy2_stage2_kernel(zr_ref, zi_ref, f2_ref, *rest, filtered):
    if filtered:
        g2_ref, hr_ref, hi_ref, vr_ref, vi_ref = rest
    else:
        vr_ref, vi_ref = rest
    for k in range(zr_ref.shape[2]):
        x = _dot(f2_ref[k], jnp.concatenate([zr_ref[0, 0, k], zi_ref[0, 0, k]], axis=0))
        xr, xi = x[:HY_FAST], x[HY_FAST:]
        if filtered:
            hr, hi = hr_ref[k], hi_ref[k]
            y = jnp.concatenate([xr * hr - xi * hi, xr * hi + xi * hr], axis=0).astype(BF16)
            v = _dot(g2_ref[k], y)
            vr_ref[0, 0, k] = v[:HY_FAST].astype(BF16)
            vi_ref[0, 0, k] = v[HY_FAST:].astype(BF16)
        else:
            vr_ref[0, 0, k] = xr
            vi_ref[0, 0, k] = xi


def _hy2_stage2(z, f2, g2=None, hr=None, hi=None):
    nb, _, nk, _, cw = z.shape
    kb = 8
    filtered = hr is not None
    zspec = lambda part: pl.BlockSpec((1, 1, kb, HY_FAST, cw), lambda k, b: (b, part, k, 0, 0))
    mat = pl.BlockSpec((kb, 2 * HY_FAST, 2 * HY_FAST), lambda k, b: (k, 0, 0))
    in_specs = [zspec(0), zspec(1), mat]
    args = [z, z, f2]
    if filtered:
        in_specs += [mat, pl.BlockSpec((kb, HY_FAST, cw), lambda k, b: (k, 0, 0)),
                     pl.BlockSpec((kb, HY_FAST, cw), lambda k, b: (k, 0, 0))]
        args += [g2, hr, hi]
    return pl.pallas_call(
        functools.partial(_hy2_stage2_kernel, filtered=filtered), grid=(nk // kb, nb),
        in_specs=in_specs, out_specs=[zspec(0), zspec(0)],
        out_shape=[jax.ShapeDtypeStruct((nb, 1, nk, HY_FAST, cw), BF16 if filtered else F32)] * 2,
        compiler_params=_params(("parallel", "parallel")),
        name="hyena2_stage2" if filtered else "hyena2_filter_spectrum",
    )(*args)


def _hy2_inv1_kernel(vr_ref, vi_ref, g1_ref, u_ref, gate_ref, bias_ref, g_ref, o_ref, *, final):
    nk = vr_ref.shape[2]
    cw = vr_ref.shape[4]
    for g in range(vr_ref.shape[3] // HY_SUB):
        rows = slice(g * HY_SUB, (g + 1) * HY_SUB)
        rhs = jnp.concatenate([vr_ref[0, 0, :, rows, :].reshape(nk * HY_SUB, cw),
                               vi_ref[0, 0, :, rows, :].reshape(nk * HY_SUB, cw)], axis=0)
        y = _dot(g1_ref[...], rhs).reshape(nk, HY_SUB, cw)
        out = gate_ref[0, 0, :, rows, :] * (y + bias_ref[...] * u_ref[0, 0, :, rows, :])
        if final:
            out = _rms(out, g_ref[...])
        o_ref[0, :, rows, :] = out.astype(o_ref.dtype)


def _hy2_inv1(vr, vi, g1k, u5, which_u, gate5, which_g, bias, g, final):
    nb, _, nk, _, cw = vr.shape
    tg = 32
    vspec = pl.BlockSpec((1, 1, nk, tg, cw), lambda b, j: (b, 0, 0, j, 0))
    return pl.pallas_call(
        functools.partial(_hy2_inv1_kernel, final=final), grid=(nb, HY_FAST // tg),
        in_specs=[vspec, vspec,
                  pl.BlockSpec(g1k.shape, lambda b, j: (0, 0)),
                  pl.BlockSpec((1, 1, nk, tg, cw), lambda b, j: (which_u, b, 0, j, 0)),
                  pl.BlockSpec((1, 1, nk, tg, cw), lambda b, j: (which_g, b, 0, j, 0)),
                  pl.BlockSpec((1, cw), lambda b, j: (0, 0)),
                  pl.BlockSpec((1, cw), lambda b, j: (0, 0))],
        out_specs=pl.BlockSpec((1, nk, tg, cw), lambda b, j: (b, 0, j, 0)),
        out_shape=jax.ShapeDtypeStruct((nb, nk, HY_FAST, cw), BF16 if final else F32),
        compiler_params=_params(("parallel", "parallel")), name="hyena2_inv1",
    )(vr, vi, g1k, u5, gate5, bias.reshape(1, cw), g.reshape(1, cw))


def _hy2_coefs(length, tables, hy_params):
    f1k, f2, _, _ = tables
    nk = length // HY_FAST
    taps, nrm = _hyena_filter_taps(length, *hy_params)
    z = _hy2_stage1(taps.reshape(1, 2 * HY_ORDER, nk, HY_FAST, HY_W), 0, f1k)
    xr, xi = _hy2_stage2(z, f2)
    scale = (1.0 / length) / nrm.reshape(HY_ORDER, 1, 1, HY_W)
    return xr[:HY_ORDER, 0] * scale, xi[HY_ORDER:, 0] * scale


def _hyena2(conv3, tables, coefs, hy_bias, hy_g):
    f1k, f2, g2, g1k = tables
    hr, hi = coefs
    _, bsz, length, cw = conv3.shape
    nk = length // HY_FAST
    c5 = conv3.reshape(3, bsz, nk, HY_FAST, cw)

    def conv(x5, which, order, gate_idx, final):
        vr, vi = _hy2_stage2(_hy2_stage1(x5, which, f1k), f2, g2, hr[order], hi[order])
        return _hy2_inv1(vr, vi, g1k, x5, which, c5, gate_idx, hy_bias[order], hy_g, final)

    zz = conv(c5, 0, 0, 1, False)
    return conv(zz[None], 0, 1, 2, True).reshape(bsz, length, cw)


def _four2_tables(length):
    nk = length // HY_FAST
    k1 = lax.broadcasted_iota(jnp.int32, (nk, nk), 0)
    s1 = lax.broadcasted_iota(jnp.int32, (nk, nk), 1)
    ang = ((k1 * s1) % nk).astype(F32) * (2.0 * math.pi / nk)
    f1 = jnp.concatenate([jnp.cos(ang), -jnp.sin(ang)], axis=0)
    shape = (nk, HY_FAST, HY_FAST)
    kk = lax.broadcasted_iota(jnp.int32, shape, 0) + nk * lax.broadcasted_iota(jnp.int32, shape, 1)
    s2 = lax.broadcasted_iota(jnp.int32, shape, 2)
    phi = ((kk * s2) % length).astype(F32) * (2.0 * math.pi / length)
    c, s = jnp.cos(phi), jnp.sin(phi)
    f2 = jnp.concatenate([jnp.concatenate([c, s], axis=2), jnp.concatenate([-s, c], axis=2)], axis=1)
    cc, sc = _cos_sin_table(FOUR_GW, FOUR_GW, FOUR_GW)
    eye = jnp.eye(FOUR_GROUPS, dtype=F32)
    return (jnp.kron(f1, jnp.eye(HY_SUB, dtype=F32)).astype(BF16), f2.astype(BF16),
            jnp.kron(eye, cc).astype(BF16), jnp.kron(eye, sc).astype(BF16))


def _four2_stage2_kernel(zr_ref, zi_ref, f2_ref, bc_ref, bs_ref, g_ref, o_ref, *, scale):
    for k in range(zr_ref.shape[2]):
        x = _dot(f2_ref[k], jnp.concatenate([zr_ref[0, 0, k], zi_ref[0, 0, k]], axis=0))
        xr, xi = x[:HY_FAST].astype(BF16), x[HY_FAST:].astype(BF16)
        z = (_dot(xr, bc_ref[...]) + _dot(xi, bs_ref[...])) * scale
        o_ref[0, k] = _rms(z, g_ref[...]).astype(o_ref.dtype)


def _fourier2(p3, tables, four_g):
    f1k, f2, bc, bs = tables
    bsz, length, pw = p3.shape
    nk = length // HY_FAST
    z = _hy2_stage1(p3.reshape(1, bsz, nk, HY_FAST, pw), 0, f1k, cw=FOUR_W)
    kb = 8
    zspec = lambda part: pl.BlockSpec((1, 1, kb, HY_FAST, FOUR_W), lambda k, b: (b, part, k, 0, 0))
    const = lambda shape: pl.BlockSpec(shape, lambda k, b: (0,) * len(shape))
    out = pl.pallas_call(
        functools.partial(_four2_stage2_kernel, scale=1.0 / math.sqrt(length * FOUR_GW)),
        grid=(nk // kb, bsz),
        in_specs=[zspec(0), zspec(1),
                  pl.BlockSpec((kb, 2 * HY_FAST, 2 * HY_FAST), lambda k, b: (k, 0, 0)),
                  const((FOUR_W, FOUR_W)), const((FOUR_W, FOUR_W)), const((1, FOUR_W))],
        out_specs=pl.BlockSpec((1, kb, HY_FAST, FOUR_W), lambda k, b: (b, k, 0, 0)),
        out_shape=jax.ShapeDtypeStruct((bsz, nk, HY_FAST, FOUR_W), BF16),
        compiler_params=_params(("parallel", "parallel")), name="fourier2_stage2",
    )(z, z, f2, bc, bs, four_g.reshape(1, FOUR_W))
    return jnp.swapaxes(out, 1, 2).reshape(bsz, length, FOUR_W)


def _softplus(v):
    return jnp.maximum(v, 0.0) + jnp.log(1.0 + jnp.exp(-jnp.abs(v)))


def _delta_prep_kernel(qkv_ref, ab_ref, abt_ref, arow_ref, drow_ref, acol_ref, dcol_ref,
                       wq_ref, u_ref, ak_ref, egl_ref):
    c = ab_ref.shape[1]
    nh = DN_HEADS
    ri = lax.broadcasted_iota(jnp.int32, (c, c), 0)
    ci = lax.broadcasted_iota(jnp.int32, (c, c), 1)
    incl = (ri >= ci, ri <= ci)
    strict = (ri > ci, ri < ci)
    low = jnp.where(incl[0], 1.0, 0.0)
    upp = jnp.where(incl[1], 1.0, 0.0)
    hdot = lambda a, b: jnp.dot(a, b, precision=HIGHEST, preferred_element_type=F32)
    ab = ab_ref[0]
    abt = abt_ref[0, 0]
    gate = -arow_ref[...] * _softplus(ab + drow_ref[...])
    gate_t = -acol_ref[...] * _softplus(abt + dcol_ref[...])
    beta_all = jax.nn.sigmoid(ab)
    gcum = (hdot(low, gate), hdot(upp, gate))
    gcum_t = (hdot(gate_t, upp), hdot(gate_t, low))
    nt = (((1,), (1,)), ((), ()))
    def l2n(t):
        return t * lax.rsqrt(jnp.sum(t * t, axis=-1, keepdims=True) + EPS)

    q = [l2n(_silu(qkv_ref[0, :, h * DN_DK:(h + 1) * DN_DK])) * (DN_DK ** -0.5) for h in range(nh)]
    k = [l2n(_silu(qkv_ref[0, :, DN_W + h * DN_DK:DN_W + (h + 1) * DN_DK])) for h in range(nh)]
    v = [_silu(qkv_ref[0, :, 2 * DN_W + h * DN_DV:2 * DN_W + (h + 1) * DN_DV]) for h in range(nh)]
    k16 = [t.astype(BF16) for t in k]
    kk0 = [lax.dot_general(k16[h], k16[h], nt, preferred_element_type=F32) for h in range(nh)]
    qk0 = [lax.dot_general(q[h].astype(BF16), k16[h], nt, preferred_element_type=F32) for h in range(nh)]
    units = [(d, h) for d in range(2) for h in range(nh)]
    mm, rr = {}, {}
    for d, h in units:
        col = d * nh + h
        gc = gcum[d][:, col:col + 1]
        gr = gcum_t[d][col:col + 1, :]
        be = beta_all[:, 2 * nh + col:2 * nh + col + 1]
        last = 0 if d else c - 1
        gl = gc[last:last + 1, :]
        dec = jnp.where(incl[d], jnp.exp(jnp.where(incl[d], gc - gr, 0.0)), 0.0)
        mm[d, h] = jnp.where(strict[d], be * kk0[h] * dec, 0.0)
        ak_ref[0, 0, d, h, 0:c, :] = jnp.where(incl[d], qk0[h] * dec, 0.0).astype(BF16)
        eg = jnp.exp(gc)
        rr[d, h] = jnp.concatenate([v[h] * be, k[h] * (be * eg)], axis=1)
        wq_ref[0, 0, d, h, c:2 * c, :] = (q[h] * eg).astype(BF16)
        k_tail = k[h] * jnp.exp(gl - gc)
        ak_ref[0, 0, d, h, c:c + DN_DK, :] = jnp.transpose(k_tail).astype(BF16)
        egl_ref[0, 0, d, h:h + 1, :] = jnp.broadcast_to(jnp.exp(gl), (1, DN_DV))
    sb = DN_SOLVE_BLOCK
    in_sb = (ri // sb) == (ci // sb)
    in_2sb = (ri // (2 * sb)) == (ci // (2 * sb))
    eye = jnp.where(ri == ci, 1.0, 0.0)
    b16 = lambda t: t.astype(BF16)
    nj = {u: jnp.where(in_sb, -mm[u], 0.0) for u in units}
    inv = {u: eye + nj[u] for u in units}
    for j in range((sb - 1).bit_length() - 1):
        nj = {u: _dot(b16(nj[u]), b16(nj[u])) for u in units}
        inv = {u: inv[u] + _dot(b16(inv[u]), b16(nj[u])) for u in units}
    off = {u: b16(jnp.where(in_2sb & ~in_sb, mm[u], 0.0)) for u in units}
    tmp = {u: _dot(off[u], b16(inv[u])) for u in units}
    inv = {u: inv[u] - _dot(b16(inv[u]), b16(tmp[u])) for u in units}
    inv16 = {u: b16(inv[u]) for u in units}
    off = {u: b16(jnp.where(in_2sb, 0.0, mm[u])) for u in units}
    part = {u: _dot(inv16[u], b16(rr[u])) for u in units}
    tmp = {u: _dot(off[u], b16(part[u])) for u in units}
    rr = {u: part[u] - _dot(inv16[u], b16(tmp[u])) for u in units}
    for d, h in units:
        u_ref[0, 0, d, h] = rr[d, h][:, :DN_DV]
        wq_ref[0, 0, d, h, 0:c, :] = rr[d, h][:, DN_DV:].astype(BF16)


def _delta_scan_kernel(wqf_ref, uf_ref, akf_ref, egf_ref, wqb_ref, ub_ref, akb_ref, egb_ref, s0f_ref, s0b_ref,
                       of_ref, ob_ref, sf_ref, sb_ref, s_scr):
    i = pl.program_id(1)
    bpb = uf_ref.shape[0]
    c = uf_ref.shape[-2]

    @pl.when(i == 0)
    def _():
        s_scr[0] = s0f_ref[...]
        s_scr[1] = s0b_ref[...]

    refs = ((wqf_ref, uf_ref, akf_ref, egf_ref, of_ref), (wqb_ref, ub_ref, akb_ref, egb_ref, ob_ref))
    units = [(d, bb, h) for d in range(2) for bb in range(bpb) for h in range(DN_HEADS)]
    s = {u: s_scr[u] for u in units}
    ws = {(d, bb, h): _dot(refs[d][0][bb, 0, 0, h], s[d, bb, h].astype(BF16)) for d, bb, h in units}
    v16 = {(d, bb, h): (refs[d][1][bb, 0, 0, h] - ws[d, bb, h][:c]).astype(BF16) for d, bb, h in units}
    av = {(d, bb, h): _dot(refs[d][2][bb, 0, 0, h], v16[d, bb, h]) for d, bb, h in units}
    for d, bb, h in units:
        u = (d, bb, h)
        refs[d][4][bb, :, h * DN_DV:(h + 1) * DN_DV] = (ws[u][c:] + av[u][:c]).astype(BF16)
        s_scr[u] = s[u] * refs[d][3][bb, 0, 0, h:h + 1, :] + av[u][c:]

    @pl.when(i == pl.num_programs(1) - 1)
    def _():
        sf_ref[...] = s_scr[0]
        sb_ref[...] = s_scr[1]


def _delta_rule(qkv, ab, a_log, dt_bias, s0_f, s0_b):
    bsz, length, _ = qkv.shape
    bpb = 4 if bsz % 4 == 0 else 1
    c = DN_CHUNK
    n = length // c
    nh = DN_HEADS
    abt = jnp.swapaxes(ab.reshape(bsz, n, c, 4 * nh), 2, 3)
    zeros = jnp.zeros((2 * nh,), F32)
    a_vec = jnp.concatenate([jnp.exp(a_log.astype(F32)).reshape(-1), zeros])
    d_vec = jnp.concatenate([dt_bias.astype(F32).reshape(-1), zeros])
    small = lambda shape: pl.BlockSpec(shape, lambda b, i: (0,) * len(shape))
    per_chunk = lambda *tail: pl.BlockSpec((1, 1, 2, nh) + tail, lambda b, i: (b, i, 0, 0) + (0,) * len(tail))
    wq, u, ak, egl = pl.pallas_call(
        _delta_prep_kernel, grid=(bsz, n),
        in_specs=[pl.BlockSpec((1, c, 3 * DN_W), lambda b, i: (b, i, 0)),
                  pl.BlockSpec((1, c, 4 * nh), lambda b, i: (b, i, 0)),
                  pl.BlockSpec((1, 1, 4 * nh, c), lambda b, i: (b, i, 0, 0)),
                  small((1, 4 * nh)), small((1, 4 * nh)), small((4 * nh, 1)), small((4 * nh, 1))],
        out_specs=[per_chunk(2 * c, DN_DK), per_chunk(c, DN_DV), per_chunk(c + DN_DK, c),
                   pl.BlockSpec((1, 1, 2, nh, DN_DV), lambda b, i: (b, i, 0, 0, 0))],
        out_shape=[jax.ShapeDtypeStruct((bsz, n, 2, nh, 2 * c, DN_DK), BF16),
                   jax.ShapeDtypeStruct((bsz, n, 2, nh, c, DN_DV), F32),
                   jax.ShapeDtypeStruct((bsz, n, 2, nh, c + DN_DK, c), BF16),
                   jax.ShapeDtypeStruct((bsz, n, 2, nh, DN_DV), F32)],
        compiler_params=_params(("parallel", "parallel")), name="delta_prep",
    )(qkv, ab, abt, a_vec.reshape(1, -1), d_vec.reshape(1, -1), a_vec.reshape(-1, 1), d_vec.reshape(-1, 1))

    def side(d, *tail):
        idx = (lambda b, i: (b, n - 1 - i, 1, 0) + (0,) * len(tail)) if d else (
            lambda b, i: (b, i, 0, 0) + (0,) * len(tail))
        return pl.BlockSpec((bpb, 1, 1, nh) + tail, idx)

    def side_specs(d):
        return [side(d, 2 * c, DN_DK), side(d, c, DN_DV), side(d, c + DN_DK, c),
                pl.BlockSpec((bpb, 1, 1, nh, DN_DV),
                             (lambda b, i: (b, n - 1 - i, 1, 0, 0)) if d else (lambda b, i: (b, i, 0, 0, 0)))]

    st_spec = pl.BlockSpec((bpb, nh, DN_DK, DN_DV), lambda b, i: (b, 0, 0, 0))
    o_shape = jax.ShapeDtypeStruct((bsz, length, DN_W), BF16)
    s_shape = jax.ShapeDtypeStruct((bsz, nh, DN_DK, DN_DV), F32)
    return pl.pallas_call(
        _delta_scan_kernel, grid=(bsz // bpb, n),
        in_specs=side_specs(0) + side_specs(1) + [st_spec, st_spec],
        out_specs=[pl.BlockSpec((bpb, c, DN_W), lambda b, i: (b, i, 0)),
                   pl.BlockSpec((bpb, c, DN_W), lambda b, i: (b, n - 1 - i, 0)), st_spec, st_spec],
        out_shape=[o_shape, o_shape, s_shape, s_shape],
        scratch_shapes=[pltpu.VMEM((2, bpb, nh, DN_DK, DN_DV), F32)],
        compiler_params=_params(("parallel", "arbitrary")), name="delta_scan",
    )(wq, u, ak, egl, wq, u, ak, egl, s0_f, s0_b)


def _proj_out_kernel(yf_ref, yh_ref, of_ref, ob_ref, z_ref, dg_ref, w_ref, res_ref, gate_ref, o_ref, a_scr):
    @pl.when(pl.program_id(1) == 0)
    def _():
        a_scr[:, 0:FOUR_W] = yf_ref[...]
        a_scr[:, FOUR_W:FOUR_W + HY_W] = yh_ref[...]
        o = of_ref[...].astype(F32) + ob_ref[...].astype(F32)
        z = z_ref[...].astype(F32)
        for h in range(DN_HEADS):
            lanes = slice(h * DN_DV, (h + 1) * DN_DV)
            y = _rms(o[:, lanes], dg_ref[...]) * _silu(z[:, lanes])
            a_scr[:, FOUR_W + HY_W + h * DN_DV:FOUR_W + HY_W + (h + 1) * DN_DV] = y.astype(BF16)

    o_ref[...] = res_ref[...] + gate_ref[0] * _dot(a_scr[...], w_ref[...])


def _proj_out(y_four, y_hy, o_f, o_b, p2d, dn_g, w, res, gate, rows_per_mod):
    m, d = res.shape
    tm = min(512, m)
    tn = d
    per = rows_per_mod // tm
    zb = PCOL_Z // DN_W
    return pl.pallas_call(
        _proj_out_kernel, grid=(m // tm, d // tn),
        in_specs=[pl.BlockSpec((tm, FOUR_W), lambda i, j: (i, 0)),
                  pl.BlockSpec((tm, HY_W), lambda i, j: (i, 0)),
                  pl.BlockSpec((tm, DN_W), lambda i, j: (i, 0)),
                  pl.BlockSpec((tm, DN_W), lambda i, j: (i, 0)),
                  pl.BlockSpec((tm, DN_W), lambda i, j: (i, zb)),
                  pl.BlockSpec((1, DN_DV), lambda i, j: (0, 0)),
                  pl.BlockSpec((w.shape[0], tn), lambda i, j: (0, j)),
                  pl.BlockSpec((tm, tn), lambda i, j: (i, j)),
                  pl.BlockSpec((1, 1, tn), lambda i, j: (i // per, 0, j))],
        out_specs=pl.BlockSpec((tm, tn), lambda i, j: (i, j)),
        out_shape=jax.ShapeDtypeStruct((m, d), F32),
        scratch_shapes=[pltpu.VMEM((tm, w.shape[0]), BF16)],
        compiler_params=_params(("parallel", "arbitrary")), name="proj_out",
    )(y_four, y_hy, o_f, o_b, p2d, dn_g.reshape(1, DN_DV), w, res, gate)


def _router_kernel(x_ref, c_ref, g_ref, sx_ref, cx_ref, sc_ref, cc_ref, w_ref, b_ref, h_ref, lg_ref, *, n_x):
    i = pl.program_id(0)

    def emit(v, shift, scale):
        h = _rms(v, g_ref[...]) * (1.0 + scale) + shift
        h_ref[...] = h
        lg_ref[...] = jnp.dot(h, w_ref[...], precision=HIGHEST, preferred_element_type=F32) + b_ref[...]

    @pl.when(i < n_x)
    def _():
        emit(x_ref[...], sx_ref[0], cx_ref[0])

    @pl.when(i >= n_x)
    def _():
        emit(c_ref[...], sc_ref[0], cc_ref[0])


def _router(x2d, c2d, g, shift_x, scale_x, shift_c, scale_c, w_r, b_r, rows_per_mod, with_ctx):
    mx, d = x2d.shape
    tm = 512
    n_x = mx // tm
    n_c = c2d.shape[0] // tm if with_ctx else 0
    per = rows_per_mod // tm
    xi = lambda i: (jnp.minimum(i, n_x - 1), 0)
    ci = lambda i: (jnp.maximum(i - n_x, 0), 0)
    return pl.pallas_call(
        functools.partial(_router_kernel, n_x=n_x), grid=(n_x + n_c,),
        in_specs=[pl.BlockSpec((tm, d), xi), pl.BlockSpec((tm, d), ci),
                  pl.BlockSpec((1, d), lambda i: (0, 0)),
                  pl.BlockSpec((1, 1, d), lambda i: (jnp.minimum(i, n_x - 1) // per, 0, 0)),
                  pl.BlockSpec((1, 1, d), lambda i: (jnp.minimum(i, n_x - 1) // per, 0, 0)),
                  pl.BlockSpec((1, 1, d), lambda i: (0, 0, 0)),
                  pl.BlockSpec((1, 1, d), lambda i: (0, 0, 0)),
                  pl.BlockSpec((d, ROUTER_W), lambda i: (0, 0)),
                  pl.BlockSpec((1, ROUTER_W), lambda i: (0, 0))],
        out_specs=[pl.BlockSpec((tm, d), lambda i: (i, 0)),
                   pl.BlockSpec((tm, ROUTER_W), lambda i: (i, 0))],
        out_shape=[jax.ShapeDtypeStruct(((n_x + n_c) * tm, d), F32),
                   jax.ShapeDtypeStruct(((n_x + n_c) * tm, ROUTER_W), F32)],
        compiler_params=_params(("parallel",)), name="moe_router",
    )(x2d, c2d, g.reshape(1, d), shift_x, scale_x, shift_c, scale_c, w_r, b_r)


def _row_copy(src_hbm, row, dst, slot, r, sem):
    return pltpu.make_async_copy(src_hbm.at[pl.ds(row, 1)], dst.at[slot, pl.ds(r, 1)], sem.at[slot])


def _expert_kernel(be_ref, tok_ref, nb_ref, h_hbm, wg_ref, wu_ref, wd_ref, ys_ref, xbuf, sem, *, bm):
    i = pl.program_id(0)
    nb = nb_ref[0]

    def start(blk, slot):
        def body(r, carry):
            _row_copy(h_hbm, tok_ref[blk * bm + r], xbuf, slot, r, sem).start()
            return carry
        lax.fori_loop(0, bm, body, 0, unroll=8)

    def wait(slot):
        pltpu.make_async_copy(h_hbm.at[pl.ds(0, bm)], xbuf.at[slot], sem.at[slot]).wait()

    @pl.when(i == 0)
    def _():
        start(0, 0)

    @pl.when(i + 1 < nb)
    def _():
        start(i + 1, (i + 1) % 2)

    @pl.when(i < nb)
    def _():
        slot = i % 2
        wait(slot)
        x = xbuf[slot].astype(BF16)
        act = (_silu(_dot(x, wg_ref[0, 0])) * _dot(x, wu_ref[0, 0])).astype(BF16)
        ys_ref[...] = _dot(act, wd_ref[0, 0])

    @pl.when(i >= nb)
    def _():
        ys_ref[...] = jnp.zeros_like(ys_ref)


def _experts(h, buf_tok, block_e, n_used, wg, wu, wd, layer, bm):
    n_blocks = block_e.shape[0]
    d = h.shape[1]
    ff = wg.shape[3]
    grid_spec = pltpu.PrefetchScalarGridSpec(
        num_scalar_prefetch=3, grid=(n_blocks,),
        in_specs=[pl.BlockSpec(memory_space=pl.ANY),
                  pl.BlockSpec((1, 1, d, ff), lambda i, be, tok, nb: (layer, be[i], 0, 0)),
                  pl.BlockSpec((1, 1, d, ff), lambda i, be, tok, nb: (layer, be[i], 0, 0)),
                  pl.BlockSpec((1, 1, ff, d), lambda i, be, tok, nb: (layer, be[i], 0, 0))],
        out_specs=pl.BlockSpec((bm, d), lambda i, be, tok, nb: (i, 0)),
        scratch_shapes=[pltpu.VMEM((2, bm, d), F32), pltpu.SemaphoreType.DMA((2,))])
    return pl.pallas_call(
        functools.partial(_expert_kernel, bm=bm), grid_spec=grid_spec,
        out_shape=jax.ShapeDtypeStruct((n_blocks * bm, d), F32),
        compiler_params=_params(("arbitrary",)), name="moe_experts",
    )(block_e, buf_tok, n_used, h, wg, wu, wd)


def _combine_kernel(pos_ref, ys_hbm, x_ref, gw_ref, gate_ref, fg_ref, o_ref, ybuf, sem, *, tm, tok0, final):
    i = pl.program_id(0)
    n = pl.num_programs(0)

    def start(blk, slot):
        def body(r, carry):
            t = tok0 + blk * tm + r
            _row_copy(ys_hbm, pos_ref[2 * t], ybuf, slot, r, sem).start()
            _row_copy(ys_hbm, pos_ref[2 * t + 1], ybuf, slot, tm + r, sem).start()
            return carry
        lax.fori_loop(0, tm, body, 0, unroll=8)

    @pl.when(i == 0)
    def _():
        start(0, 0)

    @pl.when(i + 1 < n)
    def _():
        start(i + 1, (i + 1) % 2)

    slot = i % 2

    pltpu.make_async_copy(ys_hbm.at[pl.ds(0, 2 * tm)], ybuf.at[slot], sem.at[slot]).wait()
    gw = gw_ref[...]
    y = gw[:, 0:1] * ybuf[slot, 0:tm, :] + gw[:, 1:2] * ybuf[slot, tm:2 * tm, :]
    out = x_ref[...] + gate_ref[0] * y
    if final:
        out = _rms(out, fg_ref[...])
    o_ref[...] = out


def _combine(ys, pos, gw, x2d, gate, final_g, tok0, rows_per_mod, final):
    m, d = x2d.shape
    tm = 256
    per = rows_per_mod // tm
    gb = tok0 // tm
    grid_spec = pltpu.PrefetchScalarGridSpec(
        num_scalar_prefetch=1, grid=(m // tm,),
        in_specs=[pl.BlockSpec(memory_space=pl.ANY),
                  pl.BlockSpec((tm, d), lambda i, pos: (i, 0)),
                  pl.BlockSpec((tm, TOP_K), lambda i, pos: (gb + i, 0)),
                  pl.BlockSpec((1, 1, d), lambda i, pos: (i // per, 0, 0)),
                  pl.BlockSpec((1, d), lambda i, pos: (0, 0))],
        out_specs=pl.BlockSpec((tm, d), lambda i, pos: (i, 0)),
        scratch_shapes=[pltpu.VMEM((2, 2 * tm, d), F32), pltpu.SemaphoreType.DMA((2,))])
    return pl.pallas_call(
        functools.partial(_combine_kernel, tm=tm, tok0=tok0, final=final), grid_spec=grid_spec,
        out_shape=jax.ShapeDtypeStruct((m, d), F32),
        compiler_params=_params(("arbitrary",)), name="moe_combine",
    )(pos, ys, x2d, gw, gate, final_g.reshape(1, d))


def _route(logits):
    t = logits.shape[0]
    p_grp = jax.nn.softmax(logits[:, :N_GROUPS], axis=-1)
    p_sel = jnp.max(p_grp, axis=-1, keepdims=True)
    grp = jnp.argmax(p_grp, axis=-1).astype(jnp.int32)[:, None]
    fine = logits[:, N_GROUPS:N_GROUPS + N_EXPERTS].reshape(t, N_GROUPS, EXPERTS_PER_GROUP)
    in_grp = (jnp.arange(N_GROUPS, dtype=jnp.int32)[None, :] == grp)[:, :, None]
    fine = jnp.sum(jnp.where(in_grp, fine, 0.0), axis=1)
    lanes = jnp.arange(EXPERTS_PER_GROUP, dtype=jnp.int32)[None, :]
    i1 = jnp.argmax(fine, axis=-1).astype(jnp.int32)[:, None]
    v1 = jnp.max(fine, axis=-1, keepdims=True)
    rest = jnp.where(lanes == i1, -jnp.inf, fine)
    i2 = jnp.argmax(rest, axis=-1).astype(jnp.int32)[:, None]
    v2 = jnp.max(rest, axis=-1, keepdims=True)
    top_v = jnp.concatenate([v1, v2], axis=-1)
    top_i = jnp.concatenate([i1, i2], axis=-1)
    gate = p_sel * jax.nn.softmax(top_v, axis=-1)
    return grp * EXPERTS_PER_GROUP + top_i, gate


def _rank_kernel(e_ref, rank_ref, cnt_ref, run_scr):
    @pl.when(pl.program_id(0) == 0)
    def _():
        run_scr[...] = jnp.zeros_like(run_scr)

    tb = e_ref.shape[0]
    lane = lax.broadcasted_iota(jnp.int32, (tb, LANE), 1)
    onehot = jnp.where(e_ref[...] == lane, 1.0, 0.0)
    ri = lax.broadcasted_iota(jnp.int32, (tb, tb), 0)
    ci = lax.broadcasted_iota(jnp.int32, (tb, tb), 1)
    before = jnp.where(ri > ci, 1.0, 0.0).astype(BF16)
    prefix = _dot(before, onehot.astype(BF16)) + run_scr[...]
    rank_ref[...] = jnp.sum(prefix * onehot, axis=1, keepdims=True).astype(jnp.int32)
    run_scr[...] += jnp.sum(onehot, axis=0, keepdims=True)
    cnt_ref[...] = run_scr[...]


def _expert_ranks(flat_e):
    a = flat_e.shape[0]
    tb = 512
    rank, cnt = pl.pallas_call(
        _rank_kernel, grid=(a // tb,),
        in_specs=[pl.BlockSpec((tb, 1), lambda i: (i, 0))],
        out_specs=[pl.BlockSpec((tb, 1), lambda i: (i, 0)), pl.BlockSpec((1, LANE), lambda i: (0, 0))],
        out_shape=[jax.ShapeDtypeStruct((a, 1), jnp.int32), jax.ShapeDtypeStruct((1, LANE), F32)],
        scratch_shapes=[pltpu.VMEM((1, LANE), F32)],
        compiler_params=_params(("arbitrary",)), name="moe_rank",
    )(flat_e.reshape(a, 1))
    return rank.reshape(a), cnt[0, :N_EXPERTS].astype(jnp.int32)


def _dispatch_plan(expert, bm):
    t = expert.shape[0]
    a = t * TOP_K
    flat_e = expert.reshape(a).astype(jnp.int32)
    rank, counts = _expert_ranks(flat_e)
    padded = (counts + bm - 1) // bm * bm
    pad_end = jnp.cumsum(padded)
    pad_start = pad_end - padded
    mine = flat_e[:, None] == jnp.arange(N_EXPERTS, dtype=jnp.int32)[None, :]
    pos = (jnp.sum(jnp.where(mine, pad_start[None, :], 0), axis=1) + rank).astype(jnp.int32)
    n_blocks = a // bm + N_EXPERTS
    flat_tok = jnp.arange(a, dtype=jnp.int32) // TOP_K
    buf_tok = jnp.zeros((n_blocks * bm,), jnp.int32).at[pos].set(flat_tok)
    n_used = (pad_end[-1] // bm).astype(jnp.int32)
    blk = jnp.minimum(jnp.arange(n_blocks, dtype=jnp.int32), n_used - 1) * bm
    block_e = jnp.sum((pad_end[None, :] <= blk[:, None]).astype(jnp.int32), axis=1)
    block_e = jnp.minimum(block_e, N_EXPERTS - 1).astype(jnp.int32)
    return pos, buf_tok, block_e, n_used.reshape(1)


def _mix_stream(p3, ab, rows, width, s0_f, s0_b, four_tabs, hy_tabs, hy_coefs, conv_w, conv_b, four_g,
                hy_bias, hy_g, a_log, dt_bias, with_mixers):
    qkv = _short_conv(p3, conv_w, conv_b, rows, width, OFF_DN, HY_CONV_CH, 3 * DN_W, 512)[0]
    o_f, o_b, s_f, s_b = _delta_rule(qkv, ab, a_log, dt_bias, s0_f, s0_b)
    if not with_mixers:
        return None, None, o_f, o_b, s_f, s_b
    if _hy_two_level(p3.shape[1]):
        y_four = _fourier2(p3, four_tabs, four_g)
    else:
        y_four = _fourier(p3, *four_tabs, four_g)
    if _hy_two_level(p3.shape[1]):
        conv3 = _short_conv(p3, conv_w, conv_b, rows, width, OFF_HY, 0, HY_CONV_CH, 256, split=3)
        y_hy = _hyena2(conv3, hy_tabs, hy_coefs, hy_bias, hy_g)
    else:
        convh = _short_conv(p3, conv_w, conv_b, rows, width, OFF_HY, 0, HY_CONV_CH, 256)[0]
        y_hy = _hyena(convh, hy_tabs, hy_coefs, hy_bias, hy_g)
    return y_four, y_hy, o_f, o_b, s_f, s_b


def _hy_two_level(length):
    return length % (8 * HY_FAST) == 0


def kernel(x, c, ctx, c_ctx, norm1_g, norm2_g, w_mod, b_mod, w_in, conv_w, conv_b, four_g, hy_w1, hy_b1, hy_w2, hy_b2, hy_w3, hy_b3, hy_w4, hy_freq, hy_bias, hy_g, dn_a_log, dn_dt_bias, dn_g, w_out, w_rc, b_rc, w_rf, b_rf, w_e_gate, w_e_up, w_e_down, final_g):
    bsz, length, d = x.shape
    lc = ctx.shape[1]
    depth = w_in.shape[0]
    rows = length // GRID_W
    bm = 256

    mod_all = _modulation(c, c_ctx, w_mod, b_mod)
    four_tables = lambda n: _four2_tables(n) if _hy_two_level(n) else _fourier_tables(n)
    four_x, four_c = four_tables(length), four_tables(lc)
    hy_tables = lambda n: _hy2_tables(n) if _hy_two_level(n) else _hyena_tables(n)
    hy_coefs = lambda n, tabs, prm: (_hy2_coefs if _hy_two_level(n) else _hyena_coefs)(n, tabs, prm)
    hy_x, hy_c = hy_tables(length), hy_tables(lc)
    zeros = jnp.zeros((bsz, DN_HEADS, DN_DK, DN_DV), F32)
    x2 = x.reshape(bsz * length, d)
    c2 = ctx.reshape(bsz * lc, d)
    wg16, wu16, wd16 = w_e_gate.astype(BF16), w_e_up.astype(BF16), w_e_down.astype(BF16)

    for l in range(depth):
        last = l == depth - 1
        mod = [m[:, None, :] for m in jnp.split(mod_all[l, :bsz], 6, axis=-1)]
        modc = [m[:, None, :] for m in jnp.split(mod_all[l, bsz:bsz + 1], 6, axis=-1)]
        wi = w_in[l]
        w_p = jnp.concatenate([wi[:, :OFF_AB], wi[:, OFF_Z:], wi[:, OFF_AB:OFF_Z],
                               jnp.zeros((d, PROJ_W - PCOL_AB - N_AB), wi.dtype)], axis=1).astype(BF16)
        w_o = w_out[l].astype(BF16)
        hy_params = (hy_w1[l], hy_b1[l], hy_w2[l], hy_b2[l], hy_w3[l], hy_b3[l], hy_w4[l], hy_freq[l])
        mix_args = (conv_w[l], conv_b[l], four_g[l], hy_bias[l], hy_g[l], dn_a_log[l], dn_dt_bias[l])

        pc, abc = _proj_in(c2, norm1_g[l], modc[0], modc[1], w_p, bsz * lc)
        pc3 = pc.reshape(bsz, lc, PROJ_W)
        abc = abc[:, :N_AB].reshape(bsz, lc, N_AB)
        coefs_c = None if last else hy_coefs(lc, hy_c, hy_params)
        yf, yh, o_f, o_b, s_f, s_b = _mix_stream(pc3, abc, 1, lc, zeros, zeros, four_c, hy_c, coefs_c,
                                                 *mix_args, with_mixers=not last)
        if not last:
            c2 = _proj_out(yf.reshape(-1, FOUR_W), yh.reshape(-1, HY_W), o_f.reshape(-1, DN_W),
                           o_b.reshape(-1, DN_W), pc, dn_g[l], w_o, c2, modc[2], bsz * lc)

        p, abx = _proj_in(x2, norm1_g[l], mod[0], mod[1], w_p, length)
        p3 = p.reshape(bsz, length, PROJ_W)
        abx = abx[:, :N_AB].reshape(bsz, length, N_AB)
        coefs_x = hy_coefs(length, hy_x, hy_params)
        yf, yh, o_f, o_b, _, _ = _mix_stream(p3, abx, rows, GRID_W, s_f, s_b, four_x, hy_x, coefs_x,
                                             *mix_args, with_mixers=True)
        x2 = _proj_out(yf.reshape(-1, FOUR_W), yh.reshape(-1, HY_W), o_f.reshape(-1, DN_W),
                       o_b.reshape(-1, DN_W), p, dn_g[l], w_o, x2, mod[2], length)

        w_r = jnp.concatenate([w_rc[l], w_rf[l], jnp.zeros((d, ROUTER_W - N_GROUPS - N_EXPERTS), F32)], axis=1)
        b_r = jnp.concatenate([b_rc[l], b_rf[l], jnp.zeros((ROUTER_W - N_GROUPS - N_EXPERTS,), F32)])[None, :]
        h, logits = _router(x2, c2, norm2_g[l], mod[3], mod[4], modc[3], modc[4], w_r, b_r, length,
                            with_ctx=not last)
        expert, gate = _route(logits)
        pos, buf_tok, block_e, n_used = _dispatch_plan(expert, bm)
        ys = _experts(h, buf_tok, block_e, n_used, wg16, wu16, wd16, l, bm)
        x2 = _combine(ys, pos, gate, x2, mod[5], final_g, 0, length, final=last)
        if not last:
            c2 = _combine(ys, pos, gate, c2, modc[5], final_g, bsz * length, bsz * lc, final=False)
    return x2.reshape(bsz, length, d)
```

```python
import functools
import math

import jax
import jax.numpy as jnp
from jax import lax
from jax.experimental import pallas as pl
from jax.experimental.pallas import tpu as pltpu

F32 = jnp.float32
BF16 = jnp.bfloat16
HIGHEST = lax.Precision.HIGHEST

GRID_W = 64
FOUR_W = 512
FOUR_GROUPS = 4
FOUR_GW = FOUR_W // FOUR_GROUPS
HY_W = 512
HY_ORDER = 2
DN_HEADS = 8
DN_DK = 128
DN_DV = 128
DN_W = DN_HEADS * DN_DV
DN_CHUNK = 64
DN_SOLVE_BLOCK = DN_CHUNK // 4
HY_CONV_CH = (HY_ORDER + 1) * HY_W
OFF_HY = FOUR_W
OFF_DN = OFF_HY + HY_CONV_CH
OFF_AB = OFF_DN + 3 * DN_W
N_AB = 4 * DN_HEADS
OFF_Z = OFF_AB + N_AB
HY_EMB = 33
HY_FAST_DECAY = 0.3
HY_SLOW_DECAY = 1.5
HY_TARGET = 1e-2
N_GROUPS = 4
EXPERTS_PER_GROUP = 8
N_EXPERTS = N_GROUPS * EXPERTS_PER_GROUP
TOP_K = 2
EPS = 1e-6

LANE = 128
PCOL_Z = OFF_AB
PCOL_AB = PCOL_Z + DN_W
MXU_W = 256
PROJ_TN = 5 * MXU_W
PROJ_W = -(-(PCOL_AB + LANE) // PROJ_TN) * PROJ_TN
ROUTER_W = LANE
VMEM_LIMIT = 56 * 1024 * 1024


def _params(semantics):
    return pltpu.CompilerParams(dimension_semantics=semantics, vmem_limit_bytes=VMEM_LIMIT)


def _silu(v):
    return v * jax.nn.sigmoid(v)


def _rms(v, g):
    return v * lax.rsqrt(jnp.mean(v * v, axis=-1, keepdims=True) + EPS) * g


def _dot(a, b):
    return jnp.dot(a, b, preferred_element_type=F32)


def _mod_kernel(a_ref, w_ref, b_ref, o_ref):
    o_ref[0] = jnp.dot(_silu(a_ref[...]), w_ref[0], precision=HIGHEST,
                       preferred_element_type=F32) + b_ref[0]


def _modulation(c, c_ctx, w_mod, b_mod):
    depth, d, n = w_mod.shape
    bsz = c.shape[0]
    rows = -(-(bsz + 1) // 8) * 8
    a = jnp.concatenate([c, c_ctx[None], jnp.zeros((rows - bsz - 1, d), F32)], axis=0)
    tn = 1024 if n % 1024 == 0 else 512
    assert n % tn == 0
    return pl.pallas_call(
        _mod_kernel, grid=(depth, n // tn),
        in_specs=[pl.BlockSpec((rows, d), lambda l, j: (0, 0)),
                  pl.BlockSpec((1, d, tn), lambda l, j: (l, 0, j)),
                  pl.BlockSpec((1, 1, tn), lambda l, j: (l, 0, j))],
        out_specs=pl.BlockSpec((1, rows, tn), lambda l, j: (l, 0, j)),
        out_shape=jax.ShapeDtypeStruct((depth, rows, n), F32),
        compiler_params=_params(("parallel", "parallel")), name="modulation",
    )(a, w_mod, b_mod.reshape(depth, 1, n))


def _proj_in_kernel(x_ref, g_ref, shift_ref, scale_ref, w_ref, o_ref, ab_ref, a_scr, *, ab_off):
    j = pl.program_id(1)

    @pl.when(j == 0)
    def _():
        y = _rms(x_ref[...], g_ref[...])
        a_scr[...] = (y * (1.0 + scale_ref[0]) + shift_ref[0]).astype(BF16)

    acc = _dot(a_scr[...], w_ref[...])
    o_ref[...] = acc.astype(BF16)

    @pl.when(j == pl.num_programs(1) - 1)
    def _():
        ab_ref[...] = acc[:, ab_off:ab_off + LANE]


def _proj_in(x2d, g, shift, scale, w, rows_per_mod):
    m, d = x2d.shape
    n = w.shape[1]
    tm = min(1024, m, rows_per_mod)
    tn = PROJ_TN
    assert n % tn == 0 and PCOL_AB >= n - tn
    per = rows_per_mod // tm
    return pl.pallas_call(
        functools.partial(_proj_in_kernel, ab_off=PCOL_AB - (n - tn)), grid=(m // tm, n // tn),
        in_specs=[pl.BlockSpec((tm, d), lambda i, j: (i, 0)),
                  pl.BlockSpec((1, d), lambda i, j: (0, 0)),
                  pl.BlockSpec((1, 1, d), lambda i, j: (i // per, 0, 0)),
                  pl.BlockSpec((1, 1, d), lambda i, j: (i // per, 0, 0)),
                  pl.BlockSpec((d, tn), lambda i, j: (0, j))],
        out_specs=[pl.BlockSpec((tm, tn), lambda i, j: (i, j)),
                   pl.BlockSpec((tm, LANE), lambda i, j: (i, 0))],
        out_shape=[jax.ShapeDtypeStruct((m, n), BF16), jax.ShapeDtypeStruct((m, LANE), F32)],
        scratch_shapes=[pltpu.VMEM((tm, d), BF16)],
        compiler_params=_params(("parallel", "arbitrary")), name="proj_in",
    )(x2d, g.reshape(1, d), shift, scale, w)


def _cos_sin_table(nrow, ncol, period):
    i = lax.broadcasted_iota(jnp.int32, (nrow, ncol), 0)
    j = lax.broadcasted_iota(jnp.int32, (nrow, ncol), 1)
    ang = ((i * j) % period).astype(F32) * (2.0 * math.pi / period)
    return jnp.cos(ang), jnp.sin(ang)


def _dft_tables(length, period):
    blk = 64 if length % 64 == 0 else length
    hi_i = lax.broadcasted_iota(jnp.int32, (length // blk, length), 0) * blk
    j = lax.broadcasted_iota(jnp.int32, (length // blk, length), 1)
    ang = ((hi_i * j) % period).astype(F32) * (2.0 * math.pi / period)
    hi_c, hi_s = jnp.cos(ang), jnp.sin(ang)
    lo_c, lo_s = _cos_sin_table(blk, length, period)
    c = hi_c[:, None, :] * lo_c[None, :, :] - hi_s[:, None, :] * lo_s[None, :, :]
    s = hi_s[:, None, :] * lo_c[None, :, :] + hi_c[:, None, :] * lo_s[None, :, :]
    return c.reshape(length, length), s.reshape(length, length)


def _fourier_kernel(u_ref, c_ref, s_ref, bc_ref, bs_ref, g_ref, o_ref, ub_scr, *, scale):
    @pl.when(pl.program_id(1) == 0)
    def _():
        ub_scr[...] = u_ref[0].astype(BF16)

    ub = ub_scr[...]
    p = _dot(c_ref[...], ub).astype(BF16)
    q = _dot(s_ref[...], ub).astype(BF16)
    z = (_dot(p, bc_ref[...]) - _dot(q, bs_ref[...])) * scale
    o_ref[0] = _rms(z, g_ref[...]).astype(o_ref.dtype)


def _fourier(p3, cl, sl, bc, bs, four_g):
    bsz, length, _ = p3.shape
    tm = min(512, length)
    kern = functools.partial(_fourier_kernel, scale=1.0 / math.sqrt(length * FOUR_GW))
    return pl.pallas_call(
        kern, grid=(bsz, length // tm),
        in_specs=[pl.BlockSpec((1, length, FOUR_W), lambda b, i: (b, 0, 0)),
                  pl.BlockSpec((tm, length), lambda b, i: (i, 0)),
                  pl.BlockSpec((tm, length), lambda b, i: (i, 0)),
                  pl.BlockSpec((FOUR_W, FOUR_W), lambda b, i: (0, 0)),
                  pl.BlockSpec((FOUR_W, FOUR_W), lambda b, i: (0, 0)),
                  pl.BlockSpec((1, FOUR_W), lambda b, i: (0, 0))],
        out_specs=pl.BlockSpec((1, tm, FOUR_W), lambda b, i: (b, i, 0)),
        out_shape=jax.ShapeDtypeStruct((bsz, length, FOUR_W), BF16),
        scratch_shapes=[pltpu.VMEM((length, FOUR_W), BF16)],
        compiler_params=_params(("parallel", "arbitrary")), name="fourier",
    )(p3, cl, sl, bc, bs, four_g.reshape(1, FOUR_W))


def _fourier_tables(length):
    cl, sl = _dft_tables(length, length)
    cc, sc = _cos_sin_table(FOUR_GW, FOUR_GW, FOUR_GW)
    eye = jnp.eye(FOUR_GROUPS, dtype=F32)
    return cl.astype(BF16), sl.astype(BF16), jnp.kron(eye, cc).astype(BF16), jnp.kron(eye, sc).astype(BF16)


def _conv_kernel(x_ref, w_ref, b_ref, o_ref, *, rows, width):
    tc = x_ref.shape[2]
    wts = w_ref[...]
    bias = b_ref[...]
    pos = lax.broadcasted_iota(jnp.int32, (width, tc), 0)
    not_first = pos > 0
    not_last = pos < width - 1

    def grid_row(r):
        start = r * width if isinstance(r, int) else pl.multiple_of(r * width, width)
        return x_ref[0, pl.ds(start, width), :].astype(F32)

    def body(r, carry):
        base = pl.multiple_of(r * width, width)
        up, cen = carry
        if rows > 1:
            dn = grid_row(jnp.minimum(r + 1, rows - 1))
            w_up = wts[0] * jnp.where(r > 0, 1.0, 0.0)
            w_dn = wts[2] * jnp.where(r < rows - 1, 1.0, 0.0)
            col = lambda j: up * w_up[j:j + 1, :] + cen * wts[1, j:j + 1, :] + dn * w_dn[j:j + 1, :]
        else:
            dn = cen
            col = lambda j: cen * wts[1, j:j + 1, :]
        left = jnp.where(not_first, pltpu.roll(col(0), 1, 0), 0.0)
        right = jnp.where(not_last, pltpu.roll(col(2), width - 1, 0), 0.0)
        o_ref[0, pl.ds(base, width), :] = left + col(1) + right + bias
        return cen, dn

    first = grid_row(0)
    lax.fori_loop(0, rows, body, (first, first))


def _short_conv(p3, conv_w, conv_b, rows, width, col0, ch0, nch, tc, split=1):
    bsz, length, _ = p3.shape
    kern = functools.partial(_conv_kernel, rows=rows, width=width)
    cb, wb = col0 // tc, ch0 // tc
    per = nch // split // tc
    out = pl.pallas_call(
        kern, grid=(bsz, nch // tc),
        in_specs=[pl.BlockSpec((1, length, tc), lambda b, j: (b, 0, cb + j)),
                  pl.BlockSpec((3, 3, tc), lambda b, j: (0, 0, wb + j)),
                  pl.BlockSpec((1, tc), lambda b, j: (0, wb + j))],
        out_specs=pl.BlockSpec((1, length, tc), lambda b, j: ((j // per) * bsz + b, 0, j % per)),
        out_shape=jax.ShapeDtypeStruct((split * bsz, length, nch // split), F32),
        compiler_params=_params(("parallel", "parallel")),
        name="short_conv",
    )(p3, conv_w, conv_b.reshape(1, -1))
    return out.reshape(split, bsz, length, nch // split)


def _filt_kernel(z_ref, w1_ref, b1_ref, w2_ref, b2_ref, w3_ref, b3_ref, w4_ref, fr_ref, dl_ref,
                 g_ref, nrm_ref):
    i = pl.program_id(0)
    z = z_ref[...]
    fr = fr_ref[...]
    hdot = lambda a, b: jnp.dot(a, b, precision=HIGHEST, preferred_element_type=F32)
    h = jnp.sin(fr * (hdot(z, w1_ref[...]) + b1_ref[...]))
    h = jnp.sin(fr * (hdot(h, w2_ref[...]) + b2_ref[...]))
    h = jnp.sin(fr * (hdot(h, w3_ref[...]) + b3_ref[...]))
    h = hdot(h, w4_ref[...])
    decay = jnp.exp(-z[:, 0:1] * jnp.abs(dl_ref[...]))
    decay = jnp.concatenate([decay] * HY_ORDER, axis=1)
    half = HY_ORDER * HY_W
    hf = h[:, :half] * decay
    hb = h[:, half:] * decay
    row = lax.broadcasted_iota(jnp.int32, hb.shape, 0) + i * hb.shape[0]
    hb = jnp.where(row > 0, hb, 0.0)
    gp = hf + hb
    gm = hf - hb
    for o in range(HY_ORDER):
        g_ref[o] = gp[:, o * HY_W:(o + 1) * HY_W].astype(BF16)
        g_ref[HY_ORDER + o] = gm[:, o * HY_W:(o + 1) * HY_W].astype(BF16)
    part = jnp.sum(jnp.abs(hf) + jnp.abs(hb), axis=0, keepdims=True)

    @pl.when(i == 0)
    def _():
        nrm_ref[...] = part

    @pl.when(i > 0)
    def _():
        nrm_ref[...] += part


def _hyena_filter_taps(length, w1, b1, w2, b2, w3, b3, w4, freq):
    t = jnp.linspace(0.0, 1.0, length, dtype=F32)[:, None]
    bands = (HY_EMB - 1) // 2
    ang = (2.0 * math.pi / length) * jnp.arange(length, dtype=F32)[:, None]
    f = jnp.linspace(1e-4, bands - 1, bands, dtype=F32)
    z = jnp.concatenate([t, jnp.cos(f * ang), -jnp.sin(f * ang)], axis=-1)
    max_decay = math.log(HY_TARGET) / HY_FAST_DECAY
    min_decay = math.log(HY_TARGET) / HY_SLOW_DECAY
    deltas = jnp.linspace(min_decay, max_decay, HY_W, dtype=F32)[None, :]
    fw = w1.shape[1]
    tl = min(512, length)
    full = lambda shape: pl.BlockSpec(shape, lambda i: (0,) * len(shape))
    return pl.pallas_call(
        _filt_kernel, grid=(length // tl,),
        in_specs=[pl.BlockSpec((tl, HY_EMB), lambda i: (i, 0)),
                  full((HY_EMB, fw)), full((1, fw)), full((fw, fw)), full((1, fw)),
                  full((fw, fw)), full((1, fw)), full((fw, 2 * HY_ORDER * HY_W)), full((1, fw)),
                  full((1, HY_W))],
        out_specs=[pl.BlockSpec((2 * HY_ORDER, tl, HY_W), lambda i: (0, i, 0)),
                   pl.BlockSpec((1, HY_ORDER * HY_W), lambda i: (0, 0))],
        out_shape=[jax.ShapeDtypeStruct((2 * HY_ORDER, length, HY_W), BF16),
                   jax.ShapeDtypeStruct((1, HY_ORDER * HY_W), F32)],
        compiler_params=_params(("arbitrary",)), name="hyena_filter",
    )(z, w1, b1.reshape(1, fw), w2, b2.reshape(1, fw), w3, b3.reshape(1, fw), w4,
      freq.reshape(1, fw), deltas)


def _dft_fwd_kernel(u_ref, c_ref, s_ref, *rest, with_coef):
    if with_coef:
        c1_ref, c2_ref, c4_ref, yr_ref, yi_ref, ub_scr = rest
    else:
        yr_ref, yi_ref, ub_scr = rest

    @pl.when(pl.program_id(1) == 0)
    def _():
        ub_scr[...] = u_ref[0].astype(BF16)

    ub = ub_scr[...]
    a = _dot(c_ref[...], ub)
    b = _dot(s_ref[...], ub)
    if with_coef:
        c2 = c2_ref[...]
        yr_ref[0] = (a * c1_ref[...] + b * c2).astype(BF16)
        yi_ref[0] = (b * c4_ref[...] - a * c2).astype(BF16)
    else:
        yr_ref[0] = a
        yi_ref[0] = b


def _dft_fwd(u3, col_blk, cf, sf, coef=None):
    nb, length, _ = u3.shape
    tm = min(512, length)
    in_specs = [pl.BlockSpec((1, length, HY_W), lambda b, i: (b, 0, col_blk)),
                pl.BlockSpec((tm, length), lambda b, i: (i, 0)),
                pl.BlockSpec((tm, length), lambda b, i: (i, 0))]
    args = [u3, cf, sf]
    if coef is not None:
        in_specs += [pl.BlockSpec((tm, HY_W), lambda b, i: (i, 0))] * 3
        args += list(coef)
    odt = BF16 if coef is not None else F32
    return pl.pallas_call(
        functools.partial(_dft_fwd_kernel, with_coef=coef is not None), grid=(nb, length // tm),
        in_specs=in_specs,
        out_specs=[pl.BlockSpec((1, tm, HY_W), lambda b, i: (b, i, 0))] * 2,
        out_shape=[jax.ShapeDtypeStruct((nb, length, HY_W), odt)] * 2,
        scratch_shapes=[pltpu.VMEM((length, HY_W), BF16)],
        compiler_params=_params(("parallel", "arbitrary")),
        name="hyena_dft_fwd" if coef is not None else "hyena_filter_dft",
    )(*args)


def _dft_inv_kernel(yr_ref, yi_ref, c_ref, s_ref, u_ref, gate_ref, bias_ref, g_ref, o_ref, *, final):
    y = _dot(c_ref[...], yr_ref[0]) + _dot(s_ref[...], yi_ref[0])
    out = gate_ref[0] * (y + bias_ref[...] * u_ref[0])
    if final:
        out = _rms(out, g_ref[...])
    o_ref[0] = out.astype(o_ref.dtype)


def _dft_inv(yr, yi, cf, s_inv, u3, u_blk, gate3, gate_blk, bias, g, final):
    bsz, length, _ = yr.shape
    tm = min(512, length)
    return pl.pallas_call(
        functools.partial(_dft_inv_kernel, final=final), grid=(bsz, length // tm),
        in_specs=[pl.BlockSpec((1, length, HY_W), lambda b, i: (b, 0, 0)),
                  pl.BlockSpec((1, length, HY_W), lambda b, i: (b, 0, 0)),
                  pl.BlockSpec((tm, length), lambda b, i: (i, 0)),
                  pl.BlockSpec((tm, length), lambda b, i: (i, 0)),
                  pl.BlockSpec((1, tm, HY_W), lambda b, i: (b, i, u_blk)),
                  pl.BlockSpec((1, tm, HY_W), lambda b, i: (b, i, gate_blk)),
                  pl.BlockSpec((1, HY_W), lambda b, i: (0, 0)),
                  pl.BlockSpec((1, HY_W), lambda b, i: (0, 0))],
        out_specs=pl.BlockSpec((1, tm, HY_W), lambda b, i: (b, i, 0)),
        out_shape=jax.ShapeDtypeStruct((bsz, length, HY_W), BF16 if final else F32),
        compiler_params=_params(("parallel", "arbitrary")), name="hyena_dft_inv",
    )(yr, yi, cf, s_inv, u3, gate3, bias.reshape(1, HY_W), g.reshape(1, HY_W))


def _hyena_tables(length):
    cf, sf = _dft_tables(length, 2 * length)
    alt = jnp.where(jnp.arange(length) % 2 == 0, 1.0, -1.0).astype(F32)
    sf = sf.at[0, :].set(alt)
    return cf.astype(BF16), sf.astype(BF16), sf.T.astype(BF16)


def _hyena_coefs(length, tables, hy_params):
    cf, sf, _ = tables
    taps, nrm = _hyena_filter_taps(length, *hy_params)
    a, b = _dft_fwd(taps, 0, cf, sf)
    inv = (1.0 / nrm).reshape(HY_ORDER, 1, HY_W)
    n = 2.0 * length
    hr = a[:HY_ORDER] * inv
    hi = -b[HY_ORDER:] * inv
    nyq = b[:HY_ORDER, 0:1, :] * inv
    first = (jnp.arange(length) == 0)[None, :, None]
    c1 = jnp.where(first, hr / n, hr * (2.0 / n))
    c2 = jnp.where(first, 0.0, hi * (2.0 / n))
    c4 = jnp.where(first, nyq / n, hr * (2.0 / n))
    return c1, c2, c4


def _hyena(convh, tables, coefs, hy_bias, hy_g):
    cf, sf, s_inv = tables
    c1, c2, c4 = coefs
    yr, yi = _dft_fwd(convh, 0, cf, sf, (c1[0], c2[0], c4[0]))
    zz = _dft_inv(yr, yi, cf, s_inv, convh, 0, convh, 1, hy_bias[0], hy_g, False)
    yr, yi = _dft_fwd(zz, 0, cf, sf, (c1[1], c2[1], c4[1]))
    return _dft_inv(yr, yi, cf, s_inv, zz, 0, convh, 2, hy_bias[1], hy_g, True)


HY_FAST = 128


def _hy2_tables(length):
    nk = length // HY_FAST
    n2 = 2 * length
    k1 = lax.broadcasted_iota(jnp.int32, (nk, nk), 0)
    s1 = lax.broadcasted_iota(jnp.int32, (nk, nk), 1)
    ang = (((2 * k1 + 1) * s1) % (4 * nk)).astype(F32) * (2.0 * math.pi / (4 * nk))
    f1 = jnp.concatenate([jnp.cos(ang), -jnp.sin(ang)], axis=0)
    shape = (nk, HY_FAST, HY_FAST)
    kk = lax.broadcasted_iota(jnp.int32, shape, 0) + 2 * nk * lax.broadcasted_iota(jnp.int32, shape, 1)
    s2 = lax.broadcasted_iota(jnp.int32, shape, 2)
    phi = (((2 * kk + 1) * s2) % (2 * n2)).astype(F32) * (2.0 * math.pi / (2 * n2))
    c, s = jnp.cos(phi), jnp.sin(phi)
    f2 = jnp.concatenate([jnp.concatenate([c, s], axis=2), jnp.concatenate([-s, c], axis=2)], axis=1)
    f1k = jnp.kron(f1, jnp.eye(HY_SUB, dtype=F32))
    return f1k.astype(BF16), f2.astype(BF16), jnp.swapaxes(f2, 1, 2).astype(BF16), f1k.T.astype(BF16)


HY_SUB = 8


def _hy2_stage1_kernel(x_ref, f1_ref, z_ref):
    nk = x_ref.shape[2]
    tc = x_ref.shape[4]
    for g in range(HY_FAST // HY_SUB):
        rows = slice(g * HY_SUB, (g + 1) * HY_SUB)
        rhs = x_ref[0, 0, :, rows, :].reshape(nk * HY_SUB, tc).astype(BF16)
        z = _dot(f1_ref[...], rhs).astype(BF16)
        z_ref[0, :, :, rows, :] = z.reshape(2, nk, HY_SUB, tc)


def _hy2_stage1(x5, which, f1k, cw=None):
    _, nb, nk, _, call = x5.shape
    cw = call if cw is None else cw
    tc = min(256, cw)
    return pl.pallas_call(
        _hy2_stage1_kernel, grid=(nb, cw // tc),
        in_specs=[pl.BlockSpec((1, 1, nk, HY_FAST, tc), lambda b, j: (which, b, 0, 0, j)),
                  pl.BlockSpec(f1k.shape, lambda b, j: (0, 0))],
        out_specs=pl.BlockSpec((1, 2, nk, HY_FAST, tc), lambda b, j: (b, 0, 0, 0, j)),
        out_shape=jax.ShapeDtypeStruct((nb, 2, nk, HY_FAST, cw), BF16),
        compiler_params=_params(("parallel", "parallel")), name="hyena2_stage1",
    )(x5, f1k)


def _hy2_stage2_kernel(zr_ref, zi_ref, f2_ref, *rest, filtered):
    if filtered:
        g2_ref, hr_ref, hi_ref, vr_ref, vi_ref = rest
    else:
        vr_ref, vi_ref = rest
    for k in range(zr_ref.shape[2]):
        x = _dot(f2_ref[k], jnp.concatenate([zr_ref[0, 0, k], zi_ref[0, 0, k]], axis=0))
        xr, xi = x[:HY_FAST], x[HY_FAST:]
        if filtered:
            hr, hi = hr_ref[k], hi_ref[k]
            y = jnp.concatenate([xr * hr - xi * hi, xr * hi + xi * hr], axis=0).astype(BF16)
            v = _dot(g2_ref[k], y)
            vr_ref[0, 0, k] = v[:HY_FAST].astype(BF16)
            vi_ref[0, 0, k] = v[HY_FAST:].astype(BF16)
        else:
            vr_ref[0, 0, k] = xr
            vi_ref[0, 0, k] = xi


def _hy2_stage2(z, f2, g2=None, hr=None, hi=None):
    nb, _, nk, _, cw = z.shape
    kb = 8
    filtered = hr is not None
    zspec = lambda part: pl.BlockSpec((1, 1, kb, HY_FAST, cw), lambda k, b: (b, part, k, 0, 0))
    mat = pl.BlockSpec((kb, 2 * HY_FAST, 2 * HY_FAST), lambda k, b: (k, 0, 0))
    in_specs = [zspec(0), zspec(1), mat]
    args = [z, z, f2]
    if filtered:
        in_specs += [mat, pl.BlockSpec((kb, HY_FAST, cw), lambda k, b: (k, 0, 0)),
                     pl.BlockSpec((kb, HY_FAST, cw), lambda k, b: (k, 0, 0))]
        args += [g2, hr, hi]
    return pl.pallas_call(
        functools.partial(_hy2_stage2_kernel, filtered=filtered), grid=(nk // kb, nb),
        in_specs=in_specs, out_specs=[zspec(0), zspec(0)],
        out_shape=[jax.ShapeDtypeStruct((nb, 1, nk, HY_FAST, cw), BF16 if filtered else F32)] * 2,
        compiler_params=_params(("parallel", "parallel")),
        name="hyena2_stage2" if filtered else "hyena2_filter_spectrum",
    )(*args)


def _hy2_inv1_kernel(vr_ref, vi_ref, g1_ref, u_ref, gate_ref, bias_ref, g_ref, o_ref, *, final):
    nk = vr_ref.shape[2]
    cw = vr_ref.shape[4]
    for g in range(vr_ref.shape[3] // HY_SUB):
        rows = slice(g * HY_SUB, (g + 1) * HY_SUB)
        rhs = jnp.concatenate([vr_ref[0, 0, :, rows, :].reshape(nk * HY_SUB, cw),
                               vi_ref[0, 0, :, rows, :].reshape(nk * HY_SUB, cw)], axis=0)
        y = _dot(g1_ref[...], rhs).reshape(nk, HY_SUB, cw)
        out = gate_ref[0, 0, :, rows, :] * (y + bias_ref[...] * u_ref[0, 0, :, rows, :])
        if final:
            out = _rms(out, g_ref[...])
        o_ref[0, :, rows, :] = out.astype(o_ref.dtype)


def _hy2_inv1(vr, vi, g1k, u5, which_u, gate5, which_g, bias, g, final):
    nb, _, nk, _, cw = vr.shape
    tg = 32
    vspec = pl.BlockSpec((1, 1, nk, tg, cw), lambda b, j: (b, 0, 0, j, 0))
    return pl.pallas_call(
        functools.partial(_hy2_inv1_kernel, final=final), grid=(nb, HY_FAST // tg),
        in_specs=[vspec, vspec,
                  pl.BlockSpec(g1k.shape, lambda b, j: (0, 0)),
                  pl.BlockSpec((1, 1, nk, tg, cw), lambda b, j: (which_u, b, 0, j, 0)),
                  pl.BlockSpec((1, 1, nk, tg, cw), lambda b, j: (which_g, b, 0, j, 0)),
                  pl.BlockSpec((1, cw), lambda b, j: (0, 0)),
                  pl.BlockSpec((1, cw), lambda b, j: (0, 0))],
        out_specs=pl.BlockSpec((1, nk, tg, cw), lambda b, j: (b, 0, j, 0)),
        out_shape=jax.ShapeDtypeStruct((nb, nk, HY_FAST, cw), BF16 if final else F32),
        compiler_params=_params(("parallel", "parallel")), name="hyena2_inv1",
    )(vr, vi, g1k, u5, gate5, bias.reshape(1, cw), g.reshape(1, cw))


def _hy2_coefs(length, tables, hy_params):
    f1k, f2, _, _ = tables
    nk = length // HY_FAST
    taps, nrm = _hyena_filter_taps(length, *hy_params)
    z = _hy2_stage1(taps.reshape(1, 2 * HY_ORDER, nk, HY_FAST, HY_W), 0, f1k)
    xr, xi = _hy2_stage2(z, f2)
    scale = (1.0 / length) / nrm.reshape(HY_ORDER, 1, 1, HY_W)
    return xr[:HY_ORDER, 0] * scale, xi[HY_ORDER:, 0] * scale


def _hyena2(conv3, tables, coefs, hy_bias, hy_g):
    f1k, f2, g2, g1k = tables
    hr, hi = coefs
    _, bsz, length, cw = conv3.shape
    nk = length // HY_FAST
    c5 = conv3.reshape(3, bsz, nk, HY_FAST, cw)

    def conv(x5, which, order, gate_idx, final):
        vr, vi = _hy2_stage2(_hy2_stage1(x5, which, f1k), f2, g2, hr[order], hi[order])
        return _hy2_inv1(vr, vi, g1k, x5, which, c5, gate_idx, hy_bias[order], hy_g, final)

    zz = conv(c5, 0, 0, 1, False)
    return conv(zz[None], 0, 1, 2, True).reshape(bsz, length, cw)


def _four2_tables(length):
    nk = length // HY_FAST
    k1 = lax.broadcasted_iota(jnp.int32, (nk, nk), 0)
    s1 = lax.broadcasted_iota(jnp.int32, (nk, nk), 1)
    ang = ((k1 * s1) % nk).astype(F32) * (2.0 * math.pi / nk)
    f1 = jnp.concatenate([jnp.cos(ang), -jnp.sin(ang)], axis=0)
    shape = (nk, HY_FAST, HY_FAST)
    kk = lax.broadcasted_iota(jnp.int32, shape, 0) + nk * lax.broadcasted_iota(jnp.int32, shape, 1)
    s2 = lax.broadcasted_iota(jnp.int32, shape, 2)
    phi = ((kk * s2) % length).astype(F32) * (2.0 * math.pi / length)
    c, s = jnp.cos(phi), jnp.sin(phi)
    f2 = jnp.concatenate([jnp.concatenate([c, s], axis=2), jnp.concatenate([-s, c], axis=2)], axis=1)
    cc, sc = _cos_sin_table(FOUR_GW, FOUR_GW, FOUR_GW)
    eye = jnp.eye(FOUR_GROUPS, dtype=F32)
    return (jnp.kron(f1, jnp.eye(HY_SUB, dtype=F32)).astype(BF16), f2.astype(BF16),
            jnp.kron(eye, cc).astype(BF16), jnp.kron(eye, sc).astype(BF16))


def _four2_stage2_kernel(zr_ref, zi_ref, f2_ref, bc_ref, bs_ref, g_ref, o_ref, *, scale):
    for k in range(zr_ref.shape[2]):
        x = _dot(f2_ref[k], jnp.concatenate([zr_ref[0, 0, k], zi_ref[0, 0, k]], axis=0))
        xr, xi = x[:HY_FAST].astype(BF16), x[HY_FAST:].astype(BF16)
        z = (_dot(xr, bc_ref[...]) + _dot(xi, bs_ref[...])) * scale
        o_ref[0, k] = _rms(z, g_ref[...]).astype(o_ref.dtype)


def _fourier2(p3, tables, four_g):
    f1k, f2, bc, bs = tables
    bsz, length, pw = p3.shape
    nk = length // HY_FAST
    z = _hy2_stage1(p3.reshape(1, bsz, nk, HY_FAST, pw), 0, f1k, cw=FOUR_W)
    kb = 8
    zspec = lambda part: pl.BlockSpec((1, 1, kb, HY_FAST, FOUR_W), lambda k, b: (b, part, k, 0, 0))
    const = lambda shape: pl.BlockSpec(shape, lambda k, b: (0,) * len(shape))
    out = pl.pallas_call(
        functools.partial(_four2_stage2_kernel, scale=1.0 / math.sqrt(length * FOUR_GW)),
        grid=(nk // kb, bsz),
        in_specs=[zspec(0), zspec(1),
                  pl.BlockSpec((kb, 2 * HY_FAST, 2 * HY_FAST), lambda k, b: (k, 0, 0)),
                  const((FOUR_W, FOUR_W)), const((FOUR_W, FOUR_W)), const((1, FOUR_W))],
        out_specs=pl.BlockSpec((1, kb, HY_FAST, FOUR_W), lambda k, b: (b, k, 0, 0)),
        out_shape=jax.ShapeDtypeStruct((bsz, nk, HY_FAST, FOUR_W), BF16),
        compiler_params=_params(("parallel", "parallel")), name="fourier2_stage2",
    )(z, z, f2, bc, bs, four_g.reshape(1, FOUR_W))
    return jnp.swapaxes(out, 1, 2).reshape(bsz, length, FOUR_W)


def _softplus(v):
    return jnp.maximum(v, 0.0) + jnp.log(1.0 + jnp.exp(-jnp.abs(v)))


def _delta_prep_kernel(qkv_ref, ab_ref, abt_ref, arow_ref, drow_ref, acol_ref, dcol_ref,
                       wq_ref, u_ref, ak_ref, egl_ref):
    c = ab_ref.shape[1]
    nh = DN_HEADS
    ri = lax.broadcasted_iota(jnp.int32, (c, c), 0)
    ci = lax.broadcasted_iota(jnp.int32, (c, c), 1)
    incl = (ri >= ci, ri <= ci)
    strict = (ri > ci, ri < ci)
    low = jnp.where(incl[0], 1.0, 0.0)
    upp = jnp.where(incl[1], 1.0, 0.0)
    hdot = lambda a, b: jnp.dot(a, b, precision=HIGHEST, preferred_element_type=F32)
    ab = ab_ref[0]
    abt = abt_ref[0, 0]
    gate = -arow_ref[...] * _softplus(ab + drow_ref[...])
    gate_t = -acol_ref[...] * _softplus(abt + dcol_ref[...])
    beta_all = jax.nn.sigmoid(ab)
    gcum = (hdot(low, gate), hdot(upp, gate))
    gcum_t = (hdot(gate_t, upp), hdot(gate_t, low))
    nt = (((1,), (1,)), ((), ()))
    def l2n(t):
        return t * lax.rsqrt(jnp.sum(t * t, axis=-1, keepdims=True) + EPS)

    q = [l2n(_silu(qkv_ref[0, :, h * DN_DK:(h + 1) * DN_DK])) * (DN_DK ** -0.5) for h in range(nh)]
    k = [l2n(_silu(qkv_ref[0, :, DN_W + h * DN_DK:DN_W + (h + 1) * DN_DK])) for h in range(nh)]
    v = [_silu(qkv_ref[0, :, 2 * DN_W + h * DN_DV:2 * DN_W + (h + 1) * DN_DV]) for h in range(nh)]
    k16 = [t.astype(BF16) for t in k]
    kk0 = [lax.dot_general(k16[h], k16[h], nt, preferred_element_type=F32) for h in range(nh)]
    qk0 = [lax.dot_general(q[h].astype(BF16), k16[h], nt, preferred_element_type=F32) for h in range(nh)]
    units = [(d, h) for d in range(2) for h in range(nh)]
    mm, rr = {}, {}
    for d, h in units:
        col = d * nh + h
        gc = gcum[d][:, col:col + 1]
        gr = gcum_t[d][col:col + 1, :]
        be = beta_all[:, 2 * nh + col:2 * nh + col + 1]
        last = 0 if d else c - 1
        gl = gc[last:last + 1, :]
        dec = jnp.where(incl[d], jnp.exp(jnp.where(incl[d], gc - gr, 0.0)), 0.0)
        mm[d, h] = jnp.where(strict[d], be * kk0[h] * dec, 0.0)
        ak_ref[0, 0, d, h, 0:c, :] = jnp.where(incl[d], qk0[h] * dec, 0.0).astype(BF16)
        eg = jnp.exp(gc)
        rr[d, h] = jnp.concatenate([v[h] * be, k[h] * (be * eg)], axis=1)
        wq_ref[0, 0, d, h, c:2 * c, :] = (q[h] * eg).astype(BF16)
        k_tail = k[h] * jnp.exp(gl - gc)
        ak_ref[0, 0, d, h, c:c + DN_DK, :] = jnp.transpose(k_tail).astype(BF16)
        egl_ref[0, 0, d, h:h + 1, :] = jnp.broadcast_to(jnp.exp(gl), (1, DN_DV))
    sb = DN_SOLVE_BLOCK
    in_sb = (ri // sb) == (ci // sb)
    in_2sb = (ri // (2 * sb)) == (ci // (2 * sb))
    eye = jnp.where(ri == ci, 1.0, 0.0)
    b16 = lambda t: t.astype(BF16)
    nj = {u: jnp.where(in_sb, -mm[u], 0.0) for u in units}
    inv = {u: eye + nj[u] for u in units}
    for j in range((sb - 1).bit_length() - 1):
        nj = {u: _dot(b16(nj[u]), b16(nj[u])) for u in units}
        inv = {u: inv[u] + _dot(b16(inv[u]), b16(nj[u])) for u in units}
    off = {u: b16(jnp.where(in_2sb & ~in_sb, mm[u], 0.0)) for u in units}
    tmp = {u: _dot(off[u], b16(inv[u])) for u in units}
    inv = {u: inv[u] - _dot(b16(inv[u]), b16(tmp[u])) for u in units}
    inv16 = {u: b16(inv[u]) for u in units}
    off = {u: b16(jnp.where(in_2sb, 0.0, mm[u])) for u in units}
    part = {u: _dot(inv16[u], b16(rr[u])) for u in units}
    tmp = {u: _dot(off[u], b16(part[u])) for u in units}
    rr = {u: part[u] - _dot(inv16[u], b16(tmp[u])) for u in units}
    for d, h in units:
        u_ref[0, 0, d, h] = rr[d, h][:, :DN_DV]
        wq_ref[0, 0, d, h, 0:c, :] = rr[d, h][:, DN_DV:].astype(BF16)


def _delta_scan_kernel(wqf_ref, uf_ref, akf_ref, egf_ref, wqb_ref, ub_ref, akb_ref, egb_ref, s0f_ref, s0b_ref,
                       of_ref, ob_ref, sf_ref, sb_ref, s_scr):
    i = pl.program_id(1)
    bpb = uf_ref.shape[0]
    c = uf_ref.shape[-2]

    @pl.when(i == 0)
    def _():
        s_scr[0] = s0f_ref[...]
        s_scr[1] = s0b_ref[...]

    refs = ((wqf_ref, uf_ref, akf_ref, egf_ref, of_ref), (wqb_ref, ub_ref, akb_ref, egb_ref, ob_ref))
    units = [(d, bb, h) for d in range(2) for bb in range(bpb) for h in range(DN_HEADS)]
    s = {u: s_scr[u] for u in units}
    ws = {(d, bb, h): _dot(refs[d][0][bb, 0, 0, h], s[d, bb, h].astype(BF16)) for d, bb, h in units}
    v16 = {(d, bb, h): (refs[d][1][bb, 0, 0, h] - ws[d, bb, h][:c]).astype(BF16) for d, bb, h in units}
    av = {(d, bb, h): _dot(refs[d][2][bb, 0, 0, h], v16[d, bb, h]) for d, bb, h in units}
    for d, bb, h in units:
        u = (d, bb, h)
        refs[d][4][bb, :, h * DN_DV:(h + 1) * DN_DV] = (ws[u][c:] + av[u][:c]).astype(BF16)
        s_scr[u] = s[u] * refs[d][3][bb, 0, 0, h:h + 1, :] + av[u][c:]

    @pl.when(i == pl.num_programs(1) - 1)
    def _():
        sf_ref[...] = s_scr[0]
        sb_ref[...] = s_scr[1]


def _delta_rule(qkv, ab, a_log, dt_bias, s0_f, s0_b):
    bsz, length, _ = qkv.shape
    bpb = 4 if bsz % 4 == 0 else 1
    c = DN_CHUNK
    n = length // c
    nh = DN_HEADS
    abt = jnp.swapaxes(ab.reshape(bsz, n, c, 4 * nh), 2, 3)
    zeros = jnp.zeros((2 * nh,), F32)
    a_vec = jnp.concatenate([jnp.exp(a_log.astype(F32)).reshape(-1), zeros])
    d_vec = jnp.concatenate([dt_bias.astype(F32).reshape(-1), zeros])
    small = lambda shape: pl.BlockSpec(shape, lambda b, i: (0,) * len(shape))
    per_chunk = lambda *tail: pl.BlockSpec((1, 1, 2, nh) + tail, lambda b, i: (b, i, 0, 0) + (0,) * len(tail))
    wq, u, ak, egl = pl.pallas_call(
        _delta_prep_kernel, grid=(bsz, n),
        in_specs=[pl.BlockSpec((1, c, 3 * DN_W), lambda b, i: (b, i, 0)),
                  pl.BlockSpec((1, c, 4 * nh), lambda b, i: (b, i, 0)),
                  pl.BlockSpec((1, 1, 4 * nh, c), lambda b, i: (b, i, 0, 0)),
                  small((1, 4 * nh)), small((1, 4 * nh)), small((4 * nh, 1)), small((4 * nh, 1))],
        out_specs=[per_chunk(2 * c, DN_DK), per_chunk(c, DN_DV), per_chunk(c + DN_DK, c),
                   pl.BlockSpec((1, 1, 2, nh, DN_DV), lambda b, i: (b, i, 0, 0, 0))],
        out_shape=[jax.ShapeDtypeStruct((bsz, n, 2, nh, 2 * c, DN_DK), BF16),
                   jax.ShapeDtypeStruct((bsz, n, 2, nh, c, DN_DV), F32),
                   jax.ShapeDtypeStruct((bsz, n, 2, nh, c + DN_DK, c), BF16),
                   jax.ShapeDtypeStruct((bsz, n, 2, nh, DN_DV), F32)],
        compiler_params=_params(("parallel", "parallel")), name="delta_prep",
    )(qkv, ab, abt, a_vec.reshape(1, -1), d_vec.reshape(1, -1), a_vec.reshape(-1, 1), d_vec.reshape(-1, 1))

    def side(d, *tail):
        idx = (lambda b, i: (b, n - 1 - i, 1, 0) + (0,) * len(tail)) if d else (
            lambda b, i: (b, i, 0, 0) + (0,) * len(tail))
        return pl.BlockSpec((bpb, 1, 1, nh) + tail, idx)

    def side_specs(d):
        return [side(d, 2 * c, DN_DK), side(d, c, DN_DV), side(d, c + DN_DK, c),
                pl.BlockSpec((bpb, 1, 1, nh, DN_DV),
                             (lambda b, i: (b, n - 1 - i, 1, 0, 0)) if d else (lambda b, i: (b, i, 0, 0, 0)))]

    st_spec = pl.BlockSpec((bpb, nh, DN_DK, DN_DV), lambda b, i: (b, 0, 0, 0))
    o_shape = jax.ShapeDtypeStruct((bsz, length, DN_W), BF16)
    s_shape = jax.ShapeDtypeStruct((bsz, nh, DN_DK, DN_DV), F32)
    return pl.pallas_call(
        _delta_scan_kernel, grid=(bsz // bpb, n),
        in_specs=side_specs(0) + side_specs(1) + [st_spec, st_spec],
        out_specs=[pl.BlockSpec((bpb, c, DN_W), lambda b, i: (b, i, 0)),
                   pl.BlockSpec((bpb, c, DN_W), lambda b, i: (b, n - 1 - i, 0)), st_spec, st_spec],
        out_shape=[o_shape, o_shape, s_shape, s_shape],
        scratch_shapes=[pltpu.VMEM((2, bpb, nh, DN_DK, DN_DV), F32)],
        compiler_params=_params(("parallel", "arbitrary")), name="delta_scan",
    )(wq, u, ak, egl, wq, u, ak, egl, s0_f, s0_b)


def _proj_out_kernel(yf_ref, yh_ref, of_ref, ob_ref, z_ref, dg_ref, w_ref, res_ref, gate_ref, o_ref, a_scr):
    @pl.when(pl.program_id(1) == 0)
    def _():
        a_scr[:, 0:FOUR_W] = yf_ref[...]
        a_scr[:, FOUR_W:FOUR_W + HY_W] = yh_ref[...]
        o = of_ref[...].astype(F32) + ob_ref[...].astype(F32)
        z = z_ref[...].astype(F32)
        for h in range(DN_HEADS):
            lanes = slice(h * DN_DV, (h + 1) * DN_DV)
            y = _rms(o[:, lanes], dg_ref[...]) * _silu(z[:, lanes])
            a_scr[:, FOUR_W + HY_W + h * DN_DV:FOUR_W + HY_W + (h + 1) * DN_DV] = y.astype(BF16)

    o_ref[...] = res_ref[...] + gate_ref[0] * _dot(a_scr[...], w_ref[...])


def _proj_out(y_four, y_hy, o_f, o_b, p2d, dn_g, w, res, gate, rows_per_mod):
    m, d = res.shape
    tm = min(512, m)
    tn = d
    per = rows_per_mod // tm
    zb = PCOL_Z // DN_W
    return pl.pallas_call(
        _proj_out_kernel, grid=(m // tm, d // tn),
        in_specs=[pl.BlockSpec((tm, FOUR_W), lambda i, j: (i, 0)),
                  pl.BlockSpec((tm, HY_W), lambda i, j: (i, 0)),
                  pl.BlockSpec((tm, DN_W), lambda i, j: (i, 0)),
                  pl.BlockSpec((tm, DN_W), lambda i, j: (i, 0)),
                  pl.BlockSpec((tm, DN_W), lambda i, j: (i, zb)),
                  pl.BlockSpec((1, DN_DV), lambda i, j: (0, 0)),
                  pl.BlockSpec((w.shape[0], tn), lambda i, j: (0, j)),
                  pl.BlockSpec((tm, tn), lambda i, j: (i, j)),
                  pl.BlockSpec((1, 1, tn), lambda i, j: (i // per, 0, j))],
        out_specs=pl.BlockSpec((tm, tn), lambda i, j: (i, j)),
        out_shape=jax.ShapeDtypeStruct((m, d), F32),
        scratch_shapes=[pltpu.VMEM((tm, w.shape[0]), BF16)],
        compiler_params=_params(("parallel", "arbitrary")), name="proj_out",
    )(y_four, y_hy, o_f, o_b, p2d, dn_g.reshape(1, DN_DV), w, res, gate)


def _select_experts(lg):
    lane = lax.broadcasted_iota(jnp.int32, lg.shape, 1)
    neg = -jnp.inf
    first = lambda mask: jnp.min(jnp.where(mask, lane, ROUTER_W), axis=1, keepdims=True)
    top = lambda mask: jnp.max(jnp.where(mask, lg, neg), axis=1, keepdims=True)
    coarse = lane < N_GROUPS
    m = top(coarse)
    p_sel = 1.0 / jnp.sum(jnp.where(coarse, jnp.exp(lg - m), 0.0), axis=1, keepdims=True)
    grp = first(coarse & (lg == m))
    lo = N_GROUPS + EXPERTS_PER_GROUP * grp
    fine = (lane >= lo) & (lane < lo + EXPERTS_PER_GROUP)
    v1 = top(fine)
    i1 = first(fine & (lg == v1))
    rest = fine & (lane != i1)
    v2 = top(rest)
    i2 = first(rest & (lg == v2))
    e2 = jnp.exp(v2 - v1)
    g1 = p_sel / (1.0 + e2)
    ids = jnp.where(lane == 0, i1 - N_GROUPS, jnp.where(lane == 1, i2 - N_GROUPS, 0))
    gates = jnp.where(lane == 0, g1, jnp.where(lane == 1, g1 * e2, 0.0))
    return ids, gates


def _router_kernel(x_ref, c_ref, g_ref, sx_ref, cx_ref, sc_ref, cc_ref, w_ref, b_ref, h_ref, id_ref, gt_ref,
                   *, n_x):
    i = pl.program_id(0)

    def emit(v, shift, scale):
        h = _rms(v, g_ref[...]) * (1.0 + scale) + shift
        h_ref[...] = h
        lg = jnp.dot(h, w_ref[...], precision=HIGHEST, preferred_element_type=F32) + b_ref[...]
        id_ref[...], gt_ref[...] = _select_experts(lg)

    @pl.when(i < n_x)
    def _():
        emit(x_ref[...], sx_ref[0], cx_ref[0])

    @pl.when(i >= n_x)
    def _():
        emit(c_ref[...], sc_ref[0], cc_ref[0])


def _router(x2d, c2d, g, shift_x, scale_x, shift_c, scale_c, w_r, b_r, rows_per_mod, with_ctx):
    mx, d = x2d.shape
    tm = 512
    n_x = mx // tm
    n_c = c2d.shape[0] // tm if with_ctx else 0
    per = rows_per_mod // tm
    xi = lambda i: (jnp.minimum(i, n_x - 1), 0)
    ci = lambda i: (jnp.maximum(i - n_x, 0), 0)
    h, ids, gates = pl.pallas_call(
        functools.partial(_router_kernel, n_x=n_x), grid=(n_x + n_c,),
        in_specs=[pl.BlockSpec((tm, d), xi), pl.BlockSpec((tm, d), ci),
                  pl.BlockSpec((1, d), lambda i: (0, 0)),
                  pl.BlockSpec((1, 1, d), lambda i: (jnp.minimum(i, n_x - 1) // per, 0, 0)),
                  pl.BlockSpec((1, 1, d), lambda i: (jnp.minimum(i, n_x - 1) // per, 0, 0)),
                  pl.BlockSpec((1, 1, d), lambda i: (0, 0, 0)),
                  pl.BlockSpec((1, 1, d), lambda i: (0, 0, 0)),
                  pl.BlockSpec((d, ROUTER_W), lambda i: (0, 0)),
                  pl.BlockSpec((1, ROUTER_W), lambda i: (0, 0))],
        out_specs=[pl.BlockSpec((tm, d), lambda i: (i, 0)),
                   pl.BlockSpec((tm, ROUTER_W), lambda i: (i, 0)),
                   pl.BlockSpec((tm, ROUTER_W), lambda i: (i, 0))],
        out_shape=[jax.ShapeDtypeStruct(((n_x + n_c) * tm, d), F32),
                   jax.ShapeDtypeStruct(((n_x + n_c) * tm, ROUTER_W), jnp.int32),
                   jax.ShapeDtypeStruct(((n_x + n_c) * tm, ROUTER_W), F32)],
        compiler_params=_params(("parallel",)), name="moe_router",
    )(x2d, c2d, g.reshape(1, d), shift_x, scale_x, shift_c, scale_c, w_r, b_r)
    return h, ids[:, :TOP_K], gates[:, :TOP_K]


def _row_copy(src_hbm, row, dst, slot, r, sem):
    return pltpu.make_async_copy(src_hbm.at[pl.ds(row, 1)], dst.at[slot, pl.ds(r, 1)], sem.at[slot])


def _expert_kernel(be_ref, tok_ref, nb_ref, h_hbm, wg_ref, wu_ref, wd_ref, ys_ref, xbuf, sem, *, bm):
    i = pl.program_id(0)
    nb = nb_ref[0]

    def start(blk, slot):
        def body(r, carry):
            _row_copy(h_hbm, tok_ref[blk * bm + r], xbuf, slot, r, sem).start()
            return carry
        lax.fori_loop(0, bm, body, 0, unroll=8)

    def wait(slot):
        pltpu.make_async_copy(h_hbm.at[pl.ds(0, bm)], xbuf.at[slot], sem.at[slot]).wait()

    @pl.when(i == 0)
    def _():
        start(0, 0)

    @pl.when(i + 1 < nb)
    def _():
        start(i + 1, (i + 1) % 2)

    @pl.when(i < nb)
    def _():
        slot = i % 2
        wait(slot)
        x = xbuf[slot].astype(BF16)
        act = (_silu(_dot(x, wg_ref[0, 0])) * _dot(x, wu_ref[0, 0])).astype(BF16)
        ys_ref[...] = _dot(act, wd_ref[0, 0])

    @pl.when(i >= nb)
    def _():
        ys_ref[...] = jnp.zeros_like(ys_ref)


def _experts(h, buf_tok, block_e, n_used, wg, wu, wd, layer, bm):
    n_blocks = block_e.shape[0]
    d = h.shape[1]
    ff = wg.shape[3]
    grid_spec = pltpu.PrefetchScalarGridSpec(
        num_scalar_prefetch=3, grid=(n_blocks,),
        in_specs=[pl.BlockSpec(memory_space=pl.ANY),
                  pl.BlockSpec((1, 1, d, ff), lambda i, be, tok, nb: (layer, be[i], 0, 0)),
                  pl.BlockSpec((1, 1, d, ff), lambda i, be, tok, nb: (layer, be[i], 0, 0)),
                  pl.BlockSpec((1, 1, ff, d), lambda i, be, tok, nb: (layer, be[i], 0, 0))],
        out_specs=pl.BlockSpec((bm, d), lambda i, be, tok, nb: (i, 0)),
        scratch_shapes=[pltpu.VMEM((2, bm, d), F32), pltpu.SemaphoreType.DMA((2,))])
    return pl.pallas_call(
        functools.partial(_expert_kernel, bm=bm), grid_spec=grid_spec,
        out_shape=jax.ShapeDtypeStruct((n_blocks * bm, d), F32),
        compiler_params=_params(("arbitrary",)), name="moe_experts",
    )(block_e, buf_tok, n_used, h, wg, wu, wd)


def _combine_kernel(pos_ref, ys_hbm, x_ref, gw_ref, gate_ref, fg_ref, o_ref, ybuf, sem, *, tm, tok0, final):
    i = pl.program_id(0)
    n = pl.num_programs(0)

    def start(blk, slot):
        def body(r, carry):
            t = tok0 + blk * tm + r
            _row_copy(ys_hbm, pos_ref[2 * t], ybuf, slot, r, sem).start()
            _row_copy(ys_hbm, pos_ref[2 * t + 1], ybuf, slot, tm + r, sem).start()
            return carry
        lax.fori_loop(0, tm, body, 0, unroll=8)

    @pl.when(i == 0)
    def _():
        start(0, 0)

    @pl.when(i + 1 < n)
    def _():
        start(i + 1, (i + 1) % 2)

    slot = i % 2

    pltpu.make_async_copy(ys_hbm.at[pl.ds(0, 2 * tm)], ybuf.at[slot], sem.at[slot]).wait()
    gw = gw_ref[...]
    y = gw[:, 0:1] * ybuf[slot, 0:tm, :] + gw[:, 1:2] * ybuf[slot, tm:2 * tm, :]
    out = x_ref[...] + gate_ref[0] * y
    if final:
        out = _rms(out, fg_ref[...])
    o_ref[...] = out


def _combine(ys, pos, gw, x2d, gate, final_g, tok0, rows_per_mod, final):
    m, d = x2d.shape
    tm = 256
    per = rows_per_mod // tm
    gb = tok0 // tm
    grid_spec = pltpu.PrefetchScalarGridSpec(
        num_scalar_prefetch=1, grid=(m // tm,),
        in_specs=[pl.BlockSpec(memory_space=pl.ANY),
                  pl.BlockSpec((tm, d), lambda i, pos: (i, 0)),
                  pl.BlockSpec((tm, TOP_K), lambda i, pos: (gb + i, 0)),
                  pl.BlockSpec((1, 1, d), lambda i, pos: (i // per, 0, 0)),
                  pl.BlockSpec((1, d), lambda i, pos: (0, 0))],
        out_specs=pl.BlockSpec((tm, d), lambda i, pos: (i, 0)),
        scratch_shapes=[pltpu.VMEM((2, 2 * tm, d), F32), pltpu.SemaphoreType.DMA((2,))])
    return pl.pallas_call(
        functools.partial(_combine_kernel, tm=tm, tok0=tok0, final=final), grid_spec=grid_spec,
        out_shape=jax.ShapeDtypeStruct((m, d), F32),
        compiler_params=_params(("arbitrary",)), name="moe_combine",
    )(pos, ys, x2d, gw, gate, final_g.reshape(1, d))


def _rank_kernel(e_ref, rank_ref, cnt_ref, run_scr):
    @pl.when(pl.program_id(0) == 0)
    def _():
        run_scr[...] = jnp.zeros_like(run_scr)

    tb = e_ref.shape[0]
    lane = lax.broadcasted_iota(jnp.int32, (tb, LANE), 1)
    onehot = jnp.where(e_ref[...] == lane, 1.0, 0.0)
    ri = lax.broadcasted_iota(jnp.int32, (tb, tb), 0)
    ci = lax.broadcasted_iota(jnp.int32, (tb, tb), 1)
    before = jnp.where(ri > ci, 1.0, 0.0).astype(BF16)
    prefix = _dot(before, onehot.astype(BF16)) + run_scr[...]
    rank_ref[...] = jnp.sum(prefix * onehot, axis=1, keepdims=True).astype(jnp.int32)
    run_scr[...] += jnp.sum(onehot, axis=0, keepdims=True)
    cnt_ref[...] = run_scr[...]


def _expert_ranks(flat_e):
    a = flat_e.shape[0]
    tb = 512
    rank, cnt = pl.pallas_call(
        _rank_kernel, grid=(a // tb,),
        in_specs=[pl.BlockSpec((tb, 1), lambda i: (i, 0))],
        out_specs=[pl.BlockSpec((tb, 1), lambda i: (i, 0)), pl.BlockSpec((1, LANE), lambda i: (0, 0))],
        out_shape=[jax.ShapeDtypeStruct((a, 1), jnp.int32), jax.ShapeDtypeStruct((1, LANE), F32)],
        scratch_shapes=[pltpu.VMEM((1, LANE), F32)],
        compiler_params=_params(("arbitrary",)), name="moe_rank",
    )(flat_e.reshape(a, 1))
    return rank.reshape(a), cnt[0, :N_EXPERTS].astype(jnp.int32)


def _dispatch_plan(expert, bm):
    t = expert.shape[0]
    a = t * TOP_K
    flat_e = expert.reshape(a).astype(jnp.int32)
    rank, counts = _expert_ranks(flat_e)
    padded = (counts + bm - 1) // bm * bm
    pad_end = jnp.cumsum(padded)
    pad_start = pad_end - padded
    mine = flat_e[:, None] == jnp.arange(N_EXPERTS, dtype=jnp.int32)[None, :]
    pos = (jnp.sum(jnp.where(mine, pad_start[None, :], 0), axis=1) + rank).astype(jnp.int32)
    n_blocks = a // bm + N_EXPERTS
    flat_tok = jnp.arange(a, dtype=jnp.int32) // TOP_K
    buf_tok = jnp.zeros((n_blocks * bm,), jnp.int32).at[pos].set(flat_tok)
    n_used = (pad_end[-1] // bm).astype(jnp.int32)
    blk = jnp.minimum(jnp.arange(n_blocks, dtype=jnp.int32), n_used - 1) * bm
    block_e = jnp.sum((pad_end[None, :] <= blk[:, None]).astype(jnp.int32), axis=1)
    block_e = jnp.minimum(block_e, N_EXPERTS - 1).astype(jnp.int32)
    return pos, buf_tok, block_e, n_used.reshape(1)


def _mix_stream(p3, ab, rows, width, s0_f, s0_b, four_tabs, hy_tabs, hy_coefs, conv_w, conv_b, four_g,
                hy_bias, hy_g, a_log, dt_bias, with_mixers):
    qkv = _short_conv(p3, conv_w, conv_b, rows, width, OFF_DN, HY_CONV_CH, 3 * DN_W, 512)[0]
    o_f, o_b, s_f, s_b = _delta_rule(qkv, ab, a_log, dt_bias, s0_f, s0_b)
    if not with_mixers:
        return None, None, o_f, o_b, s_f, s_b
    if _hy_two_level(p3.shape[1]):
        y_four = _fourier2(p3, four_tabs, four_g)
    else:
        y_four = _fourier(p3, *four_tabs, four_g)
    if _hy_two_level(p3.shape[1]):
        conv3 = _short_conv(p3, conv_w, conv_b, rows, width, OFF_HY, 0, HY_CONV_CH, 256, split=3)
        y_hy = _hyena2(conv3, hy_tabs, hy_coefs, hy_bias, hy_g)
    else:
        convh = _short_conv(p3, conv_w, conv_b, rows, width, OFF_HY, 0, HY_CONV_CH, 256)[0]
        y_hy = _hyena(convh, hy_tabs, hy_coefs, hy_bias, hy_g)
    return y_four, y_hy, o_f, o_b, s_f, s_b


def _hy_two_level(length):
    return length % (8 * HY_FAST) == 0


def kernel(x, c, ctx, c_ctx, norm1_g, norm2_g, w_mod, b_mod, w_in, conv_w, conv_b, four_g, hy_w1, hy_b1, hy_w2, hy_b2, hy_w3, hy_b3, hy_w4, hy_freq, hy_bias, hy_g, dn_a_log, dn_dt_bias, dn_g, w_out, w_rc, b_rc, w_rf, b_rf, w_e_gate, w_e_up, w_e_down, final_g):
    bsz, length, d = x.shape
    lc = ctx.shape[1]
    depth = w_in.shape[0]
    rows = length // GRID_W
    bm = 256

    mod_all = _modulation(c, c_ctx, w_mod, b_mod)
    four_tables = lambda n: _four2_tables(n) if _hy_two_level(n) else _fourier_tables(n)
    four_x, four_c = four_tables(length), four_tables(lc)
    hy_tables = lambda n: _hy2_tables(n) if _hy_two_level(n) else _hyena_tables(n)
    hy_coefs = lambda n, tabs, prm: (_hy2_coefs if _hy_two_level(n) else _hyena_coefs)(n, tabs, prm)
    hy_x, hy_c = hy_tables(length), hy_tables(lc)
    zeros = jnp.zeros((bsz, DN_HEADS, DN_DK, DN_DV), F32)
    x2 = x.reshape(bsz * length, d)
    c2 = ctx.reshape(bsz * lc, d)
    wg16, wu16, wd16 = w_e_gate.astype(BF16), w_e_up.astype(BF16), w_e_down.astype(BF16)

    for l in range(depth):
        last = l == depth - 1
        mod = [m[:, None, :] for m in jnp.split(mod_all[l, :bsz], 6, axis=-1)]
        modc = [m[:, None, :] for m in jnp.split(mod_all[l, bsz:bsz + 1], 6, axis=-1)]
        wi = w_in[l]
        w_p = jnp.concatenate([wi[:, :OFF_AB], wi[:, OFF_Z:], wi[:, OFF_AB:OFF_Z],
                               jnp.zeros((d, PROJ_W - PCOL_AB - N_AB), wi.dtype)], axis=1).astype(BF16)
        w_o = w_out[l].astype(BF16)
        hy_params = (hy_w1[l], hy_b1[l], hy_w2[l], hy_b2[l], hy_w3[l], hy_b3[l], hy_w4[l], hy_freq[l])
        mix_args = (conv_w[l], conv_b[l], four_g[l], hy_bias[l], hy_g[l], dn_a_log[l], dn_dt_bias[l])

        pc, abc = _proj_in(c2, norm1_g[l], modc[0], modc[1], w_p, bsz * lc)
        pc3 = pc.reshape(bsz, lc, PROJ_W)
        abc = abc[:, :N_AB].reshape(bsz, lc, N_AB)
        coefs_c = None if last else hy_coefs(lc, hy_c, hy_params)
        yf, yh, o_f, o_b, s_f, s_b = _mix_stream(pc3, abc, 1, lc, zeros, zeros, four_c, hy_c, coefs_c,
                                                 *mix_args, with_mixers=not last)
        if not last:
            c2 = _proj_out(yf.reshape(-1, FOUR_W), yh.reshape(-1, HY_W), o_f.reshape(-1, DN_W),
                           o_b.reshape(-1, DN_W), pc, dn_g[l], w_o, c2, modc[2], bsz * lc)

        p, abx = _proj_in(x2, norm1_g[l], mod[0], mod[1], w_p, length)
        p3 = p.reshape(bsz, length, PROJ_W)
        abx = abx[:, :N_AB].reshape(bsz, length, N_AB)
        coefs_x = hy_coefs(length, hy_x, hy_params)
        yf, yh, o_f, o_b, _, _ = _mix_stream(p3, abx, rows, GRID_W, s_f, s_b, four_x, hy_x, coefs_x,
                                             *mix_args, with_mixers=True)
        x2 = _proj_out(yf.reshape(-1, FOUR_W), yh.reshape(-1, HY_W), o_f.reshape(-1, DN_W),
                       o_b.reshape(-1, DN_W), p, dn_g[l], w_o, x2, mod[2], length)

        w_r = jnp.concatenate([w_rc[l], w_rf[l], jnp.zeros((d, ROUTER_W - N_GROUPS - N_EXPERTS), F32)], axis=1)
        b_r = jnp.concatenate([b_rc[l], b_rf[l], jnp.zeros((ROUTER_W - N_GROUPS - N_EXPERTS,), F32)])[None, :]
        h, expert, gate = _router(x2, c2, norm2_g[l], mod[3], mod[4], modc[3], modc[4], w_r, b_r, length,
                                  with_ctx=not last)
        pos, buf_tok, block_e, n_used = _dispatch_plan(expert, bm)
        ys = _experts(h, buf_tok, block_e, n_used, wg16, wu16, wd16, l, bm)
        x2 = _combine(ys, pos, gate, x2, mod[5], final_g, 0, length, final=last)
        if not last:
            c2 = _combine(ys, pos, gate, c2, modc[5], final_g, bsz * length, bsz * lc, final=False)
    return x2.reshape(bsz, length, d)
```

```python
import functools
import math

import jax
import jax.numpy as jnp
from jax import lax
from jax.experimental import pallas as pl
from jax.experimental.pallas import tpu as pltpu

F32 = jnp.float32
BF16 = jnp.bfloat16
HIGHEST = lax.Precision.HIGHEST

GRID_W = 64
FOUR_W = 512
FOUR_GROUPS = 4
FOUR_GW = FOUR_W // FOUR_GROUPS
HY_W = 512
HY_ORDER = 2
DN_HEADS = 8
DN_DK = 128
DN_DV = 128
DN_W = DN_HEADS * DN_DV
DN_CHUNK = 64
DN_SOLVE_BLOCK = DN_CHUNK // 4
HY_CONV_CH = (HY_ORDER + 1) * HY_W
OFF_HY = FOUR_W
OFF_DN = OFF_HY + HY_CONV_CH
OFF_AB = OFF_DN + 3 * DN_W
N_AB = 4 * DN_HEADS
OFF_Z = OFF_AB + N_AB
HY_EMB = 33
HY_FAST_DECAY = 0.3
HY_SLOW_DECAY = 1.5
HY_TARGET = 1e-2
N_GROUPS = 4
EXPERTS_PER_GROUP = 8
N_EXPERTS = N_GROUPS * EXPERTS_PER_GROUP
TOP_K = 2
EPS = 1e-6

LANE = 128
PCOL_Z = OFF_AB
PCOL_AB = PCOL_Z + DN_W
MXU_W = 256
PROJ_TN = 5 * MXU_W
PROJ_W = -(-(PCOL_AB + LANE) // PROJ_TN) * PROJ_TN
ROUTER_W = LANE
VMEM_LIMIT = 56 * 1024 * 1024


def _params(semantics):
    return pltpu.CompilerParams(dimension_semantics=semantics, vmem_limit_bytes=VMEM_LIMIT)


def _silu(v):
    return v * jax.nn.sigmoid(v)


def _rms(v, g):
    return v * lax.rsqrt(jnp.mean(v * v, axis=-1, keepdims=True) + EPS) * g


def _dot(a, b):
    return jnp.dot(a, b, preferred_element_type=F32)


def _mod_kernel(a_ref, w_ref, b_ref, o_ref):
    o_ref[0] = jnp.dot(_silu(a_ref[...]), w_ref[0], precision=HIGHEST,
                       preferred_element_type=F32) + b_ref[0]


def _modulation(c, c_ctx, w_mod, b_mod):
    depth, d, n = w_mod.shape
    bsz = c.shape[0]
    rows = -(-(bsz + 1) // 8) * 8
    a = jnp.concatenate([c, c_ctx[None], jnp.zeros((rows - bsz - 1, d), F32)], axis=0)
    tn = 1024 if n % 1024 == 0 else 512
    assert n % tn == 0
    return pl.pallas_call(
        _mod_kernel, grid=(depth, n // tn),
        in_specs=[pl.BlockSpec((rows, d), lambda l, j: (0, 0)),
                  pl.BlockSpec((1, d, tn), lambda l, j: (l, 0, j)),
                  pl.BlockSpec((1, 1, tn), lambda l, j: (l, 0, j))],
        out_specs=pl.BlockSpec((1, rows, tn), lambda l, j: (l, 0, j)),
        out_shape=jax.ShapeDtypeStruct((depth, rows, n), F32),
        compiler_params=_params(("parallel", "parallel")), name="modulation",
    )(a, w_mod, b_mod.reshape(depth, 1, n))


def _proj_in_kernel(x_ref, g_ref, shift_ref, scale_ref, w_ref, o_ref, ab_ref, a_scr, *, ab_off):
    j = pl.program_id(1)

    @pl.when(j == 0)
    def _():
        y = _rms(x_ref[...], g_ref[...])
        a_scr[...] = (y * (1.0 + scale_ref[0]) + shift_ref[0]).astype(BF16)

    acc = _dot(a_scr[...], w_ref[...])
    o_ref[...] = acc.astype(BF16)

    @pl.when(j == pl.num_programs(1) - 1)
    def _():
        ab_ref[...] = acc[:, ab_off:ab_off + LANE]


def _proj_in(x2d, g, shift, scale, w, rows_per_mod):
    m, d = x2d.shape
    n = w.shape[1]
    tm = min(1024, m, rows_per_mod)
    tn = PROJ_TN
    assert n % tn == 0 and PCOL_AB >= n - tn
    per = rows_per_mod // tm
    return pl.pallas_call(
        functools.partial(_proj_in_kernel, ab_off=PCOL_AB - (n - tn)), grid=(m // tm, n // tn),
        in_specs=[pl.BlockSpec((tm, d), lambda i, j: (i, 0)),
                  pl.BlockSpec((1, d), lambda i, j: (0, 0)),
                  pl.BlockSpec((1, 1, d), lambda i, j: (i // per, 0, 0)),
                  pl.BlockSpec((1, 1, d), lambda i, j: (i // per, 0, 0)),
                  pl.BlockSpec((d, tn), lambda i, j: (0, j))],
        out_specs=[pl.BlockSpec((tm, tn), lambda i, j: (i, j)),
                   pl.BlockSpec((tm, LANE), lambda i, j: (i, 0))],
        out_shape=[jax.ShapeDtypeStruct((m, n), BF16), jax.ShapeDtypeStruct((m, LANE), F32)],
        scratch_shapes=[pltpu.VMEM((tm, d), BF16)],
        compiler_params=_params(("parallel", "arbitrary")), name="proj_in",
    )(x2d, g.reshape(1, d), shift, scale, w)


def _cos_sin_table(nrow, ncol, period):
    i = lax.broadcasted_iota(jnp.int32, (nrow, ncol), 0)
    j = lax.broadcasted_iota(jnp.int32, (nrow, ncol), 1)
    ang = ((i * j) % period).astype(F32) * (2.0 * math.pi / period)
    return jnp.cos(ang), jnp.sin(ang)


def _dft_tables(length, period):
    blk = 64 if length % 64 == 0 else length
    hi_i = lax.broadcasted_iota(jnp.int32, (length // blk, length), 0) * blk
    j = lax.broadcasted_iota(jnp.int32, (length // blk, length), 1)
    ang = ((hi_i * j) % period).astype(F32) * (2.0 * math.pi / period)
    hi_c, hi_s = jnp.cos(ang), jnp.sin(ang)
    lo_c, lo_s = _cos_sin_table(blk, length, period)
    c = hi_c[:, None, :] * lo_c[None, :, :] - hi_s[:, None, :] * lo_s[None, :, :]
    s = hi_s[:, None, :] * lo_c[None, :, :] + hi_c[:, None, :] * lo_s[None, :, :]
    return c.reshape(length, length), s.reshape(length, length)


def _fourier_kernel(u_ref, c_ref, s_ref, bc_ref, bs_ref, g_ref, o_ref, ub_scr, *, scale):
    @pl.when(pl.program_id(1) == 0)
    def _():
        ub_scr[...] = u_ref[0].astype(BF16)

    ub = ub_scr[...]
    p = _dot(c_ref[...], ub).astype(BF16)
    q = _dot(s_ref[...], ub).astype(BF16)
    z = (_dot(p, bc_ref[...]) - _dot(q, bs_ref[...])) * scale
    o_ref[0] = _rms(z, g_ref[...]).astype(o_ref.dtype)


def _fourier(p3, cl, sl, bc, bs, four_g):
    bsz, length, _ = p3.shape
    tm = min(512, length)
    kern = functools.partial(_fourier_kernel, scale=1.0 / math.sqrt(length * FOUR_GW))
    return pl.pallas_call(
        kern, grid=(bsz, length // tm),
        in_specs=[pl.BlockSpec((1, length, FOUR_W), lambda b, i: (b, 0, 0)),
                  pl.BlockSpec((tm, length), lambda b, i: (i, 0)),
                  pl.BlockSpec((tm, length), lambda b, i: (i, 0)),
                  pl.BlockSpec((FOUR_W, FOUR_W), lambda b, i: (0, 0)),
                  pl.BlockSpec((FOUR_W, FOUR_W), lambda b, i: (0, 0)),
                  pl.BlockSpec((1, FOUR_W), lambda b, i: (0, 0))],
        out_specs=pl.BlockSpec((1, tm, FOUR_W), lambda b, i: (b, i, 0)),
        out_shape=jax.ShapeDtypeStruct((bsz, length, FOUR_W), BF16),
        scratch_shapes=[pltpu.VMEM((length, FOUR_W), BF16)],
        compiler_params=_params(("parallel", "arbitrary")), name="fourier",
    )(p3, cl, sl, bc, bs, four_g.reshape(1, FOUR_W))


def _fourier_tables(length):
    cl, sl = _dft_tables(length, length)
    cc, sc = _cos_sin_table(FOUR_GW, FOUR_GW, FOUR_GW)
    eye = jnp.eye(FOUR_GROUPS, dtype=F32)
    return cl.astype(BF16), sl.astype(BF16), jnp.kron(eye, cc).astype(BF16), jnp.kron(eye, sc).astype(BF16)


def _conv_kernel(x_ref, w_ref, b_ref, o_ref, *, rows, width):
    tc = x_ref.shape[2]
    wts = w_ref[...]
    bias = b_ref[...]
    pos = lax.broadcasted_iota(jnp.int32, (width, tc), 0)
    not_first = pos > 0
    not_last = pos < width - 1

    def grid_row(r):
        start = r * width if isinstance(r, int) else pl.multiple_of(r * width, width)
        return x_ref[0, pl.ds(start, width), :].astype(F32)

    def body(r, carry):
        base = pl.multiple_of(r * width, width)
        up, cen = carry
        if rows > 1:
            dn = grid_row(jnp.minimum(r + 1, rows - 1))
            w_up = wts[0] * jnp.where(r > 0, 1.0, 0.0)
            w_dn = wts[2] * jnp.where(r < rows - 1, 1.0, 0.0)
            col = lambda j: up * w_up[j:j + 1, :] + cen * wts[1, j:j + 1, :] + dn * w_dn[j:j + 1, :]
        else:
            dn = cen
            col = lambda j: cen * wts[1, j:j + 1, :]
        left = jnp.where(not_first, pltpu.roll(col(0), 1, 0), 0.0)
        right = jnp.where(not_last, pltpu.roll(col(2), width - 1, 0), 0.0)
        o_ref[0, pl.ds(base, width), :] = left + col(1) + right + bias
        return cen, dn

    first = grid_row(0)
    lax.fori_loop(0, rows, body, (first, first))


def _short_conv(p3, conv_w, conv_b, rows, width, col0, ch0, nch, tc, split=1):
    bsz, length, _ = p3.shape
    kern = functools.partial(_conv_kernel, rows=rows, width=width)
    cb, wb = col0 // tc, ch0 // tc
    per = nch // split // tc
    out = pl.pallas_call(
        kern, grid=(bsz, nch // tc),
        in_specs=[pl.BlockSpec((1, length, tc), lambda b, j: (b, 0, cb + j)),
                  pl.BlockSpec((3, 3, tc), lambda b, j: (0, 0, wb + j)),
                  pl.BlockSpec((1, tc), lambda b, j: (0, wb + j))],
        out_specs=pl.BlockSpec((1, length, tc), lambda b, j: ((j // per) * bsz + b, 0, j % per)),
        out_shape=jax.ShapeDtypeStruct((split * bsz, length, nch // split), F32),
        compiler_params=_params(("parallel", "parallel")),
        name="short_conv",
    )(p3, conv_w, conv_b.reshape(1, -1))
    return out.reshape(split, bsz, length, nch // split)


def _filt_kernel(z_ref, w1_ref, b1_ref, w2_ref, b2_ref, w3_ref, b3_ref, w4_ref, fr_ref, dl_ref,
                 g_ref, nrm_ref):
    i = pl.program_id(0)
    z = z_ref[...]
    fr = fr_ref[...]
    hdot = lambda a, b: jnp.dot(a, b, precision=HIGHEST, preferred_element_type=F32)
    h = jnp.sin(fr * (hdot(z, w1_ref[...]) + b1_ref[...]))
    h = jnp.sin(fr * (hdot(h, w2_ref[...]) + b2_ref[...]))
    h = jnp.sin(fr * (hdot(h, w3_ref[...]) + b3_ref[...]))
    h = hdot(h, w4_ref[...])
    decay = jnp.exp(-z[:, 0:1] * jnp.abs(dl_ref[...]))
    decay = jnp.concatenate([decay] * HY_ORDER, axis=1)
    half = HY_ORDER * HY_W
    hf = h[:, :half] * decay
    hb = h[:, half:] * decay
    row = lax.broadcasted_iota(jnp.int32, hb.shape, 0) + i * hb.shape[0]
    hb = jnp.where(row > 0, hb, 0.0)
    gp = hf + hb
    gm = hf - hb
    for o in range(HY_ORDER):
        g_ref[o] = gp[:, o * HY_W:(o + 1) * HY_W].astype(BF16)
        g_ref[HY_ORDER + o] = gm[:, o * HY_W:(o + 1) * HY_W].astype(BF16)
    part = jnp.sum(jnp.abs(hf) + jnp.abs(hb), axis=0, keepdims=True)

    @pl.when(i == 0)
    def _():
        nrm_ref[...] = part

    @pl.when(i > 0)
    def _():
        nrm_ref[...] += part


def _hyena_filter_taps(length, w1, b1, w2, b2, w3, b3, w4, freq):
    t = jnp.linspace(0.0, 1.0, length, dtype=F32)[:, None]
    bands = (HY_EMB - 1) // 2
    ang = (2.0 * math.pi / length) * jnp.arange(length, dtype=F32)[:, None]
    f = jnp.linspace(1e-4, bands - 1, bands, dtype=F32)
    z = jnp.concatenate([t, jnp.cos(f * ang), -jnp.sin(f * ang)], axis=-1)
    max_decay = math.log(HY_TARGET) / HY_FAST_DECAY
    min_decay = math.log(HY_TARGET) / HY_SLOW_DECAY
    deltas = jnp.linspace(min_decay, max_decay, HY_W, dtype=F32)[None, :]
    fw = w1.shape[1]
    tl = min(512, length)
    full = lambda shape: pl.BlockSpec(shape, lambda i: (0,) * len(shape))
    return pl.pallas_call(
        _filt_kernel, grid=(length // tl,),
        in_specs=[pl.BlockSpec((tl, HY_EMB), lambda i: (i, 0)),
                  full((HY_EMB, fw)), full((1, fw)), full((fw, fw)), full((1, fw)),
                  full((fw, fw)), full((1, fw)), full((fw, 2 * HY_ORDER * HY_W)), full((1, fw)),
                  full((1, HY_W))],
        out_specs=[pl.BlockSpec((2 * HY_ORDER, tl, HY_W), lambda i: (0, i, 0)),
                   pl.BlockSpec((1, HY_ORDER * HY_W), lambda i: (0, 0))],
        out_shape=[jax.ShapeDtypeStruct((2 * HY_ORDER, length, HY_W), BF16),
                   jax.ShapeDtypeStruct((1, HY_ORDER * HY_W), F32)],
        compiler_params=_params(("arbitrary",)), name="hyena_filter",
    )(z, w1, b1.reshape(1, fw), w2, b2.reshape(1, fw), w3, b3.reshape(1, fw), w4,
      freq.reshape(1, fw), deltas)


def _dft_fwd_kernel(u_ref, c_ref, s_ref, *rest, with_coef):
    if with_coef:
        c1_ref, c2_ref, c4_ref, yr_ref, yi_ref, ub_scr = rest
    else:
        yr_ref, yi_ref, ub_scr = rest

    @pl.when(pl.program_id(1) == 0)
    def _():
        ub_scr[...] = u_ref[0].astype(BF16)

    ub = ub_scr[...]
    a = _dot(c_ref[...], ub)
    b = _dot(s_ref[...], ub)
    if with_coef:
        c2 = c2_ref[...]
        yr_ref[0] = (a * c1_ref[...] + b * c2).astype(BF16)
        yi_ref[0] = (b * c4_ref[...] - a * c2).astype(BF16)
    else:
        yr_ref[0] = a
        yi_ref[0] = b


def _dft_fwd(u3, col_blk, cf, sf, coef=None):
    nb, length, _ = u3.shape
    tm = min(512, length)
    in_specs = [pl.BlockSpec((1, length, HY_W), lambda b, i: (b, 0, col_blk)),
                pl.BlockSpec((tm, length), lambda b, i: (i, 0)),
                pl.BlockSpec((tm, length), lambda b, i: (i, 0))]
    args = [u3, cf, sf]
    if coef is not None:
        in_specs += [pl.BlockSpec((tm, HY_W), lambda b, i: (i, 0))] * 3
        args += list(coef)
    odt = BF16 if coef is not None else F32
    return pl.pallas_call(
        functools.partial(_dft_fwd_kernel, with_coef=coef is not None), grid=(nb, length // tm),
        in_specs=in_specs,
        out_specs=[pl.BlockSpec((1, tm, HY_W), lambda b, i: (b, i, 0))] * 2,
        out_shape=[jax.ShapeDtypeStruct((nb, length, HY_W), odt)] * 2,
        scratch_shapes=[pltpu.VMEM((length, HY_W), BF16)],
        compiler_params=_params(("parallel", "arbitrary")),
        name="hyena_dft_fwd" if coef is not None else "hyena_filter_dft",
    )(*args)


def _dft_inv_kernel(yr_ref, yi_ref, c_ref, s_ref, u_ref, gate_ref, bias_ref, g_ref, o_ref, *, final):
    y = _dot(c_ref[...], yr_ref[0]) + _dot(s_ref[...], yi_ref[0])
    out = gate_ref[0] * (y + bias_ref[...] * u_ref[0])
    if final:
        out = _rms(out, g_ref[...])
    o_ref[0] = out.astype(o_ref.dtype)


def _dft_inv(yr, yi, cf, s_inv, u3, u_blk, gate3, gate_blk, bias, g, final):
    bsz, length, _ = yr.shape
    tm = min(512, length)
    return pl.pallas_call(
        functools.partial(_dft_inv_kernel, final=final), grid=(bsz, length // tm),
        in_specs=[pl.BlockSpec((1, length, HY_W), lambda b, i: (b, 0, 0)),
                  pl.BlockSpec((1, length, HY_W), lambda b, i: (b, 0, 0)),
                  pl.BlockSpec((tm, length), lambda b, i: (i, 0)),
                  pl.BlockSpec((tm, length), lambda b, i: (i, 0)),
                  pl.BlockSpec((1, tm, HY_W), lambda b, i: (b, i, u_blk)),
                  pl.BlockSpec((1, tm, HY_W), lambda b, i: (b, i, gate_blk)),
                  pl.BlockSpec((1, HY_W), lambda b, i: (0, 0)),
                  pl.BlockSpec((1, HY_W), lambda b, i: (0, 0))],
        out_specs=pl.BlockSpec((1, tm, HY_W), lambda b, i: (b, i, 0)),
        out_shape=jax.ShapeDtypeStruct((bsz, length, HY_W), BF16 if final else F32),
        compiler_params=_params(("parallel", "arbitrary")), name="hyena_dft_inv",
    )(yr, yi, cf, s_inv, u3, gate3, bias.reshape(1, HY_W), g.reshape(1, HY_W))


def _hyena_tables(length):
    cf, sf = _dft_tables(length, 2 * length)
    alt = jnp.where(jnp.arange(length) % 2 == 0, 1.0, -1.0).astype(F32)
    sf = sf.at[0, :].set(alt)
    return cf.astype(BF16), sf.astype(BF16), sf.T.astype(BF16)


def _hyena_coefs(length, tables, hy_params):
    cf, sf, _ = tables
    taps, nrm = _hyena_filter_taps(length, *hy_params)
    a, b = _dft_fwd(taps, 0, cf, sf)
    inv = (1.0 / nrm).reshape(HY_ORDER, 1, HY_W)
    n = 2.0 * length
    hr = a[:HY_ORDER] * inv
    hi = -b[HY_ORDER:] * inv
    nyq = b[:HY_ORDER, 0:1, :] * inv
    first = (jnp.arange(length) == 0)[None, :, None]
    c1 = jnp.where(first, hr / n, hr * (2.0 / n))
    c2 = jnp.where(first, 0.0, hi * (2.0 / n))
    c4 = jnp.where(first, nyq / n, hr * (2.0 / n))
    return c1, c2, c4


def _hyena(convh, tables, coefs, hy_bias, hy_g):
    cf, sf, s_inv = tables
    c1, c2, c4 = coefs
    yr, yi = _dft_fwd(convh, 0, cf, sf, (c1[0], c2[0], c4[0]))
    zz = _dft_inv(yr, yi, cf, s_inv, convh, 0, convh, 1, hy_bias[0], hy_g, False)
    yr, yi = _dft_fwd(zz, 0, cf, sf, (c1[1], c2[1], c4[1]))
    return _dft_inv(yr, yi, cf, s_inv, zz, 0, convh, 2, hy_bias[1], hy_g, True)


HY_FAST = 128


def _hy2_tables(length):
    nk = length // HY_FAST
    n2 = 2 * length
    k1 = lax.broadcasted_iota(jnp.int32, (nk, nk), 0)
    s1 = lax.broadcasted_iota(jnp.int32, (nk, nk), 1)
    ang = (((2 * k1 + 1) * s1) % (4 * nk)).astype(F32) * (2.0 * math.pi / (4 * nk))
    f1 = jnp.concatenate([jnp.cos(ang), -jnp.sin(ang)], axis=0)
    shape = (nk, HY_FAST, HY_FAST)
    kk = lax.broadcasted_iota(jnp.int32, shape, 0) + 2 * nk * lax.broadcasted_iota(jnp.int32, shape, 1)
    s2 = lax.broadcasted_iota(jnp.int32, shape, 2)
    phi = (((2 * kk + 1) * s2) % (2 * n2)).astype(F32) * (2.0 * math.pi / (2 * n2))
    c, s = jnp.cos(phi), jnp.sin(phi)
    f2 = jnp.concatenate([jnp.concatenate([c, s], axis=2), jnp.concatenate([-s, c], axis=2)], axis=1)
    f1k = jnp.kron(f1, jnp.eye(HY_SUB, dtype=F32))
    return f1k.astype(BF16), f2.astype(BF16), jnp.swapaxes(f2, 1, 2).astype(BF16), f1k.T.astype(BF16)


HY_SUB = 8


def _hy2_stage1_kernel(x_ref, f1_ref, z_ref):
    nk = x_ref.shape[2]
    tc = x_ref.shape[4]
    for g in range(HY_FAST // HY_SUB):
        rows = slice(g * HY_SUB, (g + 1) * HY_SUB)
        rhs = x_ref[0, 0, :, rows, :].reshape(nk * HY_SUB, tc).astype(BF16)
        z = _dot(f1_ref[...], rhs).astype(BF16)
        z_ref[0, :, :, rows, :] = z.reshape(2, nk, HY_SUB, tc)


def _hy2_stage1(x5, which, f1k, cw=None):
    _, nb, nk, _, call = x5.shape
    cw = call if cw is None else cw
    tc = min(256, cw)
    return pl.pallas_call(
        _hy2_stage1_kernel, grid=(nb, cw // tc),
        in_specs=[pl.BlockSpec((1, 1, nk, HY_FAST, tc), lambda b, j: (which, b, 0, 0, j)),
                  pl.BlockSpec(f1k.shape, lambda b, j: (0, 0))],
        out_specs=pl.BlockSpec((1, 2, nk, HY_FAST, tc), lambda b, j: (b, 0, 0, 0, j)),
        out_shape=jax.ShapeDtypeStruct((nb, 2, nk, HY_FAST, cw), BF16),
        compiler_params=_params(("parallel", "parallel")), name="hyena2_stage1",
    )(x5, f1k)


def _hy2_stage2_kernel(zr_ref, zi_ref, f2_ref, *rest, filtered):
    if filtered:
        g2_ref, hr_ref, hi_ref, vr_ref, vi_ref = rest
    else:
        vr_ref, vi_ref = rest
    for k in range(zr_ref.shape[2]):
        x = _dot(f2_ref[k], jnp.concatenate([zr_ref[0, 0, k], zi_ref[0, 0, k]], axis=0))
        xr, xi = x[:HY_FAST], x[HY_FAST:]
        if filtered:
            hr, hi = hr_ref[k], hi_ref[k]
            y = jnp.concatenate([xr * hr - xi * hi, xr * hi + xi * hr], axis=0).astype(BF16)
            v = _dot(g2_ref[k], y)
            vr_ref[0, 0, k] = v[:HY_FAST].astype(BF16)
            vi_ref[0, 0, k] = v[HY_FAST:].astype(BF16)
        else:
            vr_ref[0, 0, k] = xr
            vi_ref[0, 0, k] = xi


def _hy2_stage2(z, f2, g2=None, hr=None, hi=None):
    nb, _, nk, _, cw = z.shape
    kb = 8
    filtered = hr is not None
    zspec = lambda part: pl.BlockSpec((1, 1, kb, HY_FAST, cw), lambda k, b: (b, part, k, 0, 0))
    mat = pl.BlockSpec((kb, 2 * HY_FAST, 2 * HY_FAST), lambda k, b: (k, 0, 0))
    in_specs = [zspec(0), zspec(1), mat]
    args = [z, z, f2]
    if filtered:
        in_specs += [mat, pl.BlockSpec((kb, HY_FAST, cw), lambda k, b: (k, 0, 0)),
                     pl.BlockSpec((kb, HY_FAST, cw), lambda k, b: (k, 0, 0))]
        args += [g2, hr, hi]
    return pl.pallas_call(
        functools.partial(_hy2_stage2_kernel, filtered=filtered), grid=(nk // kb, nb),
        in_specs=in_specs, out_specs=[zspec(0), zspec(0)],
        out_shape=[jax.ShapeDtypeStruct((nb, 1, nk, HY_FAST, cw), BF16 if filtered else F32)] * 2,
        compiler_params=_params(("parallel", "parallel")),
        name="hyena2_stage2" if filtered else "hyena2_filter_spectrum",
    )(*args)


def _hy2_inv1_kernel(vr_ref, vi_ref, g1_ref, u_ref, gate_ref, bias_ref, g_ref, o_ref, *, final):
    nk = vr_ref.shape[2]
    cw = vr_ref.shape[4]
    for g in range(vr_ref.shape[3] // HY_SUB):
        rows = slice(g * HY_SUB, (g + 1) * HY_SUB)
        rhs = jnp.concatenate([vr_ref[0, 0, :, rows, :].reshape(nk * HY_SUB, cw),
                               vi_ref[0, 0, :, rows, :].reshape(nk * HY_SUB, cw)], axis=0)
        y = _dot(g1_ref[...], rhs).reshape(nk, HY_SUB, cw)
        out = gate_ref[0, 0, :, rows, :] * (y + bias_ref[...] * u_ref[0, 0, :, rows, :])
        if final:
            out = _rms(out, g_ref[...])
        o_ref[0, :, rows, :] = out.astype(o_ref.dtype)


def _hy2_inv1(vr, vi, g1k, u5, which_u, gate5, which_g, bias, g, final):
    nb, _, nk, _, cw = vr.shape
    tg = 32
    vspec = pl.BlockSpec((1, 1, nk, tg, cw), lambda b, j: (b, 0, 0, j, 0))
    return pl.pallas_call(
        functools.partial(_hy2_inv1_kernel, final=final), grid=(nb, HY_FAST // tg),
        in_specs=[vspec, vspec,
                  pl.BlockSpec(g1k.shape, lambda b, j: (0, 0)),
                  pl.BlockSpec((1, 1, nk, tg, cw), lambda b, j: (which_u, b, 0, j, 0)),
                  pl.BlockSpec((1, 1, nk, tg, cw), lambda b, j: (which_g, b, 0, j, 0)),
                  pl.BlockSpec((1, cw), lambda b, j: (0, 0)),
                  pl.BlockSpec((1, cw), lambda b, j: (0, 0))],
        out_specs=pl.BlockSpec((1, nk, tg, cw), lambda b, j: (b, 0, j, 0)),
        out_shape=jax.ShapeDtypeStruct((nb, nk, HY_FAST, cw), BF16 if final else F32),
        compiler_params=_params(("parallel", "parallel")), name="hyena2_inv1",
    )(vr, vi, g1k, u5, gate5, bias.reshape(1, cw), g.reshape(1, cw))


def _hy2_coefs(length, tables, hy_params):
    f1k, f2, _, _ = tables
    nk = length // HY_FAST
    taps, nrm = _hyena_filter_taps(length, *hy_params)
    z = _hy2_stage1(taps.reshape(1, 2 * HY_ORDER, nk, HY_FAST, HY_W), 0, f1k)
    xr, xi = _hy2_stage2(z, f2)
    scale = (1.0 / length) / nrm.reshape(HY_ORDER, 1, 1, HY_W)
    return xr[:HY_ORDER, 0] * scale, xi[HY_ORDER:, 0] * scale


def _hyena2(conv3, tables, coefs, hy_bias, hy_g):
    f1k, f2, g2, g1k = tables
    hr, hi = coefs
    _, bsz, length, cw = conv3.shape
    nk = length // HY_FAST
    c5 = conv3.reshape(3, bsz, nk, HY_FAST, cw)

    def conv(x5, which, order, gate_idx, final):
        vr, vi = _hy2_stage2(_hy2_stage1(x5, which, f1k), f2, g2, hr[order], hi[order])
        return _hy2_inv1(vr, vi, g1k, x5, which, c5, gate_idx, hy_bias[order], hy_g, final)

    zz = conv(c5, 0, 0, 1, False)
    return conv(zz[None], 0, 1, 2, True).reshape(bsz, length, cw)


def _four2_tables(length):
    nk = length // HY_FAST
    k1 = lax.broadcasted_iota(jnp.int32, (nk, nk), 0)
    s1 = lax.broadcasted_iota(jnp.int32, (nk, nk), 1)
    ang = ((k1 * s1) % nk).astype(F32) * (2.0 * math.pi / nk)
    f1 = jnp.concatenate([jnp.cos(ang), -jnp.sin(ang)], axis=0)
    shape = (nk, HY_FAST, HY_FAST)
    kk = lax.broadcasted_iota(jnp.int32, shape, 0) + nk * lax.broadcasted_iota(jnp.int32, shape, 1)
    s2 = lax.broadcasted_iota(jnp.int32, shape, 2)
    phi = ((kk * s2) % length).astype(F32) * (2.0 * math.pi / length)
    c, s = jnp.cos(phi), jnp.sin(phi)
    f2 = jnp.concatenate([jnp.concatenate([c, s], axis=2), jnp.concatenate([-s, c], axis=2)], axis=1)
    cc, sc = _cos_sin_table(FOUR_GW, FOUR_GW, FOUR_GW)
    eye = jnp.eye(FOUR_GROUPS, dtype=F32)
    return (jnp.kron(f1, jnp.eye(HY_SUB, dtype=F32)).astype(BF16), f2.astype(BF16),
            jnp.kron(eye, cc).astype(BF16), jnp.kron(eye, sc).astype(BF16))


def _four2_stage2_kernel(zr_ref, zi_ref, f2_ref, bc_ref, bs_ref, g_ref, o_ref, *, scale):
    for k in range(zr_ref.shape[2]):
        x = _dot(f2_ref[k], jnp.concatenate([zr_ref[0, 0, k], zi_ref[0, 0, k]], axis=0))
        xr, xi = x[:HY_FAST].astype(BF16), x[HY_FAST:].astype(BF16)
        z = (_dot(xr, bc_ref[...]) + _dot(xi, bs_ref[...])) * scale
        o_ref[0, k] = _rms(z, g_ref[...]).astype(o_ref.dtype)


def _fourier2(p3, tables, four_g):
    f1k, f2, bc, bs = tables
    bsz, length, pw = p3.shape
    nk = length // HY_FAST
    z = _hy2_stage1(p3.reshape(1, bsz, nk, HY_FAST, pw), 0, f1k, cw=FOUR_W)
    kb = 8
    zspec = lambda part: pl.BlockSpec((1, 1, kb, HY_FAST, FOUR_W), lambda k, b: (b, part, k, 0, 0))
    const = lambda shape: pl.BlockSpec(shape, lambda k, b: (0,) * len(shape))
    out = pl.pallas_call(
        functools.partial(_four2_stage2_kernel, scale=1.0 / math.sqrt(length * FOUR_GW)),
        grid=(nk // kb, bsz),
        in_specs=[zspec(0), zspec(1),
                  pl.BlockSpec((kb, 2 * HY_FAST, 2 * HY_FAST), lambda k, b: (k, 0, 0)),
                  const((FOUR_W, FOUR_W)), const((FOUR_W, FOUR_W)), const((1, FOUR_W))],
        out_specs=pl.BlockSpec((1, kb, HY_FAST, FOUR_W), lambda k, b: (b, k, 0, 0)),
        out_shape=jax.ShapeDtypeStruct((bsz, nk, HY_FAST, FOUR_W), BF16),
        compiler_params=_params(("parallel", "parallel")), name="fourier2_stage2",
    )(z, z, f2, bc, bs, four_g.reshape(1, FOUR_W))
    return jnp.swapaxes(out, 1, 2).reshape(bsz, length, FOUR_W)


def _softplus(v):
    return jnp.maximum(v, 0.0) + jnp.log(1.0 + jnp.exp(-jnp.abs(v)))


def _delta_prep_kernel(qkv_ref, ab_ref, abt_ref, arow_ref, drow_ref, acol_ref, dcol_ref,
                       wq_ref, u_ref, ak_ref, egl_ref):
    c = ab_ref.shape[1]
    nh = DN_HEADS
    ri = lax.broadcasted_iota(jnp.int32, (c, c), 0)
    ci = lax.broadcasted_iota(jnp.int32, (c, c), 1)
    incl = (ri >= ci, ri <= ci)
    strict = (ri > ci, ri < ci)
    low = jnp.where(incl[0], 1.0, 0.0)
    upp = jnp.where(incl[1], 1.0, 0.0)
    hdot = lambda a, b: jnp.dot(a, b, precision=HIGHEST, preferred_element_type=F32)
    ab = ab_ref[0]
    abt = abt_ref[0, 0]
    gate = -arow_ref[...] * _softplus(ab + drow_ref[...])
    gate_t = -acol_ref[...] * _softplus(abt + dcol_ref[...])
    beta_all = jax.nn.sigmoid(ab)
    gcum = (hdot(low, gate), hdot(upp, gate))
    gcum_t = (hdot(gate_t, upp), hdot(gate_t, low))
    nt = (((1,), (1,)), ((), ()))
    def l2n(t):
        return t * lax.rsqrt(jnp.sum(t * t, axis=-1, keepdims=True) + EPS)

    q = [l2n(_silu(qkv_ref[0, :, h * DN_DK:(h + 1) * DN_DK])) * (DN_DK ** -0.5) for h in range(nh)]
    k = [l2n(_silu(qkv_ref[0, :, DN_W + h * DN_DK:DN_W + (h + 1) * DN_DK])) for h in range(nh)]
    v = [_silu(qkv_ref[0, :, 2 * DN_W + h * DN_DV:2 * DN_W + (h + 1) * DN_DV]) for h in range(nh)]
    k16 = [t.astype(BF16) for t in k]
    kk0 = [lax.dot_general(k16[h], k16[h], nt, preferred_element_type=F32) for h in range(nh)]
    qk0 = [lax.dot_general(q[h].astype(BF16), k16[h], nt, preferred_element_type=F32) for h in range(nh)]
    units = [(d, h) for d in range(2) for h in range(nh)]
    mm, rr = {}, {}
    for d, h in units:
        col = d * nh + h
        gc = gcum[d][:, col:col + 1]
        gr = gcum_t[d][col:col + 1, :]
        be = beta_all[:, 2 * nh + col:2 * nh + col + 1]
        last = 0 if d else c - 1
        gl = gc[last:last + 1, :]
        dec = jnp.where(incl[d], jnp.exp(jnp.where(incl[d], gc - gr, 0.0)), 0.0)
        mm[d, h] = jnp.where(strict[d], be * kk0[h] * dec, 0.0)
        ak_ref[0, 0, d, h, 0:c, :] = jnp.where(incl[d], qk0[h] * dec, 0.0).astype(BF16)
        eg = jnp.exp(gc)
        rr[d, h] = jnp.concatenate([v[h] * be, k[h] * (be * eg)], axis=1)
        wq_ref[0, 0, d, h, c:2 * c, :] = (q[h] * eg).astype(BF16)
        k_tail = k[h] * jnp.exp(gl - gc)
        ak_ref[0, 0, d, h, c:c + DN_DK, :] = jnp.transpose(k_tail).astype(BF16)
        egl_ref[0, 0, d, h:h + 1, :] = jnp.broadcast_to(jnp.exp(gl), (1, DN_DV))
    sb = DN_SOLVE_BLOCK
    in_sb = (ri // sb) == (ci // sb)
    in_2sb = (ri // (2 * sb)) == (ci // (2 * sb))
    eye = jnp.where(ri == ci, 1.0, 0.0)
    b16 = lambda t: t.astype(BF16)
    nj = {u: jnp.where(in_sb, -mm[u], 0.0) for u in units}
    inv = {u: eye + nj[u] for u in units}
    for j in range((sb - 1).bit_length() - 1):
        nj = {u: _dot(b16(nj[u]), b16(nj[u])) for u in units}
        inv = {u: inv[u] + _dot(b16(inv[u]), b16(nj[u])) for u in units}
    off = {u: b16(jnp.where(in_2sb & ~in_sb, mm[u], 0.0)) for u in units}
    tmp = {u: _dot(off[u], b16(inv[u])) for u in units}
    inv = {u: inv[u] - _dot(b16(inv[u]), b16(tmp[u])) for u in units}
    inv16 = {u: b16(inv[u]) for u in units}
    off = {u: b16(jnp.where(in_2sb, 0.0, mm[u])) for u in units}
    part = {u: _dot(inv16[u], b16(rr[u])) for u in units}
    tmp = {u: _dot(off[u], b16(part[u])) for u in units}
    rr = {u: part[u] - _dot(inv16[u], b16(tmp[u])) for u in units}
    for d, h in units:
        u_ref[0, 0, d, h] = rr[d, h][:, :DN_DV]
        wq_ref[0, 0, d, h, 0:c, :] = rr[d, h][:, DN_DV:].astype(BF16)


def _delta_scan_kernel(wqf_ref, uf_ref, akf_ref, egf_ref, wqb_ref, ub_ref, akb_ref, egb_ref, s0f_ref, s0b_ref,
                       of_ref, ob_ref, sf_ref, sb_ref, s_scr):
    i = pl.program_id(1)
    bpb = uf_ref.shape[0]
    c = uf_ref.shape[-2]

    @pl.when(i == 0)
    def _():
        s_scr[0] = s0f_ref[...]
        s_scr[1] = s0b_ref[...]

    refs = ((wqf_ref, uf_ref, akf_ref, egf_ref, of_ref), (wqb_ref, ub_ref, akb_ref, egb_ref, ob_ref))
    units = [(d, bb, h) for d in range(2) for bb in range(bpb) for h in range(DN_HEADS)]
    s = {u: s_scr[u] for u in units}
    ws = {(d, bb, h): _dot(refs[d][0][bb, 0, 0, h], s[d, bb, h].astype(BF16)) for d, bb, h in units}
    v16 = {(d, bb, h): (refs[d][1][bb, 0, 0, h] - ws[d, bb, h][:c]).astype(BF16) for d, bb, h in units}
    av = {(d, bb, h): _dot(refs[d][2][bb, 0, 0, h], v16[d, bb, h]) for d, bb, h in units}
    for d, bb, h in units:
        u = (d, bb, h)
        refs[d][4][bb, :, h * DN_DV:(h + 1) * DN_DV] = (ws[u][c:] + av[u][:c]).astype(BF16)
        s_scr[u] = s[u] * refs[d][3][bb, 0, 0, h:h + 1, :] + av[u][c:]

    @pl.when(i == pl.num_programs(1) - 1)
    def _():
        sf_ref[...] = s_scr[0]
        sb_ref[...] = s_scr[1]


def _delta_rule(qkv, ab, a_log, dt_bias, s0_f, s0_b):
    bsz, length, _ = qkv.shape
    bpb = 4 if bsz % 4 == 0 else 1
    c = DN_CHUNK
    n = length // c
    nh = DN_HEADS
    abt = jnp.swapaxes(ab.reshape(bsz, n, c, 4 * nh), 2, 3)
    zeros = jnp.zeros((2 * nh,), F32)
    a_vec = jnp.concatenate([jnp.exp(a_log.astype(F32)).reshape(-1), zeros])
    d_vec = jnp.concatenate([dt_bias.astype(F32).reshape(-1), zeros])
    small = lambda shape: pl.BlockSpec(shape, lambda b, i: (0,) * len(shape))
    per_chunk = lambda *tail: pl.BlockSpec((1, 1, 2, nh) + tail, lambda b, i: (b, i, 0, 0) + (0,) * len(tail))
    wq, u, ak, egl = pl.pallas_call(
        _delta_prep_kernel, grid=(bsz, n),
        in_specs=[pl.BlockSpec((1, c, 3 * DN_W), lambda b, i: (b, i, 0)),
                  pl.BlockSpec((1, c, 4 * nh), lambda b, i: (b, i, 0)),
                  pl.BlockSpec((1, 1, 4 * nh, c), lambda b, i: (b, i, 0, 0)),
                  small((1, 4 * nh)), small((1, 4 * nh)), small((4 * nh, 1)), small((4 * nh, 1))],
        out_specs=[per_chunk(2 * c, DN_DK), per_chunk(c, DN_DV), per_chunk(c + DN_DK, c),
                   pl.BlockSpec((1, 1, 2, nh, DN_DV), lambda b, i: (b, i, 0, 0, 0))],
        out_shape=[jax.ShapeDtypeStruct((bsz, n, 2, nh, 2 * c, DN_DK), BF16),
                   jax.ShapeDtypeStruct((bsz, n, 2, nh, c, DN_DV), F32),
                   jax.ShapeDtypeStruct((bsz, n, 2, nh, c + DN_DK, c), BF16),
                   jax.ShapeDtypeStruct((bsz, n, 2, nh, DN_DV), F32)],
        compiler_params=_params(("parallel", "parallel")), name="delta_prep",
    )(qkv, ab, abt, a_vec.reshape(1, -1), d_vec.reshape(1, -1), a_vec.reshape(-1, 1), d_vec.reshape(-1, 1))

    def side(d, *tail):
        idx = (lambda b, i: (b, n - 1 - i, 1, 0) + (0,) * len(tail)) if d else (
            lambda b, i: (b, i, 0, 0) + (0,) * len(tail))
        return pl.BlockSpec((bpb, 1, 1, nh) + tail, idx)

    def side_specs(d):
        return [side(d, 2 * c, DN_DK), side(d, c, DN_DV), side(d, c + DN_DK, c),
                pl.BlockSpec((bpb, 1, 1, nh, DN_DV),
                             (lambda b, i: (b, n - 1 - i, 1, 0, 0)) if d else (lambda b, i: (b, i, 0, 0, 0)))]

    st_spec = pl.BlockSpec((bpb, nh, DN_DK, DN_DV), lambda b, i: (b, 0, 0, 0))
    o_shape = jax.ShapeDtypeStruct((bsz, length, DN_W), BF16)
    s_shape = jax.ShapeDtypeStruct((bsz, nh, DN_DK, DN_DV), F32)
    return pl.pallas_call(
        _delta_scan_kernel, grid=(bsz // bpb, n),
        in_specs=side_specs(0) + side_specs(1) + [st_spec, st_spec],
        out_specs=[pl.BlockSpec((bpb, c, DN_W), lambda b, i: (b, i, 0)),
                   pl.BlockSpec((bpb, c, DN_W), lambda b, i: (b, n - 1 - i, 0)), st_spec, st_spec],
        out_shape=[o_shape, o_shape, s_shape, s_shape],
        scratch_shapes=[pltpu.VMEM((2, bpb, nh, DN_DK, DN_DV), F32)],
        compiler_params=_params(("parallel", "arbitrary")), name="delta_scan",
    )(wq, u, ak, egl, wq, u, ak, egl, s0_f, s0_b)


def _proj_out_kernel(yf_ref, yh_ref, of_ref, ob_ref, z_ref, dg_ref, w_ref, res_ref, gate_ref, o_ref, a_scr):
    @pl.when(pl.program_id(1) == 0)
    def _():
        a_scr[:, 0:FOUR_W] = yf_ref[...]
        a_scr[:, FOUR_W:FOUR_W + HY_W] = yh_ref[...]
        o = of_ref[...].astype(F32) + ob_ref[...].astype(F32)
        z = z_ref[...].astype(F32)
        for h in range(DN_HEADS):
            lanes = slice(h * DN_DV, (h + 1) * DN_DV)
            y = _rms(o[:, lanes], dg_ref[...]) * _silu(z[:, lanes])
            a_scr[:, FOUR_W + HY_W + h * DN_DV:FOUR_W + HY_W + (h + 1) * DN_DV] = y.astype(BF16)

    o_ref[...] = res_ref[...] + gate_ref[0] * _dot(a_scr[...], w_ref[...])


def _proj_out(y_four, y_hy, o_f, o_b, p2d, dn_g, w, res, gate, rows_per_mod):
    m, d = res.shape
    tm = min(512, m)
    tn = d
    per = rows_per_mod // tm
    zb = PCOL_Z // DN_W
    return pl.pallas_call(
        _proj_out_kernel, grid=(m // tm, d // tn),
        in_specs=[pl.BlockSpec((tm, FOUR_W), lambda i, j: (i, 0)),
                  pl.BlockSpec((tm, HY_W), lambda i, j: (i, 0)),
                  pl.BlockSpec((tm, DN_W), lambda i, j: (i, 0)),
                  pl.BlockSpec((tm, DN_W), lambda i, j: (i, 0)),
                  pl.BlockSpec((tm, DN_W), lambda i, j: (i, zb)),
                  pl.BlockSpec((1, DN_DV), lambda i, j: (0, 0)),
                  pl.BlockSpec((w.shape[0], tn), lambda i, j: (0, j)),
                  pl.BlockSpec((tm, tn), lambda i, j: (i, j)),
                  pl.BlockSpec((1, 1, tn), lambda i, j: (i // per, 0, j))],
        out_specs=pl.BlockSpec((tm, tn), lambda i, j: (i, j)),
        out_shape=jax.ShapeDtypeStruct((m, d), F32),
        scratch_shapes=[pltpu.VMEM((tm, w.shape[0]), BF16)],
        compiler_params=_params(("parallel", "arbitrary")), name="proj_out",
    )(y_four, y_hy, o_f, o_b, p2d, dn_g.reshape(1, DN_DV), w, res, gate)


def _pack_rows(v):
    half = v.shape[1] // 2
    bits = pltpu.bitcast(v.astype(BF16).astype(F32), jnp.uint32)
    return (bits[:, :half] >> 16) | (bits[:, half:] & jnp.uint32(0xFFFF0000))


def _unpack_rows(u, dtype):
    lo = pltpu.bitcast(u << 16, F32).astype(dtype)
    hi = pltpu.bitcast(u & jnp.uint32(0xFFFF0000), F32).astype(dtype)
    return jnp.concatenate([lo, hi], axis=1)


def _select_experts(lg):
    lane = lax.broadcasted_iota(jnp.int32, lg.shape, 1)
    neg = -jnp.inf
    first = lambda mask: jnp.min(jnp.where(mask, lane, ROUTER_W), axis=1, keepdims=True)
    top = lambda mask: jnp.max(jnp.where(mask, lg, neg), axis=1, keepdims=True)
    coarse = lane < N_GROUPS
    m = top(coarse)
    p_sel = 1.0 / jnp.sum(jnp.where(coarse, jnp.exp(lg - m), 0.0), axis=1, keepdims=True)
    grp = first(coarse & (lg == m))
    lo = N_GROUPS + EXPERTS_PER_GROUP * grp
    fine = (lane >= lo) & (lane < lo + EXPERTS_PER_GROUP)
    v1 = top(fine)
    i1 = first(fine & (lg == v1))
    rest = fine & (lane != i1)
    v2 = top(rest)
    i2 = first(rest & (lg == v2))
    e2 = jnp.exp(v2 - v1)
    g1 = p_sel / (1.0 + e2)
    ids = jnp.where(lane == 0, i1 - N_GROUPS, jnp.where(lane == 1, i2 - N_GROUPS, 0))
    gates = jnp.where(lane == 0, g1, jnp.where(lane == 1, g1 * e2, 0.0))
    return ids, gates


def _router_kernel(x_ref, c_ref, g_ref, sx_ref, cx_ref, sc_ref, cc_ref, w_ref, b_ref, h_ref, id_ref, gt_ref,
                   *, n_x):
    i = pl.program_id(0)

    def emit(v, shift, scale):
        h = _rms(v, g_ref[...]) * (1.0 + scale) + shift
        h_ref[...] = _pack_rows(h)
        lg = jnp.dot(h, w_ref[...], precision=HIGHEST, preferred_element_type=F32) + b_ref[...]
        id_ref[...], gt_ref[...] = _select_experts(lg)

    @pl.when(i < n_x)
    def _():
        emit(x_ref[...], sx_ref[0], cx_ref[0])

    @pl.when(i >= n_x)
    def _():
        emit(c_ref[...], sc_ref[0], cc_ref[0])


def _router(x2d, c2d, g, shift_x, scale_x, shift_c, scale_c, w_r, b_r, rows_per_mod, with_ctx):
    mx, d = x2d.shape
    tm = 512
    n_x = mx // tm
    n_c = c2d.shape[0] // tm if with_ctx else 0
    per = rows_per_mod // tm
    xi = lambda i: (jnp.minimum(i, n_x - 1), 0)
    ci = lambda i: (jnp.maximum(i - n_x, 0), 0)
    h, ids, gates = pl.pallas_call(
        functools.partial(_router_kernel, n_x=n_x), grid=(n_x + n_c,),
        in_specs=[pl.BlockSpec((tm, d), xi), pl.BlockSpec((tm, d), ci),
                  pl.BlockSpec((1, d), lambda i: (0, 0)),
                  pl.BlockSpec((1, 1, d), lambda i: (jnp.minimum(i, n_x - 1) // per, 0, 0)),
                  pl.BlockSpec((1, 1, d), lambda i: (jnp.minimum(i, n_x - 1) // per, 0, 0)),
                  pl.BlockSpec((1, 1, d), lambda i: (0, 0, 0)),
                  pl.BlockSpec((1, 1, d), lambda i: (0, 0, 0)),
                  pl.BlockSpec((d, ROUTER_W), lambda i: (0, 0)),
                  pl.BlockSpec((1, ROUTER_W), lambda i: (0, 0))],
        out_specs=[pl.BlockSpec((tm, d // 2), lambda i: (i, 0)),
                   pl.BlockSpec((tm, ROUTER_W), lambda i: (i, 0)),
                   pl.BlockSpec((tm, ROUTER_W), lambda i: (i, 0))],
        out_shape=[jax.ShapeDtypeStruct(((n_x + n_c) * tm, d // 2), jnp.uint32),
                   jax.ShapeDtypeStruct(((n_x + n_c) * tm, ROUTER_W), jnp.int32),
                   jax.ShapeDtypeStruct(((n_x + n_c) * tm, ROUTER_W), F32)],
        compiler_params=_params(("parallel",)), name="moe_router",
    )(x2d, c2d, g.reshape(1, d), shift_x, scale_x, shift_c, scale_c, w_r, b_r)
    return h, ids[:, :TOP_K], gates[:, :TOP_K]


def _row_copy(src_hbm, row, dst, slot, r, sem):
    return pltpu.make_async_copy(src_hbm.at[pl.ds(row, 1)], dst.at[slot, pl.ds(r, 1)], sem.at[slot])


def _expert_kernel(be_ref, tok_ref, nb_ref, h_hbm, wg_ref, wu_ref, wd_ref, ys_ref, xbuf, sem, *, bm):
    i = pl.program_id(0)
    nb = nb_ref[0]

    def start(blk, slot):
        def body(r, carry):
            _row_copy(h_hbm, tok_ref[blk * bm + r], xbuf, slot, r, sem).start()
            return carry
        lax.fori_loop(0, bm, body, 0, unroll=8)

    def wait(slot):
        pltpu.make_async_copy(h_hbm.at[pl.ds(0, bm)], xbuf.at[slot], sem.at[slot]).wait()

    @pl.when(i == 0)
    def _():
        start(0, 0)

    @pl.when(i + 1 < nb)
    def _():
        start(i + 1, (i + 1) % 2)

    @pl.when(i < nb)
    def _():
        slot = i % 2
        wait(slot)
        x = _unpack_rows(xbuf[slot], BF16)
        act = (_silu(_dot(x, wg_ref[0, 0])) * _dot(x, wu_ref[0, 0])).astype(BF16)
        ys_ref[...] = _pack_rows(_dot(act, wd_ref[0, 0]))

    @pl.when(i >= nb)
    def _():
        ys_ref[...] = jnp.zeros_like(ys_ref)


def _experts(h, buf_tok, block_e, n_used, wg, wu, wd, layer, bm):
    n_blocks = block_e.shape[0]
    dp = h.shape[1]
    d, ff = wg.shape[2], wg.shape[3]
    grid_spec = pltpu.PrefetchScalarGridSpec(
        num_scalar_prefetch=3, grid=(n_blocks,),
        in_specs=[pl.BlockSpec(memory_space=pl.ANY),
                  pl.BlockSpec((1, 1, d, ff), lambda i, be, tok, nb: (layer, be[i], 0, 0)),
                  pl.BlockSpec((1, 1, d, ff), lambda i, be, tok, nb: (layer, be[i], 0, 0)),
                  pl.BlockSpec((1, 1, ff, d), lambda i, be, tok, nb: (layer, be[i], 0, 0))],
        out_specs=pl.BlockSpec((bm, dp), lambda i, be, tok, nb: (i, 0)),
        scratch_shapes=[pltpu.VMEM((2, bm, dp), jnp.uint32), pltpu.SemaphoreType.DMA((2,))])
    return pl.pallas_call(
        functools.partial(_expert_kernel, bm=bm), grid_spec=grid_spec,
        out_shape=jax.ShapeDtypeStruct((n_blocks * bm, dp), jnp.uint32),
        compiler_params=_params(("arbitrary",)), name="moe_experts",
    )(block_e, buf_tok, n_used, h, wg, wu, wd)


def _combine_kernel(pos_ref, ys_hbm, x_ref, gw_ref, gate_ref, fg_ref, o_ref, ybuf, sem, *, tm, tok0, final):
    i = pl.program_id(0)
    n = pl.num_programs(0)

    def start(blk, slot):
        def body(r, carry):
            t = tok0 + blk * tm + r
            _row_copy(ys_hbm, pos_ref[2 * t], ybuf, slot, r, sem).start()
            _row_copy(ys_hbm, pos_ref[2 * t + 1], ybuf, slot, tm + r, sem).start()
            return carry
        lax.fori_loop(0, tm, body, 0, unroll=8)

    @pl.when(i == 0)
    def _():
        start(0, 0)

    @pl.when(i + 1 < n)
    def _():
        start(i + 1, (i + 1) % 2)

    slot = i % 2

    pltpu.make_async_copy(ys_hbm.at[pl.ds(0, 2 * tm)], ybuf.at[slot], sem.at[slot]).wait()
    gw = gw_ref[...]
    y = (gw[:, 0:1] * _unpack_rows(ybuf[slot, 0:tm, :], F32)
         + gw[:, 1:2] * _unpack_rows(ybuf[slot, tm:2 * tm, :], F32))
    out = x_ref[...] + gate_ref[0] * y
    if final:
        out = _rms(out, fg_ref[...])
    o_ref[...] = out


def _combine(ys, pos, gw, x2d, gate, final_g, tok0, rows_per_mod, final):
    m, d = x2d.shape
    tm = 256
    per = rows_per_mod // tm
    gb = tok0 // tm
    grid_spec = pltpu.PrefetchScalarGridSpec(
        num_scalar_prefetch=1, grid=(m // tm,),
        in_specs=[pl.BlockSpec(memory_space=pl.ANY),
                  pl.BlockSpec((tm, d), lambda i, pos: (i, 0)),
                  pl.BlockSpec((tm, TOP_K), lambda i, pos: (gb + i, 0)),
                  pl.BlockSpec((1, 1, d), lambda i, pos: (i // per, 0, 0)),
                  pl.BlockSpec((1, d), lambda i, pos: (0, 0))],
        out_specs=pl.BlockSpec((tm, d), lambda i, pos: (i, 0)),
        scratch_shapes=[pltpu.VMEM((2, 2 * tm, ys.shape[1]), jnp.uint32), pltpu.SemaphoreType.DMA((2,))])
    return pl.pallas_call(
        functools.partial(_combine_kernel, tm=tm, tok0=tok0, final=final), grid_spec=grid_spec,
        out_shape=jax.ShapeDtypeStruct((m, d), F32),
        compiler_params=_params(("arbitrary",)), name="moe_combine",
    )(pos, ys, x2d, gw, gate, final_g.reshape(1, d))


def _rank_kernel(e_ref, rank_ref, cnt_ref, run_scr):
    @pl.when(pl.program_id(0) == 0)
    def _():
        run_scr[...] = jnp.zeros_like(run_scr)

    tb = e_ref.shape[0]
    lane = lax.broadcasted_iota(jnp.int32, (tb, LANE), 1)
    onehot = jnp.where(e_ref[...] == lane, 1.0, 0.0)
    ri = lax.broadcasted_iota(jnp.int32, (tb, tb), 0)
    ci = lax.broadcasted_iota(jnp.int32, (tb, tb), 1)
    before = jnp.where(ri > ci, 1.0, 0.0).astype(BF16)
    prefix = _dot(before, onehot.astype(BF16)) + run_scr[...]
    rank_ref[...] = jnp.sum(prefix * onehot, axis=1, keepdims=True).astype(jnp.int32)
    run_scr[...] += jnp.sum(onehot, axis=0, keepdims=True)
    cnt_ref[...] = run_scr[...]


def _expert_ranks(flat_e):
    a = flat_e.shape[0]
    tb = 512
    rank, cnt = pl.pallas_call(
        _rank_kernel, grid=(a // tb,),
        in_specs=[pl.BlockSpec((tb, 1), lambda i: (i, 0))],
        out_specs=[pl.BlockSpec((tb, 1), lambda i: (i, 0)), pl.BlockSpec((1, LANE), lambda i: (0, 0))],
        out_shape=[jax.ShapeDtypeStruct((a, 1), jnp.int32), jax.ShapeDtypeStruct((1, LANE), F32)],
        scratch_shapes=[pltpu.VMEM((1, LANE), F32)],
        compiler_params=_params(("arbitrary",)), name="moe_rank",
    )(flat_e.reshape(a, 1))
    return rank.reshape(a), cnt[0, :N_EXPERTS].astype(jnp.int32)


def _dispatch_plan(expert, bm):
    t = expert.shape[0]
    a = t * TOP_K
    flat_e = expert.reshape(a).astype(jnp.int32)
    rank, counts = _expert_ranks(flat_e)
    padded = (counts + bm - 1) // bm * bm
    pad_end = jnp.cumsum(padded)
    pad_start = pad_end - padded
    mine = flat_e[:, None] == jnp.arange(N_EXPERTS, dtype=jnp.int32)[None, :]
    pos = (jnp.sum(jnp.where(mine, pad_start[None, :], 0), axis=1) + rank).astype(jnp.int32)
    n_blocks = a // bm + N_EXPERTS
    flat_tok = jnp.arange(a, dtype=jnp.int32) // TOP_K
    buf_tok = jnp.zeros((n_blocks * bm,), jnp.int32).at[pos].set(flat_tok)
    n_used = (pad_end[-1] // bm).astype(jnp.int32)
    blk = jnp.minimum(jnp.arange(n_blocks, dtype=jnp.int32), n_used - 1) * bm
    block_e = jnp.sum((pad_end[None, :] <= blk[:, None]).astype(jnp.int32), axis=1)
    block_e = jnp.minimum(block_e, N_EXPERTS - 1).astype(jnp.int32)
    return pos, buf_tok, block_e, n_used.reshape(1)


def _mix_stream(p3, ab, rows, width, s0_f, s0_b, four_tabs, hy_tabs, hy_coefs, conv_w, conv_b, four_g,
                hy_bias, hy_g, a_log, dt_bias, with_mixers):
    qkv = _short_conv(p3, conv_w, conv_b, rows, width, OFF_DN, HY_CONV_CH, 3 * DN_W, 512)[0]
    o_f, o_b, s_f, s_b = _delta_rule(qkv, ab, a_log, dt_bias, s0_f, s0_b)
    if not with_mixers:
        return None, None, o_f, o_b, s_f, s_b
    if _hy_two_level(p3.shape[1]):
        y_four = _fourier2(p3, four_tabs, four_g)
    else:
        y_four = _fourier(p3, *four_tabs, four_g)
    if _hy_two_level(p3.shape[1]):
        conv3 = _short_conv(p3, conv_w, conv_b, rows, width, OFF_HY, 0, HY_CONV_CH, 256, split=3)
        y_hy = _hyena2(conv3, hy_tabs, hy_coefs, hy_bias, hy_g)
    else:
        convh = _short_conv(p3, conv_w, conv_b, rows, width, OFF_HY, 0, HY_CONV_CH, 256)[0]
        y_hy = _hyena(convh, hy_tabs, hy_coefs, hy_bias, hy_g)
    return y_four, y_hy, o_f, o_b, s_f, s_b


def _hy_two_level(length):
    return length % (8 * HY_FAST) == 0


def kernel(x, c, ctx, c_ctx, norm1_g, norm2_g, w_mod, b_mod, w_in, conv_w, conv_b, four_g, hy_w1, hy_b1, hy_w2, hy_b2, hy_w3, hy_b3, hy_w4, hy_freq, hy_bias, hy_g, dn_a_log, dn_dt_bias, dn_g, w_out, w_rc, b_rc, w_rf, b_rf, w_e_gate, w_e_up, w_e_down, final_g):
    bsz, length, d = x.shape
    lc = ctx.shape[1]
    depth = w_in.shape[0]
    rows = length // GRID_W
    bm = 256

    mod_all = _modulation(c, c_ctx, w_mod, b_mod)
    four_tables = lambda n: _four2_tables(n) if _hy_two_level(n) else _fourier_tables(n)
    four_x, four_c = four_tables(length), four_tables(lc)
    hy_tables = lambda n: _hy2_tables(n) if _hy_two_level(n) else _hyena_tables(n)
    hy_coefs = lambda n, tabs, prm: (_hy2_coefs if _hy_two_level(n) else _hyena_coefs)(n, tabs, prm)
    hy_x, hy_c = hy_tables(length), hy_tables(lc)
    zeros = jnp.zeros((bsz, DN_HEADS, DN_DK, DN_DV), F32)
    x2 = x.reshape(bsz * length, d)
    c2 = ctx.reshape(bsz * lc, d)
    wg16, wu16, wd16 = w_e_gate.astype(BF16), w_e_up.astype(BF16), w_e_down.astype(BF16)

    for l in range(depth):
        last = l == depth - 1
        mod = [m[:, None, :] for m in jnp.split(mod_all[l, :bsz], 6, axis=-1)]
        modc = [m[:, None, :] for m in jnp.split(mod_all[l, bsz:bsz + 1], 6, axis=-1)]
        wi = w_in[l]
        w_p = jnp.concatenate([wi[:, :OFF_AB], wi[:, OFF_Z:], wi[:, OFF_AB:OFF_Z],
                               jnp.zeros((d, PROJ_W - PCOL_AB - N_AB), wi.dtype)], axis=1).astype(BF16)
        w_o = w_out[l].astype(BF16)
        hy_params = (hy_w1[l], hy_b1[l], hy_w2[l], hy_b2[l], hy_w3[l], hy_b3[l], hy_w4[l], hy_freq[l])
        mix_args = (conv_w[l], conv_b[l], four_g[l], hy_bias[l], hy_g[l], dn_a_log[l], dn_dt_bias[l])

        pc, abc = _proj_in(c2, norm1_g[l], modc[0], modc[1], w_p, bsz * lc)
        pc3 = pc.reshape(bsz, lc, PROJ_W)
        abc = abc[:, :N_AB].reshape(bsz, lc, N_AB)
        coefs_c = None if last else hy_coefs(lc, hy_c, hy_params)
        yf, yh, o_f, o_b, s_f, s_b = _mix_stream(pc3, abc, 1, lc, zeros, zeros, four_c, hy_c, coefs_c,
                                                 *mix_args, with_mixers=not last)
        if not last:
            c2 = _proj_out(yf.reshape(-1, FOUR_W), yh.reshape(-1, HY_W), o_f.reshape(-1, DN_W),
                           o_b.reshape(-1, DN_W), pc, dn_g[l], w_o, c2, modc[2], bsz * lc)

        p, abx = _proj_in(x2, norm1_g[l], mod[0], mod[1], w_p, length)
        p3 = p.reshape(bsz, length, PROJ_W)
        abx = abx[:, :N_AB].reshape(bsz, length, N_AB)
        coefs_x = hy_coefs(length, hy_x, hy_params)
        yf, yh, o_f, o_b, _, _ = _mix_stream(p3, abx, rows, GRID_W, s_f, s_b, four_x, hy_x, coefs_x,
                                             *mix_args, with_mixers=True)
        x2 = _proj_out(yf.reshape(-1, FOUR_W), yh.reshape(-1, HY_W), o_f.reshape(-1, DN_W),
                       o_b.reshape(-1, DN_W), p, dn_g[l], w_o, x2, mod[2], length)

        w_r = jnp.concatenate([w_rc[l], w_rf[l], jnp.zeros((d, ROUTER_W - N_GROUPS - N_EXPERTS), F32)], axis=1)
        b_r = jnp.concatenate([b_rc[l], b_rf[l], jnp.zeros((ROUTER_W - N_GROUPS - N_EXPERTS,), F32)])[None, :]
        h, expert, gate = _router(x2, c2, norm2_g[l], mod[3], mod[4], modc[3], modc[4], w_r, b_r, length,
                                  with_ctx=not last)
        pos, buf_tok, block_e, n_used = _dispatch_plan(expert, bm)
        ys = _experts(h, buf_tok, block_e, n_used, wg16, wu16, wd16, l, bm)
        x2 = _combine(ys, pos, gate, x2, mod[5], final_g, 0, length, final=last)
        if not last:
            c2 = _combine(ys, pos, gate, c2, modc[5], final_g, bsz * length, bsz * lc, final=False)
    return x2.reshape(bsz, length, d)
```

```python
import functools
import math

import jax
import jax.numpy as jnp
from jax import lax
from jax.experimental import pallas as pl
from jax.experimental.pallas import tpu as pltpu

F32 = jnp.float32
BF16 = jnp.bfloat16
HIGHEST = lax.Precision.HIGHEST

GRID_W = 64
FOUR_W = 512
FOUR_GROUPS = 4
FOUR_GW = FOUR_W // FOUR_GROUPS
HY_W = 512
HY_ORDER = 2
DN_HEADS = 8
DN_DK = 128
DN_DV = 128
DN_W = DN_HEADS * DN_DV
DN_CHUNK = 64
DN_SOLVE_BLOCK = DN_CHUNK // 4
HY_CONV_CH = (HY_ORDER + 1) * HY_W
OFF_HY = FOUR_W
OFF_DN = OFF_HY + HY_CONV_CH
OFF_AB = OFF_DN + 3 * DN_W
N_AB = 4 * DN_HEADS
OFF_Z = OFF_AB + N_AB
HY_EMB = 33
HY_FAST_DECAY = 0.3
HY_SLOW_DECAY = 1.5
HY_TARGET = 1e-2
N_GROUPS = 4
EXPERTS_PER_GROUP = 8
N_EXPERTS = N_GROUPS * EXPERTS_PER_GROUP
TOP_K = 2
EPS = 1e-6

LANE = 128
PCOL_Z = OFF_AB
PCOL_AB = PCOL_Z + DN_W
MXU_W = 256
PROJ_TN = 5 * MXU_W
PROJ_W = -(-(PCOL_AB + LANE) // PROJ_TN) * PROJ_TN
ROUTER_W = LANE
VMEM_LIMIT = 56 * 1024 * 1024


def _params(semantics):
    return pltpu.CompilerParams(dimension_semantics=semantics, vmem_limit_bytes=VMEM_LIMIT)


def _silu(v):
    return v * jax.nn.sigmoid(v)


def _rms(v, g):
    return v * lax.rsqrt(jnp.mean(v * v, axis=-1, keepdims=True) + EPS) * g


def _dot(a, b):
    return jnp.dot(a, b, preferred_element_type=F32)


def _mod_kernel(a_ref, w_ref, b_ref, o_ref):
    o_ref[0] = jnp.dot(_silu(a_ref[...]), w_ref[0], precision=HIGHEST,
                       preferred_element_type=F32) + b_ref[0]


def _modulation(c, c_ctx, w_mod, b_mod):
    depth, d, n = w_mod.shape
    bsz = c.shape[0]
    rows = -(-(bsz + 1) // 8) * 8
    a = jnp.concatenate([c, c_ctx[None], jnp.zeros((rows - bsz - 1, d), F32)], axis=0)
    tn = 1024 if n % 1024 == 0 else 512
    assert n % tn == 0
    return pl.pallas_call(
        _mod_kernel, grid=(depth, n // tn),
        in_specs=[pl.BlockSpec((rows, d), lambda l, j: (0, 0)),
                  pl.BlockSpec((1, d, tn), lambda l, j: (l, 0, j)),
                  pl.BlockSpec((1, 1, tn), lambda l, j: (l, 0, j))],
        out_specs=pl.BlockSpec((1, rows, tn), lambda l, j: (l, 0, j)),
        out_shape=jax.ShapeDtypeStruct((depth, rows, n), F32),
        compiler_params=_params(("parallel", "parallel")), name="modulation",
    )(a, w_mod, b_mod.reshape(depth, 1, n))


def _proj_in_kernel(x_ref, g_ref, shift_ref, scale_ref, w_ref, o_ref, ab_ref, a_scr, *, ab_off):
    j = pl.program_id(1)

    @pl.when(j == 0)
    def _():
        y = _rms(x_ref[...], g_ref[...])
        a_scr[...] = (y * (1.0 + scale_ref[0]) + shift_ref[0]).astype(BF16)

    acc = _dot(a_scr[...], w_ref[...])
    o_ref[...] = acc.astype(BF16)

    @pl.when(j == pl.num_programs(1) - 1)
    def _():
        ab_ref[...] = acc[:, ab_off:ab_off + LANE]


def _proj_in(x2d, g, shift, scale, w, rows_per_mod):
    m, d = x2d.shape
    n = w.shape[1]
    tm = min(1024, m, rows_per_mod)
    tn = PROJ_TN
    assert n % tn == 0 and PCOL_AB >= n - tn
    per = rows_per_mod // tm
    return pl.pallas_call(
        functools.partial(_proj_in_kernel, ab_off=PCOL_AB - (n - tn)), grid=(m // tm, n // tn),
        in_specs=[pl.BlockSpec((tm, d), lambda i, j: (i, 0)),
                  pl.BlockSpec((1, d), lambda i, j: (0, 0)),
                  pl.BlockSpec((1, 1, d), lambda i, j: (i // per, 0, 0)),
                  pl.BlockSpec((1, 1, d), lambda i, j: (i // per, 0, 0)),
                  pl.BlockSpec((d, tn), lambda i, j: (0, j))],
        out_specs=[pl.BlockSpec((tm, tn), lambda i, j: (i, j)),
                   pl.BlockSpec((tm, LANE), lambda i, j: (i, 0))],
        out_shape=[jax.ShapeDtypeStruct((m, n), BF16), jax.ShapeDtypeStruct((m, LANE), F32)],
        scratch_shapes=[pltpu.VMEM((tm, d), BF16)],
        compiler_params=_params(("parallel", "arbitrary")), name="proj_in",
    )(x2d, g.reshape(1, d), shift, scale, w)


def _cos_sin_table(nrow, ncol, period):
    i = lax.broadcasted_iota(jnp.int32, (nrow, ncol), 0)
    j = lax.broadcasted_iota(jnp.int32, (nrow, ncol), 1)
    ang = ((i * j) % period).astype(F32) * (2.0 * math.pi / period)
    return jnp.cos(ang), jnp.sin(ang)


def _dft_tables(length, period):
    blk = 64 if length % 64 == 0 else length
    hi_i = lax.broadcasted_iota(jnp.int32, (length // blk, length), 0) * blk
    j = lax.broadcasted_iota(jnp.int32, (length // blk, length), 1)
    ang = ((hi_i * j) % period).astype(F32) * (2.0 * math.pi / period)
    hi_c, hi_s = jnp.cos(ang), jnp.sin(ang)
    lo_c, lo_s = _cos_sin_table(blk, length, period)
    c = hi_c[:, None, :] * lo_c[None, :, :] - hi_s[:, None, :] * lo_s[None, :, :]
    s = hi_s[:, None, :] * lo_c[None, :, :] + hi_c[:, None, :] * lo_s[None, :, :]
    return c.reshape(length, length), s.reshape(length, length)


def _fourier_kernel(u_ref, c_ref, s_ref, bc_ref, bs_ref, g_ref, o_ref, ub_scr, *, scale):
    @pl.when(pl.program_id(1) == 0)
    def _():
        ub_scr[...] = u_ref[0].astype(BF16)

    ub = ub_scr[...]
    p = _dot(c_ref[...], ub).astype(BF16)
    q = _dot(s_ref[...], ub).astype(BF16)
    z = (_dot(p, bc_ref[...]) - _dot(q, bs_ref[...])) * scale
    o_ref[0] = _rms(z, g_ref[...]).astype(o_ref.dtype)


def _fourier(p3, cl, sl, bc, bs, four_g):
    bsz, length, _ = p3.shape
    tm = min(512, length)
    kern = functools.partial(_fourier_kernel, scale=1.0 / math.sqrt(length * FOUR_GW))
    return pl.pallas_call(
        kern, grid=(bsz, length // tm),
        in_specs=[pl.BlockSpec((1, length, FOUR_W), lambda b, i: (b, 0, 0)),
                  pl.BlockSpec((tm, length), lambda b, i: (i, 0)),
                  pl.BlockSpec((tm, length), lambda b, i: (i, 0)),
                  pl.BlockSpec((FOUR_W, FOUR_W), lambda b, i: (0, 0)),
                  pl.BlockSpec((FOUR_W, FOUR_W), lambda b, i: (0, 0)),
                  pl.BlockSpec((1, FOUR_W), lambda b, i: (0, 0))],
        out_specs=pl.BlockSpec((1, tm, FOUR_W), lambda b, i: (b, i, 0)),
        out_shape=jax.ShapeDtypeStruct((bsz, length, FOUR_W), BF16),
        scratch_shapes=[pltpu.VMEM((length, FOUR_W), BF16)],
        compiler_params=_params(("parallel", "arbitrary")), name="fourier",
    )(p3, cl, sl, bc, bs, four_g.reshape(1, FOUR_W))


def _fourier_tables(length):
    cl, sl = _dft_tables(length, length)
    cc, sc = _cos_sin_table(FOUR_GW, FOUR_GW, FOUR_GW)
    eye = jnp.eye(FOUR_GROUPS, dtype=F32)
    return cl.astype(BF16), sl.astype(BF16), jnp.kron(eye, cc).astype(BF16), jnp.kron(eye, sc).astype(BF16)


def _conv_kernel(x_ref, w_ref, b_ref, o_ref, *, rows, width):
    tc = x_ref.shape[2]
    wts = w_ref[...]
    bias = b_ref[...]
    pos = lax.broadcasted_iota(jnp.int32, (width, tc), 0)
    not_first = pos > 0
    not_last = pos < width - 1

    def grid_row(r):
        start = r * width if isinstance(r, int) else pl.multiple_of(r * width, width)
        return x_ref[0, pl.ds(start, width), :].astype(F32)

    def body(r, carry):
        base = pl.multiple_of(r * width, width)
        up, cen = carry
        if rows > 1:
            dn = grid_row(jnp.minimum(r + 1, rows - 1))
            w_up = wts[0] * jnp.where(r > 0, 1.0, 0.0)
            w_dn = wts[2] * jnp.where(r < rows - 1, 1.0, 0.0)
            col = lambda j: up * w_up[j:j + 1, :] + cen * wts[1, j:j + 1, :] + dn * w_dn[j:j + 1, :]
        else:
            dn = cen
            col = lambda j: cen * wts[1, j:j + 1, :]
        left = jnp.where(not_first, pltpu.roll(col(0), 1, 0), 0.0)
        right = jnp.where(not_last, pltpu.roll(col(2), width - 1, 0), 0.0)
        o_ref[0, pl.ds(base, width), :] = left + col(1) + right + bias
        return cen, dn

    first = grid_row(0)
    lax.fori_loop(0, rows, body, (first, first))


def _short_conv(p3, conv_w, conv_b, rows, width, col0, ch0, nch, tc, split=1):
    bsz, length, _ = p3.shape
    kern = functools.partial(_conv_kernel, rows=rows, width=width)
    cb, wb = col0 // tc, ch0 // tc
    per = nch // split // tc
    out = pl.pallas_call(
        kern, grid=(bsz, nch // tc),
        in_specs=[pl.BlockSpec((1, length, tc), lambda b, j: (b, 0, cb + j)),
                  pl.BlockSpec((3, 3, tc), lambda b, j: (0, 0, wb + j)),
                  pl.BlockSpec((1, tc), lambda b, j: (0, wb + j))],
        out_specs=pl.BlockSpec((1, length, tc), lambda b, j: ((j // per) * bsz + b, 0, j % per)),
        out_shape=jax.ShapeDtypeStruct((split * bsz, length, nch // split), F32),
        compiler_params=_params(("parallel", "parallel")),
        name="short_conv",
    )(p3, conv_w, conv_b.reshape(1, -1))
    return out.reshape(split, bsz, length, nch // split)


def _filt_kernel(z_ref, w1_ref, b1_ref, w2_ref, b2_ref, w3_ref, b3_ref, w4_ref, fr_ref, dl_ref,
                 g_ref, nrm_ref):
    i = pl.program_id(0)
    z = z_ref[...]
    fr = fr_ref[...]
    hdot = lambda a, b: jnp.dot(a, b, precision=HIGHEST, preferred_element_type=F32)
    h = jnp.sin(fr * (hdot(z, w1_ref[...]) + b1_ref[...]))
    h = jnp.sin(fr * (hdot(h, w2_ref[...]) + b2_ref[...]))
    h = jnp.sin(fr * (hdot(h, w3_ref[...]) + b3_ref[...]))
    h = hdot(h, w4_ref[...])
    decay = jnp.exp(-z[:, 0:1] * jnp.abs(dl_ref[...]))
    decay = jnp.concatenate([decay] * HY_ORDER, axis=1)
    half = HY_ORDER * HY_W
    hf = h[:, :half] * decay
    hb = h[:, half:] * decay
    row = lax.broadcasted_iota(jnp.int32, hb.shape, 0) + i * hb.shape[0]
    hb = jnp.where(row > 0, hb, 0.0)
    gp = hf + hb
    gm = hf - hb
    for o in range(HY_ORDER):
        g_ref[o] = gp[:, o * HY_W:(o + 1) * HY_W].astype(BF16)
        g_ref[HY_ORDER + o] = gm[:, o * HY_W:(o + 1) * HY_W].astype(BF16)
    part = jnp.sum(jnp.abs(hf) + jnp.abs(hb), axis=0, keepdims=True)

    @pl.when(i == 0)
    def _():
        nrm_ref[...] = part

    @pl.when(i > 0)
    def _():
        nrm_ref[...] += part


def _hyena_filter_taps(length, w1, b1, w2, b2, w3, b3, w4, freq):
    t = jnp.linspace(0.0, 1.0, length, dtype=F32)[:, None]
    bands = (HY_EMB - 1) // 2
    ang = (2.0 * math.pi / length) * jnp.arange(length, dtype=F32)[:, None]
    f = jnp.linspace(1e-4, bands - 1, bands, dtype=F32)
    z = jnp.concatenate([t, jnp.cos(f * ang), -jnp.sin(f * ang)], axis=-1)
    max_decay = math.log(HY_TARGET) / HY_FAST_DECAY
    min_decay = math.log(HY_TARGET) / HY_SLOW_DECAY
    deltas = jnp.linspace(min_decay, max_decay, HY_W, dtype=F32)[None, :]
    fw = w1.shape[1]
    tl = min(512, length)
    full = lambda shape: pl.BlockSpec(shape, lambda i: (0,) * len(shape))
    return pl.pallas_call(
        _filt_kernel, grid=(length // tl,),
        in_specs=[pl.BlockSpec((tl, HY_EMB), lambda i: (i, 0)),
                  full((HY_EMB, fw)), full((1, fw)), full((fw, fw)), full((1, fw)),
                  full((fw, fw)), full((1, fw)), full((fw, 2 * HY_ORDER * HY_W)), full((1, fw)),
                  full((1, HY_W))],
        out_specs=[pl.BlockSpec((2 * HY_ORDER, tl, HY_W), lambda i: (0, i, 0)),
                   pl.BlockSpec((1, HY_ORDER * HY_W), lambda i: (0, 0))],
        out_shape=[jax.ShapeDtypeStruct((2 * HY_ORDER, length, HY_W), BF16),
                   jax.ShapeDtypeStruct((1, HY_ORDER * HY_W), F32)],
        compiler_params=_params(("arbitrary",)), name="hyena_filter",
    )(z, w1, b1.reshape(1, fw), w2, b2.reshape(1, fw), w3, b3.reshape(1, fw), w4,
      freq.reshape(1, fw), deltas)


def _dft_fwd_kernel(u_ref, c_ref, s_ref, *rest, with_coef):
    if with_coef:
        c1_ref, c2_ref, c4_ref, yr_ref, yi_ref, ub_scr = rest
    else:
        yr_ref, yi_ref, ub_scr = rest

    @pl.when(pl.program_id(1) == 0)
    def _():
        ub_scr[...] = u_ref[0].astype(BF16)

    ub = ub_scr[...]
    a = _dot(c_ref[...], ub)
    b = _dot(s_ref[...], ub)
    if with_coef:
        c2 = c2_ref[...]
        yr_ref[0] = (a * c1_ref[...] + b * c2).astype(BF16)
        yi_ref[0] = (b * c4_ref[...] - a * c2).astype(BF16)
    else:
        yr_ref[0] = a
        yi_ref[0] = b


def _dft_fwd(u3, col_blk, cf, sf, coef=None):
    nb, length, _ = u3.shape
    tm = min(512, length)
    in_specs = [pl.BlockSpec((1, length, HY_W), lambda b, i: (b, 0, col_blk)),
                pl.BlockSpec((tm, length), lambda b, i: (i, 0)),
                pl.BlockSpec((tm, length), lambda b, i: (i, 0))]
    args = [u3, cf, sf]
    if coef is not None:
        in_specs += [pl.BlockSpec((tm, HY_W), lambda b, i: (i, 0))] * 3
        args += list(coef)
    odt = BF16 if coef is not None else F32
    return pl.pallas_call(
        functools.partial(_dft_fwd_kernel, with_coef=coef is not None), grid=(nb, length // tm),
        in_specs=in_specs,
        out_specs=[pl.BlockSpec((1, tm, HY_W), lambda b, i: (b, i, 0))] * 2,
        out_shape=[jax.ShapeDtypeStruct((nb, length, HY_W), odt)] * 2,
        scratch_shapes=[pltpu.VMEM((length, HY_W), BF16)],
        compiler_params=_params(("parallel", "arbitrary")),
        name="hyena_dft_fwd" if coef is not None else "hyena_filter_dft",
    )(*args)


def _dft_inv_kernel(yr_ref, yi_ref, c_ref, s_ref, u_ref, gate_ref, bias_ref, g_ref, o_ref, *, final):
    y = _dot(c_ref[...], yr_ref[0]) + _dot(s_ref[...], yi_ref[0])
    out = gate_ref[0] * (y + bias_ref[...] * u_ref[0])
    if final:
        out = _rms(out, g_ref[...])
    o_ref[0] = out.astype(o_ref.dtype)


def _dft_inv(yr, yi, cf, s_inv, u3, u_blk, gate3, gate_blk, bias, g, final):
    bsz, length, _ = yr.shape
    tm = min(512, length)
    return pl.pallas_call(
        functools.partial(_dft_inv_kernel, final=final), grid=(bsz, length // tm),
        in_specs=[pl.BlockSpec((1, length, HY_W), lambda b, i: (b, 0, 0)),
                  pl.BlockSpec((1, length, HY_W), lambda b, i: (b, 0, 0)),
                  pl.BlockSpec((tm, length), lambda b, i: (i, 0)),
                  pl.BlockSpec((tm, length), lambda b, i: (i, 0)),
                  pl.BlockSpec((1, tm, HY_W), lambda b, i: (b, i, u_blk)),
                  pl.BlockSpec((1, tm, HY_W), lambda b, i: (b, i, gate_blk)),
                  pl.BlockSpec((1, HY_W), lambda b, i: (0, 0)),
                  pl.BlockSpec((1, HY_W), lambda b, i: (0, 0))],
        out_specs=pl.BlockSpec((1, tm, HY_W), lambda b, i: (b, i, 0)),
        out_shape=jax.ShapeDtypeStruct((bsz, length, HY_W), BF16 if final else F32),
        compiler_params=_params(("parallel", "arbitrary")), name="hyena_dft_inv",
    )(yr, yi, cf, s_inv, u3, gate3, bias.reshape(1, HY_W), g.reshape(1, HY_W))


def _hyena_tables(length):
    cf, sf = _dft_tables(length, 2 * length)
    alt = jnp.where(jnp.arange(length) % 2 == 0, 1.0, -1.0).astype(F32)
    sf = sf.at[0, :].set(alt)
    return cf.astype(BF16), sf.astype(BF16), sf.T.astype(BF16)


def _hyena_coefs(length, tables, hy_params):
    cf, sf, _ = tables
    taps, nrm = _hyena_filter_taps(length, *hy_params)
    a, b = _dft_fwd(taps, 0, cf, sf)
    inv = (1.0 / nrm).reshape(HY_ORDER, 1, HY_W)
    n = 2.0 * length
    hr = a[:HY_ORDER] * inv
    hi = -b[HY_ORDER:] * inv
    nyq = b[:HY_ORDER, 0:1, :] * inv
    first = (jnp.arange(length) == 0)[None, :, None]
    c1 = jnp.where(first, hr / n, hr * (2.0 / n))
    c2 = jnp.where(first, 0.0, hi * (2.0 / n))
    c4 = jnp.where(first, nyq / n, hr * (2.0 / n))
    return c1, c2, c4


def _hyena(convh, tables, coefs, hy_bias, hy_g):
    cf, sf, s_inv = tables
    c1, c2, c4 = coefs
    yr, yi = _dft_fwd(convh, 0, cf, sf, (c1[0], c2[0], c4[0]))
    zz = _dft_inv(yr, yi, cf, s_inv, convh, 0, convh, 1, hy_bias[0], hy_g, False)
    yr, yi = _dft_fwd(zz, 0, cf, sf, (c1[1], c2[1], c4[1]))
    return _dft_inv(yr, yi, cf, s_inv, zz, 0, convh, 2, hy_bias[1], hy_g, True)


HY_FAST = 128


def _hy2_tables(length):
    nk = length // HY_FAST
    n2 = 2 * length
    k1 = lax.broadcasted_iota(jnp.int32, (nk, nk), 0)
    s1 = lax.broadcasted_iota(jnp.int32, (nk, nk), 1)
    ang = (((2 * k1 + 1) * s1) % (4 * nk)).astype(F32) * (2.0 * math.pi / (4 * nk))
    f1 = jnp.concatenate([jnp.cos(ang), -jnp.sin(ang)], axis=0)
    shape = (nk, HY_FAST, HY_FAST)
    kk = lax.broadcasted_iota(jnp.int32, shape, 0) + 2 * nk * lax.broadcasted_iota(jnp.int32, shape, 1)
    s2 = lax.broadcasted_iota(jnp.int32, shape, 2)
    phi = (((2 * kk + 1) * s2) % (2 * n2)).astype(F32) * (2.0 * math.pi / (2 * n2))
    c, s = jnp.cos(phi), jnp.sin(phi)
    f2 = jnp.concatenate([jnp.concatenate([c, s], axis=2), jnp.concatenate([-s, c], axis=2)], axis=1)
    f1k = jnp.kron(f1, jnp.eye(HY_SUB, dtype=F32))
    return f1k.astype(BF16), f2.astype(BF16), jnp.swapaxes(f2, 1, 2).astype(BF16), f1k.T.astype(BF16)


HY_SUB = 8


def _hy2_stage1_kernel(x_ref, f1_ref, z_ref):
    nk = x_ref.shape[2]
    tc = x_ref.shape[4]
    for g in range(HY_FAST // HY_SUB):
        rows = slice(g * HY_SUB, (g + 1) * HY_SUB)
        rhs = x_ref[0, 0, :, rows, :].reshape(nk * HY_SUB, tc).astype(BF16)
        z = _dot(f1_ref[...], rhs).astype(BF16)
        z_ref[0, :, :, rows, :] = z.reshape(2, nk, HY_SUB, tc)


def _hy2_stage1(x5, which, f1k, cw=None):
    _, nb, nk, _, call = x5.shape
    cw = call if cw is None else cw
    tc = min(256, cw)
    return pl.pallas_call(
        _hy2_stage1_kernel, grid=(nb, cw // tc),
        in_specs=[pl.BlockSpec((1, 1, nk, HY_FAST, tc), lambda b, j: (which, b, 0, 0, j)),
                  pl.BlockSpec(f1k.shape, lambda b, j: (0, 0))],
        out_specs=pl.BlockSpec((1, 2, nk, HY_FAST, tc), lambda b, j: (b, 0, 0, 0, j)),
        out_shape=jax.ShapeDtypeStruct((nb, 2, nk, HY_FAST, cw), BF16),
        compiler_params=_params(("parallel", "parallel")), name="hyena2_stage1",
    )(x5, f1k)


def _hy2_stage2_kernel(zr_ref, zi_ref, f2_ref, *rest, filtered):
    if filtered:
        g2_ref, hr_ref, hi_ref, vr_ref, vi_ref = rest
    else:
        vr_ref, vi_ref = rest
    for k in range(zr_ref.shape[2]):
        x = _dot(f2_ref[k], jnp.concatenate([zr_ref[0, 0, k], zi_ref[0, 0, k]], axis=0))
        xr, xi = x[:HY_FAST], x[HY_FAST:]
        if filtered:
            hr, hi = hr_ref[k], hi_ref[k]
            y = jnp.concatenate([xr * hr - xi * hi, xr * hi + xi * hr], axis=0).astype(BF16)
            v = _dot(g2_ref[k], y)
            vr_ref[0, 0, k] = v[:HY_FAST].astype(BF16)
            vi_ref[0, 0, k] = v[HY_FAST:].astype(BF16)
        else:
            vr_ref[0, 0, k] = xr
            vi_ref[0, 0, k] = xi


def _hy2_stage2(z, f2, g2=None, hr=None, hi=None):
    nb, _, nk, _, cw = z.shape
    kb = 8
    filtered = hr is not None
    zspec = lambda part: pl.BlockSpec((1, 1, kb, HY_FAST, cw), lambda k, b: (b, part, k, 0, 0))
    mat = pl.BlockSpec((kb, 2 * HY_FAST, 2 * HY_FAST), lambda k, b: (k, 0, 0))
    in_specs = [zspec(0), zspec(1), mat]
    args = [z, z, f2]
    if filtered:
        in_specs += [mat, pl.BlockSpec((kb, HY_FAST, cw), lambda k, b: (k, 0, 0)),
                     pl.BlockSpec((kb, HY_FAST, cw), lambda k, b: (k, 0, 0))]
        args += [g2, hr, hi]
    return pl.pallas_call(
        functools.partial(_hy2_stage2_kernel, filtered=filtered), grid=(nk // kb, nb),
        in_specs=in_specs, out_specs=[zspec(0), zspec(0)],
        out_shape=[jax.ShapeDtypeStruct((nb, 1, nk, HY_FAST, cw), BF16 if filtered else F32)] * 2,
        compiler_params=_params(("parallel", "parallel")),
        name="hyena2_stage2" if filtered else "hyena2_filter_spectrum",
    )(*args)


def _hy2_inv1_kernel(vr_ref, vi_ref, g1_ref, u_ref, gate_ref, bias_ref, g_ref, o_ref, *, final):
    nk = vr_ref.shape[2]
    cw = vr_ref.shape[4]
    for g in range(vr_ref.shape[3] // HY_SUB):
        rows = slice(g * HY_SUB, (g + 1) * HY_SUB)
        rhs = jnp.concatenate([vr_ref[0, 0, :, rows, :].reshape(nk * HY_SUB, cw),
                               vi_ref[0, 0, :, rows, :].reshape(nk * HY_SUB, cw)], axis=0)
        y = _dot(g1_ref[...], rhs).reshape(nk, HY_SUB, cw)
        out = gate_ref[0, 0, :, rows, :] * (y + bias_ref[...] * u_ref[0, 0, :, rows, :])
        if final:
            out = _rms(out, g_ref[...])
        o_ref[0, :, rows, :] = out.astype(o_ref.dtype)


def _hy2_inv1(vr, vi, g1k, u5, which_u, gate5, which_g, bias, g, final):
    nb, _, nk, _, cw = vr.shape
    tg = 32
    vspec = pl.BlockSpec((1, 1, nk, tg, cw), lambda b, j: (b, 0, 0, j, 0))
    return pl.pallas_call(
        functools.partial(_hy2_inv1_kernel, final=final), grid=(nb, HY_FAST // tg),
        in_specs=[vspec, vspec,
                  pl.BlockSpec(g1k.shape, lambda b, j: (0, 0)),
                  pl.BlockSpec((1, 1, nk, tg, cw), lambda b, j: (which_u, b, 0, j, 0)),
                  pl.BlockSpec((1, 1, nk, tg, cw), lambda b, j: (which_g, b, 0, j, 0)),
                  pl.BlockSpec((1, cw), lambda b, j: (0, 0)),
                  pl.BlockSpec((1, cw), lambda b, j: (0, 0))],
        out_specs=pl.BlockSpec((1, nk, tg, cw), lambda b, j: (b, 0, j, 0)),
        out_shape=jax.ShapeDtypeStruct((nb, nk, HY_FAST, cw), BF16 if final else F32),
        compiler_params=_params(("parallel", "parallel")), name="hyena2_inv1",
    )(vr, vi, g1k, u5, gate5, bias.reshape(1, cw), g.reshape(1, cw))


def _hy2_coefs(length, tables, hy_params):
    f1k, f2, _, _ = tables
    nk = length // HY_FAST
    taps, nrm = _hyena_filter_taps(length, *hy_params)
    z = _hy2_stage1(taps.reshape(1, 2 * HY_ORDER, nk, HY_FAST, HY_W), 0, f1k)
    xr, xi = _hy2_stage2(z, f2)
    scale = (1.0 / length) / nrm.reshape(HY_ORDER, 1, 1, HY_W)
    return xr[:HY_ORDER, 0] * scale, xi[HY_ORDER:, 0] * scale


def _hyena2(conv3, tables, coefs, hy_bias, hy_g):
    f1k, f2, g2, g1k = tables
    hr, hi = coefs
    _, bsz, length, cw = conv3.shape
    nk = length // HY_FAST
    c5 = conv3.reshape(3, bsz, nk, HY_FAST, cw)

    def conv(x5, which, order, gate_idx, final):
        vr, vi = _hy2_stage2(_hy2_stage1(x5, which, f1k), f2, g2, hr[order], hi[order])
        return _hy2_inv1(vr, vi, g1k, x5, which, c5, gate_idx, hy_bias[order], hy_g, final)

    zz = conv(c5, 0, 0, 1, False)
    return conv(zz[None], 0, 1, 2, True).reshape(bsz, length, cw)


def _four2_tables(length):
    nk = length // HY_FAST
    k1 = lax.broadcasted_iota(jnp.int32, (nk, nk), 0)
    s1 = lax.broadcasted_iota(jnp.int32, (nk, nk), 1)
    ang = ((k1 * s1) % nk).astype(F32) * (2.0 * math.pi / nk)
    f1 = jnp.concatenate([jnp.cos(ang), -jnp.sin(ang)], axis=0)
    shape = (nk, HY_FAST, HY_FAST)
    kk = lax.broadcasted_iota(jnp.int32, shape, 0) + nk * lax.broadcasted_iota(jnp.int32, shape, 1)
    s2 = lax.broadcasted_iota(jnp.int32, shape, 2)
    phi = ((kk * s2) % length).astype(F32) * (2.0 * math.pi / length)
    c, s = jnp.cos(phi), jnp.sin(phi)
    f2 = jnp.concatenate([jnp.concatenate([c, s], axis=2), jnp.concatenate([-s, c], axis=2)], axis=1)
    cc, sc = _cos_sin_table(FOUR_GW, FOUR_GW, FOUR_GW)
    eye = jnp.eye(FOUR_GROUPS, dtype=F32)
    return (jnp.kron(f1, jnp.eye(HY_SUB, dtype=F32)).astype(BF16), f2.astype(BF16),
            jnp.kron(eye, cc).astype(BF16), jnp.kron(eye, sc).astype(BF16))


def _four2_stage2_kernel(zr_ref, zi_ref, f2_ref, bc_ref, bs_ref, g_ref, o_ref, *, scale):
    for k in range(zr_ref.shape[2]):
        x = _dot(f2_ref[k], jnp.concatenate([zr_ref[0, 0, k], zi_ref[0, 0, k]], axis=0))
        xr, xi = x[:HY_FAST].astype(BF16), x[HY_FAST:].astype(BF16)
        z = (_dot(xr, bc_ref[...]) + _dot(xi, bs_ref[...])) * scale
        o_ref[0, k] = _rms(z, g_ref[...]).astype(o_ref.dtype)


def _fourier2(p3, tables, four_g):
    f1k, f2, bc, bs = tables
    bsz, length, pw = p3.shape
    nk = length // HY_FAST
    z = _hy2_stage1(p3.reshape(1, bsz, nk, HY_FAST, pw), 0, f1k, cw=FOUR_W)
    kb = 8
    zspec = lambda part: pl.BlockSpec((1, 1, kb, HY_FAST, FOUR_W), lambda k, b: (b, part, k, 0, 0))
    const = lambda shape: pl.BlockSpec(shape, lambda k, b: (0,) * len(shape))
    out = pl.pallas_call(
        functools.partial(_four2_stage2_kernel, scale=1.0 / math.sqrt(length * FOUR_GW)),
        grid=(nk // kb, bsz),
        in_specs=[zspec(0), zspec(1),
                  pl.BlockSpec((kb, 2 * HY_FAST, 2 * HY_FAST), lambda k, b: (k, 0, 0)),
                  const((FOUR_W, FOUR_W)), const((FOUR_W, FOUR_W)), const((1, FOUR_W))],
        out_specs=pl.BlockSpec((1, kb, HY_FAST, FOUR_W), lambda k, b: (b, k, 0, 0)),
        out_shape=jax.ShapeDtypeStruct((bsz, nk, HY_FAST, FOUR_W), BF16),
        compiler_params=_params(("parallel", "parallel")), name="fourier2_stage2",
    )(z, z, f2, bc, bs, four_g.reshape(1, FOUR_W))
    return jnp.swapaxes(out, 1, 2).reshape(bsz, length, FOUR_W)


def _softplus(v):
    return jnp.maximum(v, 0.0) + jnp.log(1.0 + jnp.exp(-jnp.abs(v)))


def _delta_prep_kernel(qkv_ref, ab_ref, abt_ref, arow_ref, drow_ref, acol_ref, dcol_ref,
                       wq_ref, u_ref, ak_ref, egl_ref):
    nch = abt_ref.shape[1]
    c = ab_ref.shape[1] // nch
    nh = DN_HEADS
    ri = lax.broadcasted_iota(jnp.int32, (c, c), 0)
    ci = lax.broadcasted_iota(jnp.int32, (c, c), 1)
    incl = (ri >= ci, ri <= ci)
    strict = (ri > ci, ri < ci)
    low = jnp.where(incl[0], 1.0, 0.0)
    upp = jnp.where(incl[1], 1.0, 0.0)
    hdot = lambda a, b: jnp.dot(a, b, precision=HIGHEST, preferred_element_type=F32)
    nt = (((1,), (1,)), ((), ()))
    beta_all, gcum, gcum_t = {}, {}, {}
    for cc in range(nch):
        ab = ab_ref[0, cc * c:(cc + 1) * c, :]
        abt = abt_ref[0, cc]
        gate = -arow_ref[...] * _softplus(ab + drow_ref[...])
        gate_t = -acol_ref[...] * _softplus(abt + dcol_ref[...])
        beta_all[cc] = jax.nn.sigmoid(ab)
        gcum[cc] = (hdot(low, gate), hdot(upp, gate))
        gcum_t[cc] = (hdot(gate_t, upp), hdot(gate_t, low))

    def l2n(t):
        return t * lax.rsqrt(jnp.sum(t * t, axis=-1, keepdims=True) + EPS)

    heads = [(cc, h) for cc in range(nch) for h in range(nh)]
    rows = lambda cc: slice(cc * c, (cc + 1) * c)
    q = {(cc, h): l2n(_silu(qkv_ref[0, rows(cc), h * DN_DK:(h + 1) * DN_DK])) * (DN_DK ** -0.5) for cc, h in heads}
    k = {(cc, h): l2n(_silu(qkv_ref[0, rows(cc), DN_W + h * DN_DK:DN_W + (h + 1) * DN_DK])) for cc, h in heads}
    v = {(cc, h): _silu(qkv_ref[0, rows(cc), 2 * DN_W + h * DN_DV:2 * DN_W + (h + 1) * DN_DV]) for cc, h in heads}
    k16 = {u: k[u].astype(BF16) for u in heads}
    kk0 = {u: lax.dot_general(k16[u], k16[u], nt, preferred_element_type=F32) for u in heads}
    qk0 = {u: lax.dot_general(q[u].astype(BF16), k16[u], nt, preferred_element_type=F32) for u in heads}
    units = [(cc, d, h) for cc in range(nch) for d in range(2) for h in range(nh)]
    mm, rr = {}, {}
    for cc, d, h in units:
        col = d * nh + h
        gc = gcum[cc][d][:, col:col + 1]
        gr = gcum_t[cc][d][col:col + 1, :]
        be = beta_all[cc][:, 2 * nh + col:2 * nh + col + 1]
        last = 0 if d else c - 1
        gl = gc[last:last + 1, :]
        dec = jnp.where(incl[d], jnp.exp(jnp.where(incl[d], gc - gr, 0.0)), 0.0)
        mm[cc, d, h] = jnp.where(strict[d], be * kk0[cc, h] * dec, 0.0)
        ak_ref[0, cc, d, h, 0:c, :] = jnp.where(incl[d], qk0[cc, h] * dec, 0.0).astype(BF16)
        eg = jnp.exp(gc)
        rr[cc, d, h] = jnp.concatenate([v[cc, h] * be, k[cc, h] * (be * eg)], axis=1)
        wq_ref[0, cc, d, h, c:2 * c, :] = (q[cc, h] * eg).astype(BF16)
        k_tail = k[cc, h] * jnp.exp(gl - gc)
        ak_ref[0, cc, d, h, c:c + DN_DK, :] = jnp.transpose(k_tail).astype(BF16)
        egl_ref[0, cc, d, h:h + 1, :] = jnp.broadcast_to(jnp.exp(gl), (1, DN_DV))
    sb = DN_SOLVE_BLOCK
    in_sb = (ri // sb) == (ci // sb)
    in_2sb = (ri // (2 * sb)) == (ci // (2 * sb))
    eye = jnp.where(ri == ci, 1.0, 0.0)
    b16 = lambda t: t.astype(BF16)
    nj = {u: jnp.where(in_sb, -mm[u], 0.0) for u in units}
    inv = {u: eye + nj[u] for u in units}
    for j in range((sb - 1).bit_length() - 1):
        nj = {u: _dot(b16(nj[u]), b16(nj[u])) for u in units}
        inv = {u: inv[u] + _dot(b16(inv[u]), b16(nj[u])) for u in units}
    off = {u: b16(jnp.where(in_2sb & ~in_sb, mm[u], 0.0)) for u in units}
    tmp = {u: _dot(off[u], b16(inv[u])) for u in units}
    inv = {u: inv[u] - _dot(b16(inv[u]), b16(tmp[u])) for u in units}
    inv16 = {u: b16(inv[u]) for u in units}
    off = {u: b16(jnp.where(in_2sb, 0.0, mm[u])) for u in units}
    part = {u: _dot(inv16[u], b16(rr[u])) for u in units}
    tmp = {u: _dot(off[u], b16(part[u])) for u in units}
    rr = {u: part[u] - _dot(inv16[u], b16(tmp[u])) for u in units}
    for cc, d, h in units:
        u_ref[0, cc, d, h] = rr[cc, d, h][:, :DN_DV]
        wq_ref[0, cc, d, h, 0:c, :] = rr[cc, d, h][:, DN_DV:].astype(BF16)


def _delta_scan_kernel(wqf_ref, uf_ref, akf_ref, egf_ref, wqb_ref, ub_ref, akb_ref, egb_ref, s0f_ref, s0b_ref,
                       of_ref, ob_ref, sf_ref, sb_ref, s_scr):
    i = pl.program_id(1)
    bpb = uf_ref.shape[0]
    c = uf_ref.shape[-2]

    @pl.when(i == 0)
    def _():
        s_scr[0] = s0f_ref[...]
        s_scr[1] = s0b_ref[...]

    refs = ((wqf_ref, uf_ref, akf_ref, egf_ref, of_ref), (wqb_ref, ub_ref, akb_ref, egb_ref, ob_ref))
    units = [(d, bb, h) for d in range(2) for bb in range(bpb) for h in range(DN_HEADS)]
    s = {u: s_scr[u] for u in units}
    ws = {(d, bb, h): _dot(refs[d][0][bb, 0, 0, h], s[d, bb, h].astype(BF16)) for d, bb, h in units}
    v16 = {(d, bb, h): (refs[d][1][bb, 0, 0, h] - ws[d, bb, h][:c]).astype(BF16) for d, bb, h in units}
    av = {(d, bb, h): _dot(refs[d][2][bb, 0, 0, h], v16[d, bb, h]) for d, bb, h in units}
    for d, bb, h in units:
        u = (d, bb, h)
        refs[d][4][bb, :, h * DN_DV:(h + 1) * DN_DV] = (ws[u][c:] + av[u][:c]).astype(BF16)
        s_scr[u] = s[u] * refs[d][3][bb, 0, 0, h:h + 1, :] + av[u][c:]

    @pl.when(i == pl.num_programs(1) - 1)
    def _():
        sf_ref[...] = s_scr[0]
        sb_ref[...] = s_scr[1]


def _delta_rule(qkv, ab, a_log, dt_bias, s0_f, s0_b):
    bsz, length, _ = qkv.shape
    bpb = 4 if bsz % 4 == 0 else 1
    c = DN_CHUNK
    n = length // c
    nh = DN_HEADS
    abt = jnp.swapaxes(ab.reshape(bsz, n, c, 4 * nh), 2, 3)
    zeros = jnp.zeros((2 * nh,), F32)
    a_vec = jnp.concatenate([jnp.exp(a_log.astype(F32)).reshape(-1), zeros])
    d_vec = jnp.concatenate([dt_bias.astype(F32).reshape(-1), zeros])
    small = lambda shape: pl.BlockSpec(shape, lambda b, i: (0,) * len(shape))
    nch = 2 if n % 2 == 0 else 1
    per_chunk = lambda *tail: pl.BlockSpec((1, nch, 2, nh) + tail, lambda b, i: (b, i, 0, 0) + (0,) * len(tail))
    wq, u, ak, egl = pl.pallas_call(
        _delta_prep_kernel, grid=(bsz, n // nch),
        in_specs=[pl.BlockSpec((1, nch * c, 3 * DN_W), lambda b, i: (b, i, 0)),
                  pl.BlockSpec((1, nch * c, 4 * nh), lambda b, i: (b, i, 0)),
                  pl.BlockSpec((1, nch, 4 * nh, c), lambda b, i: (b, i, 0, 0)),
                  small((1, 4 * nh)), small((1, 4 * nh)), small((4 * nh, 1)), small((4 * nh, 1))],
        out_specs=[per_chunk(2 * c, DN_DK), per_chunk(c, DN_DV), per_chunk(c + DN_DK, c),
                   pl.BlockSpec((1, nch, 2, nh, DN_DV), lambda b, i: (b, i, 0, 0, 0))],
        out_shape=[jax.ShapeDtypeStruct((bsz, n, 2, nh, 2 * c, DN_DK), BF16),
                   jax.ShapeDtypeStruct((bsz, n, 2, nh, c, DN_DV), F32),
                   jax.ShapeDtypeStruct((bsz, n, 2, nh, c + DN_DK, c), BF16),
                   jax.ShapeDtypeStruct((bsz, n, 2, nh, DN_DV), F32)],
        compiler_params=_params(("parallel", "parallel")), name="delta_prep",
    )(qkv, ab, abt, a_vec.reshape(1, -1), d_vec.reshape(1, -1), a_vec.reshape(-1, 1), d_vec.reshape(-1, 1))

    def side(d, *tail):
        idx = (lambda b, i: (b, n - 1 - i, 1, 0) + (0,) * len(tail)) if d else (
            lambda b, i: (b, i, 0, 0) + (0,) * len(tail))
        return pl.BlockSpec((bpb, 1, 1, nh) + tail, idx)

    def side_specs(d):
        return [side(d, 2 * c, DN_DK), side(d, c, DN_DV), side(d, c + DN_DK, c),
                pl.BlockSpec((bpb, 1, 1, nh, DN_DV),
                             (lambda b, i: (b, n - 1 - i, 1, 0, 0)) if d else (lambda b, i: (b, i, 0, 0, 0)))]

    st_spec = pl.BlockSpec((bpb, nh, DN_DK, DN_DV), lambda b, i: (b, 0, 0, 0))
    o_shape = jax.ShapeDtypeStruct((bsz, length, DN_W), BF16)
    s_shape = jax.ShapeDtypeStruct((bsz, nh, DN_DK, DN_DV), F32)
    return pl.pallas_call(
        _delta_scan_kernel, grid=(bsz // bpb, n),
        in_specs=side_specs(0) + side_specs(1) + [st_spec, st_spec],
        out_specs=[pl.BlockSpec((bpb, c, DN_W), lambda b, i: (b, i, 0)),
                   pl.BlockSpec((bpb, c, DN_W), lambda b, i: (b, n - 1 - i, 0)), st_spec, st_spec],
        out_shape=[o_shape, o_shape, s_shape, s_shape],
        scratch_shapes=[pltpu.VMEM((2, bpb, nh, DN_DK, DN_DV), F32)],
        compiler_params=_params(("parallel", "arbitrary")), name="delta_scan",
    )(wq, u, ak, egl, wq, u, ak, egl, s0_f, s0_b)


def _proj_out_kernel(yf_ref, yh_ref, of_ref, ob_ref, z_ref, dg_ref, w_ref, res_ref, gate_ref, o_ref):
    acc = _dot(yf_ref[...], w_ref[0:FOUR_W, :]) + _dot(yh_ref[...], w_ref[FOUR_W:FOUR_W + HY_W, :])
    base = FOUR_W + HY_W
    per = MXU_W // DN_DV
    for h0 in range(0, DN_HEADS, per):
        parts = []
        for h in range(h0, h0 + per):
            lanes = slice(h * DN_DV, (h + 1) * DN_DV)
            o = of_ref[:, lanes].astype(F32) + ob_ref[:, lanes].astype(F32)
            parts.append((_rms(o, dg_ref[...]) * _silu(z_ref[:, lanes].astype(F32))).astype(BF16))
        acc = acc + _dot(jnp.concatenate(parts, axis=1), w_ref[base + h0 * DN_DV:base + (h0 + per) * DN_DV, :])
    o_ref[...] = res_ref[...] + gate_ref[0] * acc


def _proj_out(y_four, y_hy, o_f, o_b, p2d, dn_g, w, res, gate, rows_per_mod):
    m, d = res.shape
    tm = min(512, m)
    tn = d
    per = rows_per_mod // tm
    zb = PCOL_Z // DN_W
    return pl.pallas_call(
        _proj_out_kernel, grid=(m // tm, d // tn),
        in_specs=[pl.BlockSpec((tm, FOUR_W), lambda i, j: (i, 0)),
                  pl.BlockSpec((tm, HY_W), lambda i, j: (i, 0)),
                  pl.BlockSpec((tm, DN_W), lambda i, j: (i, 0)),
                  pl.BlockSpec((tm, DN_W), lambda i, j: (i, 0)),
                  pl.BlockSpec((tm, DN_W), lambda i, j: (i, zb)),
                  pl.BlockSpec((1, DN_DV), lambda i, j: (0, 0)),
                  pl.BlockSpec((w.shape[0], tn), lambda i, j: (0, j)),
                  pl.BlockSpec((tm, tn), lambda i, j: (i, j)),
                  pl.BlockSpec((1, 1, tn), lambda i, j: (i // per, 0, j))],
        out_specs=pl.BlockSpec((tm, tn), lambda i, j: (i, j)),
        out_shape=jax.ShapeDtypeStruct((m, d), F32),
        compiler_params=_params(("parallel", "arbitrary")), name="proj_out",
    )(y_four, y_hy, o_f, o_b, p2d, dn_g.reshape(1, DN_DV), w, res, gate)


def _pack_rows(v):
    half = v.shape[1] // 2
    bits = pltpu.bitcast(v.astype(BF16).astype(F32), jnp.uint32)
    return (bits[:, :half] >> 16) | (bits[:, half:] & jnp.uint32(0xFFFF0000))


def _unpack_rows(u, dtype):
    lo = pltpu.bitcast(u << 16, F32).astype(dtype)
    hi = pltpu.bitcast(u & jnp.uint32(0xFFFF0000), F32).astype(dtype)
    return jnp.concatenate([lo, hi], axis=1)


def _select_experts(lg):
    lane = lax.broadcasted_iota(jnp.int32, lg.shape, 1)
    neg = -jnp.inf
    first = lambda mask: jnp.min(jnp.where(mask, lane, ROUTER_W), axis=1, keepdims=True)
    top = lambda mask: jnp.max(jnp.where(mask, lg, neg), axis=1, keepdims=True)
    coarse = lane < N_GROUPS
    m = top(coarse)
    p_sel = 1.0 / jnp.sum(jnp.where(coarse, jnp.exp(lg - m), 0.0), axis=1, keepdims=True)
    grp = first(coarse & (lg == m))
    lo = N_GROUPS + EXPERTS_PER_GROUP * grp
    fine = (lane >= lo) & (lane < lo + EXPERTS_PER_GROUP)
    v1 = top(fine)
    i1 = first(fine & (lg == v1))
    rest = fine & (lane != i1)
    v2 = top(rest)
    i2 = first(rest & (lg == v2))
    e2 = jnp.exp(v2 - v1)
    g1 = p_sel / (1.0 + e2)
    ids = jnp.where(lane == 0, i1 - N_GROUPS, jnp.where(lane == 1, i2 - N_GROUPS, 0))
    gates = jnp.where(lane == 0, g1, jnp.where(lane == 1, g1 * e2, 0.0))
    return ids, gates


def _router_kernel(x_ref, c_ref, g_ref, sx_ref, cx_ref, sc_ref, cc_ref, w_ref, b_ref, h_ref, id_ref, gt_ref,
                   *, n_x):
    i = pl.program_id(0)

    def emit(v, shift, scale):
        h = _rms(v, g_ref[...]) * (1.0 + scale) + shift
        h_ref[...] = _pack_rows(h)
        lg = jnp.dot(h, w_ref[...], precision=HIGHEST, preferred_element_type=F32) + b_ref[...]
        id_ref[...], gt_ref[...] = _select_experts(lg)

    @pl.when(i < n_x)
    def _():
        emit(x_ref[...], sx_ref[0], cx_ref[0])

    @pl.when(i >= n_x)
    def _():
        emit(c_ref[...], sc_ref[0], cc_ref[0])


def _router(x2d, c2d, g, shift_x, scale_x, shift_c, scale_c, w_r, b_r, rows_per_mod, with_ctx):
    mx, d = x2d.shape
    tm = 512
    n_x = mx // tm
    n_c = c2d.shape[0] // tm if with_ctx else 0
    per = rows_per_mod // tm
    xi = lambda i: (jnp.minimum(i, n_x - 1), 0)
    ci = lambda i: (jnp.maximum(i - n_x, 0), 0)
    h, ids, gates = pl.pallas_call(
        functools.partial(_router_kernel, n_x=n_x), grid=(n_x + n_c,),
        in_specs=[pl.BlockSpec((tm, d), xi), pl.BlockSpec((tm, d), ci),
                  pl.BlockSpec((1, d), lambda i: (0, 0)),
                  pl.BlockSpec((1, 1, d), lambda i: (jnp.minimum(i, n_x - 1) // per, 0, 0)),
                  pl.BlockSpec((1, 1, d), lambda i: (jnp.minimum(i, n_x - 1) // per, 0, 0)),
                  pl.BlockSpec((1, 1, d), lambda i: (0, 0, 0)),
                  pl.BlockSpec((1, 1, d), lambda i: (0, 0, 0)),
                  pl.BlockSpec((d, ROUTER_W), lambda i: (0, 0)),
                  pl.BlockSpec((1, ROUTER_W), lambda i: (0, 0))],
        out_specs=[pl.BlockSpec((tm, d // 2), lambda i: (i, 0)),
                   pl.BlockSpec((tm, ROUTER_W), lambda i: (i, 0)),
                   pl.BlockSpec((tm, ROUTER_W), lambda i: (i, 0))],
        out_shape=[jax.ShapeDtypeStruct(((n_x + n_c) * tm, d // 2), jnp.uint32),
                   jax.ShapeDtypeStruct(((n_x + n_c) * tm, ROUTER_W), jnp.int32),
                   jax.ShapeDtypeStruct(((n_x + n_c) * tm, ROUTER_W), F32)],
        compiler_params=_params(("parallel",)), name="moe_router",
    )(x2d, c2d, g.reshape(1, d), shift_x, scale_x, shift_c, scale_c, w_r, b_r)
    return h, ids[:, :TOP_K], gates[:, :TOP_K]


def _row_copy(src_hbm, row, dst, slot, r, sem):
    return pltpu.make_async_copy(src_hbm.at[pl.ds(row, 1)], dst.at[slot, pl.ds(r, 1)], sem.at[slot])


def _expert_kernel(be_ref, tok_ref, nb_ref, h_hbm, wg_ref, wu_ref, wd_ref, ys_ref, xbuf, sem, *, bm):
    i = pl.program_id(0)
    nb = nb_ref[0]

    def start(blk, slot):
        def body(r, carry):
            _row_copy(h_hbm, tok_ref[blk * bm + r], xbuf, slot, r, sem).start()
            return carry
        lax.fori_loop(0, bm, body, 0, unroll=8)

    def wait(slot):
        pltpu.make_async_copy(h_hbm.at[pl.ds(0, bm)], xbuf.at[slot], sem.at[slot]).wait()

    @pl.when(i == 0)
    def _():
        start(0, 0)

    @pl.when(i + 1 < nb)
    def _():
        start(i + 1, (i + 1) % 2)

    @pl.when(i < nb)
    def _():
        slot = i % 2
        wait(slot)
        x = _unpack_rows(xbuf[slot], BF16)
        act = (_silu(_dot(x, wg_ref[0, 0])) * _dot(x, wu_ref[0, 0])).astype(BF16)
        ys_ref[...] = _pack_rows(_dot(act, wd_ref[0, 0]))

    @pl.when(i >= nb)
    def _():
        ys_ref[...] = jnp.zeros_like(ys_ref)


def _experts(h, buf_tok, block_e, n_used, wg, wu, wd, layer, bm):
    n_blocks = block_e.shape[0]
    dp = h.shape[1]
    d, ff = wg.shape[2], wg.shape[3]
    grid_spec = pltpu.PrefetchScalarGridSpec(
        num_scalar_prefetch=3, grid=(n_blocks,),
        in_specs=[pl.BlockSpec(memory_space=pl.ANY),
                  pl.BlockSpec((1, 1, d, ff), lambda i, be, tok, nb: (layer, be[i], 0, 0)),
                  pl.BlockSpec((1, 1, d, ff), lambda i, be, tok, nb: (layer, be[i], 0, 0)),
                  pl.BlockSpec((1, 1, ff, d), lambda i, be, tok, nb: (layer, be[i], 0, 0))],
        out_specs=pl.BlockSpec((bm, dp), lambda i, be, tok, nb: (i, 0)),
        scratch_shapes=[pltpu.VMEM((2, bm, dp), jnp.uint32), pltpu.SemaphoreType.DMA((2,))])
    return pl.pallas_call(
        functools.partial(_expert_kernel, bm=bm), grid_spec=grid_spec,
        out_shape=jax.ShapeDtypeStruct((n_blocks * bm, dp), jnp.uint32),
        compiler_params=_params(("arbitrary",)), name="moe_experts",
    )(block_e, buf_tok, n_used, h, wg, wu, wd)


def _combine_kernel(pos_ref, ys_hbm, x_ref, gw_ref, gate_ref, fg_ref, o_ref, ybuf, sem, *, tm, tok0, final):
    i = pl.program_id(0)
    n = pl.num_programs(0)

    def start(blk, slot):
        def body(r, carry):
            t = tok0 + blk * tm + r
            _row_copy(ys_hbm, pos_ref[2 * t], ybuf, slot, r, sem).start()
            _row_copy(ys_hbm, pos_ref[2 * t + 1], ybuf, slot, tm + r, sem).start()
            return carry
        lax.fori_loop(0, tm, body, 0, unroll=8)

    @pl.when(i == 0)
    def _():
        start(0, 0)

    @pl.when(i + 1 < n)
    def _():
        start(i + 1, (i + 1) % 2)

    slot = i % 2

    pltpu.make_async_copy(ys_hbm.at[pl.ds(0, 2 * tm)], ybuf.at[slot], sem.at[slot]).wait()
    gw = gw_ref[...]
    y = (gw[:, 0:1] * _unpack_rows(ybuf[slot, 0:tm, :], F32)
         + gw[:, 1:2] * _unpack_rows(ybuf[slot, tm:2 * tm, :], F32))
    out = x_ref[...] + gate_ref[0] * y
    if final:
        out = _rms(out, fg_ref[...])
    o_ref[...] = out


def _combine(ys, pos, gw, x2d, gate, final_g, tok0, rows_per_mod, final):
    m, d = x2d.shape
    tm = 256
    per = rows_per_mod // tm
    gb = tok0 // tm
    grid_spec = pltpu.PrefetchScalarGridSpec(
        num_scalar_prefetch=1, grid=(m // tm,),
        in_specs=[pl.BlockSpec(memory_space=pl.ANY),
                  pl.BlockSpec((tm, d), lambda i, pos: (i, 0)),
                  pl.BlockSpec((tm, TOP_K), lambda i, pos: (gb + i, 0)),
                  pl.BlockSpec((1, 1, d), lambda i, pos: (i // per, 0, 0)),
                  pl.BlockSpec((1, d), lambda i, pos: (0, 0))],
        out_specs=pl.BlockSpec((tm, d), lambda i, pos: (i, 0)),
        scratch_shapes=[pltpu.VMEM((2, 2 * tm, ys.shape[1]), jnp.uint32), pltpu.SemaphoreType.DMA((2,))])
    return pl.pallas_call(
        functools.partial(_combine_kernel, tm=tm, tok0=tok0, final=final), grid_spec=grid_spec,
        out_shape=jax.ShapeDtypeStruct((m, d), F32),
        compiler_params=_params(("arbitrary",)), name="moe_combine",
    )(pos, ys, x2d, gw, gate, final_g.reshape(1, d))


def _rank_kernel(e_ref, rank_ref, cnt_ref, run_scr):
    @pl.when(pl.program_id(0) == 0)
    def _():
        run_scr[...] = jnp.zeros_like(run_scr)

    tb = e_ref.shape[0]
    lane = lax.broadcasted_iota(jnp.int32, (tb, LANE), 1)
    onehot = jnp.where(e_ref[...] == lane, 1.0, 0.0)
    ri = lax.broadcasted_iota(jnp.int32, (tb, tb), 0)
    ci = lax.broadcasted_iota(jnp.int32, (tb, tb), 1)
    before = jnp.where(ri > ci, 1.0, 0.0).astype(BF16)
    prefix = _dot(before, onehot.astype(BF16)) + run_scr[...]
    rank_ref[...] = jnp.sum(prefix * onehot, axis=1, keepdims=True).astype(jnp.int32)
    run_scr[...] += jnp.sum(onehot, axis=0, keepdims=True)
    cnt_ref[...] = run_scr[...]


def _expert_ranks(flat_e):
    a = flat_e.shape[0]
    tb = 512
    rank, cnt = pl.pallas_call(
        _rank_kernel, grid=(a // tb,),
        in_specs=[pl.BlockSpec((tb, 1), lambda i: (i, 0))],
        out_specs=[pl.BlockSpec((tb, 1), lambda i: (i, 0)), pl.BlockSpec((1, LANE), lambda i: (0, 0))],
        out_shape=[jax.ShapeDtypeStruct((a, 1), jnp.int32), jax.ShapeDtypeStruct((1, LANE), F32)],
        scratch_shapes=[pltpu.VMEM((1, LANE), F32)],
        compiler_params=_params(("arbitrary",)), name="moe_rank",
    )(flat_e.reshape(a, 1))
    return rank.reshape(a), cnt[0, :N_EXPERTS].astype(jnp.int32)


def _dispatch_plan(expert, bm):
    t = expert.shape[0]
    a = t * TOP_K
    flat_e = expert.reshape(a).astype(jnp.int32)
    rank, counts = _expert_ranks(flat_e)
    padded = (counts + bm - 1) // bm * bm
    pad_end = jnp.cumsum(padded)
    pad_start = pad_end - padded
    mine = flat_e[:, None] == jnp.arange(N_EXPERTS, dtype=jnp.int32)[None, :]
    pos = (jnp.sum(jnp.where(mine, pad_start[None, :], 0), axis=1) + rank).astype(jnp.int32)
    n_blocks = a // bm + N_EXPERTS
    flat_tok = jnp.arange(a, dtype=jnp.int32) // TOP_K
    buf_tok = jnp.zeros((n_blocks * bm,), jnp.int32).at[pos].set(flat_tok)
    n_used = (pad_end[-1] // bm).astype(jnp.int32)
    blk = jnp.minimum(jnp.arange(n_blocks, dtype=jnp.int32), n_used - 1) * bm
    block_e = jnp.sum((pad_end[None, :] <= blk[:, None]).astype(jnp.int32), axis=1)
    block_e = jnp.minimum(block_e, N_EXPERTS - 1).astype(jnp.int32)
    return pos, buf_tok, block_e, n_used.reshape(1)


def _mix_stream(p3, ab, rows, width, s0_f, s0_b, four_tabs, hy_tabs, hy_coefs, conv_w, conv_b, four_g,
                hy_bias, hy_g, a_log, dt_bias, with_mixers):
    qkv = _short_conv(p3, conv_w, conv_b, rows, width, OFF_DN, HY_CONV_CH, 3 * DN_W, 512)[0]
    o_f, o_b, s_f, s_b = _delta_rule(qkv, ab, a_log, dt_bias, s0_f, s0_b)
    if not with_mixers:
        return None, None, o_f, o_b, s_f, s_b
    if _hy_two_level(p3.shape[1]):
        y_four = _fourier2(p3, four_tabs, four_g)
    else:
        y_four = _fourier(p3, *four_tabs, four_g)
    if _hy_two_level(p3.shape[1]):
        conv3 = _short_conv(p3, conv_w, conv_b, rows, width, OFF_HY, 0, HY_CONV_CH, 256, split=3)
        y_hy = _hyena2(conv3, hy_tabs, hy_coefs, hy_bias, hy_g)
    else:
        convh = _short_conv(p3, conv_w, conv_b, rows, width, OFF_HY, 0, HY_CONV_CH, 256)[0]
        y_hy = _hyena(convh, hy_tabs, hy_coefs, hy_bias, hy_g)
    return y_four, y_hy, o_f, o_b, s_f, s_b


def _hy_two_level(length):
    return length % (8 * HY_FAST) == 0


def kernel(x, c, ctx, c_ctx, norm1_g, norm2_g, w_mod, b_mod, w_in, conv_w, conv_b, four_g, hy_w1, hy_b1, hy_w2, hy_b2, hy_w3, hy_b3, hy_w4, hy_freq, hy_bias, hy_g, dn_a_log, dn_dt_bias, dn_g, w_out, w_rc, b_rc, w_rf, b_rf, w_e_gate, w_e_up, w_e_down, final_g):
    bsz, length, d = x.shape
    lc = ctx.shape[1]
    depth = w_in.shape[0]
    rows = length // GRID_W
    bm = 256

    mod_all = _modulation(c, c_ctx, w_mod, b_mod)
    four_tables = lambda n: _four2_tables(n) if _hy_two_level(n) else _fourier_tables(n)
    four_x, four_c = four_tables(length), four_tables(lc)
    hy_tables = lambda n: _hy2_tables(n) if _hy_two_level(n) else _hyena_tables(n)
    hy_coefs = lambda n, tabs, prm: (_hy2_coefs if _hy_two_level(n) else _hyena_coefs)(n, tabs, prm)
    hy_x, hy_c = hy_tables(length), hy_tables(lc)
    zeros = jnp.zeros((bsz, DN_HEADS, DN_DK, DN_DV), F32)
    x2 = x.reshape(bsz * length, d)
    c2 = ctx.reshape(bsz * lc, d)
    wg16, wu16, wd16 = w_e_gate.astype(BF16), w_e_up.astype(BF16), w_e_down.astype(BF16)

    for l in range(depth):
        last = l == depth - 1
        mod = [m[:, None, :] for m in jnp.split(mod_all[l, :bsz], 6, axis=-1)]
        modc = [m[:, None, :] for m in jnp.split(mod_all[l, bsz:bsz + 1], 6, axis=-1)]
        wi = w_in[l]
        w_p = jnp.concatenate([wi[:, :OFF_AB], wi[:, OFF_Z:], wi[:, OFF_AB:OFF_Z],
                               jnp.zeros((d, PROJ_W - PCOL_AB - N_AB), wi.dtype)], axis=1).astype(BF16)
        w_o = w_out[l].astype(BF16)
        hy_params = (hy_w1[l], hy_b1[l], hy_w2[l], hy_b2[l], hy_w3[l], hy_b3[l], hy_w4[l], hy_freq[l])
        mix_args = (conv_w[l], conv_b[l], four_g[l], hy_bias[l], hy_g[l], dn_a_log[l], dn_dt_bias[l])

        pc, abc = _proj_in(c2, norm1_g[l], modc[0], modc[1], w_p, bsz * lc)
        pc3 = pc.reshape(bsz, lc, PROJ_W)
        abc = abc[:, :N_AB].reshape(bsz, lc, N_AB)
        coefs_c = None if last else hy_coefs(lc, hy_c, hy_params)
        yf, yh, o_f, o_b, s_f, s_b = _mix_stream(pc3, abc, 1, lc, zeros, zeros, four_c, hy_c, coefs_c,
                                                 *mix_args, with_mixers=not last)
        if not last:
            c2 = _proj_out(yf.reshape(-1, FOUR_W), yh.reshape(-1, HY_W), o_f.reshape(-1, DN_W),
                           o_b.reshape(-1, DN_W), pc, dn_g[l], w_o, c2, modc[2], bsz * lc)

        p, abx = _proj_in(x2, norm1_g[l], mod[0], mod[1], w_p, length)
        p3 = p.reshape(bsz, length, PROJ_W)
        abx = abx[:, :N_AB].reshape(bsz, length, N_AB)
        coefs_x = hy_coefs(length, hy_x, hy_params)
        yf, yh, o_f, o_b, _, _ = _mix_stream(p3, abx, rows, GRID_W, s_f, s_b, four_x, hy_x, coefs_x,
                                             *mix_args, with_mixers=True)
        x2 = _proj_out(yf.reshape(-1, FOUR_W), yh.reshape(-1, HY_W), o_f.reshape(-1, DN_W),
                       o_b.reshape(-1, DN_W), p, dn_g[l], w_o, x2, mod[2], length)

        w_r = jnp.concatenate([w_rc[l], w_rf[l], jnp.zeros((d, ROUTER_W - N_GROUPS - N_EXPERTS), F32)], axis=1)
        b_r = jnp.concatenate([b_rc[l], b_rf[l], jnp.zeros((ROUTER_W - N_GROUPS - N_EXPERTS,), F32)])[None, :]
        h, expert, gate = _router(x2, c2, norm2_g[l], mod[3], mod[4], modc[3], modc[4], w_r, b_r, length,
                                  with_ctx=not last)
        pos, buf_tok, block_e, n_used = _dispatch_plan(expert, bm)
        ys = _experts(h, buf_tok, block_e, n_used, wg16, wu16, wd16, l, bm)
        x2 = _combine(ys, pos, gate, x2, mod[5], final_g, 0, length, final=last)
        if not last:
            c2 = _combine(ys, pos, gate, c2, modc[5], final_g, bsz * length, bsz * lc, final=False)
    return x2.reshape(bsz, length, d)
```

```python
import functools
import math

import jax
import jax.numpy as jnp
from jax import lax
from jax.experimental import pallas as pl
from jax.experimental.pallas import tpu as pltpu

F32 = jnp.float32
BF16 = jnp.bfloat16
HIGHEST = lax.Precision.HIGHEST

GRID_W = 64
FOUR_W = 512
FOUR_GROUPS = 4
FOUR_GW = FOUR_W // FOUR_GROUPS
HY_W = 512
HY_ORDER = 2
DN_HEADS = 8
DN_DK = 128
DN_DV = 128
DN_W = DN_HEADS * DN_DV
DN_CHUNK = 64
DN_SOLVE_BLOCK = DN_CHUNK // 4
HY_CONV_CH = (HY_ORDER + 1) * HY_W
OFF_HY = FOUR_W
OFF_DN = OFF_HY + HY_CONV_CH
OFF_AB = OFF_DN + 3 * DN_W
N_AB = 4 * DN_HEADS
OFF_Z = OFF_AB + N_AB
HY_EMB = 33
HY_FAST_DECAY = 0.3
HY_SLOW_DECAY = 1.5
HY_TARGET = 1e-2
N_GROUPS = 4
EXPERTS_PER_GROUP = 8
N_EXPERTS = N_GROUPS * EXPERTS_PER_GROUP
TOP_K = 2
EPS = 1e-6

LANE = 128
PCOL_Z = OFF_AB
PCOL_AB = PCOL_Z + DN_W
MXU_W = 256
PROJ_TN = 5 * MXU_W
PROJ_W = -(-(PCOL_AB + LANE) // PROJ_TN) * PROJ_TN
ROUTER_W = LANE
VMEM_LIMIT = 56 * 1024 * 1024


def _params(semantics):
    return pltpu.CompilerParams(dimension_semantics=semantics, vmem_limit_bytes=VMEM_LIMIT)


def _silu(v):
    return v * jax.nn.sigmoid(v)


def _rms(v, g):
    return v * lax.rsqrt(jnp.mean(v * v, axis=-1, keepdims=True) + EPS) * g


def _dot(a, b):
    return jnp.dot(a, b, preferred_element_type=F32)


def _mod_kernel(a_ref, w_ref, b_ref, o_ref):
    o_ref[0] = jnp.dot(_silu(a_ref[...]), w_ref[0], precision=HIGHEST,
                       preferred_element_type=F32) + b_ref[0]


def _modulation(c, c_ctx, w_mod, b_mod):
    depth, d, n = w_mod.shape
    bsz = c.shape[0]
    rows = -(-(bsz + 1) // 8) * 8
    a = jnp.concatenate([c, c_ctx[None], jnp.zeros((rows - bsz - 1, d), F32)], axis=0)
    tn = 1024 if n % 1024 == 0 else 512
    assert n % tn == 0
    return pl.pallas_call(
        _mod_kernel, grid=(depth, n // tn),
        in_specs=[pl.BlockSpec((rows, d), lambda l, j: (0, 0)),
                  pl.BlockSpec((1, d, tn), lambda l, j: (l, 0, j)),
                  pl.BlockSpec((1, 1, tn), lambda l, j: (l, 0, j))],
        out_specs=pl.BlockSpec((1, rows, tn), lambda l, j: (l, 0, j)),
        out_shape=jax.ShapeDtypeStruct((depth, rows, n), F32),
        compiler_params=_params(("parallel", "parallel")), name="modulation",
    )(a, w_mod, b_mod.reshape(depth, 1, n))


def _proj_in_kernel(x_ref, g_ref, shift_ref, scale_ref, w_ref, o_ref, ab_ref, a_scr, *, ab_off):
    j = pl.program_id(1)

    @pl.when(j == 0)
    def _():
        y = _rms(x_ref[...], g_ref[...])
        a_scr[...] = (y * (1.0 + scale_ref[0]) + shift_ref[0]).astype(BF16)

    acc = _dot(a_scr[...], w_ref[...])
    o_ref[...] = acc.astype(BF16)

    @pl.when(j == pl.num_programs(1) - 1)
    def _():
        ab_ref[...] = acc[:, ab_off:ab_off + LANE]


def _proj_in(x2d, g, shift, scale, w, rows_per_mod):
    m, d = x2d.shape
    n = w.shape[1]
    tm = min(1024, m, rows_per_mod)
    tn = PROJ_TN
    assert n % tn == 0 and PCOL_AB >= n - tn
    per = rows_per_mod // tm
    return pl.pallas_call(
        functools.partial(_proj_in_kernel, ab_off=PCOL_AB - (n - tn)), grid=(m // tm, n // tn),
        in_specs=[pl.BlockSpec((tm, d), lambda i, j: (i, 0)),
                  pl.BlockSpec((1, d), lambda i, j: (0, 0)),
                  pl.BlockSpec((1, 1, d), lambda i, j: (i // per, 0, 0)),
                  pl.BlockSpec((1, 1, d), lambda i, j: (i // per, 0, 0)),
                  pl.BlockSpec((d, tn), lambda i, j: (0, j))],
        out_specs=[pl.BlockSpec((tm, tn), lambda i, j: (i, j)),
                   pl.BlockSpec((tm, LANE), lambda i, j: (i, 0))],
        out_shape=[jax.ShapeDtypeStruct((m, n), BF16), jax.ShapeDtypeStruct((m, LANE), F32)],
        scratch_shapes=[pltpu.VMEM((tm, d), BF16)],
        compiler_params=_params(("parallel", "arbitrary")), name="proj_in",
    )(x2d, g.reshape(1, d), shift, scale, w)


def _cos_sin_table(nrow, ncol, period):
    i = lax.broadcasted_iota(jnp.int32, (nrow, ncol), 0)
    j = lax.broadcasted_iota(jnp.int32, (nrow, ncol), 1)
    ang = ((i * j) % period).astype(F32) * (2.0 * math.pi / period)
    return jnp.cos(ang), jnp.sin(ang)


def _dft_tables(length, period):
    blk = 64 if length % 64 == 0 else length
    hi_i = lax.broadcasted_iota(jnp.int32, (length // blk, length), 0) * blk
    j = lax.broadcasted_iota(jnp.int32, (length // blk, length), 1)
    ang = ((hi_i * j) % period).astype(F32) * (2.0 * math.pi / period)
    hi_c, hi_s = jnp.cos(ang), jnp.sin(ang)
    lo_c, lo_s = _cos_sin_table(blk, length, period)
    c = hi_c[:, None, :] * lo_c[None, :, :] - hi_s[:, None, :] * lo_s[None, :, :]
    s = hi_s[:, None, :] * lo_c[None, :, :] + hi_c[:, None, :] * lo_s[None, :, :]
    return c.reshape(length, length), s.reshape(length, length)


def _fourier_kernel(u_ref, c_ref, s_ref, bc_ref, bs_ref, g_ref, o_ref, ub_scr, *, scale):
    @pl.when(pl.program_id(1) == 0)
    def _():
        ub_scr[...] = u_ref[0].astype(BF16)

    ub = ub_scr[...]
    p = _dot(c_ref[...], ub).astype(BF16)
    q = _dot(s_ref[...], ub).astype(BF16)
    z = (_dot(p, bc_ref[...]) - _dot(q, bs_ref[...])) * scale
    o_ref[0] = _rms(z, g_ref[...]).astype(o_ref.dtype)


def _fourier(p3, cl, sl, bc, bs, four_g):
    bsz, length, _ = p3.shape
    tm = min(512, length)
    kern = functools.partial(_fourier_kernel, scale=1.0 / math.sqrt(length * FOUR_GW))
    return pl.pallas_call(
        kern, grid=(bsz, length // tm),
        in_specs=[pl.BlockSpec((1, length, FOUR_W), lambda b, i: (b, 0, 0)),
                  pl.BlockSpec((tm, length), lambda b, i: (i, 0)),
                  pl.BlockSpec((tm, length), lambda b, i: (i, 0)),
                  pl.BlockSpec((FOUR_W, FOUR_W), lambda b, i: (0, 0)),
                  pl.BlockSpec((FOUR_W, FOUR_W), lambda b, i: (0, 0)),
                  pl.BlockSpec((1, FOUR_W), lambda b, i: (0, 0))],
        out_specs=pl.BlockSpec((1, tm, FOUR_W), lambda b, i: (b, i, 0)),
        out_shape=jax.ShapeDtypeStruct((bsz, length, FOUR_W), BF16),
        scratch_shapes=[pltpu.VMEM((length, FOUR_W), BF16)],
        compiler_params=_params(("parallel", "arbitrary")), name="fourier",
    )(p3, cl, sl, bc, bs, four_g.reshape(1, FOUR_W))


def _fourier_tables(length):
    cl, sl = _dft_tables(length, length)
    cc, sc = _cos_sin_table(FOUR_GW, FOUR_GW, FOUR_GW)
    eye = jnp.eye(FOUR_GROUPS, dtype=F32)
    return cl.astype(BF16), sl.astype(BF16), jnp.kron(eye, cc).astype(BF16), jnp.kron(eye, sc).astype(BF16)


def _conv_kernel(x_ref, w_ref, b_ref, o_ref, *, rows, width):
    tc = x_ref.shape[2]
    wts = w_ref[...]
    bias = b_ref[...]
    pos = lax.broadcasted_iota(jnp.int32, (width, tc), 0)
    not_first = pos > 0
    not_last = pos < width - 1

    def grid_row(r):
        start = r * width if isinstance(r, int) else pl.multiple_of(r * width, width)
        return x_ref[0, pl.ds(start, width), :].astype(F32)

    def body(r, carry):
        base = pl.multiple_of(r * width, width)
        up, cen = carry
        if rows > 1:
            dn = grid_row(jnp.minimum(r + 1, rows - 1))
            w_up = wts[0] * jnp.where(r > 0, 1.0, 0.0)
            w_dn = wts[2] * jnp.where(r < rows - 1, 1.0, 0.0)
            col = lambda j: up * w_up[j:j + 1, :] + cen * wts[1, j:j + 1, :] + dn * w_dn[j:j + 1, :]
        else:
            dn = cen
            col = lambda j: cen * wts[1, j:j + 1, :]
        left = jnp.where(not_first, pltpu.roll(col(0), 1, 0), 0.0)
        right = jnp.where(not_last, pltpu.roll(col(2), width - 1, 0), 0.0)
        o_ref[0, pl.ds(base, width), :] = left + col(1) + right + bias
        return cen, dn

    first = grid_row(0)
    lax.fori_loop(0, rows, body, (first, first))


def _short_conv(p3, conv_w, conv_b, rows, width, col0, ch0, nch, tc, split=1):
    bsz, length, _ = p3.shape
    kern = functools.partial(_conv_kernel, rows=rows, width=width)
    cb, wb = col0 // tc, ch0 // tc
    per = nch // split // tc
    out = pl.pallas_call(
        kern, grid=(bsz, nch // tc),
        in_specs=[pl.BlockSpec((1, length, tc), lambda b, j: (b, 0, cb + j)),
                  pl.BlockSpec((3, 3, tc), lambda b, j: (0, 0, wb + j)),
                  pl.BlockSpec((1, tc), lambda b, j: (0, wb + j))],
        out_specs=pl.BlockSpec((1, length, tc), lambda b, j: ((j // per) * bsz + b, 0, j % per)),
        out_shape=jax.ShapeDtypeStruct((split * bsz, length, nch // split), F32),
        compiler_params=_params(("parallel", "parallel")),
        name="short_conv",
    )(p3, conv_w, conv_b.reshape(1, -1))
    return out.reshape(split, bsz, length, nch // split)


def _filt_kernel(z_ref, w1_ref, b1_ref, w2_ref, b2_ref, w3_ref, b3_ref, w4_ref, fr_ref, dl_ref,
                 g_ref, nrm_ref):
    i = pl.program_id(0)
    z = z_ref[...]
    fr = fr_ref[...]
    hdot = lambda a, b: jnp.dot(a, b, precision=HIGHEST, preferred_element_type=F32)
    h = jnp.sin(fr * (hdot(z, w1_ref[...]) + b1_ref[...]))
    h = jnp.sin(fr * (hdot(h, w2_ref[...]) + b2_ref[...]))
    h = jnp.sin(fr * (hdot(h, w3_ref[...]) + b3_ref[...]))
    h = hdot(h, w4_ref[...])
    decay = jnp.exp(-z[:, 0:1] * jnp.abs(dl_ref[...]))
    decay = jnp.concatenate([decay] * HY_ORDER, axis=1)
    half = HY_ORDER * HY_W
    hf = h[:, :half] * decay
    hb = h[:, half:] * decay
    row = lax.broadcasted_iota(jnp.int32, hb.shape, 0) + i * hb.shape[0]
    hb = jnp.where(row > 0, hb, 0.0)
    gp = hf + hb
    gm = hf - hb
    for o in range(HY_ORDER):
        g_ref[o] = gp[:, o * HY_W:(o + 1) * HY_W].astype(BF16)
        g_ref[HY_ORDER + o] = gm[:, o * HY_W:(o + 1) * HY_W].astype(BF16)
    part = jnp.sum(jnp.abs(hf) + jnp.abs(hb), axis=0, keepdims=True)

    @pl.when(i == 0)
    def _():
        nrm_ref[...] = part

    @pl.when(i > 0)
    def _():
        nrm_ref[...] += part


def _hyena_filter_taps(length, w1, b1, w2, b2, w3, b3, w4, freq):
    t = jnp.linspace(0.0, 1.0, length, dtype=F32)[:, None]
    bands = (HY_EMB - 1) // 2
    ang = (2.0 * math.pi / length) * jnp.arange(length, dtype=F32)[:, None]
    f = jnp.linspace(1e-4, bands - 1, bands, dtype=F32)
    z = jnp.concatenate([t, jnp.cos(f * ang), -jnp.sin(f * ang)], axis=-1)
    max_decay = math.log(HY_TARGET) / HY_FAST_DECAY
    min_decay = math.log(HY_TARGET) / HY_SLOW_DECAY
    deltas = jnp.linspace(min_decay, max_decay, HY_W, dtype=F32)[None, :]
    fw = w1.shape[1]
    tl = min(512, length)
    full = lambda shape: pl.BlockSpec(shape, lambda i: (0,) * len(shape))
    return pl.pallas_call(
        _filt_kernel, grid=(length // tl,),
        in_specs=[pl.BlockSpec((tl, HY_EMB), lambda i: (i, 0)),
                  full((HY_EMB, fw)), full((1, fw)), full((fw, fw)), full((1, fw)),
                  full((fw, fw)), full((1, fw)), full((fw, 2 * HY_ORDER * HY_W)), full((1, fw)),
                  full((1, HY_W))],
        out_specs=[pl.BlockSpec((2 * HY_ORDER, tl, HY_W), lambda i: (0, i, 0)),
                   pl.BlockSpec((1, HY_ORDER * HY_W), lambda i: (0, 0))],
        out_shape=[jax.ShapeDtypeStruct((2 * HY_ORDER, length, HY_W), BF16),
                   jax.ShapeDtypeStruct((1, HY_ORDER * HY_W), F32)],
        compiler_params=_params(("arbitrary",)), name="hyena_filter",
    )(z, w1, b1.reshape(1, fw), w2, b2.reshape(1, fw), w3, b3.reshape(1, fw), w4,
      freq.reshape(1, fw), deltas)


def _dft_fwd_kernel(u_ref, c_ref, s_ref, *rest, with_coef):
    if with_coef:
        c1_ref, c2_ref, c4_ref, yr_ref, yi_ref, ub_scr = rest
    else:
        yr_ref, yi_ref, ub_scr = rest

    @pl.when(pl.program_id(1) == 0)
    def _():
        ub_scr[...] = u_ref[0].astype(BF16)

    ub = ub_scr[...]
    a = _dot(c_ref[...], ub)
    b = _dot(s_ref[...], ub)
    if with_coef:
        c2 = c2_ref[...]
        yr_ref[0] = (a * c1_ref[...] + b * c2).astype(BF16)
        yi_ref[0] = (b * c4_ref[...] - a * c2).astype(BF16)
    else:
        yr_ref[0] = a
        yi_ref[0] = b


def _dft_fwd(u3, col_blk, cf, sf, coef=None):
    nb, length, _ = u3.shape
    tm = min(512, length)
    in_specs = [pl.BlockSpec((1, length, HY_W), lambda b, i: (b, 0, col_blk)),
                pl.BlockSpec((tm, length), lambda b, i: (i, 0)),
                pl.BlockSpec((tm, length), lambda b, i: (i, 0))]
    args = [u3, cf, sf]
    if coef is not None:
        in_specs += [pl.BlockSpec((tm, HY_W), lambda b, i: (i, 0))] * 3
        args += list(coef)
    odt = BF16 if coef is not None else F32
    return pl.pallas_call(
        functools.partial(_dft_fwd_kernel, with_coef=coef is not None), grid=(nb, length // tm),
        in_specs=in_specs,
        out_specs=[pl.BlockSpec((1, tm, HY_W), lambda b, i: (b, i, 0))] * 2,
        out_shape=[jax.ShapeDtypeStruct((nb, length, HY_W), odt)] * 2,
        scratch_shapes=[pltpu.VMEM((length, HY_W), BF16)],
        compiler_params=_params(("parallel", "arbitrary")),
        name="hyena_dft_fwd" if coef is not None else "hyena_filter_dft",
    )(*args)


def _dft_inv_kernel(yr_ref, yi_ref, c_ref, s_ref, u_ref, gate_ref, bias_ref, g_ref, o_ref, *, final):
    y = _dot(c_ref[...], yr_ref[0]) + _dot(s_ref[...], yi_ref[0])
    out = gate_ref[0] * (y + bias_ref[...] * u_ref[0])
    if final:
        out = _rms(out, g_ref[...])
    o_ref[0] = out.astype(o_ref.dtype)


def _dft_inv(yr, yi, cf, s_inv, u3, u_blk, gate3, gate_blk, bias, g, final):
    bsz, length, _ = yr.shape
    tm = min(512, length)
    return pl.pallas_call(
        functools.partial(_dft_inv_kernel, final=final), grid=(bsz, length // tm),
        in_specs=[pl.BlockSpec((1, length, HY_W), lambda b, i: (b, 0, 0)),
                  pl.BlockSpec((1, length, HY_W), lambda b, i: (b, 0, 0)),
                  pl.BlockSpec((tm, length), lambda b, i: (i, 0)),
                  pl.BlockSpec((tm, length), lambda b, i: (i, 0)),
                  pl.BlockSpec((1, tm, HY_W), lambda b, i: (b, i, u_blk)),
                  pl.BlockSpec((1, tm, HY_W), lambda b, i: (b, i, gate_blk)),
                  pl.BlockSpec((1, HY_W), lambda b, i: (0, 0)),
                  pl.BlockSpec((1, HY_W), lambda b, i: (0, 0))],
        out_specs=pl.BlockSpec((1, tm, HY_W), lambda b, i: (b, i, 0)),
        out_shape=jax.ShapeDtypeStruct((bsz, length, HY_W), BF16 if final else F32),
        compiler_params=_params(("parallel", "arbitrary")), name="hyena_dft_inv",
    )(yr, yi, cf, s_inv, u3, gate3, bias.reshape(1, HY_W), g.reshape(1, HY_W))


def _hyena_tables(length):
    cf, sf = _dft_tables(length, 2 * length)
    alt = jnp.where(jnp.arange(length) % 2 == 0, 1.0, -1.0).astype(F32)
    sf = sf.at[0, :].set(alt)
    return cf.astype(BF16), sf.astype(BF16), sf.T.astype(BF16)


def _hyena_coefs(length, tables, hy_params):
    cf, sf, _ = tables
    taps, nrm = _hyena_filter_taps(length, *hy_params)
    a, b = _dft_fwd(taps, 0, cf, sf)
    inv = (1.0 / nrm).reshape(HY_ORDER, 1, HY_W)
    n = 2.0 * length
    hr = a[:HY_ORDER] * inv
    hi = -b[HY_ORDER:] * inv
    nyq = b[:HY_ORDER, 0:1, :] * inv
    first = (jnp.arange(length) == 0)[None, :, None]
    c1 = jnp.where(first, hr / n, hr * (2.0 / n))
    c2 = jnp.where(first, 0.0, hi * (2.0 / n))
    c4 = jnp.where(first, nyq / n, hr * (2.0 / n))
    return c1, c2, c4


def _hyena(convh, tables, coefs, hy_bias, hy_g):
    cf, sf, s_inv = tables
    c1, c2, c4 = coefs
    yr, yi = _dft_fwd(convh, 0, cf, sf, (c1[0], c2[0], c4[0]))
    zz = _dft_inv(yr, yi, cf, s_inv, convh, 0, convh, 1, hy_bias[0], hy_g, False)
    yr, yi = _dft_fwd(zz, 0, cf, sf, (c1[1], c2[1], c4[1]))
    return _dft_inv(yr, yi, cf, s_inv, zz, 0, convh, 2, hy_bias[1], hy_g, True)


HY_FAST = 128


def _hy2_tables(length):
    nk = length // HY_FAST
    n2 = 2 * length
    k1 = lax.broadcasted_iota(jnp.int32, (nk, nk), 0)
    s1 = lax.broadcasted_iota(jnp.int32, (nk, nk), 1)
    ang = (((2 * k1 + 1) * s1) % (4 * nk)).astype(F32) * (2.0 * math.pi / (4 * nk))
    f1 = jnp.concatenate([jnp.cos(ang), -jnp.sin(ang)], axis=0)
    shape = (nk, HY_FAST, HY_FAST)
    kk = lax.broadcasted_iota(jnp.int32, shape, 0) + 2 * nk * lax.broadcasted_iota(jnp.int32, shape, 1)
    s2 = lax.broadcasted_iota(jnp.int32, shape, 2)
    phi = (((2 * kk + 1) * s2) % (2 * n2)).astype(F32) * (2.0 * math.pi / (2 * n2))
    c, s = jnp.cos(phi), jnp.sin(phi)
    f2 = jnp.concatenate([jnp.concatenate([c, s], axis=2), jnp.concatenate([-s, c], axis=2)], axis=1)
    f1k = jnp.kron(f1, jnp.eye(HY_SUB, dtype=F32))
    return f1k.astype(BF16), f2.astype(BF16), jnp.swapaxes(f2, 1, 2).astype(BF16), f1k.T.astype(BF16)


HY_SUB = 8


def _hy2_stage1_kernel(x_ref, f1_ref, z_ref):
    nk = x_ref.shape[2]
    tc = x_ref.shape[4]
    for g in range(HY_FAST // HY_SUB):
        rows = slice(g * HY_SUB, (g + 1) * HY_SUB)
        rhs = x_ref[0, 0, :, rows, :].reshape(nk * HY_SUB, tc).astype(BF16)
        z = _dot(f1_ref[...], rhs).astype(BF16)
        z_ref[0, :, :, rows, :] = z.reshape(2, nk, HY_SUB, tc)


def _hy2_stage1(x5, which, f1k, cw=None):
    _, nb, nk, _, call = x5.shape
    cw = call if cw is None else cw
    tc = min(256, cw)
    return pl.pallas_call(
        _hy2_stage1_kernel, grid=(nb, cw // tc),
        in_specs=[pl.BlockSpec((1, 1, nk, HY_FAST, tc), lambda b, j: (which, b, 0, 0, j)),
                  pl.BlockSpec(f1k.shape, lambda b, j: (0, 0))],
        out_specs=pl.BlockSpec((1, 2, nk, HY_FAST, tc), lambda b, j: (b, 0, 0, 0, j)),
        out_shape=jax.ShapeDtypeStruct((nb, 2, nk, HY_FAST, cw), BF16),
        compiler_params=_params(("parallel", "parallel")), name="hyena2_stage1",
    )(x5, f1k)


def _hy2_stage2_kernel(zr_ref, zi_ref, f2_ref, *rest, filtered):
    if filtered:
        g2_ref, hr_ref, hi_ref, vr_ref, vi_ref = rest
    else:
        vr_ref, vi_ref = rest
    for k in range(zr_ref.shape[2]):
        x = _dot(f2_ref[k], jnp.concatenate([zr_ref[0, 0, k], zi_ref[0, 0, k]], axis=0))
        xr, xi = x[:HY_FAST], x[HY_FAST:]
        if filtered:
            hr, hi = hr_ref[k], hi_ref[k]
            y = jnp.concatenate([xr * hr - xi * hi, xr * hi + xi * hr], axis=0).astype(BF16)
            v = _dot(g2_ref[k], y)
            vr_ref[0, 0, k] = v[:HY_FAST].astype(BF16)
            vi_ref[0, 0, k] = v[HY_FAST:].astype(BF16)
        else:
            vr_ref[0, 0, k] = xr
            vi_ref[0, 0, k] = xi


def _hy2_stage2(z, f2, g2=None, hr=None, hi=None):
    nb, _, nk, _, cw = z.shape
    kb = 8
    filtered = hr is not None
    zspec = lambda part: pl.BlockSpec((1, 1, kb, HY_FAST, cw), lambda k, b: (b, part, k, 0, 0))
    mat = pl.BlockSpec((kb, 2 * HY_FAST, 2 * HY_FAST), lambda k, b: (k, 0, 0))
    in_specs = [zspec(0), zspec(1), mat]
    args = [z, z, f2]
    if filtered:
        in_specs += [mat, pl.BlockSpec((kb, HY_FAST, cw), lambda k, b: (k, 0, 0)),
                     pl.BlockSpec((kb, HY_FAST, cw), lambda k, b: (k, 0, 0))]
        args += [g2, hr, hi]
    return pl.pallas_call(
        functools.partial(_hy2_stage2_kernel, filtered=filtered), grid=(nk // kb, nb),
        in_specs=in_specs, out_specs=[zspec(0), zspec(0)],
        out_shape=[jax.ShapeDtypeStruct((nb, 1, nk, HY_FAST, cw), BF16 if filtered else F32)] * 2,
        compiler_params=_params(("parallel", "parallel")),
        name="hyena2_stage2" if filtered else "hyena2_filter_spectrum",
    )(*args)


def _hy2_inv1_kernel(vr_ref, vi_ref, g1_ref, u_ref, gate_ref, bias_ref, g_ref, o_ref, *, final):
    nk = vr_ref.shape[2]
    cw = vr_ref.shape[4]
    for g in range(vr_ref.shape[3] // HY_SUB):
        rows = slice(g * HY_SUB, (g + 1) * HY_SUB)
        rhs = jnp.concatenate([vr_ref[0, 0, :, rows, :].reshape(nk * HY_SUB, cw),
                               vi_ref[0, 0, :, rows, :].reshape(nk * HY_SUB, cw)], axis=0)
        y = _dot(g1_ref[...], rhs).reshape(nk, HY_SUB, cw)
        out = gate_ref[0, 0, :, rows, :] * (y + bias_ref[...] * u_ref[0, 0, :, rows, :])
        if final:
            out = _rms(out, g_ref[...])
        o_ref[0, :, rows, :] = out.astype(o_ref.dtype)


def _hy2_inv1(vr, vi, g1k, u5, which_u, gate5, which_g, bias, g, final):
    nb, _, nk, _, cw = vr.shape
    tg = 64
    vspec = pl.BlockSpec((1, 1, nk, tg, cw), lambda b, j: (b, 0, 0, j, 0))
    return pl.pallas_call(
        functools.partial(_hy2_inv1_kernel, final=final), grid=(nb, HY_FAST // tg),
        in_specs=[vspec, vspec,
                  pl.BlockSpec(g1k.shape, lambda b, j: (0, 0)),
                  pl.BlockSpec((1, 1, nk, tg, cw), lambda b, j: (which_u, b, 0, j, 0)),
                  pl.BlockSpec((1, 1, nk, tg, cw), lambda b, j: (which_g, b, 0, j, 0)),
                  pl.BlockSpec((1, cw), lambda b, j: (0, 0)),
                  pl.BlockSpec((1, cw), lambda b, j: (0, 0))],
        out_specs=pl.BlockSpec((1, nk, tg, cw), lambda b, j: (b, 0, j, 0)),
        out_shape=jax.ShapeDtypeStruct((nb, nk, HY_FAST, cw), BF16 if final else F32),
        compiler_params=_params(("parallel", "parallel")), name="hyena2_inv1",
    )(vr, vi, g1k, u5, gate5, bias.reshape(1, cw), g.reshape(1, cw))


def _hy2_coefs(length, tables, hy_params):
    f1k, f2, _, _ = tables
    nk = length // HY_FAST
    taps, nrm = _hyena_filter_taps(length, *hy_params)
    z = _hy2_stage1(taps.reshape(1, 2 * HY_ORDER, nk, HY_FAST, HY_W), 0, f1k)
    xr, xi = _hy2_stage2(z, f2)
    scale = (1.0 / length) / nrm.reshape(HY_ORDER, 1, 1, HY_W)
    return xr[:HY_ORDER, 0] * scale, xi[HY_ORDER:, 0] * scale


def _hyena2(conv3, tables, coefs, hy_bias, hy_g):
    f1k, f2, g2, g1k = tables
    hr, hi = coefs
    _, bsz, length, cw = conv3.shape
    nk = length // HY_FAST
    c5 = conv3.reshape(3, bsz, nk, HY_FAST, cw)

    def conv(x5, which, order, gate_idx, final):
        vr, vi = _hy2_stage2(_hy2_stage1(x5, which, f1k), f2, g2, hr[order], hi[order])
        return _hy2_inv1(vr, vi, g1k, x5, which, c5, gate_idx, hy_bias[order], hy_g, final)

    zz = conv(c5, 0, 0, 1, False)
    return conv(zz[None], 0, 1, 2, True).reshape(bsz, length, cw)


def _four2_tables(length):
    nk = length // HY_FAST
    k1 = lax.broadcasted_iota(jnp.int32, (nk, nk), 0)
    s1 = lax.broadcasted_iota(jnp.int32, (nk, nk), 1)
    ang = ((k1 * s1) % nk).astype(F32) * (2.0 * math.pi / nk)
    f1 = jnp.concatenate([jnp.cos(ang), -jnp.sin(ang)], axis=0)
    shape = (nk, HY_FAST, HY_FAST)
    kk = lax.broadcasted_iota(jnp.int32, shape, 0) + nk * lax.broadcasted_iota(jnp.int32, shape, 1)
    s2 = lax.broadcasted_iota(jnp.int32, shape, 2)
    phi = ((kk * s2) % length).astype(F32) * (2.0 * math.pi / length)
    c, s = jnp.cos(phi), jnp.sin(phi)
    f2 = jnp.concatenate([jnp.concatenate([c, s], axis=2), jnp.concatenate([-s, c], axis=2)], axis=1)
    cc, sc = _cos_sin_table(FOUR_GW, FOUR_GW, FOUR_GW)
    eye = jnp.eye(FOUR_GROUPS, dtype=F32)
    return (jnp.kron(f1, jnp.eye(HY_SUB, dtype=F32)).astype(BF16), f2.astype(BF16),
            jnp.kron(eye, cc).astype(BF16), jnp.kron(eye, sc).astype(BF16))


def _four2_stage2_kernel(zr_ref, zi_ref, f2_ref, bc_ref, bs_ref, g_ref, o_ref, *, scale):
    for k in range(zr_ref.shape[2]):
        x = _dot(f2_ref[k], jnp.concatenate([zr_ref[0, 0, k], zi_ref[0, 0, k]], axis=0))
        xr, xi = x[:HY_FAST].astype(BF16), x[HY_FAST:].astype(BF16)
        z = (_dot(xr, bc_ref[...]) + _dot(xi, bs_ref[...])) * scale
        o_ref[0, k] = _rms(z, g_ref[...]).astype(o_ref.dtype)


def _fourier2(p3, tables, four_g):
    f1k, f2, bc, bs = tables
    bsz, length, pw = p3.shape
    nk = length // HY_FAST
    z = _hy2_stage1(p3.reshape(1, bsz, nk, HY_FAST, pw), 0, f1k, cw=FOUR_W)
    kb = 8
    zspec = lambda part: pl.BlockSpec((1, 1, kb, HY_FAST, FOUR_W), lambda k, b: (b, part, k, 0, 0))
    const = lambda shape: pl.BlockSpec(shape, lambda k, b: (0,) * len(shape))
    out = pl.pallas_call(
        functools.partial(_four2_stage2_kernel, scale=1.0 / math.sqrt(length * FOUR_GW)),
        grid=(nk // kb, bsz),
        in_specs=[zspec(0), zspec(1),
                  pl.BlockSpec((kb, 2 * HY_FAST, 2 * HY_FAST), lambda k, b: (k, 0, 0)),
                  const((FOUR_W, FOUR_W)), const((FOUR_W, FOUR_W)), const((1, FOUR_W))],
        out_specs=pl.BlockSpec((1, kb, HY_FAST, FOUR_W), lambda k, b: (b, k, 0, 0)),
        out_shape=jax.ShapeDtypeStruct((bsz, nk, HY_FAST, FOUR_W), BF16),
        compiler_params=_params(("parallel", "parallel")), name="fourier2_stage2",
    )(z, z, f2, bc, bs, four_g.reshape(1, FOUR_W))
    return jnp.swapaxes(out, 1, 2).reshape(bsz, length, FOUR_W)


def _softplus(v):
    return jnp.maximum(v, 0.0) + jnp.log(1.0 + jnp.exp(-jnp.abs(v)))


def _delta_prep_kernel(qkv_ref, ab_ref, abt_ref, arow_ref, drow_ref, acol_ref, dcol_ref,
                       wq_ref, u_ref, ak_ref, egl_ref):
    nch = abt_ref.shape[1]
    c = ab_ref.shape[1] // nch
    nh = DN_HEADS
    ri = lax.broadcasted_iota(jnp.int32, (c, c), 0)
    ci = lax.broadcasted_iota(jnp.int32, (c, c), 1)
    incl = (ri >= ci, ri <= ci)
    strict = (ri > ci, ri < ci)
    low = jnp.where(incl[0], 1.0, 0.0)
    upp = jnp.where(incl[1], 1.0, 0.0)
    hdot = lambda a, b: jnp.dot(a, b, precision=HIGHEST, preferred_element_type=F32)
    nt = (((1,), (1,)), ((), ()))
    beta_all, gcum, gcum_t = {}, {}, {}
    for cc in range(nch):
        ab = ab_ref[0, cc * c:(cc + 1) * c, :]
        abt = abt_ref[0, cc]
        gate = -arow_ref[...] * _softplus(ab + drow_ref[...])
        gate_t = -acol_ref[...] * _softplus(abt + dcol_ref[...])
        beta_all[cc] = jax.nn.sigmoid(ab)
        gcum[cc] = (hdot(low, gate), hdot(upp, gate))
        gcum_t[cc] = (hdot(gate_t, upp), hdot(gate_t, low))

    def l2n(t):
        return t * lax.rsqrt(jnp.sum(t * t, axis=-1, keepdims=True) + EPS)

    heads = [(cc, h) for cc in range(nch) for h in range(nh)]
    rows = lambda cc: slice(cc * c, (cc + 1) * c)
    q = {(cc, h): l2n(_silu(qkv_ref[0, rows(cc), h * DN_DK:(h + 1) * DN_DK])) * (DN_DK ** -0.5) for cc, h in heads}
    k = {(cc, h): l2n(_silu(qkv_ref[0, rows(cc), DN_W + h * DN_DK:DN_W + (h + 1) * DN_DK])) for cc, h in heads}
    v = {(cc, h): _silu(qkv_ref[0, rows(cc), 2 * DN_W + h * DN_DV:2 * DN_W + (h + 1) * DN_DV]) for cc, h in heads}
    k16 = {u: k[u].astype(BF16) for u in heads}
    kk0 = {u: lax.dot_general(k16[u], k16[u], nt, preferred_element_type=F32) for u in heads}
    qk0 = {u: lax.dot_general(q[u].astype(BF16), k16[u], nt, preferred_element_type=F32) for u in heads}
    units = [(cc, d, h) for cc in range(nch) for d in range(2) for h in range(nh)]
    mm, rr = {}, {}
    for cc, d, h in units:
        col = d * nh + h
        gc = gcum[cc][d][:, col:col + 1]
        gr = gcum_t[cc][d][col:col + 1, :]
        be = beta_all[cc][:, 2 * nh + col:2 * nh + col + 1]
        last = 0 if d else c - 1
        gl = gc[last:last + 1, :]
        dec = jnp.where(incl[d], jnp.exp(jnp.where(incl[d], gc - gr, 0.0)), 0.0)
        mm[cc, d, h] = jnp.where(strict[d], be * kk0[cc, h] * dec, 0.0)
        ak_ref[0, cc, d, h, 0:c, :] = jnp.where(incl[d], qk0[cc, h] * dec, 0.0).astype(BF16)
        eg = jnp.exp(gc)
        rr[cc, d, h] = jnp.concatenate([v[cc, h] * be, k[cc, h] * (be * eg)], axis=1)
        wq_ref[0, cc, d, h, c:2 * c, :] = (q[cc, h] * eg).astype(BF16)
        k_tail = k[cc, h] * jnp.exp(gl - gc)
        ak_ref[0, cc, d, h, c:c + DN_DK, :] = jnp.transpose(k_tail).astype(BF16)
        egl_ref[0, cc, d, h:h + 1, :] = jnp.broadcast_to(jnp.exp(gl), (1, DN_DV))
    sb = DN_SOLVE_BLOCK
    in_sb = (ri // sb) == (ci // sb)
    in_2sb = (ri // (2 * sb)) == (ci // (2 * sb))
    eye = jnp.where(ri == ci, 1.0, 0.0)
    b16 = lambda t: t.astype(BF16)
    nj = {u: jnp.where(in_sb, -mm[u], 0.0) for u in units}
    inv = {u: eye + nj[u] for u in units}
    for j in range((sb - 1).bit_length() - 1):
        nj = {u: _dot(b16(nj[u]), b16(nj[u])) for u in units}
        inv = {u: inv[u] + _dot(b16(inv[u]), b16(nj[u])) for u in units}
    off = {u: b16(jnp.where(in_2sb & ~in_sb, mm[u], 0.0)) for u in units}
    tmp = {u: _dot(off[u], b16(inv[u])) for u in units}
    inv = {u: inv[u] - _dot(b16(inv[u]), b16(tmp[u])) for u in units}
    inv16 = {u: b16(inv[u]) for u in units}
    off = {u: b16(jnp.where(in_2sb, 0.0, mm[u])) for u in units}
    part = {u: _dot(inv16[u], b16(rr[u])) for u in units}
    tmp = {u: _dot(off[u], b16(part[u])) for u in units}
    rr = {u: part[u] - _dot(inv16[u], b16(tmp[u])) for u in units}
    for cc, d, h in units:
        u_ref[0, cc, d, h] = rr[cc, d, h][:, :DN_DV]
        wq_ref[0, cc, d, h, 0:c, :] = rr[cc, d, h][:, DN_DV:].astype(BF16)


def _delta_scan_kernel(wqf_ref, uf_ref, akf_ref, egf_ref, wqb_ref, ub_ref, akb_ref, egb_ref, s0f_ref, s0b_ref,
                       of_ref, ob_ref, sf_ref, sb_ref, s_scr):
    i = pl.program_id(1)
    bpb = uf_ref.shape[0]
    c = uf_ref.shape[-2]

    @pl.when(i == 0)
    def _():
        s_scr[0] = s0f_ref[...]
        s_scr[1] = s0b_ref[...]

    refs = ((wqf_ref, uf_ref, akf_ref, egf_ref, of_ref), (wqb_ref, ub_ref, akb_ref, egb_ref, ob_ref))
    units = [(d, bb, h) for d in range(2) for bb in range(bpb) for h in range(DN_HEADS)]
    s = {u: s_scr[u] for u in units}
    ws = {(d, bb, h): _dot(refs[d][0][bb, 0, 0, h], s[d, bb, h].astype(BF16)) for d, bb, h in units}
    v16 = {(d, bb, h): (refs[d][1][bb, 0, 0, h] - ws[d, bb, h][:c]).astype(BF16) for d, bb, h in units}
    av = {(d, bb, h): _dot(refs[d][2][bb, 0, 0, h], v16[d, bb, h]) for d, bb, h in units}
    for d, bb, h in units:
        u = (d, bb, h)
        refs[d][4][bb, :, h * DN_DV:(h + 1) * DN_DV] = (ws[u][c:] + av[u][:c]).astype(BF16)
        s_scr[u] = s[u] * refs[d][3][bb, 0, 0, h:h + 1, :] + av[u][c:]

    @pl.when(i == pl.num_programs(1) - 1)
    def _():
        sf_ref[...] = s_scr[0]
        sb_ref[...] = s_scr[1]


def _delta_rule(qkv, ab, a_log, dt_bias, s0_f, s0_b):
    bsz, length, _ = qkv.shape
    bpb = 4 if bsz % 4 == 0 else 1
    c = DN_CHUNK
    n = length // c
    nh = DN_HEADS
    abt = jnp.swapaxes(ab.reshape(bsz, n, c, 4 * nh), 2, 3)
    zeros = jnp.zeros((2 * nh,), F32)
    a_vec = jnp.concatenate([jnp.exp(a_log.astype(F32)).reshape(-1), zeros])
    d_vec = jnp.concatenate([dt_bias.astype(F32).reshape(-1), zeros])
    small = lambda shape: pl.BlockSpec(shape, lambda b, i: (0,) * len(shape))
    nch = 2 if n % 2 == 0 else 1
    per_chunk = lambda *tail: pl.BlockSpec((1, nch, 2, nh) + tail, lambda b, i: (b, i, 0, 0) + (0,) * len(tail))
    wq, u, ak, egl = pl.pallas_call(
        _delta_prep_kernel, grid=(bsz, n // nch),
        in_specs=[pl.BlockSpec((1, nch * c, 3 * DN_W), lambda b, i: (b, i, 0)),
                  pl.BlockSpec((1, nch * c, 4 * nh), lambda b, i: (b, i, 0)),
                  pl.BlockSpec((1, nch, 4 * nh, c), lambda b, i: (b, i, 0, 0)),
                  small((1, 4 * nh)), small((1, 4 * nh)), small((4 * nh, 1)), small((4 * nh, 1))],
        out_specs=[per_chunk(2 * c, DN_DK), per_chunk(c, DN_DV), per_chunk(c + DN_DK, c),
                   pl.BlockSpec((1, nch, 2, nh, DN_DV), lambda b, i: (b, i, 0, 0, 0))],
        out_shape=[jax.ShapeDtypeStruct((bsz, n, 2, nh, 2 * c, DN_DK), BF16),
                   jax.ShapeDtypeStruct((bsz, n, 2, nh, c, DN_DV), F32),
                   jax.ShapeDtypeStruct((bsz, n, 2, nh, c + DN_DK, c), BF16),
                   jax.ShapeDtypeStruct((bsz, n, 2, nh, DN_DV), F32)],
        compiler_params=_params(("parallel", "parallel")), name="delta_prep",
    )(qkv, ab, abt, a_vec.reshape(1, -1), d_vec.reshape(1, -1), a_vec.reshape(-1, 1), d_vec.reshape(-1, 1))

    def side(d, *tail):
        idx = (lambda b, i: (b, n - 1 - i, 1, 0) + (0,) * len(tail)) if d else (
            lambda b, i: (b, i, 0, 0) + (0,) * len(tail))
        return pl.BlockSpec((bpb, 1, 1, nh) + tail, idx)

    def side_specs(d):
        return [side(d, 2 * c, DN_DK), side(d, c, DN_DV), side(d, c + DN_DK, c),
                pl.BlockSpec((bpb, 1, 1, nh, DN_DV),
                             (lambda b, i: (b, n - 1 - i, 1, 0, 0)) if d else (lambda b, i: (b, i, 0, 0, 0)))]

    st_spec = pl.BlockSpec((bpb, nh, DN_DK, DN_DV), lambda b, i: (b, 0, 0, 0))
    o_shape = jax.ShapeDtypeStruct((bsz, length, DN_W), BF16)
    s_shape = jax.ShapeDtypeStruct((bsz, nh, DN_DK, DN_DV), F32)
    return pl.pallas_call(
        _delta_scan_kernel, grid=(bsz // bpb, n),
        in_specs=side_specs(0) + side_specs(1) + [st_spec, st_spec],
        out_specs=[pl.BlockSpec((bpb, c, DN_W), lambda b, i: (b, i, 0)),
                   pl.BlockSpec((bpb, c, DN_W), lambda b, i: (b, n - 1 - i, 0)), st_spec, st_spec],
        out_shape=[o_shape, o_shape, s_shape, s_shape],
        scratch_shapes=[pltpu.VMEM((2, bpb, nh, DN_DK, DN_DV), F32)],
        compiler_params=_params(("parallel", "arbitrary")), name="delta_scan",
    )(wq, u, ak, egl, wq, u, ak, egl, s0_f, s0_b)


def _proj_out_kernel(yf_ref, yh_ref, of_ref, ob_ref, z_ref, dg_ref, w_ref, res_ref, gate_ref, o_ref):
    acc = _dot(yf_ref[...], w_ref[0:FOUR_W, :]) + _dot(yh_ref[...], w_ref[FOUR_W:FOUR_W + HY_W, :])
    base = FOUR_W + HY_W
    per = MXU_W // DN_DV
    for h0 in range(0, DN_HEADS, per):
        parts = []
        for h in range(h0, h0 + per):
            lanes = slice(h * DN_DV, (h + 1) * DN_DV)
            o = of_ref[:, lanes].astype(F32) + ob_ref[:, lanes].astype(F32)
            parts.append((_rms(o, dg_ref[...]) * _silu(z_ref[:, lanes].astype(F32))).astype(BF16))
        acc = acc + _dot(jnp.concatenate(parts, axis=1), w_ref[base + h0 * DN_DV:base + (h0 + per) * DN_DV, :])
    o_ref[...] = res_ref[...] + gate_ref[0] * acc


def _proj_out(y_four, y_hy, o_f, o_b, p2d, dn_g, w, res, gate, rows_per_mod):
    m, d = res.shape
    tm = min(512, m)
    tn = d
    per = rows_per_mod // tm
    zb = PCOL_Z // DN_W
    return pl.pallas_call(
        _proj_out_kernel, grid=(m // tm, d // tn),
        in_specs=[pl.BlockSpec((tm, FOUR_W), lambda i, j: (i, 0)),
                  pl.BlockSpec((tm, HY_W), lambda i, j: (i, 0)),
                  pl.BlockSpec((tm, DN_W), lambda i, j: (i, 0)),
                  pl.BlockSpec((tm, DN_W), lambda i, j: (i, 0)),
                  pl.BlockSpec((tm, DN_W), lambda i, j: (i, zb)),
                  pl.BlockSpec((1, DN_DV), lambda i, j: (0, 0)),
                  pl.BlockSpec((w.shape[0], tn), lambda i, j: (0, j)),
                  pl.BlockSpec((tm, tn), lambda i, j: (i, j)),
                  pl.BlockSpec((1, 1, tn), lambda i, j: (i // per, 0, j))],
        out_specs=pl.BlockSpec((tm, tn), lambda i, j: (i, j)),
        out_shape=jax.ShapeDtypeStruct((m, d), F32),
        compiler_params=_params(("parallel", "arbitrary")), name="proj_out",
    )(y_four, y_hy, o_f, o_b, p2d, dn_g.reshape(1, DN_DV), w, res, gate)


def _pack_rows(v):
    half = v.shape[1] // 2
    bits = pltpu.bitcast(v.astype(BF16).astype(F32), jnp.uint32)
    return (bits[:, :half] >> 16) | (bits[:, half:] & jnp.uint32(0xFFFF0000))


def _unpack_rows(u, dtype):
    lo = pltpu.bitcast(u << 16, F32).astype(dtype)
    hi = pltpu.bitcast(u & jnp.uint32(0xFFFF0000), F32).astype(dtype)
    return jnp.concatenate([lo, hi], axis=1)


def _select_experts(lg):
    lane = lax.broadcasted_iota(jnp.int32, lg.shape, 1)
    neg = -jnp.inf
    first = lambda mask: jnp.min(jnp.where(mask, lane, ROUTER_W), axis=1, keepdims=True)
    top = lambda mask: jnp.max(jnp.where(mask, lg, neg), axis=1, keepdims=True)
    coarse = lane < N_GROUPS
    m = top(coarse)
    p_sel = 1.0 / jnp.sum(jnp.where(coarse, jnp.exp(lg - m), 0.0), axis=1, keepdims=True)
    grp = first(coarse & (lg == m))
    lo = N_GROUPS + EXPERTS_PER_GROUP * grp
    fine = (lane >= lo) & (lane < lo + EXPERTS_PER_GROUP)
    v1 = top(fine)
    i1 = first(fine & (lg == v1))
    rest = fine & (lane != i1)
    v2 = top(rest)
    i2 = first(rest & (lg == v2))
    e2 = jnp.exp(v2 - v1)
    g1 = p_sel / (1.0 + e2)
    ids = jnp.where(lane == 0, i1 - N_GROUPS, jnp.where(lane == 1, i2 - N_GROUPS, 0))
    gates = jnp.where(lane == 0, g1, jnp.where(lane == 1, g1 * e2, 0.0))
    return ids, gates


def _router_kernel(x_ref, c_ref, g_ref, sx_ref, cx_ref, sc_ref, cc_ref, w_ref, b_ref, h_ref, id_ref, gt_ref,
                   *, n_x):
    i = pl.program_id(0)

    def emit(v, shift, scale):
        h = _rms(v, g_ref[...]) * (1.0 + scale) + shift
        h_ref[...] = _pack_rows(h)
        lg = jnp.dot(h, w_ref[...], precision=HIGHEST, preferred_element_type=F32) + b_ref[...]
        id_ref[...], gt_ref[...] = _select_experts(lg)

    @pl.when(i < n_x)
    def _():
        emit(x_ref[...], sx_ref[0], cx_ref[0])

    @pl.when(i >= n_x)
    def _():
        emit(c_ref[...], sc_ref[0], cc_ref[0])


def _router(x2d, c2d, g, shift_x, scale_x, shift_c, scale_c, w_r, b_r, rows_per_mod, with_ctx):
    mx, d = x2d.shape
    tm = 512
    n_x = mx // tm
    n_c = c2d.shape[0] // tm if with_ctx else 0
    per = rows_per_mod // tm
    xi = lambda i: (jnp.minimum(i, n_x - 1), 0)
    ci = lambda i: (jnp.maximum(i - n_x, 0), 0)
    h, ids, gates = pl.pallas_call(
        functools.partial(_router_kernel, n_x=n_x), grid=(n_x + n_c,),
        in_specs=[pl.BlockSpec((tm, d), xi), pl.BlockSpec((tm, d), ci),
                  pl.BlockSpec((1, d), lambda i: (0, 0)),
                  pl.BlockSpec((1, 1, d), lambda i: (jnp.minimum(i, n_x - 1) // per, 0, 0)),
                  pl.BlockSpec((1, 1, d), lambda i: (jnp.minimum(i, n_x - 1) // per, 0, 0)),
                  pl.BlockSpec((1, 1, d), lambda i: (0, 0, 0)),
                  pl.BlockSpec((1, 1, d), lambda i: (0, 0, 0)),
                  pl.BlockSpec((d, ROUTER_W), lambda i: (0, 0)),
                  pl.BlockSpec((1, ROUTER_W), lambda i: (0, 0))],
        out_specs=[pl.BlockSpec((tm, d // 2), lambda i: (i, 0)),
                   pl.BlockSpec((tm, ROUTER_W), lambda i: (i, 0)),
                   pl.BlockSpec((tm, ROUTER_W), lambda i: (i, 0))],
        out_shape=[jax.ShapeDtypeStruct(((n_x + n_c) * tm, d // 2), jnp.uint32),
                   jax.ShapeDtypeStruct(((n_x + n_c) * tm, ROUTER_W), jnp.int32),
                   jax.ShapeDtypeStruct(((n_x + n_c) * tm, ROUTER_W), F32)],
        compiler_params=_params(("parallel",)), name="moe_router",
    )(x2d, c2d, g.reshape(1, d), shift_x, scale_x, shift_c, scale_c, w_r, b_r)
    return h, ids[:, :TOP_K], gates[:, :TOP_K]


def _row_copy(src_hbm, row, dst, slot, r, sem):
    return pltpu.make_async_copy(src_hbm.at[pl.ds(row, 1)], dst.at[slot, pl.ds(r, 1)], sem.at[slot])


def _expert_kernel(be_ref, tok_ref, nb_ref, h_hbm, wg_ref, wu_ref, wd_ref, ys_ref, xbuf, sem, *, bm):
    i = pl.program_id(0)
    nb = nb_ref[0]

    def start(blk, slot):
        def body(r, carry):
            _row_copy(h_hbm, tok_ref[blk * bm + r], xbuf, slot, r, sem).start()
            return carry
        lax.fori_loop(0, bm, body, 0, unroll=8)

    def wait(slot):
        pltpu.make_async_copy(h_hbm.at[pl.ds(0, bm)], xbuf.at[slot], sem.at[slot]).wait()

    @pl.when(i == 0)
    def _():
        start(0, 0)

    @pl.when(i + 1 < nb)
    def _():
        start(i + 1, (i + 1) % 2)

    @pl.when(i < nb)
    def _():
        slot = i % 2
        wait(slot)
        x = _unpack_rows(xbuf[slot], BF16)
        act = (_silu(_dot(x, wg_ref[0, 0])) * _dot(x, wu_ref[0, 0])).astype(BF16)
        ys_ref[...] = _pack_rows(_dot(act, wd_ref[0, 0]))

    @pl.when(i >= nb)
    def _():
        ys_ref[...] = jnp.zeros_like(ys_ref)


def _experts(h, buf_tok, block_e, n_used, wg, wu, wd, layer, bm):
    n_blocks = block_e.shape[0]
    dp = h.shape[1]
    d, ff = wg.shape[2], wg.shape[3]
    grid_spec = pltpu.PrefetchScalarGridSpec(
        num_scalar_prefetch=3, grid=(n_blocks,),
        in_specs=[pl.BlockSpec(memory_space=pl.ANY),
                  pl.BlockSpec((1, 1, d, ff), lambda i, be, tok, nb: (layer, be[i], 0, 0)),
                  pl.BlockSpec((1, 1, d, ff), lambda i, be, tok, nb: (layer, be[i], 0, 0)),
                  pl.BlockSpec((1, 1, ff, d), lambda i, be, tok, nb: (layer, be[i], 0, 0))],
        out_specs=pl.BlockSpec((bm, dp), lambda i, be, tok, nb: (i, 0)),
        scratch_shapes=[pltpu.VMEM((2, bm, dp), jnp.uint32), pltpu.SemaphoreType.DMA((2,))])
    return pl.pallas_call(
        functools.partial(_expert_kernel, bm=bm), grid_spec=grid_spec,
        out_shape=jax.ShapeDtypeStruct((n_blocks * bm, dp), jnp.uint32),
        compiler_params=_params(("arbitrary",)), name="moe_experts",
    )(block_e, buf_tok, n_used, h, wg, wu, wd)


def _combine_kernel(pos_ref, ys_hbm, x_ref, gw_ref, gate_ref, fg_ref, o_ref, ybuf, sem, *, tm, tok0, final):
    i = pl.program_id(0)
    n = pl.num_programs(0)

    def start(blk, slot):
        def body(r, carry):
            t = tok0 + blk * tm + r
            _row_copy(ys_hbm, pos_ref[2 * t], ybuf, slot, r, sem).start()
            _row_copy(ys_hbm, pos_ref[2 * t + 1], ybuf, slot, tm + r, sem).start()
            return carry
        lax.fori_loop(0, tm, body, 0, unroll=8)

    @pl.when(i == 0)
    def _():
        start(0, 0)

    @pl.when(i + 1 < n)
    def _():
        start(i + 1, (i + 1) % 2)

    slot = i % 2

    pltpu.make_async_copy(ys_hbm.at[pl.ds(0, 2 * tm)], ybuf.at[slot], sem.at[slot]).wait()
    gw = gw_ref[...]
    y = (gw[:, 0:1] * _unpack_rows(ybuf[slot, 0:tm, :], F32)
         + gw[:, 1:2] * _unpack_rows(ybuf[slot, tm:2 * tm, :], F32))
    out = x_ref[...] + gate_ref[0] * y
    if final:
        out = _rms(out, fg_ref[...])
    o_ref[...] = out


def _combine(ys, pos, gw, x2d, gate, final_g, tok0, rows_per_mod, final):
    m, d = x2d.shape
    tm = 512
    per = rows_per_mod // tm
    gb = tok0 // tm
    grid_spec = pltpu.PrefetchScalarGridSpec(
        num_scalar_prefetch=1, grid=(m // tm,),
        in_specs=[pl.BlockSpec(memory_space=pl.ANY),
                  pl.BlockSpec((tm, d), lambda i, pos: (i, 0)),
                  pl.BlockSpec((tm, TOP_K), lambda i, pos: (gb + i, 0)),
                  pl.BlockSpec((1, 1, d), lambda i, pos: (i // per, 0, 0)),
                  pl.BlockSpec((1, d), lambda i, pos: (0, 0))],
        out_specs=pl.BlockSpec((tm, d), lambda i, pos: (i, 0)),
        scratch_shapes=[pltpu.VMEM((2, 2 * tm, ys.shape[1]), jnp.uint32), pltpu.SemaphoreType.DMA((2,))])
    return pl.pallas_call(
        functools.partial(_combine_kernel, tm=tm, tok0=tok0, final=final), grid_spec=grid_spec,
        out_shape=jax.ShapeDtypeStruct((m, d), F32),
        compiler_params=_params(("arbitrary",)), name="moe_combine",
    )(pos, ys, x2d, gw, gate, final_g.reshape(1, d))


def _rank_kernel(e_ref, rank_ref, cnt_ref, run_scr):
    @pl.when(pl.program_id(0) == 0)
    def _():
        run_scr[...] = jnp.zeros_like(run_scr)

    tb = e_ref.shape[0]
    lane = lax.broadcasted_iota(jnp.int32, (tb, LANE), 1)
    onehot = jnp.where(e_ref[...] == lane, 1.0, 0.0)
    ri = lax.broadcasted_iota(jnp.int32, (tb, tb), 0)
    ci = lax.broadcasted_iota(jnp.int32, (tb, tb), 1)
    before = jnp.where(ri > ci, 1.0, 0.0).astype(BF16)
    prefix = _dot(before, onehot.astype(BF16)) + run_scr[...]
    rank_ref[...] = jnp.sum(prefix * onehot, axis=1, keepdims=True).astype(jnp.int32)
    run_scr[...] += jnp.sum(onehot, axis=0, keepdims=True)
    cnt_ref[...] = run_scr[...]


def _expert_ranks(flat_e):
    a = flat_e.shape[0]
    tb = 1024 if a % 1024 == 0 else 512
    rank, cnt = pl.pallas_call(
        _rank_kernel, grid=(a // tb,),
        in_specs=[pl.BlockSpec((tb, 1), lambda i: (i, 0))],
        out_specs=[pl.BlockSpec((tb, 1), lambda i: (i, 0)), pl.BlockSpec((1, LANE), lambda i: (0, 0))],
        out_shape=[jax.ShapeDtypeStruct((a, 1), jnp.int32), jax.ShapeDtypeStruct((1, LANE), F32)],
        scratch_shapes=[pltpu.VMEM((1, LANE), F32)],
        compiler_params=_params(("arbitrary",)), name="moe_rank",
    )(flat_e.reshape(a, 1))
    return rank.reshape(a), cnt[0, :N_EXPERTS].astype(jnp.int32)


def _dispatch_plan(expert, bm):
    t = expert.shape[0]
    a = t * TOP_K
    flat_e = expert.reshape(a).astype(jnp.int32)
    rank, counts = _expert_ranks(flat_e)
    padded = (counts + bm - 1) // bm * bm
    pad_end = jnp.cumsum(padded)
    pad_start = pad_end - padded
    mine = flat_e[:, None] == jnp.arange(N_EXPERTS, dtype=jnp.int32)[None, :]
    pos = (jnp.sum(jnp.where(mine, pad_start[None, :], 0), axis=1) + rank).astype(jnp.int32)
    n_blocks = a // bm + N_EXPERTS
    flat_tok = jnp.arange(a, dtype=jnp.int32) // TOP_K
    buf_tok = jnp.zeros((n_blocks * bm,), jnp.int32).at[pos].set(flat_tok)
    n_used = (pad_end[-1] // bm).astype(jnp.int32)
    blk = jnp.minimum(jnp.arange(n_blocks, dtype=jnp.int32), n_used - 1) * bm
    block_e = jnp.sum((pad_end[None, :] <= blk[:, None]).astype(jnp.int32), axis=1)
    block_e = jnp.minimum(block_e, N_EXPERTS - 1).astype(jnp.int32)
    return pos, buf_tok, block_e, n_used.reshape(1)


def _mix_stream(p3, ab, rows, width, s0_f, s0_b, four_tabs, hy_tabs, hy_coefs, conv_w, conv_b, four_g,
                hy_bias, hy_g, a_log, dt_bias, with_mixers):
    qkv = _short_conv(p3, conv_w, conv_b, rows, width, OFF_DN, HY_CONV_CH, 3 * DN_W, 512)[0]
    o_f, o_b, s_f, s_b = _delta_rule(qkv, ab, a_log, dt_bias, s0_f, s0_b)
    if not with_mixers:
        return None, None, o_f, o_b, s_f, s_b
    if _hy_two_level(p3.shape[1]):
        y_four = _fourier2(p3, four_tabs, four_g)
    else:
        y_four = _fourier(p3, *four_tabs, four_g)
    if _hy_two_level(p3.shape[1]):
        conv3 = _short_conv(p3, conv_w, conv_b, rows, width, OFF_HY, 0, HY_CONV_CH, 256, split=3)
        y_hy = _hyena2(conv3, hy_tabs, hy_coefs, hy_bias, hy_g)
    else:
        convh = _short_conv(p3, conv_w, conv_b, rows, width, OFF_HY, 0, HY_CONV_CH, 256)[0]
        y_hy = _hyena(convh, hy_tabs, hy_coefs, hy_bias, hy_g)
    return y_four, y_hy, o_f, o_b, s_f, s_b


def _hy_two_level(length):
    return length % (8 * HY_FAST) == 0


def kernel(x, c, ctx, c_ctx, norm1_g, norm2_g, w_mod, b_mod, w_in, conv_w, conv_b, four_g, hy_w1, hy_b1, hy_w2, hy_b2, hy_w3, hy_b3, hy_w4, hy_freq, hy_bias, hy_g, dn_a_log, dn_dt_bias, dn_g, w_out, w_rc, b_rc, w_rf, b_rf, w_e_gate, w_e_up, w_e_down, final_g):
    bsz, length, d = x.shape
    lc = ctx.shape[1]
    depth = w_in.shape[0]
    rows = length // GRID_W
    bm = 256

    mod_all = _modulation(c, c_ctx, w_mod, b_mod)
    four_tables = lambda n: _four2_tables(n) if _hy_two_level(n) else _fourier_tables(n)
    four_x, four_c = four_tables(length), four_tables(lc)
    hy_tables = lambda n: _hy2_tables(n) if _hy_two_level(n) else _hyena_tables(n)
    hy_coefs = lambda n, tabs, prm: (_hy2_coefs if _hy_two_level(n) else _hyena_coefs)(n, tabs, prm)
    hy_x, hy_c = hy_tables(length), hy_tables(lc)
    zeros = jnp.zeros((bsz, DN_HEADS, DN_DK, DN_DV), F32)
    x2 = x.reshape(bsz * length, d)
    c2 = ctx.reshape(bsz * lc, d)
    wg16, wu16, wd16 = w_e_gate.astype(BF16), w_e_up.astype(BF16), w_e_down.astype(BF16)

    for l in range(depth):
        last = l == depth - 1
        mod = [m[:, None, :] for m in jnp.split(mod_all[l, :bsz], 6, axis=-1)]
        modc = [m[:, None, :] for m in jnp.split(mod_all[l, bsz:bsz + 1], 6, axis=-1)]
        wi = w_in[l]
        w_p = jnp.concatenate([wi[:, :OFF_AB], wi[:, OFF_Z:], wi[:, OFF_AB:OFF_Z],
                               jnp.zeros((d, PROJ_W - PCOL_AB - N_AB), wi.dtype)], axis=1).astype(BF16)
        w_o = w_out[l].astype(BF16)
        hy_params = (hy_w1[l], hy_b1[l], hy_w2[l], hy_b2[l], hy_w3[l], hy_b3[l], hy_w4[l], hy_freq[l])
        mix_args = (conv_w[l], conv_b[l], four_g[l], hy_bias[l], hy_g[l], dn_a_log[l], dn_dt_bias[l])

        pc, abc = _proj_in(c2, norm1_g[l], modc[0], modc[1], w_p, bsz * lc)
        pc3 = pc.reshape(bsz, lc, PROJ_W)
        abc = abc[:, :N_AB].reshape(bsz, lc, N_AB)
        coefs_c = None if last else hy_coefs(lc, hy_c, hy_params)
        yf, yh, o_f, o_b, s_f, s_b = _mix_stream(pc3, abc, 1, lc, zeros, zeros, four_c, hy_c, coefs_c,
                                                 *mix_args, with_mixers=not last)
        if not last:
            c2 = _proj_out(yf.reshape(-1, FOUR_W), yh.reshape(-1, HY_W), o_f.reshape(-1, DN_W),
                           o_b.reshape(-1, DN_W), pc, dn_g[l], w_o, c2, modc[2], bsz * lc)

        p, abx = _proj_in(x2, norm1_g[l], mod[0], mod[1], w_p, length)
        p3 = p.reshape(bsz, length, PROJ_W)
        abx = abx[:, :N_AB].reshape(bsz, length, N_AB)
        coefs_x = hy_coefs(length, hy_x, hy_params)
        yf, yh, o_f, o_b, _, _ = _mix_stream(p3, abx, rows, GRID_W, s_f, s_b, four_x, hy_x, coefs_x,
                                             *mix_args, with_mixers=True)
        x2 = _proj_out(yf.reshape(-1, FOUR_W), yh.reshape(-1, HY_W), o_f.reshape(-1, DN_W),
                       o_b.reshape(-1, DN_W), p, dn_g[l], w_o, x2, mod[2], length)

        w_r = jnp.concatenate([w_rc[l], w_rf[l], jnp.zeros((d, ROUTER_W - N_GROUPS - N_EXPERTS), F32)], axis=1)
        b_r = jnp.concatenate([b_rc[l], b_rf[l], jnp.zeros((ROUTER_W - N_GROUPS - N_EXPERTS,), F32)])[None, :]
        h, expert, gate = _router(x2, c2, norm2_g[l], mod[3], mod[4], modc[3], modc[4], w_r, b_r, length,
                                  with_ctx=not last)
        pos, buf_tok, block_e, n_used = _dispatch_plan(expert, bm)
        ys = _experts(h, buf_tok, block_e, n_used, wg16, wu16, wd16, l, bm)
        x2 = _combine(ys, pos, gate, x2, mod[5], final_g, 0, length, final=last)
        if not last:
            c2 = _combine(ys, pos, gate, c2, modc[5], final_g, bsz * length, bsz * lc, final=False)
    return x2.reshape(bsz, length, d)
```

```python
import functools
import math

import jax
import jax.numpy as jnp
from jax import lax
from jax.experimental import pallas as pl
from jax.experimental.pallas import tpu as pltpu

F32 = jnp.float32
BF16 = jnp.bfloat16
HIGHEST = lax.Precision.HIGHEST

GRID_W = 64
FOUR_W = 512
FOUR_GROUPS = 4
FOUR_GW = FOUR_W // FOUR_GROUPS
HY_W = 512
HY_ORDER = 2
DN_HEADS = 8
DN_DK = 128
DN_DV = 128
DN_W = DN_HEADS * DN_DV
DN_CHUNK = 64
DN_SOLVE_BLOCK = DN_CHUNK // 4
HY_CONV_CH = (HY_ORDER + 1) * HY_W
OFF_HY = FOUR_W
OFF_DN = OFF_HY + HY_CONV_CH
OFF_AB = OFF_DN + 3 * DN_W
N_AB = 4 * DN_HEADS
OFF_Z = OFF_AB + N_AB
HY_EMB = 33
HY_FAST_DECAY = 0.3
HY_SLOW_DECAY = 1.5
HY_TARGET = 1e-2
N_GROUPS = 4
EXPERTS_PER_GROUP = 8
N_EXPERTS = N_GROUPS * EXPERTS_PER_GROUP
TOP_K = 2
EPS = 1e-6

LANE = 128
PCOL_Z = OFF_AB
PCOL_AB = PCOL_Z + DN_W
MXU_W = 256
PROJ_TN = 5 * MXU_W
PROJ_W = -(-(PCOL_AB + LANE) // PROJ_TN) * PROJ_TN
ROUTER_W = LANE
VMEM_LIMIT = 56 * 1024 * 1024


def _params(semantics):
    return pltpu.CompilerParams(dimension_semantics=semantics, vmem_limit_bytes=VMEM_LIMIT)


def _silu(v):
    return v * jax.nn.sigmoid(v)


def _rms(v, g):
    return v * lax.rsqrt(jnp.mean(v * v, axis=-1, keepdims=True) + EPS) * g


def _dot(a, b):
    return jnp.dot(a, b, preferred_element_type=F32)


def _mod_kernel(a_ref, w_ref, b_ref, o_ref):
    o_ref[0] = jnp.dot(_silu(a_ref[...]), w_ref[0], precision=HIGHEST,
                       preferred_element_type=F32) + b_ref[0]


def _modulation(c, c_ctx, w_mod, b_mod):
    depth, d, n = w_mod.shape
    bsz = c.shape[0]
    rows = -(-(bsz + 1) // 8) * 8
    a = jnp.concatenate([c, c_ctx[None], jnp.zeros((rows - bsz - 1, d), F32)], axis=0)
    tn = 1024 if n % 1024 == 0 else 512
    assert n % tn == 0
    return pl.pallas_call(
        _mod_kernel, grid=(depth, n // tn),
        in_specs=[pl.BlockSpec((rows, d), lambda l, j: (0, 0)),
                  pl.BlockSpec((1, d, tn), lambda l, j: (l, 0, j)),
                  pl.BlockSpec((1, 1, tn), lambda l, j: (l, 0, j))],
        out_specs=pl.BlockSpec((1, rows, tn), lambda l, j: (l, 0, j)),
        out_shape=jax.ShapeDtypeStruct((depth, rows, n), F32),
        compiler_params=_params(("parallel", "parallel")), name="modulation",
    )(a, w_mod, b_mod.reshape(depth, 1, n))


def _proj_in_kernel(x_ref, g_ref, shift_ref, scale_ref, w_ref, o_ref, ab_ref, a_scr, *, ab_off):
    j = pl.program_id(1)

    @pl.when(j == 0)
    def _():
        y = _rms(x_ref[...], g_ref[...])
        a_scr[...] = (y * (1.0 + scale_ref[0]) + shift_ref[0]).astype(BF16)

    acc = _dot(a_scr[...], w_ref[...])
    o_ref[...] = acc.astype(BF16)

    @pl.when(j == pl.num_programs(1) - 1)
    def _():
        ab_ref[...] = acc[:, ab_off:ab_off + LANE]


def _proj_in(x2d, g, shift, scale, w, rows_per_mod):
    m, d = x2d.shape
    n = w.shape[1]
    tm = min(1024, m, rows_per_mod)
    tn = PROJ_TN
    assert n % tn == 0 and PCOL_AB >= n - tn
    per = rows_per_mod // tm
    return pl.pallas_call(
        functools.partial(_proj_in_kernel, ab_off=PCOL_AB - (n - tn)), grid=(m // tm, n // tn),
        in_specs=[pl.BlockSpec((tm, d), lambda i, j: (i, 0)),
                  pl.BlockSpec((1, d), lambda i, j: (0, 0)),
                  pl.BlockSpec((1, 1, d), lambda i, j: (i // per, 0, 0)),
                  pl.BlockSpec((1, 1, d), lambda i, j: (i // per, 0, 0)),
                  pl.BlockSpec((d, tn), lambda i, j: (0, j))],
        out_specs=[pl.BlockSpec((tm, tn), lambda i, j: (i, j)),
                   pl.BlockSpec((tm, LANE), lambda i, j: (i, 0))],
        out_shape=[jax.ShapeDtypeStruct((m, n), BF16), jax.ShapeDtypeStruct((m, LANE), F32)],
        scratch_shapes=[pltpu.VMEM((tm, d), BF16)],
        compiler_params=_params(("parallel", "arbitrary")), name="proj_in",
    )(x2d, g.reshape(1, d), shift, scale, w)


def _cos_sin_table(nrow, ncol, period):
    i = lax.broadcasted_iota(jnp.int32, (nrow, ncol), 0)
    j = lax.broadcasted_iota(jnp.int32, (nrow, ncol), 1)
    ang = ((i * j) % period).astype(F32) * (2.0 * math.pi / period)
    return jnp.cos(ang), jnp.sin(ang)


def _dft_tables(length, period):
    blk = 64 if length % 64 == 0 else length
    hi_i = lax.broadcasted_iota(jnp.int32, (length // blk, length), 0) * blk
    j = lax.broadcasted_iota(jnp.int32, (length // blk, length), 1)
    ang = ((hi_i * j) % period).astype(F32) * (2.0 * math.pi / period)
    hi_c, hi_s = jnp.cos(ang), jnp.sin(ang)
    lo_c, lo_s = _cos_sin_table(blk, length, period)
    c = hi_c[:, None, :] * lo_c[None, :, :] - hi_s[:, None, :] * lo_s[None, :, :]
    s = hi_s[:, None, :] * lo_c[None, :, :] + hi_c[:, None, :] * lo_s[None, :, :]
    return c.reshape(length, length), s.reshape(length, length)


def _fourier_kernel(u_ref, c_ref, s_ref, bc_ref, bs_ref, g_ref, o_ref, ub_scr, *, scale):
    @pl.when(pl.program_id(1) == 0)
    def _():
        ub_scr[...] = u_ref[0].astype(BF16)

    ub = ub_scr[...]
    p = _dot(c_ref[...], ub).astype(BF16)
    q = _dot(s_ref[...], ub).astype(BF16)
    z = (_dot(p, bc_ref[...]) - _dot(q, bs_ref[...])) * scale
    o_ref[0] = _rms(z, g_ref[...]).astype(o_ref.dtype)


def _fourier(p3, cl, sl, bc, bs, four_g):
    bsz, length, _ = p3.shape
    tm = min(512, length)
    kern = functools.partial(_fourier_kernel, scale=1.0 / math.sqrt(length * FOUR_GW))
    return pl.pallas_call(
        kern, grid=(bsz, length // tm),
        in_specs=[pl.BlockSpec((1, length, FOUR_W), lambda b, i: (b, 0, 0)),
                  pl.BlockSpec((tm, length), lambda b, i: (i, 0)),
                  pl.BlockSpec((tm, length), lambda b, i: (i, 0)),
                  pl.BlockSpec((FOUR_W, FOUR_W), lambda b, i: (0, 0)),
                  pl.BlockSpec((FOUR_W, FOUR_W), lambda b, i: (0, 0)),
                  pl.BlockSpec((1, FOUR_W), lambda b, i: (0, 0))],
        out_specs=pl.BlockSpec((1, tm, FOUR_W), lambda b, i: (b, i, 0)),
        out_shape=jax.ShapeDtypeStruct((bsz, length, FOUR_W), BF16),
        scratch_shapes=[pltpu.VMEM((length, FOUR_W), BF16)],
        compiler_params=_params(("parallel", "arbitrary")), name="fourier",
    )(p3, cl, sl, bc, bs, four_g.reshape(1, FOUR_W))


def _fourier_tables(length):
    cl, sl = _dft_tables(length, length)
    cc, sc = _cos_sin_table(FOUR_GW, FOUR_GW, FOUR_GW)
    eye = jnp.eye(FOUR_GROUPS, dtype=F32)
    return cl.astype(BF16), sl.astype(BF16), jnp.kron(eye, cc).astype(BF16), jnp.kron(eye, sc).astype(BF16)


def _conv_kernel(x_ref, w_ref, b_ref, o_ref, *, rows, width):
    tc = x_ref.shape[2]
    wts = w_ref[...]
    bias = b_ref[...]
    pos = lax.broadcasted_iota(jnp.int32, (width, tc), 0)
    not_first = pos > 0
    not_last = pos < width - 1

    def grid_row(r):
        start = r * width if isinstance(r, int) else pl.multiple_of(r * width, width)
        return x_ref[0, pl.ds(start, width), :].astype(F32)

    def body(r, carry):
        base = pl.multiple_of(r * width, width)
        up, cen = carry
        if rows > 1:
            dn = grid_row(jnp.minimum(r + 1, rows - 1))
            w_up = wts[0] * jnp.where(r > 0, 1.0, 0.0)
            w_dn = wts[2] * jnp.where(r < rows - 1, 1.0, 0.0)
            col = lambda j: up * w_up[j:j + 1, :] + cen * wts[1, j:j + 1, :] + dn * w_dn[j:j + 1, :]
        else:
            dn = cen
            col = lambda j: cen * wts[1, j:j + 1, :]
        left = jnp.where(not_first, pltpu.roll(col(0), 1, 0), 0.0)
        right = jnp.where(not_last, pltpu.roll(col(2), width - 1, 0), 0.0)
        o_ref[0, pl.ds(base, width), :] = left + col(1) + right + bias
        return cen, dn

    first = grid_row(0)
    lax.fori_loop(0, rows, body, (first, first))


def _short_conv(p3, conv_w, conv_b, rows, width, col0, ch0, nch, tc, split=1):
    bsz, length, _ = p3.shape
    kern = functools.partial(_conv_kernel, rows=rows, width=width)
    cb, wb = col0 // tc, ch0 // tc
    per = nch // split // tc
    out = pl.pallas_call(
        kern, grid=(bsz, nch // tc),
        in_specs=[pl.BlockSpec((1, length, tc), lambda b, j: (b, 0, cb + j)),
                  pl.BlockSpec((3, 3, tc), lambda b, j: (0, 0, wb + j)),
                  pl.BlockSpec((1, tc), lambda b, j: (0, wb + j))],
        out_specs=pl.BlockSpec((1, length, tc), lambda b, j: ((j // per) * bsz + b, 0, j % per)),
        out_shape=jax.ShapeDtypeStruct((split * bsz, length, nch // split), F32),
        compiler_params=_params(("parallel", "parallel")),
        name="short_conv",
    )(p3, conv_w, conv_b.reshape(1, -1))
    return out.reshape(split, bsz, length, nch // split)


def _filt_kernel(z_ref, w1_ref, b1_ref, w2_ref, b2_ref, w3_ref, b3_ref, w4_ref, fr_ref, dl_ref,
                 g_ref, nrm_ref):
    i = pl.program_id(0)
    z = z_ref[...]
    fr = fr_ref[...]
    hdot = lambda a, b: jnp.dot(a, b, precision=HIGHEST, preferred_element_type=F32)
    h = jnp.sin(fr * (hdot(z, w1_ref[...]) + b1_ref[...]))
    h = jnp.sin(fr * (hdot(h, w2_ref[...]) + b2_ref[...]))
    h = jnp.sin(fr * (hdot(h, w3_ref[...]) + b3_ref[...]))
    h = hdot(h, w4_ref[...])
    decay = jnp.exp(-z[:, 0:1] * jnp.abs(dl_ref[...]))
    decay = jnp.concatenate([decay] * HY_ORDER, axis=1)
    half = HY_ORDER * HY_W
    hf = h[:, :half] * decay
    hb = h[:, half:] * decay
    row = lax.broadcasted_iota(jnp.int32, hb.shape, 0) + i * hb.shape[0]
    hb = jnp.where(row > 0, hb, 0.0)
    gp = hf + hb
    gm = hf - hb
    for o in range(HY_ORDER):
        g_ref[o] = gp[:, o * HY_W:(o + 1) * HY_W].astype(BF16)
        g_ref[HY_ORDER + o] = gm[:, o * HY_W:(o + 1) * HY_W].astype(BF16)
    part = jnp.sum(jnp.abs(hf) + jnp.abs(hb), axis=0, keepdims=True)

    @pl.when(i == 0)
    def _():
        nrm_ref[...] = part

    @pl.when(i > 0)
    def _():
        nrm_ref[...] += part


def _hyena_filter_taps(length, w1, b1, w2, b2, w3, b3, w4, freq):
    t = jnp.linspace(0.0, 1.0, length, dtype=F32)[:, None]
    bands = (HY_EMB - 1) // 2
    ang = (2.0 * math.pi / length) * jnp.arange(length, dtype=F32)[:, None]
    f = jnp.linspace(1e-4, bands - 1, bands, dtype=F32)
    z = jnp.concatenate([t, jnp.cos(f * ang), -jnp.sin(f * ang)], axis=-1)
    max_decay = math.log(HY_TARGET) / HY_FAST_DECAY
    min_decay = math.log(HY_TARGET) / HY_SLOW_DECAY
    deltas = jnp.linspace(min_decay, max_decay, HY_W, dtype=F32)[None, :]
    fw = w1.shape[1]
    tl = min(512, length)
    full = lambda shape: pl.BlockSpec(shape, lambda i: (0,) * len(shape))
    return pl.pallas_call(
        _filt_kernel, grid=(length // tl,),
        in_specs=[pl.BlockSpec((tl, HY_EMB), lambda i: (i, 0)),
                  full((HY_EMB, fw)), full((1, fw)), full((fw, fw)), full((1, fw)),
                  full((fw, fw)), full((1, fw)), full((fw, 2 * HY_ORDER * HY_W)), full((1, fw)),
                  full((1, HY_W))],
        out_specs=[pl.BlockSpec((2 * HY_ORDER, tl, HY_W), lambda i: (0, i, 0)),
                   pl.BlockSpec((1, HY_ORDER * HY_W), lambda i: (0, 0))],
        out_shape=[jax.ShapeDtypeStruct((2 * HY_ORDER, length, HY_W), BF16),
                   jax.ShapeDtypeStruct((1, HY_ORDER * HY_W), F32)],
        compiler_params=_params(("arbitrary",)), name="hyena_filter",
    )(z, w1, b1.reshape(1, fw), w2, b2.reshape(1, fw), w3, b3.reshape(1, fw), w4,
      freq.reshape(1, fw), deltas)


def _dft_fwd_kernel(u_ref, c_ref, s_ref, *rest, with_coef):
    if with_coef:
        c1_ref, c2_ref, c4_ref, yr_ref, yi_ref, ub_scr = rest
    else:
        yr_ref, yi_ref, ub_scr = rest

    @pl.when(pl.program_id(1) == 0)
    def _():
        ub_scr[...] = u_ref[0].astype(BF16)

    ub = ub_scr[...]
    a = _dot(c_ref[...], ub)
    b = _dot(s_ref[...], ub)
    if with_coef:
        c2 = c2_ref[...]
        yr_ref[0] = (a * c1_ref[...] + b * c2).astype(BF16)
        yi_ref[0] = (b * c4_ref[...] - a * c2).astype(BF16)
    else:
        yr_ref[0] = a
        yi_ref[0] = b


def _dft_fwd(u3, col_blk, cf, sf, coef=None):
    nb, length, _ = u3.shape
    tm = min(512, length)
    in_specs = [pl.BlockSpec((1, length, HY_W), lambda b, i: (b, 0, col_blk)),
                pl.BlockSpec((tm, length), lambda b, i: (i, 0)),
                pl.BlockSpec((tm, length), lambda b, i: (i, 0))]
    args = [u3, cf, sf]
    if coef is not None:
        in_specs += [pl.BlockSpec((tm, HY_W), lambda b, i: (i, 0))] * 3
        args += list(coef)
    odt = BF16 if coef is not None else F32
    return pl.pallas_call(
        functools.partial(_dft_fwd_kernel, with_coef=coef is not None), grid=(nb, length // tm),
        in_specs=in_specs,
        out_specs=[pl.BlockSpec((1, tm, HY_W), lambda b, i: (b, i, 0))] * 2,
        out_shape=[jax.ShapeDtypeStruct((nb, length, HY_W), odt)] * 2,
        scratch_shapes=[pltpu.VMEM((length, HY_W), BF16)],
        compiler_params=_params(("parallel", "arbitrary")),
        name="hyena_dft_fwd" if coef is not None else "hyena_filter_dft",
    )(*args)


def _dft_inv_kernel(yr_ref, yi_ref, c_ref, s_ref, u_ref, gate_ref, bias_ref, g_ref, o_ref, *, final):
    y = _dot(c_ref[...], yr_ref[0]) + _dot(s_ref[...], yi_ref[0])
    out = gate_ref[0] * (y + bias_ref[...] * u_ref[0])
    if final:
        out = _rms(out, g_ref[...])
    o_ref[0] = out.astype(o_ref.dtype)


def _dft_inv(yr, yi, cf, s_inv, u3, u_blk, gate3, gate_blk, bias, g, final):
    bsz, length, _ = yr.shape
    tm = min(512, length)
    return pl.pallas_call(
        functools.partial(_dft_inv_kernel, final=final), grid=(bsz, length // tm),
        in_specs=[pl.BlockSpec((1, length, HY_W), lambda b, i: (b, 0, 0)),
                  pl.BlockSpec((1, length, HY_W), lambda b, i: (b, 0, 0)),
                  pl.BlockSpec((tm, length), lambda b, i: (i, 0)),
                  pl.BlockSpec((tm, length), lambda b, i: (i, 0)),
                  pl.BlockSpec((1, tm, HY_W), lambda b, i: (b, i, u_blk)),
                  pl.BlockSpec((1, tm, HY_W), lambda b, i: (b, i, gate_blk)),
                  pl.BlockSpec((1, HY_W), lambda b, i: (0, 0)),
                  pl.BlockSpec((1, HY_W), lambda b, i: (0, 0))],
        out_specs=pl.BlockSpec((1, tm, HY_W), lambda b, i: (b, i, 0)),
        out_shape=jax.ShapeDtypeStruct((bsz, length, HY_W), BF16 if final else F32),
        compiler_params=_params(("parallel", "arbitrary")), name="hyena_dft_inv",
    )(yr, yi, cf, s_inv, u3, gate3, bias.reshape(1, HY_W), g.reshape(1, HY_W))


def _hyena_tables(length):
    cf, sf = _dft_tables(length, 2 * length)
    alt = jnp.where(jnp.arange(length) % 2 == 0, 1.0, -1.0).astype(F32)
    sf = sf.at[0, :].set(alt)
    return cf.astype(BF16), sf.astype(BF16), sf.T.astype(BF16)


def _hyena_coefs(length, tables, hy_params):
    cf, sf, _ = tables
    taps, nrm = _hyena_filter_taps(length, *hy_params)
    a, b = _dft_fwd(taps, 0, cf, sf)
    inv = (1.0 / nrm).reshape(HY_ORDER, 1, HY_W)
    n = 2.0 * length
    hr = a[:HY_ORDER] * inv
    hi = -b[HY_ORDER:] * inv
    nyq = b[:HY_ORDER, 0:1, :] * inv
    first = (jnp.arange(length) == 0)[None, :, None]
    c1 = jnp.where(first, hr / n, hr * (2.0 / n))
    c2 = jnp.where(first, 0.0, hi * (2.0 / n))
    c4 = jnp.where(first, nyq / n, hr * (2.0 / n))
    return c1, c2, c4


def _hyena(convh, tables, coefs, hy_bias, hy_g):
    cf, sf, s_inv = tables
    c1, c2, c4 = coefs
    yr, yi = _dft_fwd(convh, 0, cf, sf, (c1[0], c2[0], c4[0]))
    zz = _dft_inv(yr, yi, cf, s_inv, convh, 0, convh, 1, hy_bias[0], hy_g, False)
    yr, yi = _dft_fwd(zz, 0, cf, sf, (c1[1], c2[1], c4[1]))
    return _dft_inv(yr, yi, cf, s_inv, zz, 0, convh, 2, hy_bias[1], hy_g, True)


HY_FAST = 128


def _hy2_tables(length):
    nk = length // HY_FAST
    n2 = 2 * length
    k1 = lax.broadcasted_iota(jnp.int32, (nk, nk), 0)
    s1 = lax.broadcasted_iota(jnp.int32, (nk, nk), 1)
    ang = (((2 * k1 + 1) * s1) % (4 * nk)).astype(F32) * (2.0 * math.pi / (4 * nk))
    f1 = jnp.concatenate([jnp.cos(ang), -jnp.sin(ang)], axis=0)
    shape = (nk, HY_FAST, HY_FAST)
    kk = lax.broadcasted_iota(jnp.int32, shape, 0) + 2 * nk * lax.broadcasted_iota(jnp.int32, shape, 1)
    s2 = lax.broadcasted_iota(jnp.int32, shape, 2)
    phi = (((2 * kk + 1) * s2) % (2 * n2)).astype(F32) * (2.0 * math.pi / (2 * n2))
    c, s = jnp.cos(phi), jnp.sin(phi)
    f2 = jnp.concatenate([jnp.concatenate([c, s], axis=2), jnp.concatenate([-s, c], axis=2)], axis=1)
    f1k = jnp.kron(f1, jnp.eye(HY_SUB, dtype=F32))
    return f1k.astype(BF16), f2.astype(BF16), jnp.swapaxes(f2, 1, 2).astype(BF16), f1k.T.astype(BF16)


HY_SUB = 8


def _hy2_stage1_kernel(x_ref, f1_ref, z_ref):
    nk = x_ref.shape[2]
    tc = x_ref.shape[4]
    for g in range(HY_FAST // HY_SUB):
        rows = slice(g * HY_SUB, (g + 1) * HY_SUB)
        rhs = x_ref[0, 0, :, rows, :].reshape(nk * HY_SUB, tc).astype(BF16)
        z = _dot(f1_ref[...], rhs).astype(BF16)
        z_ref[0, :, :, rows, :] = z.reshape(2, nk, HY_SUB, tc)


def _hy2_stage1(x5, which, f1k, cw=None):
    _, nb, nk, _, call = x5.shape
    cw = call if cw is None else cw
    tc = min(256, cw)
    return pl.pallas_call(
        _hy2_stage1_kernel, grid=(nb, cw // tc),
        in_specs=[pl.BlockSpec((1, 1, nk, HY_FAST, tc), lambda b, j: (which, b, 0, 0, j)),
                  pl.BlockSpec(f1k.shape, lambda b, j: (0, 0))],
        out_specs=pl.BlockSpec((1, 2, nk, HY_FAST, tc), lambda b, j: (b, 0, 0, 0, j)),
        out_shape=jax.ShapeDtypeStruct((nb, 2, nk, HY_FAST, cw), BF16),
        compiler_params=_params(("parallel", "parallel")), name="hyena2_stage1",
    )(x5, f1k)


def _hy2_stage2_kernel(zr_ref, zi_ref, f2_ref, *rest, filtered):
    if filtered:
        g2_ref, hr_ref, hi_ref, vr_ref, vi_ref = rest
    else:
        vr_ref, vi_ref = rest
    for k in range(zr_ref.shape[2]):
        x = _dot(f2_ref[k], jnp.concatenate([zr_ref[0, 0, k], zi_ref[0, 0, k]], axis=0))
        xr, xi = x[:HY_FAST], x[HY_FAST:]
        if filtered:
            hr, hi = hr_ref[k], hi_ref[k]
            y = jnp.concatenate([xr * hr - xi * hi, xr * hi + xi * hr], axis=0).astype(BF16)
            v = _dot(g2_ref[k], y)
            vr_ref[0, 0, k] = v[:HY_FAST].astype(BF16)
            vi_ref[0, 0, k] = v[HY_FAST:].astype(BF16)
        else:
            vr_ref[0, 0, k] = xr
            vi_ref[0, 0, k] = xi


def _hy2_stage2(z, f2, g2=None, hr=None, hi=None):
    nb, _, nk, _, cw = z.shape
    kb = 8
    filtered = hr is not None
    zspec = lambda part: pl.BlockSpec((1, 1, kb, HY_FAST, cw), lambda k, b: (b, part, k, 0, 0))
    mat = pl.BlockSpec((kb, 2 * HY_FAST, 2 * HY_FAST), lambda k, b: (k, 0, 0))
    in_specs = [zspec(0), zspec(1), mat]
    args = [z, z, f2]
    if filtered:
        in_specs += [mat, pl.BlockSpec((kb, HY_FAST, cw), lambda k, b: (k, 0, 0)),
                     pl.BlockSpec((kb, HY_FAST, cw), lambda k, b: (k, 0, 0))]
        args += [g2, hr, hi]
    return pl.pallas_call(
        functools.partial(_hy2_stage2_kernel, filtered=filtered), grid=(nk // kb, nb),
        in_specs=in_specs, out_specs=[zspec(0), zspec(0)],
        out_shape=[jax.ShapeDtypeStruct((nb, 1, nk, HY_FAST, cw), BF16 if filtered else F32)] * 2,
        compiler_params=_params(("parallel", "parallel")),
        name="hyena2_stage2" if filtered else "hyena2_filter_spectrum",
    )(*args)


def _hy2_inv1_kernel(vr_ref, vi_ref, g1_ref, u_ref, gate_ref, bias_ref, g_ref, o_ref, *, final):
    nk = vr_ref.shape[2]
    cw = vr_ref.shape[4]
    for g in range(vr_ref.shape[3] // HY_SUB):
        rows = slice(g * HY_SUB, (g + 1) * HY_SUB)
        rhs = jnp.concatenate([vr_ref[0, 0, :, rows, :].reshape(nk * HY_SUB, cw),
                               vi_ref[0, 0, :, rows, :].reshape(nk * HY_SUB, cw)], axis=0)
        y = _dot(g1_ref[...], rhs).reshape(nk, HY_SUB, cw)
        out = gate_ref[0, 0, :, rows, :] * (y + bias_ref[...] * u_ref[0, 0, :, rows, :])
        if final:
            out = _rms(out, g_ref[...])
        o_ref[0, :, rows, :] = out.astype(o_ref.dtype)


def _hy2_inv1(vr, vi, g1k, u5, which_u, gate5, which_g, bias, g, final):
    nb, _, nk, _, cw = vr.shape
    tg = 64
    vspec = pl.BlockSpec((1, 1, nk, tg, cw), lambda b, j: (b, 0, 0, j, 0))
    return pl.pallas_call(
        functools.partial(_hy2_inv1_kernel, final=final), grid=(nb, HY_FAST // tg),
        in_specs=[vspec, vspec,
                  pl.BlockSpec(g1k.shape, lambda b, j: (0, 0)),
                  pl.BlockSpec((1, 1, nk, tg, cw), lambda b, j: (which_u, b, 0, j, 0)),
                  pl.BlockSpec((1, 1, nk, tg, cw), lambda b, j: (which_g, b, 0, j, 0)),
                  pl.BlockSpec((1, cw), lambda b, j: (0, 0)),
                  pl.BlockSpec((1, cw), lambda b, j: (0, 0))],
        out_specs=pl.BlockSpec((1, nk, tg, cw), lambda b, j: (b, 0, j, 0)),
        out_shape=jax.ShapeDtypeStruct((nb, nk, HY_FAST, cw), BF16 if final else F32),
        compiler_params=_params(("parallel", "parallel")), name="hyena2_inv1",
    )(vr, vi, g1k, u5, gate5, bias.reshape(1, cw), g.reshape(1, cw))


def _hy2_coefs(length, tables, hy_params):
    f1k, f2, _, _ = tables
    nk = length // HY_FAST
    taps, nrm = _hyena_filter_taps(length, *hy_params)
    z = _hy2_stage1(taps.reshape(1, 2 * HY_ORDER, nk, HY_FAST, HY_W), 0, f1k)
    xr, xi = _hy2_stage2(z, f2)
    scale = (1.0 / length) / nrm.reshape(HY_ORDER, 1, 1, HY_W)
    return xr[:HY_ORDER, 0] * scale, xi[HY_ORDER:, 0] * scale


def _hyena2(conv3, tables, coefs, hy_bias, hy_g):
    f1k, f2, g2, g1k = tables
    hr, hi = coefs
    _, bsz, length, cw = conv3.shape
    nk = length // HY_FAST
    c5 = conv3.reshape(3, bsz, nk, HY_FAST, cw)

    def conv(x5, which, order, gate_idx, final):
        vr, vi = _hy2_stage2(_hy2_stage1(x5, which, f1k), f2, g2, hr[order], hi[order])
        return _hy2_inv1(vr, vi, g1k, x5, which, c5, gate_idx, hy_bias[order], hy_g, final)

    zz = conv(c5, 0, 0, 1, False)
    return conv(zz[None], 0, 1, 2, True).reshape(bsz, length, cw)


def _four2_tables(length):
    nk = length // HY_FAST
    k1 = lax.broadcasted_iota(jnp.int32, (nk, nk), 0)
    s1 = lax.broadcasted_iota(jnp.int32, (nk, nk), 1)
    ang = ((k1 * s1) % nk).astype(F32) * (2.0 * math.pi / nk)
    f1 = jnp.concatenate([jnp.cos(ang), -jnp.sin(ang)], axis=0)
    shape = (nk, HY_FAST, HY_FAST)
    kk = lax.broadcasted_iota(jnp.int32, shape, 0) + nk * lax.broadcasted_iota(jnp.int32, shape, 1)
    s2 = lax.broadcasted_iota(jnp.int32, shape, 2)
    phi = ((kk * s2) % length).astype(F32) * (2.0 * math.pi / length)
    c, s = jnp.cos(phi), jnp.sin(phi)
    f2 = jnp.concatenate([jnp.concatenate([c, s], axis=2), jnp.concatenate([-s, c], axis=2)], axis=1)
    cc, sc = _cos_sin_table(FOUR_GW, FOUR_GW, FOUR_GW)
    eye = jnp.eye(FOUR_GROUPS, dtype=F32)
    return (jnp.kron(f1, jnp.eye(HY_SUB, dtype=F32)).astype(BF16), f2.astype(BF16),
            jnp.kron(eye, cc).astype(BF16), jnp.kron(eye, sc).astype(BF16))


def _four2_stage2_kernel(zr_ref, zi_ref, f2_ref, bc_ref, bs_ref, g_ref, o_ref, *, scale):
    for k in range(zr_ref.shape[2]):
        x = _dot(f2_ref[k], jnp.concatenate([zr_ref[0, 0, k], zi_ref[0, 0, k]], axis=0))
        xr, xi = x[:HY_FAST].astype(BF16), x[HY_FAST:].astype(BF16)
        z = (_dot(xr, bc_ref[...]) + _dot(xi, bs_ref[...])) * scale
        o_ref[0, k] = _rms(z, g_ref[...]).astype(o_ref.dtype)


def _fourier2(p3, tables, four_g):
    f1k, f2, bc, bs = tables
    bsz, length, pw = p3.shape
    nk = length // HY_FAST
    z = _hy2_stage1(p3.reshape(1, bsz, nk, HY_FAST, pw), 0, f1k, cw=FOUR_W)
    kb = 8
    zspec = lambda part: pl.BlockSpec((1, 1, kb, HY_FAST, FOUR_W), lambda k, b: (b, part, k, 0, 0))
    const = lambda shape: pl.BlockSpec(shape, lambda k, b: (0,) * len(shape))
    out = pl.pallas_call(
        functools.partial(_four2_stage2_kernel, scale=1.0 / math.sqrt(length * FOUR_GW)),
        grid=(nk // kb, bsz),
        in_specs=[zspec(0), zspec(1),
                  pl.BlockSpec((kb, 2 * HY_FAST, 2 * HY_FAST), lambda k, b: (k, 0, 0)),
                  const((FOUR_W, FOUR_W)), const((FOUR_W, FOUR_W)), const((1, FOUR_W))],
        out_specs=pl.BlockSpec((1, kb, HY_FAST, FOUR_W), lambda k, b: (b, k, 0, 0)),
        out_shape=jax.ShapeDtypeStruct((bsz, nk, HY_FAST, FOUR_W), BF16),
        compiler_params=_params(("parallel", "parallel")), name="fourier2_stage2",
    )(z, z, f2, bc, bs, four_g.reshape(1, FOUR_W))
    return jnp.swapaxes(out, 1, 2).reshape(bsz, length, FOUR_W)


def _softplus(v):
    return jnp.maximum(v, 0.0) + jnp.log(1.0 + jnp.exp(-jnp.abs(v)))


def _delta_prep_kernel(qkv_ref, ab_ref, abt_ref, arow_ref, drow_ref, acol_ref, dcol_ref,
                       wq_ref, u_ref, ak_ref, egl_ref):
    nch = abt_ref.shape[1]
    c = ab_ref.shape[1] // nch
    nh = DN_HEADS
    ri = lax.broadcasted_iota(jnp.int32, (c, c), 0)
    ci = lax.broadcasted_iota(jnp.int32, (c, c), 1)
    incl = (ri >= ci, ri <= ci)
    strict = (ri > ci, ri < ci)
    low = jnp.where(incl[0], 1.0, 0.0)
    upp = jnp.where(incl[1], 1.0, 0.0)
    hdot = lambda a, b: jnp.dot(a, b, precision=HIGHEST, preferred_element_type=F32)
    nt = (((1,), (1,)), ((), ()))
    beta_all, gcum, gcum_t = {}, {}, {}
    for cc in range(nch):
        ab = ab_ref[0, cc * c:(cc + 1) * c, :]
        abt = abt_ref[0, cc]
        gate = -arow_ref[...] * _softplus(ab + drow_ref[...])
        gate_t = -acol_ref[...] * _softplus(abt + dcol_ref[...])
        beta_all[cc] = jax.nn.sigmoid(ab)
        gcum[cc] = (hdot(low, gate), hdot(upp, gate))
        gcum_t[cc] = (hdot(gate_t, upp), hdot(gate_t, low))

    def l2n(t):
        return t * lax.rsqrt(jnp.sum(t * t, axis=-1, keepdims=True) + EPS)

    heads = [(cc, h) for cc in range(nch) for h in range(nh)]
    rows = lambda cc: slice(cc * c, (cc + 1) * c)
    q = {(cc, h): l2n(_silu(qkv_ref[0, rows(cc), h * DN_DK:(h + 1) * DN_DK])) * (DN_DK ** -0.5) for cc, h in heads}
    k = {(cc, h): l2n(_silu(qkv_ref[0, rows(cc), DN_W + h * DN_DK:DN_W + (h + 1) * DN_DK])) for cc, h in heads}
    v = {(cc, h): _silu(qkv_ref[0, rows(cc), 2 * DN_W + h * DN_DV:2 * DN_W + (h + 1) * DN_DV]) for cc, h in heads}
    k16 = {u: k[u].astype(BF16) for u in heads}
    kk0 = {u: lax.dot_general(k16[u], k16[u], nt, preferred_element_type=F32) for u in heads}
    qk0 = {u: lax.dot_general(q[u].astype(BF16), k16[u], nt, preferred_element_type=F32) for u in heads}
    units = [(cc, d, h) for cc in range(nch) for d in range(2) for h in range(nh)]
    mm, rr = {}, {}
    for cc, d, h in units:
        col = d * nh + h
        gc = gcum[cc][d][:, col:col + 1]
        gr = gcum_t[cc][d][col:col + 1, :]
        be = beta_all[cc][:, 2 * nh + col:2 * nh + col + 1]
        last = 0 if d else c - 1
        gl = gc[last:last + 1, :]
        dec = jnp.where(incl[d], jnp.exp(jnp.where(incl[d], gc - gr, 0.0)), 0.0)
        mm[cc, d, h] = jnp.where(strict[d], be * kk0[cc, h] * dec, 0.0)
        ak_ref[0, cc, d, h, 0:c, :] = jnp.where(incl[d], qk0[cc, h] * dec, 0.0).astype(BF16)
        eg = jnp.exp(gc)
        rr[cc, d, h] = jnp.concatenate([v[cc, h] * be, k[cc, h] * (be * eg)], axis=1)
        wq_ref[0, cc, d, h, c:2 * c, :] = (q[cc, h] * eg).astype(BF16)
        k_tail = k[cc, h] * jnp.exp(gl - gc)
        ak_ref[0, cc, d, h, c:c + DN_DK, :] = jnp.transpose(k_tail).astype(BF16)
        egl_ref[0, cc, d, h:h + 1, :] = jnp.broadcast_to(jnp.exp(gl), (1, DN_DV))
    sb = DN_SOLVE_BLOCK
    in_sb = (ri // sb) == (ci // sb)
    in_2sb = (ri // (2 * sb)) == (ci // (2 * sb))
    eye = jnp.where(ri == ci, 1.0, 0.0)
    b16 = lambda t: t.astype(BF16)
    nj = {u: jnp.where(in_sb, -mm[u], 0.0) for u in units}
    inv = {u: eye + nj[u] for u in units}
    for j in range((sb - 1).bit_length() - 1):
        nj = {u: _dot(b16(nj[u]), b16(nj[u])) for u in units}
        inv = {u: inv[u] + _dot(b16(inv[u]), b16(nj[u])) for u in units}
    off = {u: b16(jnp.where(in_2sb & ~in_sb, mm[u], 0.0)) for u in units}
    tmp = {u: _dot(off[u], b16(inv[u])) for u in units}
    inv = {u: inv[u] - _dot(b16(inv[u]), b16(tmp[u])) for u in units}
    inv16 = {u: b16(inv[u]) for u in units}
    off = {u: b16(jnp.where(in_2sb, 0.0, mm[u])) for u in units}
    part = {u: _dot(inv16[u], b16(rr[u])) for u in units}
    tmp = {u: _dot(off[u], b16(part[u])) for u in units}
    rr = {u: part[u] - _dot(inv16[u], b16(tmp[u])) for u in units}
    for cc, d, h in units:
        u_ref[0, cc, d, h] = rr[cc, d, h][:, :DN_DV]
        wq_ref[0, cc, d, h, 0:c, :] = rr[cc, d, h][:, DN_DV:].astype(BF16)


def _delta_scan_kernel(wqf_ref, uf_ref, akf_ref, egf_ref, wqb_ref, ub_ref, akb_ref, egb_ref, s0f_ref, s0b_ref,
                       of_ref, ob_ref, sf_ref, sb_ref, s_scr):
    i = pl.program_id(1)
    bpb = uf_ref.shape[0]
    c = uf_ref.shape[-2]

    @pl.when(i == 0)
    def _():
        s_scr[0] = s0f_ref[...]
        s_scr[1] = s0b_ref[...]

    refs = ((wqf_ref, uf_ref, akf_ref, egf_ref, of_ref), (wqb_ref, ub_ref, akb_ref, egb_ref, ob_ref))
    units = [(d, bb, h) for d in range(2) for bb in range(bpb) for h in range(DN_HEADS)]
    s = {u: s_scr[u] for u in units}
    ws = {(d, bb, h): _dot(refs[d][0][bb, 0, 0, h], s[d, bb, h].astype(BF16)) for d, bb, h in units}
    v16 = {(d, bb, h): (refs[d][1][bb, 0, 0, h] - ws[d, bb, h][:c]).astype(BF16) for d, bb, h in units}
    av = {(d, bb, h): _dot(refs[d][2][bb, 0, 0, h], v16[d, bb, h]) for d, bb, h in units}
    for d, bb, h in units:
        u = (d, bb, h)
        refs[d][4][bb, :, h * DN_DV:(h + 1) * DN_DV] = (ws[u][c:] + av[u][:c]).astype(BF16)
        s_scr[u] = s[u] * refs[d][3][bb, 0, 0, h:h + 1, :] + av[u][c:]

    @pl.when(i == pl.num_programs(1) - 1)
    def _():
        sf_ref[...] = s_scr[0]
        sb_ref[...] = s_scr[1]


def _delta_rule(qkv, ab, a_log, dt_bias, s0_f, s0_b):
    bsz, length, _ = qkv.shape
    bpb = 4 if bsz % 4 == 0 else 1
    c = DN_CHUNK
    n = length // c
    nh = DN_HEADS
    abt = jnp.swapaxes(ab.reshape(bsz, n, c, 4 * nh), 2, 3)
    zeros = jnp.zeros((2 * nh,), F32)
    a_vec = jnp.concatenate([jnp.exp(a_log.astype(F32)).reshape(-1), zeros])
    d_vec = jnp.concatenate([dt_bias.astype(F32).reshape(-1), zeros])
    small = lambda shape: pl.BlockSpec(shape, lambda b, i: (0,) * len(shape))
    nch = 2 if n % 2 == 0 else 1
    per_chunk = lambda *tail: pl.BlockSpec((1, nch, 2, nh) + tail, lambda b, i: (b, i, 0, 0) + (0,) * len(tail))
    wq, u, ak, egl = pl.pallas_call(
        _delta_prep_kernel, grid=(bsz, n // nch),
        in_specs=[pl.BlockSpec((1, nch * c, 3 * DN_W), lambda b, i: (b, i, 0)),
                  pl.BlockSpec((1, nch * c, 4 * nh), lambda b, i: (b, i, 0)),
                  pl.BlockSpec((1, nch, 4 * nh, c), lambda b, i: (b, i, 0, 0)),
                  small((1, 4 * nh)), small((1, 4 * nh)), small((4 * nh, 1)), small((4 * nh, 1))],
        out_specs=[per_chunk(2 * c, DN_DK), per_chunk(c, DN_DV), per_chunk(c + DN_DK, c),
                   pl.BlockSpec((1, nch, 2, nh, DN_DV), lambda b, i: (b, i, 0, 0, 0))],
        out_shape=[jax.ShapeDtypeStruct((bsz, n, 2, nh, 2 * c, DN_DK), BF16),
                   jax.ShapeDtypeStruct((bsz, n, 2, nh, c, DN_DV), F32),
                   jax.ShapeDtypeStruct((bsz, n, 2, nh, c + DN_DK, c), BF16),
                   jax.ShapeDtypeStruct((bsz, n, 2, nh, DN_DV), F32)],
        compiler_params=_params(("parallel", "parallel")), name="delta_prep",
    )(qkv, ab, abt, a_vec.reshape(1, -1), d_vec.reshape(1, -1), a_vec.reshape(-1, 1), d_vec.reshape(-1, 1))

    def side(d, *tail):
        idx = (lambda b, i: (b, n - 1 - i, 1, 0) + (0,) * len(tail)) if d else (
            lambda b, i: (b, i, 0, 0) + (0,) * len(tail))
        return pl.BlockSpec((bpb, 1, 1, nh) + tail, idx)

    def side_specs(d):
        return [side(d, 2 * c, DN_DK), side(d, c, DN_DV), side(d, c + DN_DK, c),
                pl.BlockSpec((bpb, 1, 1, nh, DN_DV),
                             (lambda b, i: (b, n - 1 - i, 1, 0, 0)) if d else (lambda b, i: (b, i, 0, 0, 0)))]

    st_spec = pl.BlockSpec((bpb, nh, DN_DK, DN_DV), lambda b, i: (b, 0, 0, 0))
    o_shape = jax.ShapeDtypeStruct((bsz, length, DN_W), BF16)
    s_shape = jax.ShapeDtypeStruct((bsz, nh, DN_DK, DN_DV), F32)
    return pl.pallas_call(
        _delta_scan_kernel, grid=(bsz // bpb, n),
        in_specs=side_specs(0) + side_specs(1) + [st_spec, st_spec],
        out_specs=[pl.BlockSpec((bpb, c, DN_W), lambda b, i: (b, i, 0)),
                   pl.BlockSpec((bpb, c, DN_W), lambda b, i: (b, n - 1 - i, 0)), st_spec, st_spec],
        out_shape=[o_shape, o_shape, s_shape, s_shape],
        scratch_shapes=[pltpu.VMEM((2, bpb, nh, DN_DK, DN_DV), F32)],
        compiler_params=_params(("parallel", "arbitrary")), name="delta_scan",
    )(wq, u, ak, egl, wq, u, ak, egl, s0_f, s0_b)


def _proj_out_kernel(yf_ref, yh_ref, of_ref, ob_ref, z_ref, dg_ref, w_ref, res_ref, gate_ref, o_ref):
    acc = _dot(yf_ref[...], w_ref[0:FOUR_W, :]) + _dot(yh_ref[...], w_ref[FOUR_W:FOUR_W + HY_W, :])
    base = FOUR_W + HY_W
    per = MXU_W // DN_DV
    for h0 in range(0, DN_HEADS, per):
        parts = []
        for h in range(h0, h0 + per):
            lanes = slice(h * DN_DV, (h + 1) * DN_DV)
            o = of_ref[:, lanes].astype(F32) + ob_ref[:, lanes].astype(F32)
            parts.append((_rms(o, dg_ref[...]) * _silu(z_ref[:, lanes].astype(F32))).astype(BF16))
        acc = acc + _dot(jnp.concatenate(parts, axis=1), w_ref[base + h0 * DN_DV:base + (h0 + per) * DN_DV, :])
    o_ref[...] = res_ref[...] + gate_ref[0] * acc


def _proj_out(y_four, y_hy, o_f, o_b, p2d, dn_g, w, res, gate, rows_per_mod):
    m, d = res.shape
    tm = min(512, m)
    tn = d
    per = rows_per_mod // tm
    zb = PCOL_Z // DN_W
    return pl.pallas_call(
        _proj_out_kernel, grid=(m // tm, d // tn),
        in_specs=[pl.BlockSpec((tm, FOUR_W), lambda i, j: (i, 0)),
                  pl.BlockSpec((tm, HY_W), lambda i, j: (i, 0)),
                  pl.BlockSpec((tm, DN_W), lambda i, j: (i, 0)),
                  pl.BlockSpec((tm, DN_W), lambda i, j: (i, 0)),
                  pl.BlockSpec((tm, DN_W), lambda i, j: (i, zb)),
                  pl.BlockSpec((1, DN_DV), lambda i, j: (0, 0)),
                  pl.BlockSpec((w.shape[0], tn), lambda i, j: (0, j)),
                  pl.BlockSpec((tm, tn), lambda i, j: (i, j)),
                  pl.BlockSpec((1, 1, tn), lambda i, j: (i // per, 0, j))],
        out_specs=pl.BlockSpec((tm, tn), lambda i, j: (i, j)),
        out_shape=jax.ShapeDtypeStruct((m, d), F32),
        compiler_params=_params(("parallel", "arbitrary")), name="proj_out",
    )(y_four, y_hy, o_f, o_b, p2d, dn_g.reshape(1, DN_DV), w, res, gate)


def _pack_rows(v):
    half = v.shape[1] // 2
    bits = pltpu.bitcast(v.astype(BF16).astype(F32), jnp.uint32)
    return (bits[:, :half] >> 16) | (bits[:, half:] & jnp.uint32(0xFFFF0000))


def _unpack_rows(u, dtype):
    lo = pltpu.bitcast(u << 16, F32).astype(dtype)
    hi = pltpu.bitcast(u & jnp.uint32(0xFFFF0000), F32).astype(dtype)
    return jnp.concatenate([lo, hi], axis=1)


def _select_experts(lg):
    lane = lax.broadcasted_iota(jnp.int32, lg.shape, 1)
    neg = -jnp.inf
    first = lambda mask: jnp.min(jnp.where(mask, lane, ROUTER_W), axis=1, keepdims=True)
    top = lambda mask: jnp.max(jnp.where(mask, lg, neg), axis=1, keepdims=True)
    coarse = lane < N_GROUPS
    m = top(coarse)
    p_sel = 1.0 / jnp.sum(jnp.where(coarse, jnp.exp(lg - m), 0.0), axis=1, keepdims=True)
    grp = first(coarse & (lg == m))
    lo = N_GROUPS + EXPERTS_PER_GROUP * grp
    fine = (lane >= lo) & (lane < lo + EXPERTS_PER_GROUP)
    v1 = top(fine)
    i1 = first(fine & (lg == v1))
    rest = fine & (lane != i1)
    v2 = top(rest)
    i2 = first(rest & (lg == v2))
    e2 = jnp.exp(v2 - v1)
    g1 = p_sel / (1.0 + e2)
    ids = jnp.where(lane == 0, i1 - N_GROUPS, jnp.where(lane == 1, i2 - N_GROUPS, 0))
    gates = jnp.where(lane == 0, g1, jnp.where(lane == 1, g1 * e2, 0.0))
    return ids, gates


def _router_kernel(x_ref, c_ref, g_ref, sx_ref, cx_ref, sc_ref, cc_ref, w_ref, b_ref, h_ref, id_ref, gt_ref,
                   *, n_x):
    i = pl.program_id(0)

    def emit(v, shift, scale):
        h = _rms(v, g_ref[...]) * (1.0 + scale) + shift
        h_ref[...] = _pack_rows(h)
        lg = jnp.dot(h, w_ref[...], precision=HIGHEST, preferred_element_type=F32) + b_ref[...]
        id_ref[...], gt_ref[...] = _select_experts(lg)

    @pl.when(i < n_x)
    def _():
        emit(x_ref[...], sx_ref[0], cx_ref[0])

    @pl.when(i >= n_x)
    def _():
        emit(c_ref[...], sc_ref[0], cc_ref[0])


def _router(x2d, c2d, g, shift_x, scale_x, shift_c, scale_c, w_r, b_r, rows_per_mod, with_ctx):
    mx, d = x2d.shape
    tm = 512
    n_x = mx // tm
    n_c = c2d.shape[0] // tm if with_ctx else 0
    per = rows_per_mod // tm
    xi = lambda i: (jnp.minimum(i, n_x - 1), 0)
    ci = lambda i: (jnp.maximum(i - n_x, 0), 0)
    h, ids, gates = pl.pallas_call(
        functools.partial(_router_kernel, n_x=n_x), grid=(n_x + n_c,),
        in_specs=[pl.BlockSpec((tm, d), xi), pl.BlockSpec((tm, d), ci),
                  pl.BlockSpec((1, d), lambda i: (0, 0)),
                  pl.BlockSpec((1, 1, d), lambda i: (jnp.minimum(i, n_x - 1) // per, 0, 0)),
                  pl.BlockSpec((1, 1, d), lambda i: (jnp.minimum(i, n_x - 1) // per, 0, 0)),
                  pl.BlockSpec((1, 1, d), lambda i: (0, 0, 0)),
                  pl.BlockSpec((1, 1, d), lambda i: (0, 0, 0)),
                  pl.BlockSpec((d, ROUTER_W), lambda i: (0, 0)),
                  pl.BlockSpec((1, ROUTER_W), lambda i: (0, 0))],
        out_specs=[pl.BlockSpec((tm, d // 2), lambda i: (i, 0)),
                   pl.BlockSpec((tm, ROUTER_W), lambda i: (i, 0)),
                   pl.BlockSpec((tm, ROUTER_W), lambda i: (i, 0))],
        out_shape=[jax.ShapeDtypeStruct(((n_x + n_c) * tm, d // 2), jnp.uint32),
                   jax.ShapeDtypeStruct(((n_x + n_c) * tm, ROUTER_W), jnp.int32),
                   jax.ShapeDtypeStruct(((n_x + n_c) * tm, ROUTER_W), F32)],
        compiler_params=_params(("parallel",)), name="moe_router",
    )(x2d, c2d, g.reshape(1, d), shift_x, scale_x, shift_c, scale_c, w_r, b_r)
    return h, ids[:, :TOP_K], gates[:, :TOP_K]


def _row_copy(src_hbm, row, dst, slot, r, sem):
    return pltpu.make_async_copy(src_hbm.at[pl.ds(row, 1)], dst.at[slot, pl.ds(r, 1)], sem.at[slot])


def _expert_kernel(be_ref, tok_ref, nb_ref, h_hbm, wg_ref, wu_ref, wd_ref, ys_ref, xbuf, sem, *, bm):
    i = pl.program_id(0)
    nb = nb_ref[0]

    def start(blk, slot):
        def body(r, carry):
            _row_copy(h_hbm, tok_ref[blk * bm + r], xbuf, slot, r, sem).start()
            return carry
        lax.fori_loop(0, bm, body, 0, unroll=8)

    def wait(slot):
        pltpu.make_async_copy(h_hbm.at[pl.ds(0, bm)], xbuf.at[slot], sem.at[slot]).wait()

    @pl.when(i == 0)
    def _():
        start(0, 0)

    @pl.when(i + 1 < nb)
    def _():
        start(i + 1, (i + 1) % 2)

    @pl.when(i < nb)
    def _():
        slot = i % 2
        wait(slot)
        x = _unpack_rows(xbuf[slot], BF16)
        act = (_silu(_dot(x, wg_ref[0, 0])) * _dot(x, wu_ref[0, 0])).astype(BF16)
        ys_ref[...] = _pack_rows(_dot(act, wd_ref[0, 0]))

    @pl.when(i >= nb)
    def _():
        ys_ref[...] = jnp.zeros_like(ys_ref)


def _experts(h, buf_tok, block_e, n_used, wg, wu, wd, layer, bm):
    n_blocks = block_e.shape[0]
    dp = h.shape[1]
    d, ff = wg.shape[2], wg.shape[3]
    grid_spec = pltpu.PrefetchScalarGridSpec(
        num_scalar_prefetch=3, grid=(n_blocks,),
        in_specs=[pl.BlockSpec(memory_space=pl.ANY),
                  pl.BlockSpec((1, 1, d, ff), lambda i, be, tok, nb: (layer, be[i], 0, 0)),
                  pl.BlockSpec((1, 1, d, ff), lambda i, be, tok, nb: (layer, be[i], 0, 0)),
                  pl.BlockSpec((1, 1, ff, d), lambda i, be, tok, nb: (layer, be[i], 0, 0))],
        out_specs=pl.BlockSpec((bm, dp), lambda i, be, tok, nb: (i, 0)),
        scratch_shapes=[pltpu.VMEM((2, bm, dp), jnp.uint32), pltpu.SemaphoreType.DMA((2,))])
    return pl.pallas_call(
        functools.partial(_expert_kernel, bm=bm), grid_spec=grid_spec,
        out_shape=jax.ShapeDtypeStruct((n_blocks * bm, dp), jnp.uint32),
        compiler_params=_params(("arbitrary",)), name="moe_experts",
    )(block_e, buf_tok, n_used, h, wg, wu, wd)


def _combine_kernel(pos_ref, ys_hbm, x_ref, gw_ref, gate_ref, fg_ref, o_ref, ybuf, sem, *, tm, tok0, final):
    i = pl.program_id(0)
    n = pl.num_programs(0)

    def start(blk, slot):
        def body(r, carry):
            t = tok0 + blk * tm + r
            _row_copy(ys_hbm, pos_ref[2 * t], ybuf, slot, r, sem).start(priority=0)
            _row_copy(ys_hbm, pos_ref[2 * t + 1], ybuf, slot, tm + r, sem).start(priority=1)
            return carry
        lax.fori_loop(0, tm, body, 0, unroll=8)

    @pl.when(i == 0)
    def _():
        start(0, 0)

    @pl.when(i + 1 < n)
    def _():
        start(i + 1, (i + 1) % 2)

    slot = i % 2

    pltpu.make_async_copy(ys_hbm.at[pl.ds(0, 2 * tm)], ybuf.at[slot], sem.at[slot]).wait()
    gw = gw_ref[...]
    y = (gw[:, 0:1] * _unpack_rows(ybuf[slot, 0:tm, :], F32)
         + gw[:, 1:2] * _unpack_rows(ybuf[slot, tm:2 * tm, :], F32))
    out = x_ref[...] + gate_ref[0] * y
    if final:
        out = _rms(out, fg_ref[...])
    o_ref[...] = out


def _combine(ys, pos, gw, x2d, gate, final_g, tok0, rows_per_mod, final):
    m, d = x2d.shape
    tm = 512
    per = rows_per_mod // tm
    gb = tok0 // tm
    grid_spec = pltpu.PrefetchScalarGridSpec(
        num_scalar_prefetch=1, grid=(m // tm,),
        in_specs=[pl.BlockSpec(memory_space=pl.ANY),
                  pl.BlockSpec((tm, d), lambda i, pos: (i, 0)),
                  pl.BlockSpec((tm, TOP_K), lambda i, pos: (gb + i, 0)),
                  pl.BlockSpec((1, 1, d), lambda i, pos: (i // per, 0, 0)),
                  pl.BlockSpec((1, d), lambda i, pos: (0, 0))],
        out_specs=pl.BlockSpec((tm, d), lambda i, pos: (i, 0)),
        scratch_shapes=[pltpu.VMEM((2, 2 * tm, ys.shape[1]), jnp.uint32), pltpu.SemaphoreType.DMA((2,))])
    return pl.pallas_call(
        functools.partial(_combine_kernel, tm=tm, tok0=tok0, final=final), grid_spec=grid_spec,
        out_shape=jax.ShapeDtypeStruct((m, d), F32),
        compiler_params=_params(("arbitrary",)), name="moe_combine",
    )(pos, ys, x2d, gw, gate, final_g.reshape(1, d))


def _rank_kernel(e_ref, rank_ref, cnt_ref, run_scr):
    @pl.when(pl.program_id(0) == 0)
    def _():
        run_scr[...] = jnp.zeros_like(run_scr)

    tb = e_ref.shape[0]
    lane = lax.broadcasted_iota(jnp.int32, (tb, LANE), 1)
    onehot = jnp.where(e_ref[...] == lane, 1.0, 0.0)
    ri = lax.broadcasted_iota(jnp.int32, (tb, tb), 0)
    ci = lax.broadcasted_iota(jnp.int32, (tb, tb), 1)
    before = jnp.where(ri > ci, 1.0, 0.0).astype(BF16)
    prefix = _dot(before, onehot.astype(BF16)) + run_scr[...]
    rank_ref[...] = jnp.sum(prefix * onehot, axis=1, keepdims=True).astype(jnp.int32)
    run_scr[...] += jnp.sum(onehot, axis=0, keepdims=True)
    cnt_ref[...] = run_scr[...]


def _expert_ranks(flat_e):
    a = flat_e.shape[0]
    tb = 1024 if a % 1024 == 0 else 512
    rank, cnt = pl.pallas_call(
        _rank_kernel, grid=(a // tb,),
        in_specs=[pl.BlockSpec((tb, 1), lambda i: (i, 0))],
        out_specs=[pl.BlockSpec((tb, 1), lambda i: (i, 0)), pl.BlockSpec((1, LANE), lambda i: (0, 0))],
        out_shape=[jax.ShapeDtypeStruct((a, 1), jnp.int32), jax.ShapeDtypeStruct((1, LANE), F32)],
        scratch_shapes=[pltpu.VMEM((1, LANE), F32)],
        compiler_params=_params(("arbitrary",)), name="moe_rank",
    )(flat_e.reshape(a, 1))
    return rank.reshape(a), cnt[0, :N_EXPERTS].astype(jnp.int32)


def _dispatch_plan(expert, bm):
    t = expert.shape[0]
    a = t * TOP_K
    flat_e = expert.reshape(a).astype(jnp.int32)
    rank, counts = _expert_ranks(flat_e)
    padded = (counts + bm - 1) // bm * bm
    pad_end = jnp.cumsum(padded)
    pad_start = pad_end - padded
    mine = flat_e[:, None] == jnp.arange(N_EXPERTS, dtype=jnp.int32)[None, :]
    pos = (jnp.sum(jnp.where(mine, pad_start[None, :], 0), axis=1) + rank).astype(jnp.int32)
    n_blocks = a // bm + N_EXPERTS
    flat_tok = jnp.arange(a, dtype=jnp.int32) // TOP_K
    buf_tok = jnp.zeros((n_blocks * bm,), jnp.int32).at[pos].set(flat_tok)
    n_used = (pad_end[-1] // bm).astype(jnp.int32)
    blk = jnp.minimum(jnp.arange(n_blocks, dtype=jnp.int32), n_used - 1) * bm
    block_e = jnp.sum((pad_end[None, :] <= blk[:, None]).astype(jnp.int32), axis=1)
    block_e = jnp.minimum(block_e, N_EXPERTS - 1).astype(jnp.int32)
    return pos, buf_tok, block_e, n_used.reshape(1)


def _mix_stream(p3, ab, rows, width, s0_f, s0_b, four_tabs, hy_tabs, hy_coefs, conv_w, conv_b, four_g,
                hy_bias, hy_g, a_log, dt_bias, with_mixers):
    qkv = _short_conv(p3, conv_w, conv_b, rows, width, OFF_DN, HY_CONV_CH, 3 * DN_W, 512)[0]
    o_f, o_b, s_f, s_b = _delta_rule(qkv, ab, a_log, dt_bias, s0_f, s0_b)
    if not with_mixers:
        return None, None, o_f, o_b, s_f, s_b
    if _hy_two_level(p3.shape[1]):
        y_four = _fourier2(p3, four_tabs, four_g)
    else:
        y_four = _fourier(p3, *four_tabs, four_g)
    if _hy_two_level(p3.shape[1]):
        conv3 = _short_conv(p3, conv_w, conv_b, rows, width, OFF_HY, 0, HY_CONV_CH, 256, split=3)
        y_hy = _hyena2(conv3, hy_tabs, hy_coefs, hy_bias, hy_g)
    else:
        convh = _short_conv(p3, conv_w, conv_b, rows, width, OFF_HY, 0, HY_CONV_CH, 256)[0]
        y_hy = _hyena(convh, hy_tabs, hy_coefs, hy_bias, hy_g)
    return y_four, y_hy, o_f, o_b, s_f, s_b


def _hy_two_level(length):
    return length % (8 * HY_FAST) == 0


def kernel(x, c, ctx, c_ctx, norm1_g, norm2_g, w_mod, b_mod, w_in, conv_w, conv_b, four_g, hy_w1, hy_b1, hy_w2, hy_b2, hy_w3, hy_b3, hy_w4, hy_freq, hy_bias, hy_g, dn_a_log, dn_dt_bias, dn_g, w_out, w_rc, b_rc, w_rf, b_rf, w_e_gate, w_e_up, w_e_down, final_g):
    bsz, length, d = x.shape
    lc = ctx.shape[1]
    depth = w_in.shape[0]
    rows = length // GRID_W
    bm = 256

    mod_all = _modulation(c, c_ctx, w_mod, b_mod)
    four_tables = lambda n: _four2_tables(n) if _hy_two_level(n) else _fourier_tables(n)
    four_x, four_c = four_tables(length), four_tables(lc)
    hy_tables = lambda n: _hy2_tables(n) if _hy_two_level(n) else _hyena_tables(n)
    hy_coefs = lambda n, tabs, prm: (_hy2_coefs if _hy_two_level(n) else _hyena_coefs)(n, tabs, prm)
    hy_x, hy_c = hy_tables(length), hy_tables(lc)
    zeros = jnp.zeros((bsz, DN_HEADS, DN_DK, DN_DV), F32)
    x2 = x.reshape(bsz * length, d)
    c2 = ctx.reshape(bsz * lc, d)
    wg16, wu16, wd16 = w_e_gate.astype(BF16), w_e_up.astype(BF16), w_e_down.astype(BF16)

    for l in range(depth):
        last = l == depth - 1
        mod = [m[:, None, :] for m in jnp.split(mod_all[l, :bsz], 6, axis=-1)]
        modc = [m[:, None, :] for m in jnp.split(mod_all[l, bsz:bsz + 1], 6, axis=-1)]
        wi = w_in[l]
        w_p = jnp.concatenate([wi[:, :OFF_AB], wi[:, OFF_Z:], wi[:, OFF_AB:OFF_Z],
                               jnp.zeros((d, PROJ_W - PCOL_AB - N_AB), wi.dtype)], axis=1).astype(BF16)
        w_o = w_out[l].astype(BF16)
        hy_params = (hy_w1[l], hy_b1[l], hy_w2[l], hy_b2[l], hy_w3[l], hy_b3[l], hy_w4[l], hy_freq[l])
        mix_args = (conv_w[l], conv_b[l], four_g[l], hy_bias[l], hy_g[l], dn_a_log[l], dn_dt_bias[l])

        pc, abc = _proj_in(c2, norm1_g[l], modc[0], modc[1], w_p, bsz * lc)
        pc3 = pc.reshape(bsz, lc, PROJ_W)
        abc = abc[:, :N_AB].reshape(bsz, lc, N_AB)
        coefs_c = None if last else hy_coefs(lc, hy_c, hy_params)
        yf, yh, o_f, o_b, s_f, s_b = _mix_stream(pc3, abc, 1, lc, zeros, zeros, four_c, hy_c, coefs_c,
                                                 *mix_args, with_mixers=not last)
        if not last:
            c2 = _proj_out(yf.reshape(-1, FOUR_W), yh.reshape(-1, HY_W), o_f.reshape(-1, DN_W),
                           o_b.reshape(-1, DN_W), pc, dn_g[l], w_o, c2, modc[2], bsz * lc)

        p, abx = _proj_in(x2, norm1_g[l], mod[0], mod[1], w_p, length)
        p3 = p.reshape(bsz, length, PROJ_W)
        abx = abx[:, :N_AB].reshape(bsz, length, N_AB)
        coefs_x = hy_coefs(length, hy_x, hy_params)
        yf, yh, o_f, o_b, _, _ = _mix_stream(p3, abx, rows, GRID_W, s_f, s_b, four_x, hy_x, coefs_x,
                                             *mix_args, with_mixers=True)
        x2 = _proj_out(yf.reshape(-1, FOUR_W), yh.reshape(-1, HY_W), o_f.reshape(-1, DN_W),
                       o_b.reshape(-1, DN_W), p, dn_g[l], w_o, x2, mod[2], length)

        w_r = jnp.concatenate([w_rc[l], w_rf[l], jnp.zeros((d, ROUTER_W - N_GROUPS - N_EXPERTS), F32)], axis=1)
        b_r = jnp.concatenate([b_rc[l], b_rf[l], jnp.zeros((ROUTER_W - N_GROUPS - N_EXPERTS,), F32)])[None, :]
        h, expert, gate = _router(x2, c2, norm2_g[l], mod[3], mod[4], modc[3], modc[4], w_r, b_r, length,
                                  with_ctx=not last)
        pos, buf_tok, block_e, n_used = _dispatch_plan(expert, bm)
        ys = _experts(h, buf_tok, block_e, n_used, wg16, wu16, wd16, l, bm)
        x2 = _combine(ys, pos, gate, x2, mod[5], final_g, 0, length, final=last)
        if not last:
            c2 = _combine(ys, pos, gate, c2, modc[5], final_g, bsz * length, bsz * lc, final=False)
    return x2.reshape(bsz, length, d)
```
